```python
import math
import jax, jax.numpy as jnp
from jax import lax
import numpy as np

D_MODEL = 1024
BATCH = 2
SEQ = 8192
DEPTH = 2

GRID_W = 64
CTX_LEN = 256
N_EVEN = (DEPTH + 1) // 2
N_ODD = DEPTH // 2
NORM_EPS = 1e-6
ROPE_THETA = 10000.0
Q_BLOCK = 128
N_MOD = 6
S5_WIDTH = D_MODEL // 2
S5_GROUP_DIM = 16
S5_GROUPS = S5_WIDTH // S5_GROUP_DIM
S5_STATE = 64
S5_DT_MIN = 1e-3
S5_DT_MAX = 1e-1
MLA_HEADS = 4
MLA_NOPE = 128
MLA_ROPE = 64
MLA_QK_DIM = MLA_NOPE + MLA_ROPE
MLA_V = 128
MLA_Q_RANK = 384
MLA_KV_RANK = 256
MLA_SCALE = MLA_QK_DIM ** -0.5
A_IN_WIDTH = S5_WIDTH + MLA_Q_RANK + MLA_KV_RANK + MLA_ROPE
A_OUT_WIDTH = S5_WIDTH + MLA_HEADS * MLA_V
WIN_HEADS = 16
WIN_KV_HEADS = 4
WIN_GROUP = WIN_HEADS // WIN_KV_HEADS
WIN_HEAD_DIM = 64
WINDOW = 128
WIN_SCALE = WIN_HEAD_DIM ** -0.5
C_IN_WIDTH = (WIN_HEADS + 2 * WIN_KV_HEADS) * WIN_HEAD_DIM
C_OUT_WIDTH = WIN_HEADS * WIN_HEAD_DIM
FFN_HIDDEN = -(-8 * D_MODEL // (3 * 256)) * 256

kernel_name = 'hybrid_s5_mla_window_dit_prefix'


def rms_norm(x, gain):
    xf = x.astype(jnp.float32)
    y = xf * lax.rsqrt(jnp.mean(xf * xf, axis=-1, keepdims=True) + NORM_EPS)
    return (y * gain.astype(jnp.float32)).astype(x.dtype)


def ada_modulation(cond, w, b):
    m = (jax.nn.silu(cond) @ w + b)[..., None, :]
    return jnp.split(m, N_MOD, axis=-1)


def modulate(x, gain, shift, scale):
    return rms_norm(x, gain) * (1 + scale) + shift


def grid_rope_tables(rows, rot_dim):
    row = jnp.repeat(jnp.arange(rows, dtype=jnp.int32), GRID_W)
    col = jnp.tile(jnp.arange(GRID_W, dtype=jnp.int32), rows)
    n_freq = rot_dim // 4
    inv_freq = ROPE_THETA ** (-jnp.arange(n_freq, dtype=jnp.float32) / n_freq)
    ang_r = row.astype(jnp.float32)[:, None] * inv_freq
    ang_c = col.astype(jnp.float32)[:, None] * inv_freq
    ang = jnp.concatenate([ang_r, ang_r, ang_c, ang_c], axis=-1)
    return jnp.cos(ang), jnp.sin(ang)


def apply_axial_rope(x, cos, sin):
    x1, x2, x3, x4 = jnp.split(x, 4, axis=-1)
    rot = jnp.concatenate([-x2, x1, -x4, x3], axis=-1)
    return x * cos[:, None, :].astype(x.dtype) + rot * sin[:, None, :].astype(x.dtype)


def swiglu(h, w_gate, w_up, w_down):
    return (jax.nn.silu(h @ w_gate) * (h @ w_up)) @ w_down


def zoh_discretize(lam_re, lam_im, log_step, b_re, b_im):
    lam = lax.complex(lam_re.astype(jnp.float32), lam_im.astype(jnp.float32))
    step = jnp.exp(log_step.astype(jnp.float32))[:, None]
    lam_bar = jnp.exp(lam * step)
    b = lax.complex(b_re.astype(jnp.float32), b_im.astype(jnp.float32))
    b_bar = ((lam_bar - 1.0) / lam)[..., None] * b
    return lam_bar, b_bar


def diag_scan(lam_bar, bu, h0):
    if h0 is not None:
        bu = bu.at[:, 0].add(lam_bar * h0)
    a = jnp.broadcast_to(lam_bar, (1, bu.shape[1]) + lam_bar.shape)

    def combine(left, right):
        a_l, b_l = left
        a_r, b_r = right
        return a_l * a_r, a_r * b_l + b_r

    _, h = lax.associative_scan(combine, (a, bu), axis=1)
    return h


def s5_bidirectional(u_ctx, u_lat, lam_re, lam_im, log_step, b_re, b_im, c_re, c_im,
                     d_skip, w_glu, b_glu, need_ctx):
    def grouped(u):
        return u.reshape(u.shape[0], u.shape[1], S5_GROUPS, S5_GROUP_DIM)

    uc, ul = grouped(u_ctx), grouped(u_lat)
    d = d_skip.reshape(S5_GROUPS, S5_GROUP_DIM).astype(jnp.float32)
    y_lat = ul.astype(jnp.float32) * d
    y_ctx = uc.astype(jnp.float32) * d if need_ctx else None
    for direction in range(2):
        rev = direction == 1
        lam_bar, b_bar = zoh_discretize(lam_re[direction], lam_im[direction], log_step[direction],
                                        b_re[direction], b_im[direction])
        c_mat = lax.complex(c_re[direction].astype(jnp.float32), c_im[direction].astype(jnp.float32))

        def drive(u):
            u = jnp.flip(u, axis=1) if rev else u
            return jnp.einsum('btgs,gps->btgp', u.astype(jnp.complex64), b_bar)

        def readout(h):
            y = jnp.real(jnp.einsum('btgp,gsp->btgs', h, c_mat))
            return jnp.flip(y, axis=1) if rev else y

        h_c = diag_scan(lam_bar, drive(uc), None)
        h_l = diag_scan(lam_bar, drive(ul), h_c[:, -1])
        y_lat = y_lat + readout(h_l)
        if need_ctx:
            y_ctx = y_ctx + readout(h_c)

    def glu(y):
        y = jax.nn.gelu(y.reshape(y.shape[0], y.shape[1], S5_WIDTH))
        return (y * jax.nn.sigmoid(y @ w_glu + b_glu)).astype(u_lat.dtype)

    return (glu(y_ctx) if need_ctx else None), glu(y_lat)


def mla_queries(cq, qa_norm, w_q_b, q_norm, cos, sin):
    b, t, _ = cq.shape
    q = (rms_norm(cq, qa_norm) @ w_q_b).reshape(b, t, MLA_HEADS, MLA_QK_DIM)
    q = rms_norm(q, q_norm)
    if cos is not None:
        q = jnp.concatenate([q[..., :MLA_NOPE], apply_axial_rope(q[..., MLA_NOPE:], cos, sin)], axis=-1)
    return q


def mla_keys_values(ckv, k_rope, kva_norm, w_kv_b, k_norm, cos, sin):
    b, t, _ = ckv.shape
    kv = (rms_norm(ckv, kva_norm) @ w_kv_b).reshape(b, t, MLA_HEADS, MLA_NOPE + MLA_V)
    k_nope, v = kv[..., :MLA_NOPE], kv[..., MLA_NOPE:]
    k_pe = jnp.broadcast_to(k_rope[:, :, None, :], (b, t, MLA_HEADS, MLA_ROPE))
    k = rms_norm(jnp.concatenate([k_nope, k_pe], axis=-1), k_norm)
    if cos is not None:
        k = jnp.concatenate([k[..., :MLA_NOPE], apply_axial_rope(k[..., MLA_NOPE:], cos, sin)], axis=-1)
    return k, v


def full_attention(q, k, v, scale):
    s = jnp.einsum('bqhd,bkhd->bhqk', q, k).astype(jnp.float32) * scale
    p = jax.nn.softmax(s, axis=-1).astype(v.dtype)
    return jnp.einsum('bhqk,bkhd->bqhd', p, v)


def blocked_dense_attention(q, k, v, scale):
    b, n, h, dq = q.shape
    nb = n // Q_BLOCK
    qb = q.reshape(b, nb, Q_BLOCK, h, dq).swapaxes(0, 1)
    out = lax.map(lambda q_blk: full_attention(q_blk, k, v, scale), qb)
    return out.swapaxes(0, 1).reshape(b, n, h * v.shape[-1])


def ssm_mla_mixer(h_ctx, h_lat, w_in, w_out, lam_re, lam_im, log_step, b_re, b_im, c_re, c_im,
                  d_skip, w_glu, b_glu, qa_norm, w_q_b, kva_norm, w_kv_b, q_norm, k_norm,
                  cos, sin, need_ctx):
    cuts = [S5_WIDTH, S5_WIDTH + MLA_Q_RANK, S5_WIDTH + MLA_Q_RANK + MLA_KV_RANK]
    u_c, cq_c, ckv_c, kr_c = jnp.split(h_ctx @ w_in, cuts, axis=-1)
    u_l, cq_l, ckv_l, kr_l = jnp.split(h_lat @ w_in, cuts, axis=-1)
    b, n, _ = h_lat.shape
    s5_c, s5_l = s5_bidirectional(u_c, u_l, lam_re, lam_im, log_step, b_re, b_im, c_re, c_im,
                                  d_skip, w_glu, b_glu, need_ctx)
    k_c, v_c = mla_keys_values(ckv_c, kr_c, kva_norm, w_kv_b, k_norm, None, None)
    k_l, v_l = mla_keys_values(ckv_l, kr_l, kva_norm, w_kv_b, k_norm, cos, sin)
    q_l = mla_queries(cq_l, qa_norm, w_q_b, q_norm, cos, sin)
    o_l = blocked_dense_attention(q_l, jnp.concatenate([k_c, k_l], axis=1),
                                  jnp.concatenate([v_c, v_l], axis=1), MLA_SCALE)
    out_l = jnp.concatenate([s5_l, o_l], axis=-1) @ w_out
    out_c = None
    if need_ctx:
        q_c = mla_queries(cq_c, qa_norm, w_q_b, q_norm, None, None)
        o_c = full_attention(q_c, k_c, v_c, MLA_SCALE)
        o_c = o_c.reshape(b, o_c.shape[1], MLA_HEADS * MLA_V)
        out_c = jnp.concatenate([s5_c, o_c], axis=-1) @ w_out
    return out_c, out_l


def sink_softmax(s, sink_logit):
    m = jnp.maximum(jnp.max(s, axis=-1, keepdims=True), sink_logit)
    p = jnp.exp(s - m)
    return p / (jnp.sum(p, axis=-1, keepdims=True) + jnp.exp(sink_logit - m))


def context_sink_attention(q, k, v, sink_l):
    s = jnp.einsum('bqkgd,bjkd->bkgqj', q, k).astype(jnp.float32) * WIN_SCALE
    w = sink_softmax(s, sink_l).astype(v.dtype)
    return jnp.einsum('bkgqj,bjkd->bqkgd', w, v)


def banded_sink_attention(q, k, v, k_ctx, v_ctx, sink_l):
    b, n = q.shape[0], q.shape[1]
    nb = n // Q_BLOCK
    band = 3 * Q_BLOCK
    qb = q.reshape(b, nb, Q_BLOCK, WIN_KV_HEADS, WIN_GROUP, WIN_HEAD_DIM).swapaxes(0, 1)
    pad = ((0, 0), (Q_BLOCK, Q_BLOCK), (0, 0), (0, 0))
    k_pad, v_pad = jnp.pad(k, pad), jnp.pad(v, pad)
    rel = (jnp.arange(band)[None, :] - Q_BLOCK) - jnp.arange(Q_BLOCK)[:, None]
    in_window = jnp.abs(rel) <= WINDOW

    def block(args):
        i, q_blk = args
        start = i * Q_BLOCK
        k_blk = lax.dynamic_slice_in_dim(k_pad, start, band, axis=1)
        v_blk = lax.dynamic_slice_in_dim(v_pad, start, band, axis=1)
        key_pos = start - Q_BLOCK + jnp.arange(band)
        valid = in_window & ((key_pos >= 0) & (key_pos < n))[None, :]
        s_loc = jnp.einsum('bqkgd,bjkd->bkgqj', q_blk, k_blk).astype(jnp.float32) * WIN_SCALE
        s_loc = jnp.where(valid, s_loc, -jnp.inf)
        s_ctx = jnp.einsum('bqkgd,bjkd->bkgqj', q_blk, k_ctx).astype(jnp.float32) * WIN_SCALE
        w = sink_softmax(jnp.concatenate([s_ctx, s_loc], axis=-1), sink_l).astype(v.dtype)
        return jnp.einsum('bkgqj,bjkd->bqkgd', w, jnp.concatenate([v_ctx, v_blk], axis=1))

    out = lax.map(block, (jnp.arange(nb), qb))
    return out.swapaxes(0, 1).reshape(b, n, C_OUT_WIDTH)


def window_gqa_mixer(h_ctx, h_lat, w_in, w_out, q_norm, k_norm, sink, cos, sin, need_ctx):
    q_w = WIN_HEADS * WIN_HEAD_DIM
    kv_w = WIN_KV_HEADS * WIN_HEAD_DIM
    sink_l = sink.astype(jnp.float32).reshape(WIN_KV_HEADS, WIN_GROUP)[None, :, :, None, None]

    def project(h, rope_cos, rope_sin):
        bb, t, _ = h.shape
        q, k, v = jnp.split(h @ w_in, [q_w, q_w + kv_w], axis=-1)
        k = rms_norm(k.reshape(bb, t, WIN_KV_HEADS, WIN_HEAD_DIM), k_norm)
        v = v.reshape(bb, t, WIN_KV_HEADS, WIN_HEAD_DIM)
        if rope_cos is not None:
            k = apply_axial_rope(k, rope_cos, rope_sin)
        return q, k, v

    def prep_queries(q, rope_cos, rope_sin):
        bb, t, _ = q.shape
        q = rms_norm(q.reshape(bb, t, WIN_HEADS, WIN_HEAD_DIM), q_norm)
        if rope_cos is not None:
            q = apply_axial_rope(q, rope_cos, rope_sin)
        return q.reshape(bb, t, WIN_KV_HEADS, WIN_GROUP, WIN_HEAD_DIM)

    q_c, k_c, v_c = project(h_ctx, None, None)
    q_l, k_l, v_l = project(h_lat, cos, sin)
    o_l = banded_sink_attention(prep_queries(q_l, cos, sin), k_l, v_l, k_c, v_c, sink_l)
    out_l = o_l @ w_out
    out_c = None
    if need_ctx:
        o_c = context_sink_attention(prep_queries(q_c, None, None), k_c, v_c, sink_l)
        out_c = o_c.reshape(o_c.shape[0], o_c.shape[1], C_OUT_WIDTH) @ w_out
    return out_c, out_l


def setup_inputs(seed: int = 0) -> dict:
    key = jax.random.key(seed)
    ks = list(jax.random.split(key, 48))

    def nrm(shape, scale):
        return jax.random.normal(ks.pop(), shape, jnp.float32) * scale

    def gain(shape):
        return 1.0 + nrm(shape, 0.05)

    G, P, S, W = S5_GROUPS, S5_STATE, S5_GROUP_DIM, S5_WIDTH
    lam_im_base = jnp.pi * jnp.arange(P, dtype=jnp.float32)
    return {
        'x': nrm((BATCH, SEQ, D_MODEL), 1.0),
        'c': nrm((BATCH, D_MODEL), 1.0),
        'ctx': nrm((BATCH, CTX_LEN, D_MODEL), 1.0),
        'c_ctx': nrm((D_MODEL,), 1.0),
        'ada_w': nrm((DEPTH, D_MODEL, N_MOD * D_MODEL), 0.5 * D_MODEL ** -0.5),
        'ada_b': nrm((DEPTH, N_MOD * D_MODEL), 0.02),
        'norm_mix': gain((DEPTH, D_MODEL)),
        'norm_ffn': gain((DEPTH, D_MODEL)),
        'ffn_w_gate': nrm((DEPTH, D_MODEL, FFN_HIDDEN), D_MODEL ** -0.5),
        'ffn_w_up': nrm((DEPTH, D_MODEL, FFN_HIDDEN), D_MODEL ** -0.5),
        'ffn_w_down': nrm((DEPTH, FFN_HIDDEN, D_MODEL), FFN_HIDDEN ** -0.5),
        'a_w_in': nrm((N_EVEN, D_MODEL, A_IN_WIDTH), D_MODEL ** -0.5),
        'a_w_out': nrm((N_EVEN, A_OUT_WIDTH, D_MODEL), A_OUT_WIDTH ** -0.5),
        's5_lam_re': -0.5 + nrm((N_EVEN, 2, G, P), 0.01),
        's5_lam_im': lam_im_base + nrm((N_EVEN, 2, G, P), 0.01),
        's5_log_step': jax.random.uniform(ks.pop(), (N_EVEN, 2, G), jnp.float32,
                                          math.log(S5_DT_MIN), math.log(S5_DT_MAX)),
        's5_b_re': nrm((N_EVEN, 2, G, P, S), (2 * S) ** -0.5),
        's5_b_im': nrm((N_EVEN, 2, G, P, S), (2 * S) ** -0.5),
        's5_c_re': nrm((N_EVEN, 2, G, S, P), (2 * P) ** -0.5),
        's5_c_im': nrm((N_EVEN, 2, G, S, P), (2 * P) ** -0.5),
        's5_d': nrm((N_EVEN, W), 1.0),
        's5_w_glu': nrm((N_EVEN, W, W), W ** -0.5),
        's5_b_glu': nrm((N_EVEN, W), 0.02),
        'mla_qa_norm': gain((N_EVEN, MLA_Q_RANK)),
        'mla_w_q_b': nrm((N_EVEN, MLA_Q_RANK, MLA_HEADS * MLA_QK_DIM), MLA_Q_RANK ** -0.5),
        'mla_kva_norm': gain((N_EVEN, MLA_KV_RANK)),
        'mla_w_kv_b': nrm((N_EVEN, MLA_KV_RANK, MLA_HEADS * (MLA_NOPE + MLA_V)), MLA_KV_RANK ** -0.5),
        'mla_q_norm': gain((N_EVEN, MLA_QK_DIM)),
        'mla_k_norm': gain((N_EVEN, MLA_QK_DIM)),
        'c_w_in': nrm((N_ODD, D_MODEL, C_IN_WIDTH), D_MODEL ** -0.5),
        'c_w_out': nrm((N_ODD, C_OUT_WIDTH, D_MODEL), C_OUT_WIDTH ** -0.5),
        'c_q_norm': gain((N_ODD, WIN_HEAD_DIM)),
        'c_k_norm': gain((N_ODD, WIN_HEAD_DIM)),
        'c_sink': nrm((N_ODD, WIN_HEADS), 0.5),
    }


def reference(x, c, ctx, c_ctx, ada_w, ada_b, norm_mix, norm_ffn, ffn_w_gate, ffn_w_up, ffn_w_down,
              a_w_in, a_w_out, s5_lam_re, s5_lam_im, s5_log_step, s5_b_re, s5_b_im, s5_c_re, s5_c_im,
              s5_d, s5_w_glu, s5_b_glu, mla_qa_norm, mla_w_q_b, mla_kva_norm, mla_w_kv_b,
              mla_q_norm, mla_k_norm, c_w_in, c_w_out, c_q_norm, c_k_norm, c_sink):
    rows = x.shape[1] // GRID_W
    cos_a, sin_a = grid_rope_tables(rows, MLA_ROPE)
    cos_c, sin_c = grid_rope_tables(rows, WIN_HEAD_DIM)
    h_ctx, h_lat = ctx, x
    for i in range(DEPTH):
        need_ctx = i < DEPTH - 1
        j = i // 2
        sh_l, sc_l, g_l, sh2_l, sc2_l, g2_l = ada_modulation(c, ada_w[i], ada_b[i])
        sh_c, sc_c, g_c, sh2_c, sc2_c, g2_c = ada_modulation(c_ctx, ada_w[i], ada_b[i])
        a_l = modulate(h_lat, norm_mix[i], sh_l, sc_l)
        a_c = modulate(h_ctx, norm_mix[i], sh_c, sc_c)
        if i % 2 == 0:
            o_c, o_l = ssm_mla_mixer(a_c, a_l, a_w_in[j], a_w_out[j], s5_lam_re[j], s5_lam_im[j],
                                     s5_log_step[j], s5_b_re[j], s5_b_im[j], s5_c_re[j], s5_c_im[j],
                                     s5_d[j], s5_w_glu[j], s5_b_glu[j], mla_qa_norm[j], mla_w_q_b[j],
                                     mla_kva_norm[j], mla_w_kv_b[j], mla_q_norm[j], mla_k_norm[j],
                                     cos_a, sin_a, need_ctx)
        else:
            o_c, o_l = window_gqa_mixer(a_c, a_l, c_w_in[j], c_w_out[j], c_q_norm[j], c_k_norm[j],
                                        c_sink[j], cos_c, sin_c, need_ctx)
        h_lat = h_lat + g_l * o_l
        h_lat = h_lat + g2_l * swiglu(modulate(h_lat, norm_ffn[i], sh2_l, sc2_l),
                                      ffn_w_gate[i], ffn_w_up[i], ffn_w_down[i])
        if need_ctx:
            h_ctx = h_ctx + g_c * o_c
            h_ctx = h_ctx + g2_c * swiglu(modulate(h_ctx, norm_ffn[i], sh2_c, sc2_c),
                                          ffn_w_gate[i], ffn_w_up[i], ffn_w_down[i])
    return h_lat
```

```python
import functools
import math

import jax
import jax.numpy as jnp
from jax import lax
from jax.experimental import pallas as pl
from jax.experimental.pallas import tpu as pltpu

F32 = jnp.float32
BF16 = jnp.bfloat16

GRID_W = 64
NORM_EPS = 1e-6
ROPE_THETA = 10000.0
N_MOD = 6
S5_GROUP_DIM = 16
S5_STATE = 64
S5_CHUNK = 16
MLA_HEADS = 4
MLA_NOPE = 128
MLA_ROPE = 64
MLA_QK_DIM = MLA_NOPE + MLA_ROPE
MLA_V = 128
MLA_Q_RANK = 384
MLA_KV_RANK = 256
MLA_SCALE = MLA_QK_DIM ** -0.5
MLA_HEAD_PAD = 256
WIN_HEADS = 16
WIN_KV_HEADS = 4
WIN_GROUP = WIN_HEADS // WIN_KV_HEADS
WIN_HEAD_DIM = 64
WINDOW = 128
WIN_SCALE = WIN_HEAD_DIM ** -0.5
LANES = 128
SUBLANES = 8
MXU_TILE = 256
VMEM_LIMIT_BYTES = 48 * 1024 * 1024
NEG_BIG = -1e30
MOD_ROWS = 8


def _dot(a, b):
    return jnp.dot(a, b, preferred_element_type=F32)


def _dot_nt(a, b):
    return lax.dot_general(a, b, (((1,), (1,)), ((), ())), preferred_element_type=F32)


def _split_bf16(x):
    hi = x.astype(BF16)
    lo = (x - hi.astype(F32)).astype(BF16)
    return hi, lo


def _rms(x, gain):
    return x * lax.rsqrt(jnp.mean(x * x, axis=-1, keepdims=True) + NORM_EPS) * gain


def _silu(x):
    return x * jax.nn.sigmoid(x)


def _params(*sem):
    return pltpu.CompilerParams(dimension_semantics=sem, vmem_limit_bytes=VMEM_LIMIT_BYTES)


def _const_spec(shape):
    nd = len(shape)
    return pl.BlockSpec(shape, lambda *_: (0,) * nd, pipeline_mode=pl.Buffered(1))


def _ada_kernel(cond_ref, w_ref, b_ref, o_ref):
    s = _silu(cond_ref[...])
    s_hi, s_lo = _split_bf16(s)
    w_hi, w_lo = _split_bf16(w_ref[...])
    o_ref[...] = _dot(s_hi, w_hi) + _dot(s_lo, w_hi) + _dot(s_hi, w_lo) + b_ref[...]


def _ada_modulation(cond, ada_w, ada_b):
    depth, d, n = ada_w.shape
    tn = 1024
    return pl.pallas_call(
        _ada_kernel,
        grid=(depth, n // tn),
        in_specs=[pl.BlockSpec((MOD_ROWS, d), lambda i, j: (0, 0)),
                  pl.BlockSpec((None, d, tn), lambda i, j: (i, 0, j)),
                  pl.BlockSpec((None, 1, tn), lambda i, j: (i, 0, j))],
        out_specs=pl.BlockSpec((None, MOD_ROWS, tn), lambda i, j: (i, 0, j)),
        out_shape=jax.ShapeDtypeStruct((depth, MOD_ROWS, n), F32),
        compiler_params=_params("arbitrary", "arbitrary"),
        name="ada_modulation",
    )(cond, ada_w, ada_b.reshape(depth, 1, n))


def _mod_specs(d, slots, ctx_row):
    def make(slot):
        if ctx_row is None:
            return pl.BlockSpec((None, None, 1, d), lambda b, i: (b, slot, 0, 0))
        return pl.BlockSpec((None, None, 1, d), lambda b, i: (ctx_row, slot, 0, 0))
    return [make(s) for s in slots]


def _grid_rope_tables(rows, rot_dim):
    row = jnp.repeat(jnp.arange(rows, dtype=jnp.int32), GRID_W)
    col = jnp.tile(jnp.arange(GRID_W, dtype=jnp.int32), rows)
    n_freq = rot_dim // 4
    inv_freq = ROPE_THETA ** (-jnp.arange(n_freq, dtype=F32) / n_freq)
    ang_r = row.astype(F32)[:, None] * inv_freq
    ang_c = col.astype(F32)[:, None] * inv_freq
    ang = jnp.concatenate([ang_r, ang_r, ang_c, ang_c], axis=-1)
    return jnp.cos(ang), jnp.sin(ang)


def _rot_perm_sign(rot_dim):
    q = rot_dim // 4
    idx = jnp.arange(rot_dim)
    perm = jnp.where((idx // q) % 2 == 0, idx + q, idx - q)
    sign = jnp.where((idx // q) % 2 == 0, -1.0, 1.0).astype(F32)
    return perm, sign


def _proj_a_kernel(h_ref, sh_ref, sc_ref, gn_ref, w1_ref, gqa_ref, wq_ref, gkv_ref, wkv_ref,
                   gq_ref, gk_ref, cs_ref, msk_ref, u_ref, q_ref, k_ref, v_ref):
    a = _rms(h_ref[...], gn_ref[...]) * (1.0 + sc_ref[...]) + sh_ref[...]
    p1 = _dot(a.astype(BF16), w1_ref[...])
    u_ref[...] = p1[:, :512].astype(BF16)
    cq = p1[:, 512:512 + MLA_Q_RANK]
    ckv = p1[:, 896:896 + MLA_KV_RANK]
    krr = p1[:, 1152:1280]
    qb = _dot(_rms(cq, gqa_ref[...]).astype(BF16), wq_ref[...])
    kv = _dot(_rms(ckv, gkv_ref[...]).astype(BF16), wkv_ref[...])
    v_ref[...] = kv[:, 512:].astype(BF16)
    cs = cs_ref[...]
    msk = msk_ref[...]
    low_half = lax.broadcasted_iota(jnp.int32, cs.shape, 1) < MLA_ROPE

    def finish(xh, gain, out_ref, h):
        hi, lo = _split_bf16(xh * xh)
        ssq = _dot(hi, msk) + _dot(lo, msk)
        xn = xh * lax.rsqrt(ssq * (1.0 / MLA_QK_DIM) + NORM_EPS) * gain
        rr = xn[:, LANES:] * cs
        rot = rr + pltpu.roll(rr, MLA_ROPE, axis=1)
        base = h * MLA_HEAD_PAD
        out_ref[:, base:base + LANES] = xn[:, :LANES].astype(BF16)
        out_ref[:, base + LANES:base + 2 * LANES] = jnp.where(low_half, rot, 0.0).astype(BF16)

    for h in range(MLA_HEADS):
        base = h * MLA_HEAD_PAD
        finish(qb[:, base:base + MLA_HEAD_PAD], gq_ref[:, base:base + MLA_HEAD_PAD], q_ref, h)
        kh = jnp.concatenate([kv[:, h * MLA_NOPE:(h + 1) * MLA_NOPE], krr], axis=1)
        finish(kh, gk_ref[:, base:base + MLA_HEAD_PAD], k_ref, h)


def _proj_a(h, mods, ctx_row, tm, wts, cs):
    b, t, d = h.shape
    qw = MLA_HEADS * MLA_HEAD_PAD
    grid = (b, t // tm)
    tile = lambda w: pl.BlockSpec((None, tm, w), lambda bi, i: (bi, i, 0))
    in_specs = ([tile(d)] + _mod_specs(d, (0, 1), ctx_row)
                + [_const_spec(wts[k].shape) for k in
                   ("gn", "w1", "gqa", "wq", "gkv", "wkv", "gq", "gk")]
                + [pl.BlockSpec((tm, LANES), lambda bi, i: (i, 0)), _const_spec(wts["msk"].shape)])
    return pl.pallas_call(
        _proj_a_kernel,
        grid=grid,
        in_specs=in_specs,
        out_specs=[tile(512), tile(qw), tile(qw), tile(MLA_HEADS * MLA_V)],
        out_shape=[jax.ShapeDtypeStruct((b, t, 512), BF16),
                   jax.ShapeDtypeStruct((b, t, qw), BF16),
                   jax.ShapeDtypeStruct((b, t, qw), BF16),
                   jax.ShapeDtypeStruct((b, t, MLA_HEADS * MLA_V), BF16)],
        compiler_params=_params("parallel", "parallel"),
        name="proj_a",
    )(h, mods, mods, wts["gn"], wts["w1"], wts["gqa"], wts["wq"], wts["gkv"], wts["wkv"],
      wts["gq"], wts["gk"], cs, wts["msk"])


def _prep_proj_a(norm_mix, a_w_in, qa_norm, w_q_b, kva_norm, w_kv_b, q_norm, k_norm):
    perm, sign = _rot_perm_sign(MLA_ROPE)
    s5w = a_w_in.shape[1] - (MLA_Q_RANK + MLA_KV_RANK + MLA_ROPE)
    assert s5w == 512
    kr = a_w_in[:, -MLA_ROPE:]
    w1 = jnp.concatenate([a_w_in, kr[:, perm] * sign], axis=1).astype(BF16)
    wq = w_q_b.reshape(MLA_Q_RANK, MLA_HEADS, MLA_QK_DIM)
    rope = wq[:, :, MLA_NOPE:]
    wq = jnp.concatenate([wq, rope[:, :, perm] * sign], axis=2)
    wq = wq.reshape(MLA_Q_RANK, MLA_HEADS * MLA_HEAD_PAD).astype(BF16)
    wkv = w_kv_b.reshape(MLA_KV_RANK, MLA_HEADS, MLA_NOPE + MLA_V)
    wkv = jnp.concatenate([wkv[:, :, :MLA_NOPE].reshape(MLA_KV_RANK, -1),
                           wkv[:, :, MLA_NOPE:].reshape(MLA_KV_RANK, -1)], axis=1).astype(BF16)

    def head_gain(g, scale):
        gb = jnp.concatenate([g, g[MLA_NOPE:][perm]]) * scale
        return jnp.tile(gb, MLA_HEADS)[None, :]

    rows = jnp.arange(MLA_HEAD_PAD)[:, None] < MLA_QK_DIM
    msk = jnp.broadcast_to(rows, (MLA_HEAD_PAD, MLA_HEAD_PAD)).astype(BF16)
    return dict(gn=norm_mix[None, :], w1=w1, gqa=qa_norm[None, :], wq=wq, gkv=kva_norm[None, :],
                wkv=wkv, gq=head_gain(q_norm, MLA_SCALE), gk=head_gain(k_norm, 1.0), msk=msk)


def _s5_kernel(x_ref, toep_ref, bst_ref, cst_ref, a_ref, d_ref, y_ref, z_ref, sin_ref,
               *, nb, n_ctx, n_all):
    x = x_ref[...]
    half = MXU_TILE
    y = jnp.concatenate([_dot(x[:, :half], toep_ref[0]), _dot(x[:, half:], toep_ref[1])], axis=1)
    z_ref[...] = _dot(x, bst_ref[...])
    a = a_ref[...]
    af_re, af_im, ab_re, ab_im = a[0:1], a[1:2], a[2:3], a[3:4]

    rows8 = SUBLANES
    t_ctx, t_all = n_ctx // rows8, n_all // rows8

    def body(it, carry):
        jt = jnp.where(it < t_ctx, t_ctx - 1 - it, t_all + t_ctx - 1 - it)
        new = []
        for bi in range(nb):
            sf_re, sf_im, sb_re, sb_im = carry[4 * bi:4 * bi + 4]
            rf = pl.multiple_of(bi * n_all + it * rows8, rows8)
            rb = pl.multiple_of(bi * n_all + jt * rows8, rows8)
            zf = z_ref[pl.ds(rf, rows8), 0:2 * LANES]
            zb = z_ref[pl.ds(rb, rows8), 2 * LANES:4 * LANES]
            f_re, f_im, b_re, b_im = [], [], [None] * rows8, [None] * rows8
            for r in range(rows8):
                f_re.append(sf_re)
                f_im.append(sf_im)
                sf_re, sf_im = (af_re * sf_re - af_im * sf_im + zf[r:r + 1, :LANES],
                                af_re * sf_im + af_im * sf_re + zf[r:r + 1, LANES:])
            for r in reversed(range(rows8)):
                b_re[r] = sb_re
                b_im[r] = sb_im
                sb_re, sb_im = (ab_re * sb_re - ab_im * sb_im + zb[r:r + 1, :LANES],
                                ab_re * sb_im + ab_im * sb_re + zb[r:r + 1, LANES:])
            sin_ref[pl.ds(rf, rows8), 0:LANES] = jnp.concatenate(f_re, axis=0)
            sin_ref[pl.ds(rf, rows8), LANES:2 * LANES] = jnp.concatenate(f_im, axis=0)
            sin_ref[pl.ds(rb, rows8), 2 * LANES:3 * LANES] = jnp.concatenate(b_re, axis=0)
            sin_ref[pl.ds(rb, rows8), 3 * LANES:4 * LANES] = jnp.concatenate(b_im, axis=0)
            new += [sf_re, sf_im, sb_re, sb_im]
        return tuple(new)

    zero = jnp.zeros((1, LANES), F32)
    lax.fori_loop(0, t_all, body, (zero,) * (4 * nb))
    y = y + _dot(sin_ref[...].astype(BF16), cst_ref[...]) + x.astype(F32) * d_ref[...]
    y_ref[...] = jax.nn.gelu(y).astype(BF16)


def _s5_tables(lam_re, lam_im, log_step, b_re, b_im, c_re, c_im, d_skip):
    hp = lax.Precision.HIGHEST
    L = S5_CHUNK
    _, G, P = lam_re.shape
    S = S5_GROUP_DIM
    step = jnp.exp(log_step.astype(F32))[..., None]
    ar, ai = lam_re.astype(F32) * step, lam_im.astype(F32) * step
    k = jnp.arange(L + 1, dtype=F32)[:, None, None, None]
    mag = jnp.exp(k * ar)
    pw_re, pw_im = mag * jnp.cos(k * ai), mag * jnp.sin(k * ai)
    n_re = jnp.expm1(ar) * jnp.cos(ai) - 2.0 * jnp.sin(0.5 * ai) ** 2
    n_im = jnp.exp(ar) * jnp.sin(ai)
    den = lam_re * lam_re + lam_im * lam_im
    co_re = (n_re * lam_re + n_im * lam_im) / den
    co_im = (n_im * lam_re - n_re * lam_im) / den
    bb_re = co_re[..., None] * b_re - co_im[..., None] * b_im
    bb_im = co_re[..., None] * b_im + co_im[..., None] * b_re
    cl_re = c_re[None] * pw_re[:, :, :, None, :] - c_im[None] * pw_im[:, :, :, None, :]
    cl_im = c_re[None] * pw_im[:, :, :, None, :] + c_im[None] * pw_re[:, :, :, None, :]
    kk = (jnp.einsum("kdgop,dgpi->kdgoi", cl_re, bb_re, precision=hp)
          - jnp.einsum("kdgop,dgpi->kdgoi", cl_im, bb_im, precision=hp))
    tau = jnp.arange(L)[:, None]
    tt = jnp.arange(L)[None, :]
    lag_f = jnp.clip(tt - tau, 0, L)
    lag_b = jnp.clip(tau - tt, 0, L)
    kf = jnp.where((tt >= tau)[:, :, None, None, None], kk[lag_f, 0], 0.0)
    kb = jnp.where((tau >= tt)[:, :, None, None, None], kk[lag_b, 1], 0.0)
    toep = jnp.transpose(kf + kb, (2, 0, 4, 1, 3)).reshape(G, L * S, L * S)
    toep = toep.reshape(G // 2, 2, L * S, L * S).astype(BF16)
    def drive(pw_r, pw_i, d):
        re = pw_r[..., None] * bb_re[d][None] - pw_i[..., None] * bb_im[d][None]
        im = pw_r[..., None] * bb_im[d][None] + pw_i[..., None] * bb_re[d][None]
        to_rows = lambda m: jnp.transpose(m, (1, 0, 3, 2)).reshape(G, L * S, P)
        return to_rows(re), to_rows(im)
    f_re, f_im = drive(pw_re[:L, 0][::-1], pw_im[:L, 0][::-1], 0)
    g_re, g_im = drive(pw_re[:L, 1], pw_im[:L, 1], 1)

    def pair_cols(m):
        m = m.reshape(G // 2, 2, m.shape[1], P)
        z = jnp.zeros_like(m[:, 0])
        top = jnp.concatenate([m[:, 0], z], axis=2)
        bot = jnp.concatenate([z, m[:, 1]], axis=2)
        return jnp.concatenate([top, bot], axis=1)
    bst = jnp.concatenate([pair_cols(f_re), pair_cols(f_im), pair_cols(g_re), pair_cols(g_im)],
                          axis=2).astype(BF16)
    def read(cre, cim):
        to_cols = lambda m: jnp.transpose(m, (1, 3, 0, 2)).reshape(G, P, L * S)
        return to_cols(cre), to_cols(-cim)
    rf_re, rf_im = read(cl_re[1:L + 1, 0], cl_im[1:L + 1, 0])
    rb_re, rb_im = read(cl_re[1:L + 1, 1][::-1], cl_im[1:L + 1, 1][::-1])

    def pair_rows(m):
        m = m.reshape(G // 2, 2, P, m.shape[2])
        z = jnp.zeros_like(m[:, 0])
        top = jnp.concatenate([m[:, 0], z], axis=2)
        bot = jnp.concatenate([z, m[:, 1]], axis=2)
        return jnp.concatenate([top, bot], axis=1)
    cst = jnp.concatenate([pair_rows(rf_re), pair_rows(rf_im), pair_rows(rb_re), pair_rows(rb_im)],
                          axis=1).astype(BF16)
    a_chunk = jnp.stack([pw_re[L, 0], pw_im[L, 0], pw_re[L, 1], pw_im[L, 1]], axis=0)
    a_chunk = jnp.transpose(a_chunk.reshape(4, G // 2, 2 * P), (1, 0, 2))
    d_pair = jnp.broadcast_to(d_skip.astype(F32).reshape(G // 2, 2, 1, S), (G // 2, 2, L, S))
    d_pair = d_pair.reshape(G // 2, 1, 2 * L * S)
    return toep, bst, cst, a_chunk, d_pair


def _s5(u_ctx, u_lat, tables):
    toep, bst, cst, a_chunk, d_pair = tables
    b, n_c, w = u_ctx.shape
    n_l = u_lat.shape[1]
    L, S = S5_CHUNK, S5_GROUP_DIM
    pairs = w // (2 * S)
    n_all = (n_c + n_l) // L
    n_ctx = n_c // L
    u = jnp.concatenate([u_ctx, u_lat], axis=1)
    x = u.reshape(b, n_all, L, pairs, 2, S)
    x = jnp.transpose(x, (3, 0, 1, 4, 2, 5)).reshape(pairs, b * n_all, 2 * L * S)
    rows = b * n_all
    pw = 2 * L * S
    y = pl.pallas_call(
        functools.partial(_s5_kernel, nb=b, n_ctx=n_ctx, n_all=n_all),
        grid=(pairs,),
        in_specs=[pl.BlockSpec((None, rows, pw), lambda g: (g, 0, 0)),
                  pl.BlockSpec((None, 2, MXU_TILE, MXU_TILE), lambda g: (g, 0, 0, 0)),
                  pl.BlockSpec((None, pw, pw), lambda g: (g, 0, 0)),
                  pl.BlockSpec((None, pw, pw), lambda g: (g, 0, 0)),
                  pl.BlockSpec((None, 4, LANES), lambda g: (g, 0, 0)),
                  pl.BlockSpec((None, 1, pw), lambda g: (g, 0, 0))],
        out_specs=pl.BlockSpec((None, rows, pw), lambda g: (g, 0, 0)),
        out_shape=jax.ShapeDtypeStruct((pairs, rows, pw), BF16),
        scratch_shapes=[pltpu.VMEM((rows, pw), F32), pltpu.VMEM((rows, pw), F32)],
        compiler_params=_params("parallel"),
        name="s5_scan",
    )(x, toep, bst, cst, a_chunk, d_pair)
    y = y.reshape(pairs, b, n_all, 2, L, S)
    y = jnp.transpose(y, (1, 2, 4, 0, 3, 5)).reshape(b, n_c + n_l, w)
    return y[:, :n_c], y[:, n_c:]


def _mla_kernel(*refs, tk, n_steps):
    if n_steps:
        q_ref, kc_ref, vc_ref, k_ref, v_ref, o_ref, m_ref, l_ref, acc_ref = refs
    else:
        q_ref, kc_ref, vc_ref, o_ref, m_ref, l_ref, acc_ref = refs
    q = q_ref[...]

    def step(k, v, first):
        s = _dot_nt(q, k)
        s_max = jnp.max(s, axis=1, keepdims=True)
        if first:
            m_new = s_max
            p = jnp.exp(s - m_new)
            l_ref[...] = jnp.sum(p, axis=1, keepdims=True)
            acc_ref[...] = _dot(p.astype(BF16), v)
        else:
            m_old = m_ref[...]
            m_new = jnp.maximum(m_old, s_max)
            alpha = jnp.exp(m_old - m_new)
            p = jnp.exp(s - m_new)
            l_ref[...] = alpha * l_ref[...] + jnp.sum(p, axis=1, keepdims=True)
            acc_ref[...] = alpha * acc_ref[...] + _dot(p.astype(BF16), v)
        m_ref[...] = m_new

    step(kc_ref[...], vc_ref[...], True)
    if n_steps:
        def body(j, _):
            off = pl.multiple_of(j * tk, tk)
            step(k_ref[pl.ds(off, tk), :], v_ref[pl.ds(off, tk), :], False)
            return 0
        lax.fori_loop(0, n_steps, body, 0)
    o_ref[...] = (acc_ref[...] / l_ref[...]).astype(BF16)


def _mla_attention(q, k_ctx, v_ctx, k_lat=None, v_lat=None, *, tq, tk=512):
    b, t, _ = q.shape
    n_c = k_ctx.shape[1]
    hp, hv = MLA_HEAD_PAD, MLA_V
    in_specs = [pl.BlockSpec((None, tq, hp), lambda bi, h, i: (bi, i, h)),
                pl.BlockSpec((None, n_c, hp), lambda bi, h, i: (bi, 0, h)),
                pl.BlockSpec((None, n_c, hv), lambda bi, h, i: (bi, 0, h))]
    args = [q, k_ctx, v_ctx]
    n_steps = 0
    if k_lat is not None:
        n_l = k_lat.shape[1]
        n_steps = n_l // tk
        in_specs += [pl.BlockSpec((None, n_l, hp), lambda bi, h, i: (bi, 0, h)),
                     pl.BlockSpec((None, n_l, hv), lambda bi, h, i: (bi, 0, h))]
        args += [k_lat, v_lat]
    return pl.pallas_call(
        functools.partial(_mla_kernel, tk=tk, n_steps=n_steps),
        grid=(b, MLA_HEADS, t // tq),
        in_specs=in_specs,
        out_specs=pl.BlockSpec((None, tq, hv), lambda bi, h, i: (bi, i, h)),
        out_shape=jax.ShapeDtypeStruct((b, t, MLA_HEADS * hv), BF16),
        scratch_shapes=[pltpu.VMEM((tq, 1), F32), pltpu.VMEM((tq, 1), F32), pltpu.VMEM((tq, hv), F32)],
        compiler_params=_params("parallel", "parallel", "arbitrary"),
        name="mla_attention",
    )(*args)


def _proj_c_kernel(h_ref, sh_ref, sc_ref, gn_ref, wc_ref, gqk_ref, cos_ref, sin_ref, bd_ref,
                   q_ref, k_ref, v_ref):
    a = _rms(h_ref[...], gn_ref[...]) * (1.0 + sc_ref[...]) + sh_ref[...]
    p = _dot(a.astype(BF16), wc_ref[...])
    qw = WIN_HEADS * WIN_HEAD_DIM
    kw = WIN_KV_HEADS * WIN_HEAD_DIM
    v_ref[...] = p[:, qw + kw:].astype(BF16)
    bd = bd_ref[...]
    cos, sin = cos_ref[...], sin_ref[...]
    lane = lax.broadcasted_iota(jnp.int32, cos.shape, 1)
    first_quarter = (lane % (WIN_HEAD_DIM // 2)) < (WIN_HEAD_DIM // 4)
    for j in range((qw + kw) // MXU_TILE):
        xh = p[:, j * MXU_TILE:(j + 1) * MXU_TILE]
        hi, lo = _split_bf16(xh * xh)
        ssq = _dot(hi, bd) + _dot(lo, bd)
        xn = xh * lax.rsqrt(ssq * (1.0 / WIN_HEAD_DIM) + NORM_EPS) * gqk_ref[:, j * MXU_TILE:(j + 1) * MXU_TILE]
        for c in range(MXU_TILE // LANES):
            xc = xn[:, c * LANES:(c + 1) * LANES]
            fwd = pltpu.roll(xc, WIN_HEAD_DIM // 4, axis=1)
            bwd = pltpu.roll(xc, LANES - WIN_HEAD_DIM // 4, axis=1)
            y = (xc * cos + jnp.where(first_quarter, -bwd, fwd) * sin).astype(BF16)
            col = j * MXU_TILE + c * LANES
            if col < qw:
                q_ref[:, col:col + LANES] = y
            else:
                k_ref[:, col - qw:col - qw + LANES] = y


def _proj_c(h, mods, ctx_row, tm, wts, cos2, sin2):
    b, t, d = h.shape
    qw = WIN_HEADS * WIN_HEAD_DIM
    kw = WIN_KV_HEADS * WIN_HEAD_DIM
    tile = lambda w: pl.BlockSpec((None, tm, w), lambda bi, i: (bi, i, 0))
    tab = pl.BlockSpec((tm, LANES), lambda bi, i: (i, 0))
    in_specs = ([tile(d)] + _mod_specs(d, (0, 1), ctx_row)
                + [_const_spec(wts[k].shape) for k in ("gn", "wc", "gqk")]
                + [tab, tab, _const_spec(wts["bd"].shape)])
    return pl.pallas_call(
        _proj_c_kernel,
        grid=(b, t // tm),
        in_specs=in_specs,
        out_specs=[tile(qw), tile(kw), tile(kw)],
        out_shape=[jax.ShapeDtypeStruct((b, t, qw), BF16),
                   jax.ShapeDtypeStruct((b, t, kw), BF16),
                   jax.ShapeDtypeStruct((b, t, kw), BF16)],
        compiler_params=_params("parallel", "parallel"),
        name="proj_c",
    )(h, mods, mods, wts["gn"], wts["wc"], wts["gqk"], cos2, sin2, wts["bd"])


def _prep_proj_c(norm_mix, c_w_in, q_norm, k_norm):
    gqk = jnp.concatenate([jnp.tile(q_norm * WIN_SCALE, WIN_HEADS), jnp.tile(k_norm, WIN_KV_HEADS)])
    idx = jnp.arange(MXU_TILE) // WIN_HEAD_DIM
    bd = (idx[:, None] == idx[None, :]).astype(BF16)
    return dict(gn=norm_mix[None, :], wc=c_w_in.astype(BF16), gqk=gqk[None, :], bd=bd)


def _win_kernel(sink_ref, q_ref, k_ref, v_ref, kc_ref, vc_ref, o_ref, *, tq, band, n_lat):
    i = pl.program_id(1)
    start = jnp.clip(i * tq - WINDOW, 0, n_lat - band)
    start = pl.multiple_of(start, WINDOW)
    shape = (WIN_GROUP * tq, band)
    q_pos = i * tq + lax.rem(lax.broadcasted_iota(jnp.int32, shape, 0), tq)
    k_pos = start + lax.broadcasted_iota(jnp.int32, shape, 1)
    valid = jnp.abs(k_pos - q_pos) <= WINDOW
    hd = WIN_HEAD_DIM
    outs = []
    for kv in range(WIN_KV_HEADS):
        qs = jnp.concatenate([q_ref[:, (kv * WIN_GROUP + g) * hd:(kv * WIN_GROUP + g + 1) * hd]
                              for g in range(WIN_GROUP)], axis=0)
        kb = k_ref[pl.ds(start, band), kv * hd:(kv + 1) * hd]
        vb = v_ref[pl.ds(start, band), kv * hd:(kv + 1) * hd]
        kc = kc_ref[:, kv * hd:(kv + 1) * hd]
        vc = vc_ref[:, kv * hd:(kv + 1) * hd]
        s_loc = jnp.where(valid, _dot_nt(qs, kb), NEG_BIG)
        s_ctx = _dot_nt(qs, kc)
        sink = jnp.concatenate([jnp.full((tq, 1), sink_ref[kv * WIN_GROUP + g], F32)
                                for g in range(WIN_GROUP)], axis=0)
        m = jnp.maximum(jnp.maximum(jnp.max(s_loc, axis=1, keepdims=True),
                                    jnp.max(s_ctx, axis=1, keepdims=True)), sink)
        p_loc = jnp.exp(s_loc - m)
        p_ctx = jnp.exp(s_ctx - m)
        den = (jnp.sum(p_loc, axis=1, keepdims=True) + jnp.sum(p_ctx, axis=1, keepdims=True)
               + jnp.exp(sink - m))
        o = (_dot(p_ctx.astype(BF16), vc) + _dot(p_loc.astype(BF16), vb)) / den
        outs += [o[g * tq:(g + 1) * tq] for g in range(WIN_GROUP)]
    o_ref[...] = jnp.concatenate(outs, axis=1).astype(BF16)


def _win_attention(q, k, v, k_ctx, v_ctx, sink, *, tq):
    b, n, qw = q.shape
    n_c = k_ctx.shape[1]
    kw = k.shape[2]
    band = tq + 2 * WINDOW
    full = lambda t, w: pl.BlockSpec((None, t, w), lambda bi, i: (bi, 0, 0))
    return pl.pallas_call(
        functools.partial(_win_kernel, tq=tq, band=band, n_lat=n),
        grid=(b, n // tq),
        in_specs=[pl.BlockSpec(memory_space=pltpu.SMEM),
                  pl.BlockSpec((None, tq, qw), lambda bi, i: (bi, i, 0)),
                  full(n, kw), full(n, kw), full(n_c, kw), full(n_c, kw)],
        out_specs=pl.BlockSpec((None, tq, qw), lambda bi, i: (bi, i, 0)),
        out_shape=jax.ShapeDtypeStruct((b, n, qw), BF16),
        compiler_params=_params("parallel", "arbitrary"),
        name="win_attention",
    )(sink.astype(F32), q, k, v, k_ctx, v_ctx)


def _post_kernel(*refs, s5_width, n_chunks):
    if s5_width:
        (h_ref, g_ref, sh_ref, sc_ref, g2_ref, gn_ref, yg_ref, o_ref, wglu_ref, bglu_ref, wo_ref,
         wg_ref, wu_ref, wd_ref, out_ref, a_ref, acc_ref) = refs
        yg = yg_ref[...]
        s5 = yg.astype(F32) * jax.nn.sigmoid(_dot(yg, wglu_ref[...]) + bglu_ref[...])
        mix = _dot(s5.astype(BF16), wo_ref[:s5_width, :]) + _dot(o_ref[...], wo_ref[s5_width:, :])
    else:
        (h_ref, g_ref, sh_ref, sc_ref, g2_ref, gn_ref, o_ref, wo_ref,
         wg_ref, wu_ref, wd_ref, out_ref, a_ref, acc_ref) = refs
        mix = _dot(o_ref[...], wo_ref[...])
    h1 = h_ref[...] + g_ref[...] * mix
    a_ref[...] = (_rms(h1, gn_ref[...]) * (1.0 + sc_ref[...]) + sh_ref[...]).astype(BF16)
    acc_ref[...] = jnp.zeros_like(acc_ref)

    def body(c, _):
        a = a_ref[...]
        act = _silu(_dot(a, wg_ref[c])) * _dot(a, wu_ref[c])
        acc_ref[...] += _dot(act.astype(BF16), wd_ref[c])
        return 0

    lax.fori_loop(0, n_chunks, body, 0)
    out_ref[...] = h1 + g2_ref[...] * acc_ref[...]


def _post(h, mods, ctx_row, tm, wts, o, yg=None):
    b, t, d = h.shape
    tile = lambda w: pl.BlockSpec((None, tm, w), lambda bi, i: (bi, i, 0))
    s5_width = 0 if yg is None else yg.shape[2]
    in_specs = [tile(d)] + _mod_specs(d, (2, 3, 4, 5), ctx_row) + [_const_spec(wts["gn"].shape)]
    args = [h, mods, mods, mods, mods, wts["gn"]]
    if yg is not None:
        in_specs += [tile(s5_width), tile(o.shape[2]), _const_spec(wts["wglu"].shape),
                     _const_spec(wts["bglu"].shape)]
        args += [yg, o, wts["wglu"], wts["bglu"]]
    else:
        in_specs += [tile(o.shape[2])]
        args += [o]
    for k in ("wo", "wg", "wu", "wd"):
        in_specs.append(_const_spec(wts[k].shape))
        args.append(wts[k])
    return pl.pallas_call(
        functools.partial(_post_kernel, s5_width=s5_width, n_chunks=wts["wg"].shape[0]),
        grid=(b, t // tm),
        in_specs=in_specs,
        out_specs=tile(d),
        out_shape=jax.ShapeDtypeStruct((b, t, d), F32),
        scratch_shapes=[pltpu.VMEM((tm, d), BF16), pltpu.VMEM((tm, d), F32)],
        compiler_params=_params("parallel", "parallel"),
        name="post_ffn",
    )(*args)


def _prep_post(norm_ffn, w_gate, w_up, w_down, w_out, w_glu=None, b_glu=None):
    d, hidden = w_gate.shape
    n_chunks = hidden // MXU_TILE
    chunk_cols = lambda w: jnp.transpose(w.reshape(d, n_chunks, MXU_TILE), (1, 0, 2)).astype(BF16)
    wts = dict(gn=norm_ffn[None, :], wo=w_out.astype(BF16), wg=chunk_cols(w_gate), wu=chunk_cols(w_up),
               wd=w_down.reshape(n_chunks, MXU_TILE, d).astype(BF16))
    if w_glu is not None:
        wts.update(wglu=w_glu.astype(BF16), bglu=b_glu[None, :])
    return wts


def kernel(x, c, ctx, c_ctx, ada_w, ada_b, norm_mix, norm_ffn, ffn_w_gate, ffn_w_up, ffn_w_down,
           a_w_in, a_w_out, s5_lam_re, s5_lam_im, s5_log_step, s5_b_re, s5_b_im, s5_c_re, s5_c_im,
           s5_d, s5_w_glu, s5_b_glu, mla_qa_norm, mla_w_q_b, mla_kva_norm, mla_w_kv_b,
           mla_q_norm, mla_k_norm, c_w_in, c_w_out, c_q_norm, c_k_norm, c_sink):
    b, n, d = x.shape
    n_c = ctx.shape[1]
    depth = ada_w.shape[0]
    assert b + 1 <= MOD_ROWS and n % 512 == 0 and n_c % S5_CHUNK == 0
    rows = n // GRID_W
    tm_lat, tm_ctx = 512, n_c

    cond = jnp.zeros((MOD_ROWS, d), F32).at[:b].set(c).at[b].set(c_ctx)
    mods = _ada_modulation(cond, ada_w, ada_b)
    mods = mods.reshape(depth, MOD_ROWS, N_MOD, 1, d)

    cos_a, sin_a = _grid_rope_tables(rows, MLA_ROPE)
    cs_a_lat = jnp.concatenate([cos_a, sin_a], axis=1)
    cs_a_ctx = jnp.concatenate([jnp.ones((n_c, MLA_ROPE), F32), jnp.zeros((n_c, MLA_ROPE), F32)], axis=1)
    cos_c, sin_c = _grid_rope_tables(rows, WIN_HEAD_DIM)
    cos_c2, sin_c2 = jnp.tile(cos_c, (1, 2)), jnp.tile(sin_c, (1, 2))
    one_c, zero_c = jnp.ones((n_c, LANES), F32), jnp.zeros((n_c, LANES), F32)

    h_ctx, h_lat = ctx, x
    for i in range(depth):
        need_ctx = i < depth - 1
        j = i // 2
        m_i = mods[i]
        if i % 2 == 0:
            pw = _prep_proj_a(norm_mix[i], a_w_in[j], mla_qa_norm[j], mla_w_q_b[j], mla_kva_norm[j],
                              mla_w_kv_b[j], mla_q_norm[j], mla_k_norm[j])
            u_l, q_l, k_l, v_l = _proj_a(h_lat, m_i, None, tm_lat, pw, cs_a_lat)
            u_c, q_c, k_c, v_c = _proj_a(h_ctx, m_i, b, tm_ctx, pw, cs_a_ctx)
            tables = _s5_tables(s5_lam_re[j], s5_lam_im[j], s5_log_step[j], s5_b_re[j], s5_b_im[j],
                                s5_c_re[j], s5_c_im[j], s5_d[j])
            yg_c, yg_l = _s5(u_c, u_l, tables)
            o_l = _mla_attention(q_l, k_c, v_c, k_l, v_l, tq=512)
            post_w = _prep_post(norm_ffn[i], ffn_w_gate[i], ffn_w_up[i], ffn_w_down[i], a_w_out[j],
                                s5_w_glu[j], s5_b_glu[j])
            h_lat_new = _post(h_lat, m_i, None, tm_lat, post_w, o_l, yg_l)
            if need_ctx:
                o_c = _mla_attention(q_c, k_c, v_c, tq=n_c)
                h_ctx = _post(h_ctx, m_i, b, tm_ctx, post_w, o_c, yg_c)
            h_lat = h_lat_new
        else:
            pw = _prep_proj_c(norm_mix[i], c_w_in[j], c_q_norm[j], c_k_norm[j])
            q_l, k_l, v_l = _proj_c(h_lat, m_i, None, tm_lat, pw, cos_c2, sin_c2)
            q_c, k_c, v_c = _proj_c(h_ctx, m_i, b, tm_ctx, pw, one_c, zero_c)
            o_l = _win_attention(q_l, k_l, v_l, k_c, v_c, c_sink[j], tq=256)
            post_w = _prep_post(norm_ffn[i], ffn_w_gate[i], ffn_w_up[i], ffn_w_down[i], c_w_out[j])
            h_lat_new = _post(h_lat, m_i, None, tm_lat, post_w, o_l)
            if need_ctx:
                raise NotImplementedError("context queries of a windowed layer")
            h_lat = h_lat_new
    return h_lat
```

```python
import functools
import math

import jax
import jax.numpy as jnp
from jax import lax
from jax.experimental import pallas as pl
from jax.experimental.pallas import tpu as pltpu

F32 = jnp.float32
BF16 = jnp.bfloat16

GRID_W = 64
NORM_EPS = 1e-6
ROPE_THETA = 10000.0
N_MOD = 6
S5_GROUP_DIM = 16
S5_STATE = 64
S5_CHUNK = 16
MLA_HEADS = 4
MLA_NOPE = 128
MLA_ROPE = 64
MLA_QK_DIM = MLA_NOPE + MLA_ROPE
MLA_V = 128
MLA_Q_RANK = 384
MLA_KV_RANK = 256
MLA_SCALE = MLA_QK_DIM ** -0.5
MLA_HEAD_PAD = 256
WIN_HEADS = 16
WIN_KV_HEADS = 4
WIN_GROUP = WIN_HEADS // WIN_KV_HEADS
WIN_HEAD_DIM = 64
WINDOW = 128
WIN_SCALE = WIN_HEAD_DIM ** -0.5
LANES = 128
SUBLANES = 8
MXU_TILE = 256
VMEM_LIMIT_BYTES = 48 * 1024 * 1024
NEG_BIG = -1e30
LOG2_E = math.log2(math.e)
S5_BLOCK_PAIRS = 4
MOD_ROWS = 8


def _dot(a, b):
    return jnp.dot(a, b, preferred_element_type=F32)


def _dot_nt(a, b):
    return lax.dot_general(a, b, (((1,), (1,)), ((), ())), preferred_element_type=F32)


def _split_bf16(x):
    hi = x.astype(BF16)
    lo = (x - hi.astype(F32)).astype(BF16)
    return hi, lo


def _rms(x, gain):
    return x * lax.rsqrt(jnp.mean(x * x, axis=-1, keepdims=True) + NORM_EPS) * gain


def _silu(x):
    return x * jax.nn.sigmoid(x)


def _params(*sem):
    return pltpu.CompilerParams(dimension_semantics=sem, vmem_limit_bytes=VMEM_LIMIT_BYTES)


def _const_spec(shape):
    nd = len(shape)
    return pl.BlockSpec(shape, lambda *_: (0,) * nd, pipeline_mode=pl.Buffered(1))


def _ada_kernel(cond_ref, w_ref, b_ref, o_ref):
    s = _silu(cond_ref[...])
    s_hi, s_lo = _split_bf16(s)
    w_hi, w_lo = _split_bf16(w_ref[...])
    o_ref[...] = _dot(s_hi, w_hi) + _dot(s_lo, w_hi) + _dot(s_hi, w_lo) + b_ref[...]


def _ada_modulation(cond, ada_w, ada_b):
    depth, d, n = ada_w.shape
    tn = 1024
    return pl.pallas_call(
        _ada_kernel,
        grid=(depth, n // tn),
        in_specs=[pl.BlockSpec((MOD_ROWS, d), lambda i, j: (0, 0)),
                  pl.BlockSpec((None, d, tn), lambda i, j: (i, 0, j)),
                  pl.BlockSpec((None, 1, tn), lambda i, j: (i, 0, j))],
        out_specs=pl.BlockSpec((None, MOD_ROWS, tn), lambda i, j: (i, 0, j)),
        out_shape=jax.ShapeDtypeStruct((depth, MOD_ROWS, n), F32),
        compiler_params=_params("arbitrary", "arbitrary"),
        name="ada_modulation",
    )(cond, ada_w, ada_b.reshape(depth, 1, n))


def _mod_specs(d, slots, ctx_row):
    def make(slot):
        if ctx_row is None:
            return pl.BlockSpec((None, None, 1, d), lambda b, i: (b, slot, 0, 0))
        return pl.BlockSpec((None, None, 1, d), lambda b, i: (ctx_row, slot, 0, 0))
    return [make(s) for s in slots]


def _grid_rope_tables(rows, rot_dim):
    row = jnp.repeat(jnp.arange(rows, dtype=jnp.int32), GRID_W)
    col = jnp.tile(jnp.arange(GRID_W, dtype=jnp.int32), rows)
    n_freq = rot_dim // 4
    inv_freq = ROPE_THETA ** (-jnp.arange(n_freq, dtype=F32) / n_freq)
    ang_r = row.astype(F32)[:, None] * inv_freq
    ang_c = col.astype(F32)[:, None] * inv_freq
    ang = jnp.concatenate([ang_r, ang_r, ang_c, ang_c], axis=-1)
    return jnp.cos(ang), jnp.sin(ang)


def _rot_perm_sign(rot_dim):
    q = rot_dim // 4
    idx = jnp.arange(rot_dim)
    perm = jnp.where((idx // q) % 2 == 0, idx + q, idx - q)
    sign = jnp.where((idx // q) % 2 == 0, -1.0, 1.0).astype(F32)
    return perm, sign


def _proj_a_kernel(h_ref, sh_ref, sc_ref, gn_ref, w1_ref, gqa_ref, wq_ref, gkv_ref, wkv_ref, wvt_ref,
                   gq_ref, gk_ref, cs_ref, msk_ref, u_ref, q_ref, k_ref, vt_ref):
    a = _rms(h_ref[...], gn_ref[...]) * (1.0 + sc_ref[...]) + sh_ref[...]
    p1 = _dot(a.astype(BF16), w1_ref[...])
    u_ref[...] = p1[:, :512]
    cq = p1[:, 512:512 + MLA_Q_RANK]
    ckv = p1[:, 896:896 + MLA_KV_RANK]
    krr = p1[:, 1152:1280]
    qb = _dot(_rms(cq, gqa_ref[...]).astype(BF16), wq_ref[...])
    ckv_n = _rms(ckv, gkv_ref[...]).astype(BF16)
    kv = _dot(ckv_n, wkv_ref[...])
    vt_ref[...] = _dot_nt(wvt_ref[...], ckv_n).astype(BF16)
    cs = cs_ref[...]
    msk = msk_ref[...]
    low_half = lax.broadcasted_iota(jnp.int32, cs.shape, 1) < MLA_ROPE

    def finish(xh, gain, out_ref, h):
        hi, lo = _split_bf16(xh * xh)
        ssq = _dot(hi, msk) + _dot(lo, msk)
        xn = xh * lax.rsqrt(ssq * (1.0 / MLA_QK_DIM) + NORM_EPS) * gain
        rr = xn[:, LANES:] * cs
        rot = rr + pltpu.roll(rr, MLA_ROPE, axis=1)
        base = h * MLA_HEAD_PAD
        out_ref[:, base:base + LANES] = xn[:, :LANES].astype(BF16)
        out_ref[:, base + LANES:base + 2 * LANES] = jnp.where(low_half, rot, 0.0).astype(BF16)

    for h in range(MLA_HEADS):
        base = h * MLA_HEAD_PAD
        finish(qb[:, base:base + MLA_HEAD_PAD], gq_ref[:, base:base + MLA_HEAD_PAD], q_ref, h)
        kh = jnp.concatenate([kv[:, h * MLA_NOPE:(h + 1) * MLA_NOPE], krr], axis=1)
        finish(kh, gk_ref[:, base:base + MLA_HEAD_PAD], k_ref, h)


def _proj_a(h, mods, ctx_row, tm, wts, cs):
    b, t, d = h.shape
    qw = MLA_HEADS * MLA_HEAD_PAD
    grid = (b, t // tm)
    tile = lambda w: pl.BlockSpec((None, tm, w), lambda bi, i: (bi, i, 0))
    in_specs = ([tile(d)] + _mod_specs(d, (0, 1), ctx_row)
                + [_const_spec(wts[k].shape) for k in
                   ("gn", "w1", "gqa", "wq", "gkv", "wkv", "wvt", "gq", "gk")]
                + [pl.BlockSpec((tm, LANES), lambda bi, i: (i, 0)), _const_spec(wts["msk"].shape)])
    vw = MLA_HEADS * MLA_V
    return pl.pallas_call(
        _proj_a_kernel,
        grid=grid,
        in_specs=in_specs,
        out_specs=[tile(512), tile(qw), tile(qw), pl.BlockSpec((None, vw, tm), lambda bi, i: (bi, 0, i))],
        out_shape=[jax.ShapeDtypeStruct((b, t, 512), F32),
                   jax.ShapeDtypeStruct((b, t, qw), BF16),
                   jax.ShapeDtypeStruct((b, t, qw), BF16),
                   jax.ShapeDtypeStruct((b, vw, t), BF16)],
        compiler_params=_params("parallel", "parallel"),
        name="proj_a",
    )(h, mods, mods, wts["gn"], wts["w1"], wts["gqa"], wts["wq"], wts["gkv"], wts["wkv"], wts["wvt"],
      wts["gq"], wts["gk"], cs, wts["msk"])


def _prep_proj_a(norm_mix, a_w_in, qa_norm, w_q_b, kva_norm, w_kv_b, q_norm, k_norm):
    perm, sign = _rot_perm_sign(MLA_ROPE)
    s5w = a_w_in.shape[1] - (MLA_Q_RANK + MLA_KV_RANK + MLA_ROPE)
    assert s5w == 512
    kr = a_w_in[:, -MLA_ROPE:]
    w1 = jnp.concatenate([a_w_in, kr[:, perm] * sign], axis=1).astype(BF16)
    wq = w_q_b.reshape(MLA_Q_RANK, MLA_HEADS, MLA_QK_DIM)
    rope = wq[:, :, MLA_NOPE:]
    wq = jnp.concatenate([wq, rope[:, :, perm] * sign], axis=2)
    wq = wq.reshape(MLA_Q_RANK, MLA_HEADS * MLA_HEAD_PAD).astype(BF16)
    wkv3 = w_kv_b.reshape(MLA_KV_RANK, MLA_HEADS, MLA_NOPE + MLA_V)
    wkv = wkv3[:, :, :MLA_NOPE].reshape(MLA_KV_RANK, -1).astype(BF16)
    wvt = wkv3[:, :, MLA_NOPE:].reshape(MLA_KV_RANK, -1).T.astype(BF16)

    def head_gain(g, scale):
        gb = jnp.concatenate([g, g[MLA_NOPE:][perm]]) * scale
        return jnp.tile(gb, MLA_HEADS)[None, :]

    rows = jnp.arange(MLA_HEAD_PAD)[:, None] < MLA_QK_DIM
    msk = jnp.broadcast_to(rows, (MLA_HEAD_PAD, MLA_HEAD_PAD)).astype(BF16)
    return dict(gn=norm_mix[None, :], w1=w1, gqa=qa_norm[None, :], wq=wq, gkv=kva_norm[None, :],
                wkv=wkv, wvt=wvt, gq=head_gain(q_norm, MLA_SCALE * LOG2_E), gk=head_gain(k_norm, 1.0),
                msk=msk)


def _s5_kernel(uc_ref, ul_ref, toep_ref, bst_ref, cst_ref, a_ref, d_ref, yc_ref, yl_ref,
               slab_ref, x_ref, z_ref, *, n_ctx, n_all):
    L, S = S5_CHUNK, S5_GROUP_DIM
    n_lat = n_all - n_ctx
    npair = S5_BLOCK_PAIRS
    lane = lax.broadcasted_iota(jnp.int32, (n_all, LANES), 1)
    for tl in range(L):
        slab_ref[tl, 0:n_ctx, :] = uc_ref[pl.ds(tl, n_ctx, stride=L), :]
        slab_ref[tl, n_ctx:n_all, :] = ul_ref[pl.ds(tl, n_lat, stride=L), :]
    for pp in range(npair):
        for gl in range(2):
            src_off = (2 * pp + gl) * S
            for hh in range(2):
                acc = None
                for k in range(SUBLANES):
                    src = slab_ref[SUBLANES * hh + k]
                    sh = (S * k - src_off) % LANES
                    r = pltpu.roll(src, sh, axis=1) if sh else src
                    acc = r if acc is None else jnp.where((lane >= S * k) & (lane < S * (k + 1)), r, acc)
                col = gl * MXU_TILE + hh * LANES
                x_ref[pp, :, col:col + LANES] = acc
    ys = []
    for pp in range(npair):
        x = x_ref[pp].astype(BF16)
        ys.append(jnp.concatenate([_dot(x[:, :MXU_TILE], toep_ref[pp, 0]),
                                   _dot(x[:, MXU_TILE:], toep_ref[pp, 1])], axis=1))
        z_ref[pp] = _dot(x, bst_ref[pp])
    t_ctx, t_all = n_ctx // SUBLANES, n_all // SUBLANES

    def body(it, carry):
        jt = jnp.where(it < t_ctx, t_ctx - 1 - it, t_all + t_ctx - 1 - it)
        rf = pl.multiple_of(it * SUBLANES, SUBLANES)
        rb = pl.multiple_of(jt * SUBLANES, SUBLANES)
        new = []
        for pp in range(npair):
            a = a_ref[pp]
            af_re, af_im, ab_re, ab_im = a[0:1], a[1:2], a[2:3], a[3:4]
            sf_re, sf_im, sb_re, sb_im = carry[4 * pp:4 * pp + 4]
            zf = z_ref[pp, pl.ds(rf, SUBLANES), 0:2 * LANES]
            zb = z_ref[pp, pl.ds(rb, SUBLANES), 2 * LANES:4 * LANES]
            f_re, f_im, b_re, b_im = [], [], [None] * SUBLANES, [None] * SUBLANES
            for r in range(SUBLANES):
                f_re.append(sf_re)
                f_im.append(sf_im)
                sf_re, sf_im = (af_re * sf_re - af_im * sf_im + zf[r:r + 1, :LANES],
                                af_re * sf_im + af_im * sf_re + zf[r:r + 1, LANES:])
            for r in reversed(range(SUBLANES)):
                b_re[r] = sb_re
                b_im[r] = sb_im
                sb_re, sb_im = (ab_re * sb_re - ab_im * sb_im + zb[r:r + 1, :LANES],
                                ab_re * sb_im + ab_im * sb_re + zb[r:r + 1, LANES:])
            slab_ref[4 * pp + 0, pl.ds(rf, SUBLANES), :] = jnp.concatenate(f_re, axis=0)
            slab_ref[4 * pp + 1, pl.ds(rf, SUBLANES), :] = jnp.concatenate(f_im, axis=0)
            slab_ref[4 * pp + 2, pl.ds(rb, SUBLANES), :] = jnp.concatenate(b_re, axis=0)
            slab_ref[4 * pp + 3, pl.ds(rb, SUBLANES), :] = jnp.concatenate(b_im, axis=0)
            new += [sf_re, sf_im, sb_re, sb_im]
        return tuple(new)

    zero = jnp.zeros((1, LANES), F32)
    lax.fori_loop(0, t_all, body, (zero,) * (4 * npair))
    for pp in range(npair):
        s_in = jnp.concatenate([slab_ref[4 * pp + j] for j in range(4)], axis=1).astype(BF16)
        y = ys[pp] + _dot(s_in, cst_ref[pp]) + x_ref[pp] * d_ref[pp]
        z_ref[pp] = jax.nn.gelu(y)
    for tl in range(L):
        k, hh = tl % SUBLANES, tl // SUBLANES
        acc = None
        for g8 in range(2 * npair):
            col = (g8 % 2) * MXU_TILE + hh * LANES
            src = z_ref[g8 // 2, :, col:col + LANES]
            sh = (S * g8 - S * k) % LANES
            r = pltpu.roll(src, sh, axis=1) if sh else src
            acc = r if acc is None else jnp.where((lane >= S * g8) & (lane < S * (g8 + 1)), r, acc)
        yc_ref[pl.ds(tl, n_ctx, stride=L), :] = acc[:n_ctx]
        yl_ref[pl.ds(tl, n_lat, stride=L), :] = acc[n_ctx:]


def _s5_tables(lam_re, lam_im, log_step, b_re, b_im, c_re, c_im, d_skip):
    hp = lax.Precision.HIGHEST
    L = S5_CHUNK
    _, G, P = lam_re.shape
    S = S5_GROUP_DIM
    step = jnp.exp(log_step.astype(F32))[..., None]
    ar, ai = lam_re.astype(F32) * step, lam_im.astype(F32) * step
    k = jnp.arange(L + 1, dtype=F32)[:, None, None, None]
    mag = jnp.exp(k * ar)
    pw_re, pw_im = mag * jnp.cos(k * ai), mag * jnp.sin(k * ai)
    n_re = jnp.expm1(ar) * jnp.cos(ai) - 2.0 * jnp.sin(0.5 * ai) ** 2
    n_im = jnp.exp(ar) * jnp.sin(ai)
    den = lam_re * lam_re + lam_im * lam_im
    co_re = (n_re * lam_re + n_im * lam_im) / den
    co_im = (n_im * lam_re - n_re * lam_im) / den
    bb_re = co_re[..., None] * b_re - co_im[..., None] * b_im
    bb_im = co_re[..., None] * b_im + co_im[..., None] * b_re
    cl_re = c_re[None] * pw_re[:, :, :, None, :] - c_im[None] * pw_im[:, :, :, None, :]
    cl_im = c_re[None] * pw_im[:, :, :, None, :] + c_im[None] * pw_re[:, :, :, None, :]
    kk = (jnp.einsum("kdgop,dgpi->kdgoi", cl_re, bb_re, precision=hp)
          - jnp.einsum("kdgop,dgpi->kdgoi", cl_im, bb_im, precision=hp))
    tau = jnp.arange(L)[:, None]
    tt = jnp.arange(L)[None, :]
    lag_f = jnp.clip(tt - tau, 0, L)
    lag_b = jnp.clip(tau - tt, 0, L)
    kf = jnp.where((tt >= tau)[:, :, None, None, None], kk[lag_f, 0], 0.0)
    kb = jnp.where((tau >= tt)[:, :, None, None, None], kk[lag_b, 1], 0.0)
    toep = jnp.transpose(kf + kb, (2, 0, 4, 1, 3)).reshape(G, L * S, L * S)
    toep = toep.reshape(G // 2, 2, L * S, L * S).astype(BF16)
    def drive(pw_r, pw_i, d):
        re = pw_r[..., None] * bb_re[d][None] - pw_i[..., None] * bb_im[d][None]
        im = pw_r[..., None] * bb_im[d][None] + pw_i[..., None] * bb_re[d][None]
        to_rows = lambda m: jnp.transpose(m, (1, 0, 3, 2)).reshape(G, L * S, P)
        return to_rows(re), to_rows(im)
    f_re, f_im = drive(pw_re[:L, 0][::-1], pw_im[:L, 0][::-1], 0)
    g_re, g_im = drive(pw_re[:L, 1], pw_im[:L, 1], 1)

    def pair_cols(m):
        m = m.reshape(G // 2, 2, m.shape[1], P)
        z = jnp.zeros_like(m[:, 0])
        top = jnp.concatenate([m[:, 0], z], axis=2)
        bot = jnp.concatenate([z, m[:, 1]], axis=2)
        return jnp.concatenate([top, bot], axis=1)
    bst = jnp.concatenate([pair_cols(f_re), pair_cols(f_im), pair_cols(g_re), pair_cols(g_im)],
                          axis=2).astype(BF16)
    def read(cre, cim):
        to_cols = lambda m: jnp.transpose(m, (1, 3, 0, 2)).reshape(G, P, L * S)
        return to_cols(cre), to_cols(-cim)
    rf_re, rf_im = read(cl_re[1:L + 1, 0], cl_im[1:L + 1, 0])
    rb_re, rb_im = read(cl_re[1:L + 1, 1][::-1], cl_im[1:L + 1, 1][::-1])

    def pair_rows(m):
        m = m.reshape(G // 2, 2, P, m.shape[2])
        z = jnp.zeros_like(m[:, 0])
        top = jnp.concatenate([m[:, 0], z], axis=2)
        bot = jnp.concatenate([z, m[:, 1]], axis=2)
        return jnp.concatenate([top, bot], axis=1)
    cst = jnp.concatenate([pair_rows(rf_re), pair_rows(rf_im), pair_rows(rb_re), pair_rows(rb_im)],
                          axis=1).astype(BF16)
    a_chunk = jnp.stack([pw_re[L, 0], pw_im[L, 0], pw_re[L, 1], pw_im[L, 1]], axis=0)
    a_chunk = jnp.transpose(a_chunk.reshape(4, G // 2, 2 * P), (1, 0, 2))
    d_pair = jnp.broadcast_to(d_skip.astype(F32).reshape(G // 2, 2, 1, S), (G // 2, 2, L, S))
    d_pair = d_pair.reshape(G // 2, 1, 2 * L * S)
    return toep, bst, cst, a_chunk, d_pair


def _s5(u_ctx, u_lat, tables):
    toep, bst, cst, a_chunk, d_pair = tables
    b, n_c, w = u_ctx.shape
    n_l = u_lat.shape[1]
    L = S5_CHUNK
    n_ctx, n_all = n_c // L, (n_c + n_l) // L
    nblk = w // LANES
    pw = 2 * L * S5_GROUP_DIM
    npair = S5_BLOCK_PAIRS
    wspec = lambda shape: pl.BlockSpec((npair,) + shape, lambda g, bi: (g,) + (0,) * len(shape))
    return pl.pallas_call(
        functools.partial(_s5_kernel, n_ctx=n_ctx, n_all=n_all),
        grid=(nblk, b),
        in_specs=[pl.BlockSpec((None, n_c, LANES), lambda g, bi: (bi, 0, g)),
                  pl.BlockSpec((None, n_l, LANES), lambda g, bi: (bi, 0, g)),
                  wspec((2, MXU_TILE, MXU_TILE)), wspec((pw, pw)), wspec((pw, pw)),
                  wspec((4, LANES)), wspec((1, pw))],
        out_specs=[pl.BlockSpec((None, n_c, LANES), lambda g, bi: (bi, 0, g)),
                   pl.BlockSpec((None, n_l, LANES), lambda g, bi: (bi, 0, g))],
        out_shape=[jax.ShapeDtypeStruct((b, n_c, w), F32), jax.ShapeDtypeStruct((b, n_l, w), F32)],
        scratch_shapes=[pltpu.VMEM((L, n_all, LANES), F32), pltpu.VMEM((npair, n_all, pw), F32),
                        pltpu.VMEM((npair, n_all, pw), F32)],
        compiler_params=_params("parallel", "arbitrary"),
        name="s5_scan",
    )(u_ctx, u_lat, toep, bst, cst, a_chunk, d_pair)


def _mla_kernel(*refs, tk, n_steps):
    if n_steps:
        q_ref, kc_ref, vc_ref, k_ref, v_ref, o_ref, qt_ref, m_ref, l_ref, acc_ref = refs
    else:
        q_ref, kc_ref, vc_ref, o_ref, qt_ref, m_ref, l_ref, acc_ref = refs
    qt_ref[...] = q_ref[...].astype(F32).T.astype(BF16)

    def step(k, vt, first):
        st = _dot(k, qt_ref[...])
        s_max = jnp.max(st, axis=0, keepdims=True)
        if first:
            m_new = s_max
            p = jnp.exp2(st - m_new)
            l_ref[...] = jnp.sum(p, axis=0, keepdims=True)
            acc_ref[...] = _dot(vt, p.astype(BF16))
        else:
            m_old = m_ref[...]
            m_new = jnp.maximum(m_old, s_max)
            alpha = jnp.exp2(m_old - m_new)
            p = jnp.exp2(st - m_new)
            l_ref[...] = alpha * l_ref[...] + jnp.sum(p, axis=0, keepdims=True)
            acc_ref[...] = alpha * acc_ref[...] + _dot(vt, p.astype(BF16))
        m_ref[...] = m_new

    step(kc_ref[...], vc_ref[...], True)
    if n_steps:
        def body(j, _):
            off = pl.multiple_of(j * tk, tk)
            step(k_ref[pl.ds(off, tk), :], v_ref[:, pl.ds(off, tk)], False)
            return 0
        lax.fori_loop(0, n_steps, body, 0)
    o_ref[...] = (acc_ref[...] / l_ref[...]).T.astype(BF16)


def _mla_attention(q, k_ctx, vt_ctx, k_lat=None, vt_lat=None, *, tq, tk=1024):
    b, t, _ = q.shape
    n_c = k_ctx.shape[1]
    in_specs = [pl.BlockSpec((None, tq, MLA_HEAD_PAD), lambda bi, h, i: (bi, i, h)),
                pl.BlockSpec((None, n_c, MLA_HEAD_PAD), lambda bi, h, i: (bi, 0, h)),
                pl.BlockSpec((None, MLA_V, n_c), lambda bi, h, i: (bi, h, 0))]
    args = [q, k_ctx, vt_ctx]
    n_steps = 0
    if k_lat is not None:
        n_l = k_lat.shape[1]
        n_steps = n_l // tk
        in_specs += [pl.BlockSpec((None, n_l, MLA_HEAD_PAD), lambda bi, h, i: (bi, 0, h)),
                     pl.BlockSpec((None, MLA_V, n_l), lambda bi, h, i: (bi, h, 0))]
        args += [k_lat, vt_lat]
    return pl.pallas_call(
        functools.partial(_mla_kernel, tk=tk, n_steps=n_steps),
        grid=(b, MLA_HEADS, t // tq),
        in_specs=in_specs,
        out_specs=pl.BlockSpec((None, tq, MLA_V), lambda bi, h, i: (bi, i, h)),
        out_shape=jax.ShapeDtypeStruct((b, t, MLA_HEADS * MLA_V), BF16),
        scratch_shapes=[pltpu.VMEM((MLA_HEAD_PAD, tq), BF16), pltpu.VMEM((1, tq), F32), pltpu.VMEM((1, tq), F32),
                        pltpu.VMEM((MLA_V, tq), F32)],
        compiler_params=_params("parallel", "parallel", "arbitrary"),
        name="mla_attention",
    )(*args)


def _proj_c_kernel(h_ref, sh_ref, sc_ref, gn_ref, wc_ref, wvt_ref, gqk_ref, cos_ref, sin_ref, bd_ref,
                   q_ref, k_ref, vt_ref):
    a = (_rms(h_ref[...], gn_ref[...]) * (1.0 + sc_ref[...]) + sh_ref[...]).astype(BF16)
    p = _dot(a, wc_ref[...])
    qw = WIN_HEADS * WIN_HEAD_DIM
    kw = WIN_KV_HEADS * WIN_HEAD_DIM
    vt_ref[...] = _dot_nt(wvt_ref[...], a).astype(BF16)
    bd = bd_ref[...]
    cos, sin = cos_ref[...], sin_ref[...]
    lane = lax.broadcasted_iota(jnp.int32, cos.shape, 1)
    first_quarter = (lane % (WIN_HEAD_DIM // 2)) < (WIN_HEAD_DIM // 4)
    for j in range((qw + kw) // MXU_TILE):
        xh = p[:, j * MXU_TILE:(j + 1) * MXU_TILE]
        hi, lo = _split_bf16(xh * xh)
        ssq = _dot(hi, bd) + _dot(lo, bd)
        xn = xh * lax.rsqrt(ssq * (1.0 / WIN_HEAD_DIM) + NORM_EPS) * gqk_ref[:, j * MXU_TILE:(j + 1) * MXU_TILE]
        for c in range(MXU_TILE // LANES):
            xc = xn[:, c * LANES:(c + 1) * LANES]
            fwd = pltpu.roll(xc, WIN_HEAD_DIM // 4, axis=1)
            bwd = pltpu.roll(xc, LANES - WIN_HEAD_DIM // 4, axis=1)
            y = (xc * cos + jnp.where(first_quarter, -bwd, fwd) * sin).astype(BF16)
            col = j * MXU_TILE + c * LANES
            if col < qw:
                q_ref[:, col:col + LANES] = y
            else:
                k_ref[:, col - qw:col - qw + LANES] = y


def _proj_c(h, mods, ctx_row, tm, wts, cos2, sin2):
    b, t, d = h.shape
    qw = WIN_HEADS * WIN_HEAD_DIM
    kw = WIN_KV_HEADS * WIN_HEAD_DIM
    tile = lambda w: pl.BlockSpec((None, tm, w), lambda bi, i: (bi, i, 0))
    tab = pl.BlockSpec((tm, LANES), lambda bi, i: (i, 0))
    in_specs = ([tile(d)] + _mod_specs(d, (0, 1), ctx_row)
                + [_const_spec(wts[k].shape) for k in ("gn", "wc", "wvt", "gqk")]
                + [tab, tab, _const_spec(wts["bd"].shape)])
    return pl.pallas_call(
        _proj_c_kernel,
        grid=(b, t // tm),
        in_specs=in_specs,
        out_specs=[tile(qw), tile(kw), pl.BlockSpec((None, kw, tm), lambda bi, i: (bi, 0, i))],
        out_shape=[jax.ShapeDtypeStruct((b, t, qw), BF16),
                   jax.ShapeDtypeStruct((b, t, kw), BF16),
                   jax.ShapeDtypeStruct((b, kw, t), BF16)],
        compiler_params=_params("parallel", "parallel"),
        name="proj_c",
    )(h, mods, mods, wts["gn"], wts["wc"], wts["wvt"], wts["gqk"], cos2, sin2, wts["bd"])


def _prep_proj_c(norm_mix, c_w_in, q_norm, k_norm):
    gqk = jnp.concatenate([jnp.tile(q_norm * (WIN_SCALE * LOG2_E), WIN_HEADS), jnp.tile(k_norm, WIN_KV_HEADS)])
    idx = jnp.arange(MXU_TILE) // WIN_HEAD_DIM
    bd = (idx[:, None] == idx[None, :]).astype(BF16)
    qk = (WIN_HEADS + WIN_KV_HEADS) * WIN_HEAD_DIM
    return dict(gn=norm_mix[None, :], wc=c_w_in[:, :qk].astype(BF16), wvt=c_w_in[:, qk:].T.astype(BF16),
                gqk=gqk[None, :], bd=bd)


def _win_kernel(sink_ref, q_ref, k_ref, vt_ref, kc_ref, vct_ref, o_ref, *, tq, band, n_lat):
    i = pl.program_id(1)
    start = pl.multiple_of(jnp.clip(i * tq - WINDOW, 0, n_lat - band), WINDOW)
    hd, grp = WIN_HEAD_DIM, WIN_GROUP
    qt = q_ref[...].astype(F32).T.astype(BF16)
    k_pos = start + lax.broadcasted_iota(jnp.int32, (band, tq), 0)
    q_pos = i * tq + lax.broadcasted_iota(jnp.int32, (band, tq), 1)
    bias1 = jnp.where(jnp.abs(k_pos - q_pos) <= WINDOW, 0.0, NEG_BIG)
    bias = jnp.concatenate([bias1] * grp, axis=1)
    zeros = jnp.zeros((hd, grp * tq), BF16)
    outs = []
    for kv in range(WIN_KV_HEADS):
        qg = jnp.concatenate([qt[(kv * grp + g) * hd:(kv * grp + g + 1) * hd, :] for g in range(grp)], axis=1)
        qg = jnp.concatenate([qg, zeros] if kv % 2 == 0 else [zeros, qg], axis=0)
        col = (kv // 2) * LANES
        s_ctx = _dot(kc_ref[:, col:col + LANES], qg)
        s_loc = _dot(k_ref[pl.ds(start, band), col:col + LANES], qg) + bias
        sink = jnp.concatenate([jnp.full((1, tq), sink_ref[kv * grp + g], F32) for g in range(grp)], axis=1)
        m = jnp.maximum(jnp.maximum(jnp.max(s_loc, axis=0, keepdims=True),
                                    jnp.max(s_ctx, axis=0, keepdims=True)), sink)
        p_loc = jnp.exp2(s_loc - m)
        p_ctx = jnp.exp2(s_ctx - m)
        den = (jnp.sum(p_loc, axis=0, keepdims=True) + jnp.sum(p_ctx, axis=0, keepdims=True)
               + jnp.exp2(sink - m))
        ot = (_dot(vct_ref[kv * hd:(kv + 1) * hd, :], p_ctx.astype(BF16))
              + _dot(vt_ref[kv * hd:(kv + 1) * hd, pl.ds(start, band)], p_loc.astype(BF16))) / den
        outs += [ot[:, g * tq:(g + 1) * tq] for g in range(grp)]
    o_ref[...] = jnp.concatenate(outs, axis=0).T.astype(BF16)


def _win_attention(q, k, vt, k_ctx, vt_ctx, sink, *, tq):
    b, n, qw = q.shape
    n_c = k_ctx.shape[1]
    kw = k.shape[2]
    band = tq + 2 * WINDOW
    full = lambda r, w: pl.BlockSpec((None, r, w), lambda bi, i: (bi, 0, 0))
    return pl.pallas_call(
        functools.partial(_win_kernel, tq=tq, band=band, n_lat=n),
        grid=(b, n // tq),
        in_specs=[pl.BlockSpec(memory_space=pltpu.SMEM),
                  pl.BlockSpec((None, tq, qw), lambda bi, i: (bi, i, 0)),
                  full(n, kw), full(kw, n), full(n_c, kw), full(kw, n_c)],
        out_specs=pl.BlockSpec((None, tq, qw), lambda bi, i: (bi, i, 0)),
        out_shape=jax.ShapeDtypeStruct((b, n, qw), BF16),
        compiler_params=_params("parallel", "arbitrary"),
        name="win_attention",
    )(sink.astype(F32) * LOG2_E, q, k, vt, k_ctx, vt_ctx)


def _post_kernel(*refs, s5_width, n_chunks):
    if s5_width:
        (h_ref, g_ref, sh_ref, sc_ref, g2_ref, gn_ref, yg_ref, o_ref, wglu_ref, bglu_ref, wo_ref,
         wg_ref, wu_ref, wd_ref, out_ref, a_ref, acc_ref) = refs
        yg = yg_ref[...]
        s5 = yg * jax.nn.sigmoid(_dot(yg.astype(BF16), wglu_ref[...]) + bglu_ref[...])
        mix = _dot(s5.astype(BF16), wo_ref[:s5_width, :]) + _dot(o_ref[...], wo_ref[s5_width:, :])
    else:
        (h_ref, g_ref, sh_ref, sc_ref, g2_ref, gn_ref, o_ref, wo_ref,
         wg_ref, wu_ref, wd_ref, out_ref, a_ref, acc_ref) = refs
        mix = _dot(o_ref[...], wo_ref[...])
    h1 = h_ref[...] + g_ref[...] * mix
    a_ref[...] = (_rms(h1, gn_ref[...]) * (1.0 + sc_ref[...]) + sh_ref[...]).astype(BF16)
    acc_ref[...] = jnp.zeros_like(acc_ref)

    def body(c, _):
        a = a_ref[...]
        cols = pl.ds(pl.multiple_of(c * MXU_TILE, MXU_TILE), MXU_TILE)
        act = _silu(_dot(a, wg_ref[:, cols])) * _dot(a, wu_ref[:, cols])
        acc_ref[...] += _dot(act.astype(BF16), wd_ref[cols, :])
        return 0

    lax.fori_loop(0, n_chunks, body, 0)
    out_ref[...] = h1 + g2_ref[...] * acc_ref[...]


def _post(h, mods, ctx_row, tm, wts, o, yg=None):
    b, t, d = h.shape
    tile = lambda w: pl.BlockSpec((None, tm, w), lambda bi, i: (bi, i, 0))
    s5_width = 0 if yg is None else yg.shape[2]
    in_specs = [tile(d)] + _mod_specs(d, (2, 3, 4, 5), ctx_row) + [_const_spec(wts["gn"].shape)]
    args = [h, mods, mods, mods, mods, wts["gn"]]
    if yg is not None:
        in_specs += [tile(s5_width), tile(o.shape[2]), _const_spec(wts["wglu"].shape),
                     _const_spec(wts["bglu"].shape)]
        args += [yg, o, wts["wglu"], wts["bglu"]]
    else:
        in_specs += [tile(o.shape[2])]
        args += [o]
    for k in ("wo", "wg", "wu", "wd"):
        in_specs.append(_const_spec(wts[k].shape))
        args.append(wts[k])
    return pl.pallas_call(
        functools.partial(_post_kernel, s5_width=s5_width, n_chunks=wts["wg"].shape[1] // MXU_TILE),
        grid=(b, t // tm),
        in_specs=in_specs,
        out_specs=tile(d),
        out_shape=jax.ShapeDtypeStruct((b, t, d), F32),
        scratch_shapes=[pltpu.VMEM((tm, d), BF16), pltpu.VMEM((tm, d), F32)],
        compiler_params=_params("parallel", "parallel"),
        name="post_ffn",
    )(*args)


def _prep_post(norm_ffn, w_gate, w_up, w_down, w_out, w_glu=None, b_glu=None):
    assert w_gate.shape[1] % MXU_TILE == 0
    wts = dict(gn=norm_ffn[None, :], wo=w_out.astype(BF16), wg=w_gate.astype(BF16), wu=w_up.astype(BF16),
               wd=w_down.astype(BF16))
    if w_glu is not None:
        wts.update(wglu=w_glu.astype(BF16), bglu=b_glu[None, :])
    return wts


def kernel(x, c, ctx, c_ctx, ada_w, ada_b, norm_mix, norm_ffn, ffn_w_gate, ffn_w_up, ffn_w_down,
           a_w_in, a_w_out, s5_lam_re, s5_lam_im, s5_log_step, s5_b_re, s5_b_im, s5_c_re, s5_c_im,
           s5_d, s5_w_glu, s5_b_glu, mla_qa_norm, mla_w_q_b, mla_kva_norm, mla_w_kv_b,
           mla_q_norm, mla_k_norm, c_w_in, c_w_out, c_q_norm, c_k_norm, c_sink):
    b, n, d = x.shape
    n_c = ctx.shape[1]
    depth = ada_w.shape[0]
    assert b + 1 <= MOD_ROWS and n % 1024 == 0 and n_c % (S5_CHUNK * SUBLANES) == 0
    rows = n // GRID_W
    tm_lat, tm_ctx = 512, n_c

    cond = jnp.zeros((MOD_ROWS, d), F32).at[:b].set(c).at[b].set(c_ctx)
    mods = _ada_modulation(cond, ada_w, ada_b)
    mods = mods.reshape(depth, MOD_ROWS, N_MOD, 1, d)

    cos_a, sin_a = _grid_rope_tables(rows, MLA_ROPE)
    cs_a_lat = jnp.concatenate([cos_a, sin_a], axis=1)
    cs_a_ctx = jnp.concatenate([jnp.ones((n_c, MLA_ROPE), F32), jnp.zeros((n_c, MLA_ROPE), F32)], axis=1)
    cos_c, sin_c = _grid_rope_tables(rows, WIN_HEAD_DIM)
    cos_c2, sin_c2 = jnp.tile(cos_c, (1, 2)), jnp.tile(sin_c, (1, 2))
    one_c, zero_c = jnp.ones((n_c, LANES), F32), jnp.zeros((n_c, LANES), F32)

    h_ctx, h_lat = ctx, x
    for i in range(depth):
        need_ctx = i < depth - 1
        j = i // 2
        m_i = mods[i]
        if i % 2 == 0:
            pw = _prep_proj_a(norm_mix[i], a_w_in[j], mla_qa_norm[j], mla_w_q_b[j], mla_kva_norm[j],
                              mla_w_kv_b[j], mla_q_norm[j], mla_k_norm[j])
            u_l, q_l, k_l, vt_l = _proj_a(h_lat, m_i, None, tm_lat, pw, cs_a_lat)
            u_c, q_c, k_c, vt_c = _proj_a(h_ctx, m_i, b, tm_ctx, pw, cs_a_ctx)
            tables = _s5_tables(s5_lam_re[j], s5_lam_im[j], s5_log_step[j], s5_b_re[j], s5_b_im[j],
                                s5_c_re[j], s5_c_im[j], s5_d[j])
            yg_c, yg_l = _s5(u_c, u_l, tables)
            o_l = _mla_attention(q_l, k_c, vt_c, k_l, vt_l, tq=1024)
            post_w = _prep_post(norm_ffn[i], ffn_w_gate[i], ffn_w_up[i], ffn_w_down[i], a_w_out[j],
                                s5_w_glu[j], s5_b_glu[j])
            h_lat_new = _post(h_lat, m_i, None, tm_lat, post_w, o_l, yg_l)
            if need_ctx:
                o_c = _mla_attention(q_c, k_c, vt_c, tq=n_c)
                h_ctx = _post(h_ctx, m_i, b, tm_ctx, post_w, o_c, yg_c)
            h_lat = h_lat_new
        else:
            pw = _prep_proj_c(norm_mix[i], c_w_in[j], c_q_norm[j], c_k_norm[j])
            q_l, k_l, vt_l = _proj_c(h_lat, m_i, None, tm_lat, pw, cos_c2, sin_c2)
            q_c, k_c, vt_c = _proj_c(h_ctx, m_i, b, tm_ctx, pw, one_c, zero_c)
            o_l = _win_attention(q_l, k_l, vt_l, k_c, vt_c, c_sink[j], tq=256)
            post_w = _prep_post(norm_ffn[i], ffn_w_gate[i], ffn_w_up[i], ffn_w_down[i], c_w_out[j])
            h_lat_new = _post(h_lat, m_i, None, tm_lat, post_w, o_l)
            if need_ctx:
                raise NotImplementedError("context queries of a windowed layer")
            h_lat = h_lat_new
    return h_lat
```

```python
import functools
import math

import jax
import jax.numpy as jnp
from jax import lax
from jax.experimental import pallas as pl
from jax.experimental.pallas import tpu as pltpu

F32 = jnp.float32
BF16 = jnp.bfloat16

GRID_W = 64
NORM_EPS = 1e-6
ROPE_THETA = 10000.0
N_MOD = 6
S5_GROUP_DIM = 16
S5_STATE = 64
S5_CHUNK = 16
MLA_HEADS = 4
MLA_NOPE = 128
MLA_ROPE = 64
MLA_QK_DIM = MLA_NOPE + MLA_ROPE
MLA_V = 128
MLA_Q_RANK = 384
MLA_KV_RANK = 256
MLA_SCALE = MLA_QK_DIM ** -0.5
MLA_HEAD_PAD = 256
WIN_HEADS = 16
WIN_KV_HEADS = 4
WIN_GROUP = WIN_HEADS // WIN_KV_HEADS
WIN_HEAD_DIM = 64
WINDOW = 128
WIN_SCALE = WIN_HEAD_DIM ** -0.5
LANES = 128
SUBLANES = 8
MXU_TILE = 256
VMEM_LIMIT_BYTES = 48 * 1024 * 1024
NEG_BIG = -1e30
LOG2_E = math.log2(math.e)
S5_BLOCK_PAIRS = 4
MLA_Q_BLOCK = 256
MLA_K_PIECE = 256
MOD_ROWS = 8


def _dot(a, b):
    return jnp.dot(a, b, preferred_element_type=F32)


def _dot_nt(a, b):
    return lax.dot_general(a, b, (((1,), (1,)), ((), ())), preferred_element_type=F32)


def _split_bf16(x):
    hi = x.astype(BF16)
    lo = (x - hi.astype(F32)).astype(BF16)
    return hi, lo


def _rms(x, gain):
    return x * lax.rsqrt(jnp.mean(x * x, axis=-1, keepdims=True) + NORM_EPS) * gain


def _silu(x):
    return x * jax.nn.sigmoid(x)


def _params(*sem):
    return pltpu.CompilerParams(dimension_semantics=sem, vmem_limit_bytes=VMEM_LIMIT_BYTES)


def _const_spec(shape):
    nd = len(shape)
    return pl.BlockSpec(shape, lambda *_: (0,) * nd, pipeline_mode=pl.Buffered(1))


def _ada_kernel(cond_ref, w_ref, b_ref, o_ref):
    s = _silu(cond_ref[...])
    s_hi, s_lo = _split_bf16(s)
    w_hi, w_lo = _split_bf16(w_ref[...])
    o_ref[...] = _dot(s_hi, w_hi) + _dot(s_lo, w_hi) + _dot(s_hi, w_lo) + b_ref[...]


def _ada_modulation(cond, ada_w, ada_b):
    depth, d, n = ada_w.shape
    tn = 1024
    return pl.pallas_call(
        _ada_kernel,
        grid=(depth, n // tn),
        in_specs=[pl.BlockSpec((MOD_ROWS, d), lambda i, j: (0, 0)),
                  pl.BlockSpec((None, d, tn), lambda i, j: (i, 0, j)),
                  pl.BlockSpec((None, 1, tn), lambda i, j: (i, 0, j))],
        out_specs=pl.BlockSpec((None, MOD_ROWS, tn), lambda i, j: (i, 0, j)),
        out_shape=jax.ShapeDtypeStruct((depth, MOD_ROWS, n), F32),
        compiler_params=_params("arbitrary", "arbitrary"),
        name="ada_modulation",
    )(cond, ada_w, ada_b.reshape(depth, 1, n))


def _mod_specs(d, slots, ctx_row):
    def make(slot):
        if ctx_row is None:
            return pl.BlockSpec((None, None, 1, d), lambda b, i: (b, slot, 0, 0))
        return pl.BlockSpec((None, None, 1, d), lambda b, i: (ctx_row, slot, 0, 0))
    return [make(s) for s in slots]


def _grid_rope_tables(rows, rot_dim):
    row = jnp.repeat(jnp.arange(rows, dtype=jnp.int32), GRID_W)
    col = jnp.tile(jnp.arange(GRID_W, dtype=jnp.int32), rows)
    n_freq = rot_dim // 4
    inv_freq = ROPE_THETA ** (-jnp.arange(n_freq, dtype=F32) / n_freq)
    ang_r = row.astype(F32)[:, None] * inv_freq
    ang_c = col.astype(F32)[:, None] * inv_freq
    ang = jnp.concatenate([ang_r, ang_r, ang_c, ang_c], axis=-1)
    return jnp.cos(ang), jnp.sin(ang)


def _rot_perm_sign(rot_dim):
    q = rot_dim // 4
    idx = jnp.arange(rot_dim)
    perm = jnp.where((idx // q) % 2 == 0, idx + q, idx - q)
    sign = jnp.where((idx // q) % 2 == 0, -1.0, 1.0).astype(F32)
    return perm, sign


def _proj_a_kernel(h_ref, sh_ref, sc_ref, gn_ref, w1_ref, gqa_ref, wq_ref, gkv_ref, wkv_ref, wvt_ref,
                   gq_ref, gk_ref, cs_ref, msk_ref, u_ref, q_ref, k_ref, vt_ref):
    a = _rms(h_ref[...], gn_ref[...]) * (1.0 + sc_ref[...]) + sh_ref[...]
    p1 = _dot(a.astype(BF16), w1_ref[...])
    u_ref[...] = p1[:, :512]
    cq = p1[:, 512:512 + MLA_Q_RANK]
    ckv = p1[:, 896:896 + MLA_KV_RANK]
    krr = p1[:, 1152:1280]
    qb = _dot(_rms(cq, gqa_ref[...]).astype(BF16), wq_ref[...])
    ckv_n = _rms(ckv, gkv_ref[...]).astype(BF16)
    kv = _dot(ckv_n, wkv_ref[...])
    vt_ref[...] = _dot_nt(wvt_ref[...], ckv_n).astype(BF16)
    cs = cs_ref[...]
    msk = msk_ref[...]
    low_half = lax.broadcasted_iota(jnp.int32, cs.shape, 1) < MLA_ROPE

    def finish(xh, gain, out_ref, h):
        hi, lo = _split_bf16(xh * xh)
        ssq = _dot(hi, msk) + _dot(lo, msk)
        xn = xh * lax.rsqrt(ssq * (1.0 / MLA_QK_DIM) + NORM_EPS) * gain
        rr = xn[:, LANES:] * cs
        rot = rr + pltpu.roll(rr, MLA_ROPE, axis=1)
        base = h * MLA_HEAD_PAD
        out_ref[:, base:base + LANES] = xn[:, :LANES].astype(BF16)
        out_ref[:, base + LANES:base + 2 * LANES] = jnp.where(low_half, rot, 0.0).astype(BF16)

    for h in range(MLA_HEADS):
        base = h * MLA_HEAD_PAD
        finish(qb[:, base:base + MLA_HEAD_PAD], gq_ref[:, base:base + MLA_HEAD_PAD], q_ref, h)
        kh = jnp.concatenate([kv[:, h * MLA_NOPE:(h + 1) * MLA_NOPE], krr], axis=1)
        finish(kh, gk_ref[:, base:base + MLA_HEAD_PAD], k_ref, h)


def _proj_a(h, mods, ctx_row, tm, wts, cs):
    b, t, d = h.shape
    qw = MLA_HEADS * MLA_HEAD_PAD
    grid = (b, t // tm)
    tile = lambda w: pl.BlockSpec((None, tm, w), lambda bi, i: (bi, i, 0))
    in_specs = ([tile(d)] + _mod_specs(d, (0, 1), ctx_row)
                + [_const_spec(wts[k].shape) for k in
                   ("gn", "w1", "gqa", "wq", "gkv", "wkv", "wvt", "gq", "gk")]
                + [pl.BlockSpec((tm, LANES), lambda bi, i: (i, 0)), _const_spec(wts["msk"].shape)])
    vw = MLA_HEADS * MLA_V
    return pl.pallas_call(
        _proj_a_kernel,
        grid=grid,
        in_specs=in_specs,
        out_specs=[tile(512), tile(qw), tile(qw), pl.BlockSpec((None, vw, tm), lambda bi, i: (bi, 0, i))],
        out_shape=[jax.ShapeDtypeStruct((b, t, 512), F32),
                   jax.ShapeDtypeStruct((b, t, qw), BF16),
                   jax.ShapeDtypeStruct((b, t, qw), BF16),
                   jax.ShapeDtypeStruct((b, vw, t), BF16)],
        compiler_params=_params("parallel", "parallel"),
        name="proj_a",
    )(h, mods, mods, wts["gn"], wts["w1"], wts["gqa"], wts["wq"], wts["gkv"], wts["wkv"], wts["wvt"],
      wts["gq"], wts["gk"], cs, wts["msk"])


def _prep_proj_a(norm_mix, a_w_in, qa_norm, w_q_b, kva_norm, w_kv_b, q_norm, k_norm):
    perm, sign = _rot_perm_sign(MLA_ROPE)
    s5w = a_w_in.shape[1] - (MLA_Q_RANK + MLA_KV_RANK + MLA_ROPE)
    assert s5w == 512
    kr = a_w_in[:, -MLA_ROPE:]
    w1 = jnp.concatenate([a_w_in, kr[:, perm] * sign], axis=1).astype(BF16)
    wq = w_q_b.reshape(MLA_Q_RANK, MLA_HEADS, MLA_QK_DIM)
    rope = wq[:, :, MLA_NOPE:]
    wq = jnp.concatenate([wq, rope[:, :, perm] * sign], axis=2)
    wq = wq.reshape(MLA_Q_RANK, MLA_HEADS * MLA_HEAD_PAD).astype(BF16)
    wkv3 = w_kv_b.reshape(MLA_KV_RANK, MLA_HEADS, MLA_NOPE + MLA_V)
    wkv = wkv3[:, :, :MLA_NOPE].reshape(MLA_KV_RANK, -1).astype(BF16)
    wvt = wkv3[:, :, MLA_NOPE:].reshape(MLA_KV_RANK, -1).T.astype(BF16)

    def head_gain(g, scale):
        gb = jnp.concatenate([g, g[MLA_NOPE:][perm]]) * scale
        return jnp.tile(gb, MLA_HEADS)[None, :]

    rows = jnp.arange(MLA_HEAD_PAD)[:, None] < MLA_QK_DIM
    msk = jnp.broadcast_to(rows, (MLA_HEAD_PAD, MLA_HEAD_PAD)).astype(BF16)
    return dict(gn=norm_mix[None, :], w1=w1, gqa=qa_norm[None, :], wq=wq, gkv=kva_norm[None, :],
                wkv=wkv, wvt=wvt, gq=head_gain(q_norm, MLA_SCALE * LOG2_E), gk=head_gain(k_norm, 1.0),
                msk=msk)


def _s5_kernel(uc_ref, ul_ref, toep_ref, bst_ref, cst_ref, a_ref, d_ref, yc_ref, yl_ref,
               slab_ref, x_ref, z_ref, *, n_ctx, n_all):
    L, S = S5_CHUNK, S5_GROUP_DIM
    n_lat = n_all - n_ctx
    npair = S5_BLOCK_PAIRS
    lane = lax.broadcasted_iota(jnp.int32, (n_all, LANES), 1)
    for tl in range(L):
        slab_ref[tl, 0:n_ctx, :] = uc_ref[pl.ds(tl, n_ctx, stride=L), :]
        slab_ref[tl, n_ctx:n_all, :] = ul_ref[pl.ds(tl, n_lat, stride=L), :]
    for pp in range(npair):
        for gl in range(2):
            src_off = (2 * pp + gl) * S
            for hh in range(2):
                acc = None
                for k in range(SUBLANES):
                    src = slab_ref[SUBLANES * hh + k]
                    sh = (S * k - src_off) % LANES
                    r = pltpu.roll(src, sh, axis=1) if sh else src
                    acc = r if acc is None else jnp.where((lane >= S * k) & (lane < S * (k + 1)), r, acc)
                col = gl * MXU_TILE + hh * LANES
                x_ref[pp, :, col:col + LANES] = acc
    ys = []
    for pp in range(npair):
        x = x_ref[pp].astype(BF16)
        ys.append(jnp.concatenate([_dot(x[:, :MXU_TILE], toep_ref[pp, 0]),
                                   _dot(x[:, MXU_TILE:], toep_ref[pp, 1])], axis=1))
        z_ref[pp] = _dot(x, bst_ref[pp])
    t_ctx, t_all = n_ctx // SUBLANES, n_all // SUBLANES

    def body(it, carry):
        jt = jnp.where(it < t_ctx, t_ctx - 1 - it, t_all + t_ctx - 1 - it)
        rf = pl.multiple_of(it * SUBLANES, SUBLANES)
        rb = pl.multiple_of(jt * SUBLANES, SUBLANES)
        new = []
        for pp in range(npair):
            a = a_ref[pp]
            af_re, af_im, ab_re, ab_im = a[0:1], a[1:2], a[2:3], a[3:4]
            sf_re, sf_im, sb_re, sb_im = carry[4 * pp:4 * pp + 4]
            zf = z_ref[pp, pl.ds(rf, SUBLANES), 0:2 * LANES]
            zb = z_ref[pp, pl.ds(rb, SUBLANES), 2 * LANES:4 * LANES]
            f_re, f_im, b_re, b_im = [], [], [None] * SUBLANES, [None] * SUBLANES
            for r in range(SUBLANES):
                f_re.append(sf_re)
                f_im.append(sf_im)
                sf_re, sf_im = (af_re * sf_re - af_im * sf_im + zf[r:r + 1, :LANES],
                                af_re * sf_im + af_im * sf_re + zf[r:r + 1, LANES:])
            for r in reversed(range(SUBLANES)):
                b_re[r] = sb_re
                b_im[r] = sb_im
                sb_re, sb_im = (ab_re * sb_re - ab_im * sb_im + zb[r:r + 1, :LANES],
                                ab_re * sb_im + ab_im * sb_re + zb[r:r + 1, LANES:])
            slab_ref[4 * pp + 0, pl.ds(rf, SUBLANES), :] = jnp.concatenate(f_re, axis=0)
            slab_ref[4 * pp + 1, pl.ds(rf, SUBLANES), :] = jnp.concatenate(f_im, axis=0)
            slab_ref[4 * pp + 2, pl.ds(rb, SUBLANES), :] = jnp.concatenate(b_re, axis=0)
            slab_ref[4 * pp + 3, pl.ds(rb, SUBLANES), :] = jnp.concatenate(b_im, axis=0)
            new += [sf_re, sf_im, sb_re, sb_im]
        return tuple(new)

    zero = jnp.zeros((1, LANES), F32)
    lax.fori_loop(0, t_all, body, (zero,) * (4 * npair))
    for pp in range(npair):
        s_in = jnp.concatenate([slab_ref[4 * pp + j] for j in range(4)], axis=1).astype(BF16)
        y = ys[pp] + _dot(s_in, cst_ref[pp]) + x_ref[pp] * d_ref[pp]
        z_ref[pp] = jax.nn.gelu(y)
    for tl in range(L):
        k, hh = tl % SUBLANES, tl // SUBLANES
        acc = None
        for g8 in range(2 * npair):
            col = (g8 % 2) * MXU_TILE + hh * LANES
            src = z_ref[g8 // 2, :, col:col + LANES]
            sh = (S * g8 - S * k) % LANES
            r = pltpu.roll(src, sh, axis=1) if sh else src
            acc = r if acc is None else jnp.where((lane >= S * g8) & (lane < S * (g8 + 1)), r, acc)
        yc_ref[pl.ds(tl, n_ctx, stride=L), :] = acc[:n_ctx]
        yl_ref[pl.ds(tl, n_lat, stride=L), :] = acc[n_ctx:]


def _s5_tables(lam_re, lam_im, log_step, b_re, b_im, c_re, c_im, d_skip):
    hp = lax.Precision.HIGHEST
    L = S5_CHUNK
    _, G, P = lam_re.shape
    S = S5_GROUP_DIM
    step = jnp.exp(log_step.astype(F32))[..., None]
    ar, ai = lam_re.astype(F32) * step, lam_im.astype(F32) * step
    k = jnp.arange(L + 1, dtype=F32)[:, None, None, None]
    mag = jnp.exp(k * ar)
    pw_re, pw_im = mag * jnp.cos(k * ai), mag * jnp.sin(k * ai)
    n_re = jnp.expm1(ar) * jnp.cos(ai) - 2.0 * jnp.sin(0.5 * ai) ** 2
    n_im = jnp.exp(ar) * jnp.sin(ai)
    den = lam_re * lam_re + lam_im * lam_im
    co_re = (n_re * lam_re + n_im * lam_im) / den
    co_im = (n_im * lam_re - n_re * lam_im) / den
    bb_re = co_re[..., None] * b_re - co_im[..., None] * b_im
    bb_im = co_re[..., None] * b_im + co_im[..., None] * b_re
    cl_re = c_re[None] * pw_re[:, :, :, None, :] - c_im[None] * pw_im[:, :, :, None, :]
    cl_im = c_re[None] * pw_im[:, :, :, None, :] + c_im[None] * pw_re[:, :, :, None, :]
    kk = (jnp.einsum("kdgop,dgpi->kdgoi", cl_re, bb_re, precision=hp)
          - jnp.einsum("kdgop,dgpi->kdgoi", cl_im, bb_im, precision=hp))
    tau = jnp.arange(L)[:, None]
    tt = jnp.arange(L)[None, :]
    lag_f = jnp.clip(tt - tau, 0, L)
    lag_b = jnp.clip(tau - tt, 0, L)
    kf = jnp.where((tt >= tau)[:, :, None, None, None], kk[lag_f, 0], 0.0)
    kb = jnp.where((tau >= tt)[:, :, None, None, None], kk[lag_b, 1], 0.0)
    toep = jnp.transpose(kf + kb, (2, 0, 4, 1, 3)).reshape(G, L * S, L * S)
    toep = toep.reshape(G // 2, 2, L * S, L * S).astype(BF16)
    def drive(pw_r, pw_i, d):
        re = pw_r[..., None] * bb_re[d][None] - pw_i[..., None] * bb_im[d][None]
        im = pw_r[..., None] * bb_im[d][None] + pw_i[..., None] * bb_re[d][None]
        to_rows = lambda m: jnp.transpose(m, (1, 0, 3, 2)).reshape(G, L * S, P)
        return to_rows(re), to_rows(im)
    f_re, f_im = drive(pw_re[:L, 0][::-1], pw_im[:L, 0][::-1], 0)
    g_re, g_im = drive(pw_re[:L, 1], pw_im[:L, 1], 1)

    def pair_cols(m):
        m = m.reshape(G // 2, 2, m.shape[1], P)
        z = jnp.zeros_like(m[:, 0])
        top = jnp.concatenate([m[:, 0], z], axis=2)
        bot = jnp.concatenate([z, m[:, 1]], axis=2)
        return jnp.concatenate([top, bot], axis=1)
    bst = jnp.concatenate([pair_cols(f_re), pair_cols(f_im), pair_cols(g_re), pair_cols(g_im)],
                          axis=2).astype(BF16)
    def read(cre, cim):
        to_cols = lambda m: jnp.transpose(m, (1, 3, 0, 2)).reshape(G, P, L * S)
        return to_cols(cre), to_cols(-cim)
    rf_re, rf_im = read(cl_re[1:L + 1, 0], cl_im[1:L + 1, 0])
    rb_re, rb_im = read(cl_re[1:L + 1, 1][::-1], cl_im[1:L + 1, 1][::-1])

    def pair_rows(m):
        m = m.reshape(G // 2, 2, P, m.shape[2])
        z = jnp.zeros_like(m[:, 0])
        top = jnp.concatenate([m[:, 0], z], axis=2)
        bot = jnp.concatenate([z, m[:, 1]], axis=2)
        return jnp.concatenate([top, bot], axis=1)
    cst = jnp.concatenate([pair_rows(rf_re), pair_rows(rf_im), pair_rows(rb_re), pair_rows(rb_im)],
                          axis=1).astype(BF16)
    a_chunk = jnp.stack([pw_re[L, 0], pw_im[L, 0], pw_re[L, 1], pw_im[L, 1]], axis=0)
    a_chunk = jnp.transpose(a_chunk.reshape(4, G // 2, 2 * P), (1, 0, 2))
    d_pair = jnp.broadcast_to(d_skip.astype(F32).reshape(G // 2, 2, 1, S), (G // 2, 2, L, S))
    d_pair = d_pair.reshape(G // 2, 1, 2 * L * S)
    return toep, bst, cst, a_chunk, d_pair


def _s5(u_ctx, u_lat, tables):
    toep, bst, cst, a_chunk, d_pair = tables
    b, n_c, w = u_ctx.shape
    n_l = u_lat.shape[1]
    L = S5_CHUNK
    n_ctx, n_all = n_c // L, (n_c + n_l) // L
    nblk = w // LANES
    pw = 2 * L * S5_GROUP_DIM
    npair = S5_BLOCK_PAIRS
    wspec = lambda shape: pl.BlockSpec((npair,) + shape, lambda g, bi: (g,) + (0,) * len(shape))
    return pl.pallas_call(
        functools.partial(_s5_kernel, n_ctx=n_ctx, n_all=n_all),
        grid=(nblk, b),
        in_specs=[pl.BlockSpec((None, n_c, LANES), lambda g, bi: (bi, 0, g)),
                  pl.BlockSpec((None, n_l, LANES), lambda g, bi: (bi, 0, g)),
                  wspec((2, MXU_TILE, MXU_TILE)), wspec((pw, pw)), wspec((pw, pw)),
                  wspec((4, LANES)), wspec((1, pw))],
        out_specs=[pl.BlockSpec((None, n_c, LANES), lambda g, bi: (bi, 0, g)),
                   pl.BlockSpec((None, n_l, LANES), lambda g, bi: (bi, 0, g))],
        out_shape=[jax.ShapeDtypeStruct((b, n_c, w), F32), jax.ShapeDtypeStruct((b, n_l, w), F32)],
        scratch_shapes=[pltpu.VMEM((L, n_all, LANES), F32), pltpu.VMEM((npair, n_all, pw), F32),
                        pltpu.VMEM((npair, n_all, pw), F32)],
        compiler_params=_params("parallel", "arbitrary"),
        name="s5_scan",
    )(u_ctx, u_lat, toep, bst, cst, a_chunk, d_pair)


def _mla_kernel(*refs, tk, n_steps):
    if n_steps:
        q_ref, kc_ref, vc_ref, k_ref, v_ref, o_ref, qt_ref, s_ref, m_ref, l_ref, acc_ref = refs
    else:
        q_ref, kc_ref, vc_ref, o_ref, qt_ref, s_ref, m_ref, l_ref, acc_ref = refs
    tq = q_ref.shape[0]
    ncb = tq // MLA_Q_BLOCK
    kp = MLA_K_PIECE
    qt_ref[...] = q_ref[...].astype(F32).T.astype(BF16)

    def chunk(load_k, load_vt, nkeys, first):
        nr = nkeys // kp

        def score_piece(c, r):
            st = _dot(load_k(r), qt_ref[:, c * MLA_Q_BLOCK:(c + 1) * MLA_Q_BLOCK])
            s_ref[c % 2, r * kp:(r + 1) * kp, :] = st
            return jnp.max(st, axis=0, keepdims=True)

        def block_stats(c, mx):
            if first:
                return mx, None
            m_old = m_ref[:, c * MLA_Q_BLOCK:(c + 1) * MLA_Q_BLOCK]
            m_new = jnp.maximum(m_old, mx)
            return m_new, jnp.exp2(m_old - m_new)

        def prob_piece(c, r, m_new):
            p = jnp.exp2(s_ref[c % 2, r * kp:(r + 1) * kp, :] - m_new)
            return jnp.sum(p, axis=0, keepdims=True), _dot(load_vt(r), p.astype(BF16))

        def finish(c, m_new, alpha, lsum, pv):
            cols = slice(c * MLA_Q_BLOCK, (c + 1) * MLA_Q_BLOCK)
            if first:
                l_ref[:, cols] = lsum
                acc_ref[:, cols] = pv
            else:
                l_ref[:, cols] = alpha * l_ref[:, cols] + lsum
                acc_ref[:, cols] = alpha * acc_ref[:, cols] + pv
            m_ref[:, cols] = m_new

        mx = None
        for r in range(nr):
            pm = score_piece(0, r)
            mx = pm if mx is None else jnp.maximum(mx, pm)
        for c in range(ncb):
            m_new, alpha = block_stats(c, mx)
            mx = lsum = pv = None
            for r in range(nr):
                if c + 1 < ncb:
                    pm = score_piece(c + 1, r)
                    mx = pm if mx is None else jnp.maximum(mx, pm)
                ls, pvr = prob_piece(c, r, m_new)
                lsum = ls if lsum is None else lsum + ls
                pv = pvr if pv is None else pv + pvr
            finish(c, m_new, alpha, lsum, pv)

    chunk(lambda r: kc_ref[r * kp:(r + 1) * kp, :], lambda r: vc_ref[:, r * kp:(r + 1) * kp],
          kc_ref.shape[0], True)
    if n_steps:
        def body(j, _):
            off = pl.multiple_of(j * tk, tk)
            chunk(lambda r: k_ref[pl.ds(off + r * kp, kp), :], lambda r: v_ref[:, pl.ds(off + r * kp, kp)],
                  tk, False)
            return 0
        lax.fori_loop(0, n_steps, body, 0)
    o_ref[...] = (acc_ref[...] / l_ref[...]).T.astype(BF16)


def _mla_attention(q, k_ctx, vt_ctx, k_lat=None, vt_lat=None, *, tq, tk=1024):
    b, t, _ = q.shape
    n_c = k_ctx.shape[1]
    in_specs = [pl.BlockSpec((None, tq, MLA_HEAD_PAD), lambda bi, h, i: (bi, i, h)),
                pl.BlockSpec((None, n_c, MLA_HEAD_PAD), lambda bi, h, i: (bi, 0, h)),
                pl.BlockSpec((None, MLA_V, n_c), lambda bi, h, i: (bi, h, 0))]
    args = [q, k_ctx, vt_ctx]
    n_steps = 0
    if k_lat is not None:
        n_l = k_lat.shape[1]
        n_steps = n_l // tk
        in_specs += [pl.BlockSpec((None, n_l, MLA_HEAD_PAD), lambda bi, h, i: (bi, 0, h)),
                     pl.BlockSpec((None, MLA_V, n_l), lambda bi, h, i: (bi, h, 0))]
        args += [k_lat, vt_lat]
    return pl.pallas_call(
        functools.partial(_mla_kernel, tk=tk, n_steps=n_steps),
        grid=(b, MLA_HEADS, t // tq),
        in_specs=in_specs,
        out_specs=pl.BlockSpec((None, tq, MLA_V), lambda bi, h, i: (bi, i, h)),
        out_shape=jax.ShapeDtypeStruct((b, t, MLA_HEADS * MLA_V), BF16),
        scratch_shapes=[pltpu.VMEM((MLA_HEAD_PAD, tq), BF16), pltpu.VMEM((2, max(tk, n_c), MLA_Q_BLOCK), F32),
                        pltpu.VMEM((1, tq), F32), pltpu.VMEM((1, tq), F32), pltpu.VMEM((MLA_V, tq), F32)],
        compiler_params=_params("parallel", "parallel", "arbitrary"),
        name="mla_attention",
    )(*args)


def _proj_c_kernel(h_ref, sh_ref, sc_ref, gn_ref, wc_ref, wvt_ref, gqk_ref, cos_ref, sin_ref, bd_ref,
                   q_ref, k_ref, vt_ref):
    a = (_rms(h_ref[...], gn_ref[...]) * (1.0 + sc_ref[...]) + sh_ref[...]).astype(BF16)
    p = _dot(a, wc_ref[...])
    qw = WIN_HEADS * WIN_HEAD_DIM
    kw = WIN_KV_HEADS * WIN_HEAD_DIM
    vt_ref[...] = _dot_nt(wvt_ref[...], a).astype(BF16)
    bd = bd_ref[...]
    cos, sin = cos_ref[...], sin_ref[...]
    lane = lax.broadcasted_iota(jnp.int32, cos.shape, 1)
    first_quarter = (lane % (WIN_HEAD_DIM // 2)) < (WIN_HEAD_DIM // 4)
    for j in range((qw + kw) // MXU_TILE):
        xh = p[:, j * MXU_TILE:(j + 1) * MXU_TILE]
        hi, lo = _split_bf16(xh * xh)
        ssq = _dot(hi, bd) + _dot(lo, bd)
        xn = xh * lax.rsqrt(ssq * (1.0 / WIN_HEAD_DIM) + NORM_EPS) * gqk_ref[:, j * MXU_TILE:(j + 1) * MXU_TILE]
        for c in range(MXU_TILE // LANES):
            xc = xn[:, c * LANES:(c + 1) * LANES]
            fwd = pltpu.roll(xc, WIN_HEAD_DIM // 4, axis=1)
            bwd = pltpu.roll(xc, LANES - WIN_HEAD_DIM // 4, axis=1)
            y = (xc * cos + jnp.where(first_quarter, -bwd, fwd) * sin).astype(BF16)
            col = j * MXU_TILE + c * LANES
            if col < qw:
                q_ref[:, col:col + LANES] = y
            else:
                k_ref[:, col - qw:col - qw + LANES] = y


def _proj_c(h, mods, ctx_row, tm, wts, cos2, sin2):
    b, t, d = h.shape
    qw = WIN_HEADS * WIN_HEAD_DIM
    kw = WIN_KV_HEADS * WIN_HEAD_DIM
    tile = lambda w: pl.BlockSpec((None, tm, w), lambda bi, i: (bi, i, 0))
    tab = pl.BlockSpec((tm, LANES), lambda bi, i: (i, 0))
    in_specs = ([tile(d)] + _mod_specs(d, (0, 1), ctx_row)
                + [_const_spec(wts[k].shape) for k in ("gn", "wc", "wvt", "gqk")]
                + [tab, tab, _const_spec(wts["bd"].shape)])
    return pl.pallas_call(
        _proj_c_kernel,
        grid=(b, t // tm),
        in_specs=in_specs,
        out_specs=[tile(qw), tile(kw), pl.BlockSpec((None, kw, tm), lambda bi, i: (bi, 0, i))],
        out_shape=[jax.ShapeDtypeStruct((b, t, qw), BF16),
                   jax.ShapeDtypeStruct((b, t, kw), BF16),
                   jax.ShapeDtypeStruct((b, kw, t), BF16)],
        compiler_params=_params("parallel", "parallel"),
        name="proj_c",
    )(h, mods, mods, wts["gn"], wts["wc"], wts["wvt"], wts["gqk"], cos2, sin2, wts["bd"])


def _prep_proj_c(norm_mix, c_w_in, q_norm, k_norm):
    gqk = jnp.concatenate([jnp.tile(q_norm * (WIN_SCALE * LOG2_E), WIN_HEADS), jnp.tile(k_norm, WIN_KV_HEADS)])
    idx = jnp.arange(MXU_TILE) // WIN_HEAD_DIM
    bd = (idx[:, None] == idx[None, :]).astype(BF16)
    qk = (WIN_HEADS + WIN_KV_HEADS) * WIN_HEAD_DIM
    return dict(gn=norm_mix[None, :], wc=c_w_in[:, :qk].astype(BF16), wvt=c_w_in[:, qk:].T.astype(BF16),
                gqk=gqk[None, :], bd=bd)


def _win_kernel(sink_ref, q_ref, k_ref, vt_ref, kc_ref, vct_ref, o_ref, *, tq, band, n_lat):
    i = pl.program_id(1)
    start = pl.multiple_of(jnp.clip(i * tq - WINDOW, 0, n_lat - band), WINDOW)
    hd, grp = WIN_HEAD_DIM, WIN_GROUP
    qt = q_ref[...].astype(F32).T.astype(BF16)
    k_pos = start + lax.broadcasted_iota(jnp.int32, (band, tq), 0)
    q_pos = i * tq + lax.broadcasted_iota(jnp.int32, (band, tq), 1)
    bias1 = jnp.where(jnp.abs(k_pos - q_pos) <= WINDOW, 0.0, NEG_BIG)
    bias = jnp.concatenate([bias1] * grp, axis=1)
    zeros = jnp.zeros((hd, grp * tq), BF16)
    outs = []
    for kv in range(WIN_KV_HEADS):
        qg = jnp.concatenate([qt[(kv * grp + g) * hd:(kv * grp + g + 1) * hd, :] for g in range(grp)], axis=1)
        qg = jnp.concatenate([qg, zeros] if kv % 2 == 0 else [zeros, qg], axis=0)
        col = (kv // 2) * LANES
        s_ctx = _dot(kc_ref[:, col:col + LANES], qg)
        s_loc = _dot(k_ref[pl.ds(start, band), col:col + LANES], qg) + bias
        sink = jnp.concatenate([jnp.full((1, tq), sink_ref[kv * grp + g], F32) for g in range(grp)], axis=1)
        m = jnp.maximum(jnp.maximum(jnp.max(s_loc, axis=0, keepdims=True),
                                    jnp.max(s_ctx, axis=0, keepdims=True)), sink)
        p_loc = jnp.exp2(s_loc - m)
        p_ctx = jnp.exp2(s_ctx - m)
        den = (jnp.sum(p_loc, axis=0, keepdims=True) + jnp.sum(p_ctx, axis=0, keepdims=True)
               + jnp.exp2(sink - m))
        ot = (_dot(vct_ref[kv * hd:(kv + 1) * hd, :], p_ctx.astype(BF16))
              + _dot(vt_ref[kv * hd:(kv + 1) * hd, pl.ds(start, band)], p_loc.astype(BF16))) / den
        outs += [ot[:, g * tq:(g + 1) * tq] for g in range(grp)]
    o_ref[...] = jnp.concatenate(outs, axis=0).T.astype(BF16)


def _win_attention(q, k, vt, k_ctx, vt_ctx, sink, *, tq):
    b, n, qw = q.shape
    n_c = k_ctx.shape[1]
    kw = k.shape[2]
    band = tq + 2 * WINDOW
    full = lambda r, w: pl.BlockSpec((None, r, w), lambda bi, i: (bi, 0, 0))
    return pl.pallas_call(
        functools.partial(_win_kernel, tq=tq, band=band, n_lat=n),
        grid=(b, n // tq),
        in_specs=[pl.BlockSpec(memory_space=pltpu.SMEM),
                  pl.BlockSpec((None, tq, qw), lambda bi, i: (bi, i, 0)),
                  full(n, kw), full(kw, n), full(n_c, kw), full(kw, n_c)],
        out_specs=pl.BlockSpec((None, tq, qw), lambda bi, i: (bi, i, 0)),
        out_shape=jax.ShapeDtypeStruct((b, n, qw), BF16),
        compiler_params=_params("parallel", "arbitrary"),
        name="win_attention",
    )(sink.astype(F32) * LOG2_E, q, k, vt, k_ctx, vt_ctx)


def _post_kernel(*refs, s5_width, n_chunks):
    if s5_width:
        (h_ref, g_ref, sh_ref, sc_ref, g2_ref, gn_ref, yg_ref, o_ref, wglu_ref, bglu_ref, wo_ref,
         wg_ref, wu_ref, wd_ref, out_ref, a_ref, acc_ref) = refs
        yg = yg_ref[...]
        s5 = yg * jax.nn.sigmoid(_dot(yg.astype(BF16), wglu_ref[...]) + bglu_ref[...])
        mix = _dot(s5.astype(BF16), wo_ref[:s5_width, :]) + _dot(o_ref[...], wo_ref[s5_width:, :])
    else:
        (h_ref, g_ref, sh_ref, sc_ref, g2_ref, gn_ref, o_ref, wo_ref,
         wg_ref, wu_ref, wd_ref, out_ref, a_ref, acc_ref) = refs
        mix = _dot(o_ref[...], wo_ref[...])
    h1 = h_ref[...] + g_ref[...] * mix
    a_ref[...] = (_rms(h1, gn_ref[...]) * (1.0 + sc_ref[...]) + sh_ref[...]).astype(BF16)
    acc_ref[...] = jnp.zeros_like(acc_ref)

    def body(c, _):
        a = a_ref[...]
        cols = pl.ds(pl.multiple_of(c * MXU_TILE, MXU_TILE), MXU_TILE)
        act = _silu(_dot(a, wg_ref[:, cols])) * _dot(a, wu_ref[:, cols])
        acc_ref[...] += _dot(act.astype(BF16), wd_ref[cols, :])
        return 0

    lax.fori_loop(0, n_chunks, body, 0)
    out_ref[...] = h1 + g2_ref[...] * acc_ref[...]


def _post(h, mods, ctx_row, tm, wts, o, yg=None):
    b, t, d = h.shape
    tile = lambda w: pl.BlockSpec((None, tm, w), lambda bi, i: (bi, i, 0))
    s5_width = 0 if yg is None else yg.shape[2]
    in_specs = [tile(d)] + _mod_specs(d, (2, 3, 4, 5), ctx_row) + [_const_spec(wts["gn"].shape)]
    args = [h, mods, mods, mods, mods, wts["gn"]]
    if yg is not None:
        in_specs += [tile(s5_width), tile(o.shape[2]), _const_spec(wts["wglu"].shape),
                     _const_spec(wts["bglu"].shape)]
        args += [yg, o, wts["wglu"], wts["bglu"]]
    else:
        in_specs += [tile(o.shape[2])]
        args += [o]
    for k in ("wo", "wg", "wu", "wd"):
        in_specs.append(_const_spec(wts[k].shape))
        args.append(wts[k])
    return pl.pallas_call(
        functools.partial(_post_kernel, s5_width=s5_width, n_chunks=wts["wg"].shape[1] // MXU_TILE),
        grid=(b, t // tm),
        in_specs=in_specs,
        out_specs=tile(d),
        out_shape=jax.ShapeDtypeStruct((b, t, d), F32),
        scratch_shapes=[pltpu.VMEM((tm, d), BF16), pltpu.VMEM((tm, d), F32)],
        compiler_params=_params("parallel", "parallel"),
        name="post_ffn",
    )(*args)


def _prep_post(norm_ffn, w_gate, w_up, w_down, w_out, w_glu=None, b_glu=None):
    assert w_gate.shape[1] % MXU_TILE == 0
    wts = dict(gn=norm_ffn[None, :], wo=w_out.astype(BF16), wg=w_gate.astype(BF16), wu=w_up.astype(BF16),
               wd=w_down.astype(BF16))
    if w_glu is not None:
        wts.update(wglu=w_glu.astype(BF16), bglu=b_glu[None, :])
    return wts


def kernel(x, c, ctx, c_ctx, ada_w, ada_b, norm_mix, norm_ffn, ffn_w_gate, ffn_w_up, ffn_w_down,
           a_w_in, a_w_out, s5_lam_re, s5_lam_im, s5_log_step, s5_b_re, s5_b_im, s5_c_re, s5_c_im,
           s5_d, s5_w_glu, s5_b_glu, mla_qa_norm, mla_w_q_b, mla_kva_norm, mla_w_kv_b,
           mla_q_norm, mla_k_norm, c_w_in, c_w_out, c_q_norm, c_k_norm, c_sink):
    b, n, d = x.shape
    n_c = ctx.shape[1]
    depth = ada_w.shape[0]
    assert b + 1 <= MOD_ROWS and n % 1024 == 0 and n_c % (S5_CHUNK * SUBLANES) == 0
    rows = n // GRID_W
    tm_lat, tm_ctx = 512, n_c

    cond = jnp.zeros((MOD_ROWS, d), F32).at[:b].set(c).at[b].set(c_ctx)
    mods = _ada_modulation(cond, ada_w, ada_b)
    mods = mods.reshape(depth, MOD_ROWS, N_MOD, 1, d)

    cos_a, sin_a = _grid_rope_tables(rows, MLA_ROPE)
    cs_a_lat = jnp.concatenate([cos_a, sin_a], axis=1)
    cs_a_ctx = jnp.concatenate([jnp.ones((n_c, MLA_ROPE), F32), jnp.zeros((n_c, MLA_ROPE), F32)], axis=1)
    cos_c, sin_c = _grid_rope_tables(rows, WIN_HEAD_DIM)
    cos_c2, sin_c2 = jnp.tile(cos_c, (1, 2)), jnp.tile(sin_c, (1, 2))
    one_c, zero_c = jnp.ones((n_c, LANES), F32), jnp.zeros((n_c, LANES), F32)

    h_ctx, h_lat = ctx, x
    for i in range(depth):
        need_ctx = i < depth - 1
        j = i // 2
        m_i = mods[i]
        if i % 2 == 0:
            pw = _prep_proj_a(norm_mix[i], a_w_in[j], mla_qa_norm[j], mla_w_q_b[j], mla_kva_norm[j],
                              mla_w_kv_b[j], mla_q_norm[j], mla_k_norm[j])
            u_l, q_l, k_l, vt_l = _proj_a(h_lat, m_i, None, tm_lat, pw, cs_a_lat)
            u_c, q_c, k_c, vt_c = _proj_a(h_ctx, m_i, b, tm_ctx, pw, cs_a_ctx)
            tables = _s5_tables(s5_lam_re[j], s5_lam_im[j], s5_log_step[j], s5_b_re[j], s5_b_im[j],
                                s5_c_re[j], s5_c_im[j], s5_d[j])
            yg_c, yg_l = _s5(u_c, u_l, tables)
            o_l = _mla_attention(q_l, k_c, vt_c, k_l, vt_l, tq=min(n, 4096))
            post_w = _prep_post(norm_ffn[i], ffn_w_gate[i], ffn_w_up[i], ffn_w_down[i], a_w_out[j],
                                s5_w_glu[j], s5_b_glu[j])
            h_lat_new = _post(h_lat, m_i, None, tm_lat, post_w, o_l, yg_l)
            if need_ctx:
                o_c = _mla_attention(q_c, k_c, vt_c, tq=n_c)
                h_ctx = _post(h_ctx, m_i, b, tm_ctx, post_w, o_c, yg_c)
            h_lat = h_lat_new
        else:
            pw = _prep_proj_c(norm_mix[i], c_w_in[j], c_q_norm[j], c_k_norm[j])
            q_l, k_l, vt_l = _proj_c(h_lat, m_i, None, tm_lat, pw, cos_c2, sin_c2)
            q_c, k_c, vt_c = _proj_c(h_ctx, m_i, b, tm_ctx, pw, one_c, zero_c)
            o_l = _win_attention(q_l, k_l, vt_l, k_c, vt_c, c_sink[j], tq=256)
            post_w = _prep_post(norm_ffn[i], ffn_w_gate[i], ffn_w_up[i], ffn_w_down[i], c_w_out[j])
            h_lat_new = _post(h_lat, m_i, None, tm_lat, post_w, o_l)
            if need_ctx:
                raise NotImplementedError("context queries of a windowed layer")
            h_lat = h_lat_new
    return h_lat
```

```python
import functools
import math

import jax
import jax.numpy as jnp
from jax import lax
from jax.experimental import pallas as pl
from jax.experimental.pallas import tpu as pltpu

F32 = jnp.float32
BF16 = jnp.bfloat16

GRID_W = 64
NORM_EPS = 1e-6
ROPE_THETA = 10000.0
N_MOD = 6
S5_GROUP_DIM = 16
S5_STATE = 64
S5_CHUNK = 16
MLA_HEADS = 4
MLA_NOPE = 128
MLA_ROPE = 64
MLA_QK_DIM = MLA_NOPE + MLA_ROPE
MLA_V = 128
MLA_Q_RANK = 384
MLA_KV_RANK = 256
MLA_SCALE = MLA_QK_DIM ** -0.5
MLA_HEAD_PAD = 256
WIN_HEADS = 16
WIN_KV_HEADS = 4
WIN_GROUP = WIN_HEADS // WIN_KV_HEADS
WIN_HEAD_DIM = 64
WINDOW = 128
WIN_SCALE = WIN_HEAD_DIM ** -0.5
LANES = 128
SUBLANES = 8
MXU_TILE = 256
VMEM_LIMIT_BYTES = 48 * 1024 * 1024
NEG_BIG = -1e30
LOG2_E = math.log2(math.e)
S5_BLOCK_PAIRS = 4
MLA_Q_BLOCK = 256
MLA_K_PIECE = 256
MOD_ROWS = 8


def _dot(a, b):
    return jnp.dot(a, b, preferred_element_type=F32)


def _dot_nt(a, b):
    return lax.dot_general(a, b, (((1,), (1,)), ((), ())), preferred_element_type=F32)


def _split_bf16(x):
    hi = x.astype(BF16)
    lo = (x - hi.astype(F32)).astype(BF16)
    return hi, lo


def _rms(x, gain):
    return x * lax.rsqrt(jnp.mean(x * x, axis=-1, keepdims=True) + NORM_EPS) * gain


def _silu(x):
    return x * jax.nn.sigmoid(x)


def _params(*sem):
    return pltpu.CompilerParams(dimension_semantics=sem, vmem_limit_bytes=VMEM_LIMIT_BYTES)


def _const_spec(shape, layer=None):
    if layer is None:
        nd = len(shape)
        return pl.BlockSpec(shape, lambda *_: (0,) * nd, pipeline_mode=pl.Buffered(1))
    nd = len(shape) - 1
    return pl.BlockSpec((None,) + tuple(shape[1:]), lambda *_: (layer,) + (0,) * nd,
                        pipeline_mode=pl.Buffered(1))


def _ada_kernel(cond_ref, w_ref, b_ref, o_ref):
    s = _silu(cond_ref[...])
    s_hi, s_lo = _split_bf16(s)
    w_hi, w_lo = _split_bf16(w_ref[...])
    o_ref[...] = _dot(s_hi, w_hi) + _dot(s_lo, w_hi) + _dot(s_hi, w_lo) + b_ref[...]


def _ada_modulation(cond, ada_w, ada_b):
    depth, d, n = ada_w.shape
    tn = 1024
    return pl.pallas_call(
        _ada_kernel,
        grid=(depth, n // tn),
        in_specs=[pl.BlockSpec((MOD_ROWS, d), lambda i, j: (0, 0)),
                  pl.BlockSpec((None, d, tn), lambda i, j: (i, 0, j)),
                  pl.BlockSpec((None, 1, tn), lambda i, j: (i, 0, j))],
        out_specs=pl.BlockSpec((None, MOD_ROWS, tn), lambda i, j: (i, 0, j)),
        out_shape=jax.ShapeDtypeStruct((depth, MOD_ROWS, n), F32),
        compiler_params=_params("arbitrary", "arbitrary"),
        name="ada_modulation",
    )(cond, ada_w, ada_b.reshape(depth, 1, n))


def _mod_specs(d, slots, ctx_row):
    def make(slot):
        if ctx_row is None:
            return pl.BlockSpec((None, None, 1, d), lambda b, i: (b, slot, 0, 0))
        return pl.BlockSpec((None, None, 1, d), lambda b, i: (ctx_row, slot, 0, 0))
    return [make(s) for s in slots]


def _grid_rope_tables(rows, rot_dim):
    n_freq = rot_dim // 4
    inv_freq = ROPE_THETA ** (-jnp.arange(n_freq, dtype=F32) / n_freq)
    ang_r = jnp.arange(rows, dtype=jnp.int32).astype(F32)[:, None] * inv_freq
    ang_c = jnp.arange(GRID_W, dtype=jnp.int32).astype(F32)[:, None] * inv_freq

    def expand(r, c):
        r = jnp.broadcast_to(r[:, None, :], (rows, GRID_W, n_freq))
        c = jnp.broadcast_to(c[None, :, :], (rows, GRID_W, n_freq))
        return jnp.concatenate([r, r, c, c], axis=-1).reshape(rows * GRID_W, rot_dim)

    return expand(jnp.cos(ang_r), jnp.cos(ang_c)), expand(jnp.sin(ang_r), jnp.sin(ang_c))


def _rot_perm_sign(rot_dim):
    q = rot_dim // 4
    idx = jnp.arange(rot_dim)
    perm = jnp.where((idx // q) % 2 == 0, idx + q, idx - q)
    sign = jnp.where((idx // q) % 2 == 0, -1.0, 1.0).astype(F32)
    return perm, sign


def _proj_a_kernel(h_ref, sh_ref, sc_ref, gn_ref, w1_ref, gqa_ref, wq_ref, gkv_ref, wkv_ref, wvt_ref,
                   gq_ref, gk_ref, cs_ref, msk_ref, u_ref, q_ref, k_ref, vt_ref):
    a = _rms(h_ref[...], gn_ref[...]) * (1.0 + sc_ref[...]) + sh_ref[...]
    p1 = _dot(a.astype(BF16), w1_ref[...])
    u_ref[...] = p1[:, :512]
    cq = p1[:, 512:512 + MLA_Q_RANK]
    ckv = p1[:, 896:896 + MLA_KV_RANK]
    krr = p1[:, 1152:1280]
    qb = _dot(_rms(cq, gqa_ref[...]).astype(BF16), wq_ref[...])
    ckv_n = _rms(ckv, gkv_ref[...]).astype(BF16)
    kv = _dot(ckv_n, wkv_ref[...])
    vt_ref[...] = _dot_nt(wvt_ref[...], ckv_n).astype(BF16)
    cs = cs_ref[...]
    msk = msk_ref[...]
    low_half = lax.broadcasted_iota(jnp.int32, cs.shape, 1) < MLA_ROPE

    def finish(xh, gain, out_ref, h):
        hi, lo = _split_bf16(xh * xh)
        ssq = _dot(hi, msk) + _dot(lo, msk)
        xn = xh * lax.rsqrt(ssq * (1.0 / MLA_QK_DIM) + NORM_EPS) * gain
        rr = xn[:, LANES:] * cs
        rot = rr + pltpu.roll(rr, MLA_ROPE, axis=1)
        base = h * MLA_HEAD_PAD
        out_ref[:, base:base + LANES] = xn[:, :LANES].astype(BF16)
        out_ref[:, base + LANES:base + 2 * LANES] = jnp.where(low_half, rot, 0.0).astype(BF16)

    for h in range(MLA_HEADS):
        base = h * MLA_HEAD_PAD
        finish(qb[:, base:base + MLA_HEAD_PAD], gq_ref[:, base:base + MLA_HEAD_PAD], q_ref, h)
        kh = jnp.concatenate([kv[:, h * MLA_NOPE:(h + 1) * MLA_NOPE], krr], axis=1)
        finish(kh, gk_ref[:, base:base + MLA_HEAD_PAD], k_ref, h)


def _proj_a(h, mods, ctx_row, tm, wts, cs):
    b, t, d = h.shape
    qw = MLA_HEADS * MLA_HEAD_PAD
    grid = (b, t // tm)
    tile = lambda w: pl.BlockSpec((None, tm, w), lambda bi, i: (bi, i, 0))
    in_specs = ([tile(d)] + _mod_specs(d, (0, 1), ctx_row)
                + [_const_spec(wts[k].shape) for k in
                   ("gn", "w1", "gqa", "wq", "gkv", "wkv", "wvt", "gq", "gk")]
                + [pl.BlockSpec((tm, LANES), lambda bi, i: (i, 0)), _const_spec(wts["msk"].shape)])
    vw = MLA_HEADS * MLA_V
    return pl.pallas_call(
        _proj_a_kernel,
        grid=grid,
        in_specs=in_specs,
        out_specs=[tile(512), tile(qw), tile(qw), pl.BlockSpec((None, vw, tm), lambda bi, i: (bi, 0, i))],
        out_shape=[jax.ShapeDtypeStruct((b, t, 512), F32),
                   jax.ShapeDtypeStruct((b, t, qw), BF16),
                   jax.ShapeDtypeStruct((b, t, qw), BF16),
                   jax.ShapeDtypeStruct((b, vw, t), BF16)],
        compiler_params=_params("parallel", "parallel"),
        name="proj_a",
    )(h, mods, mods, wts["gn"], wts["w1"], wts["gqa"], wts["wq"], wts["gkv"], wts["wkv"], wts["wvt"],
      wts["gq"], wts["gk"], cs, wts["msk"])


def _prep_proj_a(norm_mix, a_w_in, qa_norm, w_q_b, kva_norm, w_kv_b, q_norm, k_norm):
    perm, sign = _rot_perm_sign(MLA_ROPE)
    s5w = a_w_in.shape[1] - (MLA_Q_RANK + MLA_KV_RANK + MLA_ROPE)
    assert s5w == 512
    kr = a_w_in[:, -MLA_ROPE:]
    w1 = jnp.concatenate([a_w_in, kr[:, perm] * sign], axis=1).astype(BF16)
    wq = w_q_b.reshape(MLA_Q_RANK, MLA_HEADS, MLA_QK_DIM)
    rope = wq[:, :, MLA_NOPE:]
    wq = jnp.concatenate([wq, rope[:, :, perm] * sign], axis=2)
    wq = wq.reshape(MLA_Q_RANK, MLA_HEADS * MLA_HEAD_PAD).astype(BF16)
    wkv3 = w_kv_b.reshape(MLA_KV_RANK, MLA_HEADS, MLA_NOPE + MLA_V)
    wkv = wkv3[:, :, :MLA_NOPE].reshape(MLA_KV_RANK, -1).astype(BF16)
    wvt = wkv3[:, :, MLA_NOPE:].reshape(MLA_KV_RANK, -1).T.astype(BF16)

    def head_gain(g, scale):
        gb = jnp.concatenate([g, g[MLA_NOPE:][perm]]) * scale
        return jnp.tile(gb, MLA_HEADS)[None, :]

    rows = jnp.arange(MLA_HEAD_PAD)[:, None] < MLA_QK_DIM
    msk = jnp.broadcast_to(rows, (MLA_HEAD_PAD, MLA_HEAD_PAD)).astype(BF16)
    return dict(gn=norm_mix[None, :], w1=w1, gqa=qa_norm[None, :], wq=wq, gkv=kva_norm[None, :],
                wkv=wkv, wvt=wvt, gq=head_gain(q_norm, MLA_SCALE * LOG2_E), gk=head_gain(k_norm, 1.0),
                msk=msk)


def _s5_kernel(uc_ref, ul_ref, toep_ref, bst_ref, cst_ref, a_ref, d_ref, yc_ref, yl_ref,
               slab_ref, x_ref, z_ref, *, n_ctx, n_all):
    L, S = S5_CHUNK, S5_GROUP_DIM
    n_lat = n_all - n_ctx
    npair = S5_BLOCK_PAIRS
    lane = lax.broadcasted_iota(jnp.int32, (n_all, LANES), 1)
    for tl in range(L):
        slab_ref[tl, 0:n_ctx, :] = uc_ref[pl.ds(tl, n_ctx, stride=L), :]
        slab_ref[tl, n_ctx:n_all, :] = ul_ref[pl.ds(tl, n_lat, stride=L), :]
    for pp in range(npair):
        for gl in range(2):
            src_off = (2 * pp + gl) * S
            for hh in range(2):
                acc = None
                for k in range(SUBLANES):
                    src = slab_ref[SUBLANES * hh + k]
                    sh = (S * k - src_off) % LANES
                    r = pltpu.roll(src, sh, axis=1) if sh else src
                    acc = r if acc is None else jnp.where((lane >= S * k) & (lane < S * (k + 1)), r, acc)
                col = gl * MXU_TILE + hh * LANES
                x_ref[pp, :, col:col + LANES] = acc
    ys = []
    for pp in range(npair):
        x = x_ref[pp].astype(BF16)
        ys.append(jnp.concatenate([_dot(x[:, :MXU_TILE], toep_ref[pp, 0]),
                                   _dot(x[:, MXU_TILE:], toep_ref[pp, 1])], axis=1))
        z_ref[pp] = _dot(x, bst_ref[pp])
    t_ctx, t_all = n_ctx // SUBLANES, n_all // SUBLANES

    def body(it, carry):
        jt = jnp.where(it < t_ctx, t_ctx - 1 - it, t_all + t_ctx - 1 - it)
        rf = pl.multiple_of(it * SUBLANES, SUBLANES)
        rb = pl.multiple_of(jt * SUBLANES, SUBLANES)
        new = []
        for pp in range(npair):
            a = a_ref[pp]
            af_re, af_im, ab_re, ab_im = a[0:1], a[1:2], a[2:3], a[3:4]
            sf_re, sf_im, sb_re, sb_im = carry[4 * pp:4 * pp + 4]
            zf = z_ref[pp, pl.ds(rf, SUBLANES), 0:2 * LANES]
            zb = z_ref[pp, pl.ds(rb, SUBLANES), 2 * LANES:4 * LANES]
            f_re, f_im, b_re, b_im = [], [], [None] * SUBLANES, [None] * SUBLANES
            for r in range(SUBLANES):
                f_re.append(sf_re)
                f_im.append(sf_im)
                sf_re, sf_im = (af_re * sf_re - af_im * sf_im + zf[r:r + 1, :LANES],
                                af_re * sf_im + af_im * sf_re + zf[r:r + 1, LANES:])
            for r in reversed(range(SUBLANES)):
                b_re[r] = sb_re
                b_im[r] = sb_im
                sb_re, sb_im = (ab_re * sb_re - ab_im * sb_im + zb[r:r + 1, :LANES],
                                ab_re * sb_im + ab_im * sb_re + zb[r:r + 1, LANES:])
            slab_ref[4 * pp + 0, pl.ds(rf, SUBLANES), :] = jnp.concatenate(f_re, axis=0)
            slab_ref[4 * pp + 1, pl.ds(rf, SUBLANES), :] = jnp.concatenate(f_im, axis=0)
            slab_ref[4 * pp + 2, pl.ds(rb, SUBLANES), :] = jnp.concatenate(b_re, axis=0)
            slab_ref[4 * pp + 3, pl.ds(rb, SUBLANES), :] = jnp.concatenate(b_im, axis=0)
            new += [sf_re, sf_im, sb_re, sb_im]
        return tuple(new)

    zero = jnp.zeros((1, LANES), F32)
    lax.fori_loop(0, t_all, body, (zero,) * (4 * npair))
    for pp in range(npair):
        s_in = jnp.concatenate([slab_ref[4 * pp + j] for j in range(4)], axis=1).astype(BF16)
        y = ys[pp] + _dot(s_in, cst_ref[pp]) + x_ref[pp] * d_ref[pp]
        z_ref[pp] = jax.nn.gelu(y)
    for tl in range(L):
        k, hh = tl % SUBLANES, tl // SUBLANES
        acc = None
        for g8 in range(2 * npair):
            col = (g8 % 2) * MXU_TILE + hh * LANES
            src = z_ref[g8 // 2, :, col:col + LANES]
            sh = (S * g8 - S * k) % LANES
            r = pltpu.roll(src, sh, axis=1) if sh else src
            acc = r if acc is None else jnp.where((lane >= S * g8) & (lane < S * (g8 + 1)), r, acc)
        yc_ref[pl.ds(tl, n_ctx, stride=L), :] = acc[:n_ctx]
        yl_ref[pl.ds(tl, n_lat, stride=L), :] = acc[n_ctx:]


def _dot_nt_f32(a, b):
    a_hi, a_lo = _split_bf16(a)
    b_hi, b_lo = _split_bf16(b)
    return _dot_nt(a_hi, b_hi) + _dot_nt(a_lo, b_hi) + _dot_nt(a_hi, b_lo)


def _s5_table_kernel(par_ref, bre_ref, bim_ref, cre_ref, cim_ref, toep_ref, bst_ref, cst_ref, a_ref):
    L, S, P = S5_CHUNK, S5_GROUP_DIM, S5_STATE
    kk = lax.broadcasted_iota(jnp.int32, (3 * SUBLANES, LANES), 0).astype(F32)
    lane = lax.broadcasted_iota(jnp.int32, (L, LANES), 1)
    rows512 = lax.broadcasted_iota(jnp.int32, (2 * L * S, LANES), 0)
    lanes512 = lax.broadcasted_iota(jnp.int32, (2 * L * S, LANES), 1)
    own_group = (rows512 // (L * S)) == (lanes512 // P)
    lane_pad = jnp.zeros((2 * P, LANES - S), F32)

    def rows_of_powers(pw, ks, groups):
        one = jnp.concatenate([jnp.broadcast_to(pw[k:k + 1, :], (S, LANES)) for k in ks], axis=0)
        return jnp.concatenate([one] * groups, axis=0) if groups > 1 else one

    def cmul(a_re, a_im, b_re, b_im):
        return a_re * b_re - a_im * b_im, a_re * b_im + a_im * b_re

    lag_tables = []
    a_rows = []
    for d in range(2):
        lam_re, lam_im = par_ref[d, 0:1, :], par_ref[d, 1:2, :]
        step = jnp.exp(par_ref[d, 2:3, :])
        ar, ai = lam_re * step, lam_im * step
        mag = jnp.exp(kk * ar)
        pw_re, pw_im = mag * jnp.cos(kk * ai), mag * jnp.sin(kk * ai)
        th = jnp.tanh(0.5 * ar)
        em1 = 2.0 * th / (1.0 - th)
        sh = jnp.sin(0.5 * ai)
        n_re = em1 * jnp.cos(ai) - 2.0 * sh * sh
        n_im = (em1 + 1.0) * jnp.sin(ai)
        den = lam_re * lam_re + lam_im * lam_im
        co_re = (n_re * lam_re + n_im * lam_im) / den
        co_im = (n_im * lam_re - n_re * lam_im) / den
        bt_re = jnp.concatenate([bre_ref[d], lane_pad], axis=1).T[:S]
        bt_im = jnp.concatenate([bim_ref[d], lane_pad], axis=1).T[:S]
        bb_re, bb_im = cmul(co_re, co_im, bt_re, bt_im)
        cc_re = jnp.concatenate([cre_ref[d, 0], cre_ref[d, 1]], axis=1)
        cc_im = jnp.concatenate([cim_ref[d, 0], cim_ref[d, 1]], axis=1)
        ks = [L - 1 - t for t in range(L)] if d == 0 else list(range(L))
        r_re, r_im = rows_of_powers(pw_re, ks, 2), rows_of_powers(pw_im, ks, 2)
        bbt_re, bbt_im = jnp.concatenate([bb_re] * (2 * L), axis=0), jnp.concatenate([bb_im] * (2 * L), axis=0)
        v_re, v_im = cmul(r_re, r_im, bbt_re, bbt_im)
        bst_ref[:, (2 * d) * LANES:(2 * d + 1) * LANES] = jnp.where(own_group, v_re, 0.0).astype(BF16)
        bst_ref[:, (2 * d + 1) * LANES:(2 * d + 2) * LANES] = jnp.where(own_group, v_im, 0.0).astype(BF16)
        ks = [t + 1 for t in range(L)] if d == 0 else [L - t for t in range(L)]
        r_re, r_im = rows_of_powers(pw_re, ks, 2), rows_of_powers(pw_im, ks, 2)
        cct_re, cct_im = jnp.concatenate([cc_re] * (2 * L), axis=0), jnp.concatenate([cc_im] * (2 * L), axis=0)
        v_re, v_im = cmul(cct_re, cct_im, r_re, r_im)
        cst_ref[(2 * d) * LANES:(2 * d + 1) * LANES, :] = jnp.where(own_group, v_re, 0.0).T.astype(BF16)
        cst_ref[(2 * d + 1) * LANES:(2 * d + 2) * LANES, :] = jnp.where(own_group, -v_im, 0.0).T.astype(BF16)
        ks = list(range(L)) if d == 0 else [L - 1 - j for j in range(L)]
        r_re, r_im = rows_of_powers(pw_re, ks, 1), rows_of_powers(pw_im, ks, 1)
        cl_re, cl_im = cmul(jnp.concatenate([cc_re] * L, axis=0), jnp.concatenate([cc_im] * L, axis=0), r_re, r_im)
        per_group = []
        for g in range(2):
            mine = (lane // P) == g
            per_group.append(_dot_nt_f32(jnp.where(mine, bb_re, 0.0), cl_re)
                             - _dot_nt_f32(jnp.where(mine, bb_im, 0.0), cl_im))
        lag_tables.append(per_group)
        a_rows += [pw_re[L:L + 1, :], pw_im[L:L + 1, :]]
    a_ref[...] = jnp.concatenate(a_rows, axis=0)

    def shift_right(x, s):
        x0, x1 = x[:, :LANES], x[:, LANES:]
        a, r = divmod(s, LANES)
        r0 = pltpu.roll(x0, r, axis=1) if r else x0
        r1 = pltpu.roll(x1, r, axis=1) if r else x1
        if a == 0:
            return jnp.concatenate([jnp.where(lane >= r, r0, 0.0), jnp.where(lane >= r, r1, r0)], axis=1)
        return jnp.concatenate([jnp.zeros_like(x0), jnp.where(lane >= r, r0, 0.0)], axis=1)

    def shift_left(x, s):
        x0, x1 = x[:, :LANES], x[:, LANES:]
        a, r = divmod(s, LANES)
        r0 = pltpu.roll(x0, LANES - r, axis=1) if r else x0
        r1 = pltpu.roll(x1, LANES - r, axis=1) if r else x1
        if a == 0:
            return jnp.concatenate([jnp.where(lane < LANES - r, r0, r1), jnp.where(lane < LANES - r, r1, 0.0)], axis=1)
        return jnp.concatenate([jnp.where(lane < LANES - r, r1, 0.0), jnp.zeros_like(x0)], axis=1)

    for g in range(2):
        kf, kb = lag_tables[0][g], lag_tables[1][g]
        for tau in range(L):
            blk = shift_right(kf, S * tau) + shift_left(kb, S * (L - 1 - tau))
            toep_ref[g, tau * S:(tau + 1) * S, :] = blk.astype(BF16)


def _s5_tables(lam_re, lam_im, log_step, b_re, b_im, c_re, c_im, d_skip):
    _, G, P = lam_re.shape
    S, L = S5_GROUP_DIM, S5_CHUNK
    assert P == S5_STATE and 2 * P == LANES and 2 * L * S == 2 * MXU_TILE
    pairs = G // 2
    par = jnp.stack([lam_re.reshape(2, pairs, 2 * P), lam_im.reshape(2, pairs, 2 * P),
                     jnp.repeat(log_step, P, axis=-1).reshape(2, pairs, 2 * P)], axis=2)
    par = jnp.transpose(par, (1, 0, 2, 3)).astype(F32)
    bshape = (2, pairs, 2 * P, S)
    cshape = (2, pairs, 2, S, P)
    pw = 2 * L * S
    bspec = pl.BlockSpec((2, None, 2 * P, S), lambda g: (0, g, 0, 0))
    cspec = pl.BlockSpec((2, None, 2, S, P), lambda g: (0, g, 0, 0, 0))
    toep, bst, cst, a_chunk = pl.pallas_call(
        _s5_table_kernel,
        grid=(pairs,),
        in_specs=[pl.BlockSpec((None, 2, 3, 2 * P), lambda g: (g, 0, 0, 0)), bspec, bspec, cspec, cspec],
        out_specs=[pl.BlockSpec((None, 2, MXU_TILE, MXU_TILE), lambda g: (g, 0, 0, 0)),
                   pl.BlockSpec((None, pw, pw), lambda g: (g, 0, 0)),
                   pl.BlockSpec((None, pw, pw), lambda g: (g, 0, 0)),
                   pl.BlockSpec((None, 4, LANES), lambda g: (g, 0, 0))],
        out_shape=[jax.ShapeDtypeStruct((pairs, 2, MXU_TILE, MXU_TILE), BF16),
                   jax.ShapeDtypeStruct((pairs, pw, pw), BF16),
                   jax.ShapeDtypeStruct((pairs, pw, pw), BF16),
                   jax.ShapeDtypeStruct((pairs, 4, LANES), F32)],
        compiler_params=_params("parallel"),
        name="s5_tables",
    )(par, b_re.reshape(bshape), b_im.reshape(bshape), c_re.reshape(cshape), c_im.reshape(cshape))
    d_pair = jnp.broadcast_to(d_skip.astype(F32).reshape(pairs, 2, 1, S), (pairs, 2, L, S))
    return toep, bst, cst, a_chunk, d_pair.reshape(pairs, 1, pw)


def _s5(u_ctx, u_lat, tables):
    toep, bst, cst, a_chunk, d_pair = tables
    b, n_c, w = u_ctx.shape
    n_l = u_lat.shape[1]
    L = S5_CHUNK
    n_ctx, n_all = n_c // L, (n_c + n_l) // L
    nblk = w // LANES
    pw = 2 * L * S5_GROUP_DIM
    npair = S5_BLOCK_PAIRS
    wspec = lambda shape: pl.BlockSpec((npair,) + shape, lambda g, bi: (g,) + (0,) * len(shape))
    return pl.pallas_call(
        functools.partial(_s5_kernel, n_ctx=n_ctx, n_all=n_all),
        grid=(nblk, b),
        in_specs=[pl.BlockSpec((None, n_c, LANES), lambda g, bi: (bi, 0, g)),
                  pl.BlockSpec((None, n_l, LANES), lambda g, bi: (bi, 0, g)),
                  wspec((2, MXU_TILE, MXU_TILE)), wspec((pw, pw)), wspec((pw, pw)),
                  wspec((4, LANES)), wspec((1, pw))],
        out_specs=[pl.BlockSpec((None, n_c, LANES), lambda g, bi: (bi, 0, g)),
                   pl.BlockSpec((None, n_l, LANES), lambda g, bi: (bi, 0, g))],
        out_shape=[jax.ShapeDtypeStruct((b, n_c, w), F32), jax.ShapeDtypeStruct((b, n_l, w), F32)],
        scratch_shapes=[pltpu.VMEM((L, n_all, LANES), F32), pltpu.VMEM((npair, n_all, pw), F32),
                        pltpu.VMEM((npair, n_all, pw), F32)],
        compiler_params=_params("parallel", "arbitrary"),
        name="s5_scan",
    )(u_ctx, u_lat, toep, bst, cst, a_chunk, d_pair)


def _mla_kernel(*refs, tk, n_steps):
    if n_steps:
        q_ref, kc_ref, vc_ref, k_ref, v_ref, o_ref, qt_ref, s_ref, m_ref, l_ref, acc_ref = refs
    else:
        q_ref, kc_ref, vc_ref, o_ref, qt_ref, s_ref, m_ref, l_ref, acc_ref = refs
    tq = q_ref.shape[0]
    ncb = tq // MLA_Q_BLOCK
    kp = MLA_K_PIECE
    qt_ref[...] = q_ref[...].astype(F32).T.astype(BF16)

    def chunk(load_k, load_vt, nkeys, first):
        nr = nkeys // kp

        def score_piece(c, r):
            st = _dot(load_k(r), qt_ref[:, c * MLA_Q_BLOCK:(c + 1) * MLA_Q_BLOCK])
            s_ref[c % 2, r * kp:(r + 1) * kp, :] = st
            return jnp.max(st, axis=0, keepdims=True)

        def block_stats(c, mx):
            if first:
                return mx, None
            m_old = m_ref[:, c * MLA_Q_BLOCK:(c + 1) * MLA_Q_BLOCK]
            m_new = jnp.maximum(m_old, mx)
            return m_new, jnp.exp2(m_old - m_new)

        def prob_piece(c, r, m_new):
            p = jnp.exp2(s_ref[c % 2, r * kp:(r + 1) * kp, :] - m_new)
            return jnp.sum(p, axis=0, keepdims=True), _dot(load_vt(r), p.astype(BF16))

        def finish(c, m_new, alpha, lsum, pv):
            cols = slice(c * MLA_Q_BLOCK, (c + 1) * MLA_Q_BLOCK)
            if first:
                l_ref[:, cols] = lsum
                acc_ref[:, cols] = pv
            else:
                l_ref[:, cols] = alpha * l_ref[:, cols] + lsum
                acc_ref[:, cols] = alpha * acc_ref[:, cols] + pv
            m_ref[:, cols] = m_new

        mx = None
        for r in range(nr):
            pm = score_piece(0, r)
            mx = pm if mx is None else jnp.maximum(mx, pm)
        for c in range(ncb):
            m_new, alpha = block_stats(c, mx)
            mx = lsum = pv = None
            for r in range(nr):
                if c + 1 < ncb:
                    pm = score_piece(c + 1, r)
                    mx = pm if mx is None else jnp.maximum(mx, pm)
                ls, pvr = prob_piece(c, r, m_new)
                lsum = ls if lsum is None else lsum + ls
                pv = pvr if pv is None else pv + pvr
            finish(c, m_new, alpha, lsum, pv)

    chunk(lambda r: kc_ref[r * kp:(r + 1) * kp, :], lambda r: vc_ref[:, r * kp:(r + 1) * kp],
          kc_ref.shape[0], True)
    if n_steps:
        def body(j, _):
            off = pl.multiple_of(j * tk, tk)
            chunk(lambda r: k_ref[pl.ds(off + r * kp, kp), :], lambda r: v_ref[:, pl.ds(off + r * kp, kp)],
                  tk, False)
            return 0
        lax.fori_loop(0, n_steps, body, 0)
    o_ref[...] = (acc_ref[...] / l_ref[...]).T.astype(BF16)


def _mla_attention(q, k_ctx, vt_ctx, k_lat=None, vt_lat=None, *, tq, tk=1024):
    b, t, _ = q.shape
    n_c = k_ctx.shape[1]
    in_specs = [pl.BlockSpec((None, tq, MLA_HEAD_PAD), lambda bi, h, i: (bi, i, h)),
                pl.BlockSpec((None, n_c, MLA_HEAD_PAD), lambda bi, h, i: (bi, 0, h)),
                pl.BlockSpec((None, MLA_V, n_c), lambda bi, h, i: (bi, h, 0))]
    args = [q, k_ctx, vt_ctx]
    n_steps = 0
    if k_lat is not None:
        n_l = k_lat.shape[1]
        n_steps = n_l // tk
        in_specs += [pl.BlockSpec((None, n_l, MLA_HEAD_PAD), lambda bi, h, i: (bi, 0, h)),
                     pl.BlockSpec((None, MLA_V, n_l), lambda bi, h, i: (bi, h, 0))]
        args += [k_lat, vt_lat]
    return pl.pallas_call(
        functools.partial(_mla_kernel, tk=tk, n_steps=n_steps),
        grid=(b, MLA_HEADS, t // tq),
        in_specs=in_specs,
        out_specs=pl.BlockSpec((None, tq, MLA_V), lambda bi, h, i: (bi, i, h)),
        out_shape=jax.ShapeDtypeStruct((b, t, MLA_HEADS * MLA_V), BF16),
        scratch_shapes=[pltpu.VMEM((MLA_HEAD_PAD, tq), BF16), pltpu.VMEM((2, max(tk, n_c), MLA_Q_BLOCK), F32),
                        pltpu.VMEM((1, tq), F32), pltpu.VMEM((1, tq), F32), pltpu.VMEM((MLA_V, tq), F32)],
        compiler_params=_params("parallel", "parallel", "arbitrary"),
        name="mla_attention",
    )(*args)


def _proj_c_kernel(h_ref, sh_ref, sc_ref, gn_ref, wc_ref, wvt_ref, gqk_ref, cos_ref, sin_ref, bd_ref,
                   q_ref, k_ref, vt_ref):
    a = (_rms(h_ref[...], gn_ref[...]) * (1.0 + sc_ref[...]) + sh_ref[...]).astype(BF16)
    p = _dot(a, wc_ref[...])
    qw = WIN_HEADS * WIN_HEAD_DIM
    kw = WIN_KV_HEADS * WIN_HEAD_DIM
    vt_ref[...] = _dot_nt(wvt_ref[...], a).astype(BF16)
    bd = bd_ref[...]
    cos, sin = cos_ref[...], sin_ref[...]
    lane = lax.broadcasted_iota(jnp.int32, cos.shape, 1)
    first_quarter = (lane % (WIN_HEAD_DIM // 2)) < (WIN_HEAD_DIM // 4)
    for j in range((qw + kw) // MXU_TILE):
        xh = p[:, j * MXU_TILE:(j + 1) * MXU_TILE]
        hi, lo = _split_bf16(xh * xh)
        ssq = _dot(hi, bd) + _dot(lo, bd)
        xn = xh * lax.rsqrt(ssq * (1.0 / WIN_HEAD_DIM) + NORM_EPS) * gqk_ref[:, j * MXU_TILE:(j + 1) * MXU_TILE]
        for c in range(MXU_TILE // LANES):
            xc = xn[:, c * LANES:(c + 1) * LANES]
            fwd = pltpu.roll(xc, WIN_HEAD_DIM // 4, axis=1)
            bwd = pltpu.roll(xc, LANES - WIN_HEAD_DIM // 4, axis=1)
            y = (xc * cos + jnp.where(first_quarter, -bwd, fwd) * sin).astype(BF16)
            col = j * MXU_TILE + c * LANES
            if col < qw:
                q_ref[:, col:col + LANES] = y
            else:
                k_ref[:, col - qw:col - qw + LANES] = y


def _proj_c(h, mods, ctx_row, tm, wts, cos2, sin2):
    b, t, d = h.shape
    qw = WIN_HEADS * WIN_HEAD_DIM
    kw = WIN_KV_HEADS * WIN_HEAD_DIM
    tile = lambda w: pl.BlockSpec((None, tm, w), lambda bi, i: (bi, i, 0))
    tab = pl.BlockSpec((tm, LANES), lambda bi, i: (i, 0))
    in_specs = ([tile(d)] + _mod_specs(d, (0, 1), ctx_row)
                + [_const_spec(wts[k].shape) for k in ("gn", "wc", "wvt", "gqk")]
                + [tab, tab, _const_spec(wts["bd"].shape)])
    return pl.pallas_call(
        _proj_c_kernel,
        grid=(b, t // tm),
        in_specs=in_specs,
        out_specs=[tile(qw), tile(kw), pl.BlockSpec((None, kw, tm), lambda bi, i: (bi, 0, i))],
        out_shape=[jax.ShapeDtypeStruct((b, t, qw), BF16),
                   jax.ShapeDtypeStruct((b, t, kw), BF16),
                   jax.ShapeDtypeStruct((b, kw, t), BF16)],
        compiler_params=_params("parallel", "parallel"),
        name="proj_c",
    )(h, mods, mods, wts["gn"], wts["wc"], wts["wvt"], wts["gqk"], cos2, sin2, wts["bd"])


def _prep_proj_c(norm_mix, c_w_in, q_norm, k_norm):
    gqk = jnp.concatenate([jnp.tile(q_norm * (WIN_SCALE * LOG2_E), WIN_HEADS), jnp.tile(k_norm, WIN_KV_HEADS)])
    idx = jnp.arange(MXU_TILE) // WIN_HEAD_DIM
    bd = (idx[:, None] == idx[None, :]).astype(BF16)
    qk = (WIN_HEADS + WIN_KV_HEADS) * WIN_HEAD_DIM
    return dict(gn=norm_mix[None, :], wc=c_w_in[:, :qk].astype(BF16), wvt=c_w_in[:, qk:].T.astype(BF16),
                gqk=gqk[None, :], bd=bd)


def _win_kernel(sink_ref, q_ref, k_ref, vt_ref, kc_ref, vct_ref, o_ref, *, tq, band, n_lat):
    i = pl.program_id(1)
    start = pl.multiple_of(jnp.clip(i * tq - WINDOW, 0, n_lat - band), WINDOW)
    hd, grp = WIN_HEAD_DIM, WIN_GROUP
    qt = q_ref[...].astype(F32).T.astype(BF16)
    k_pos = start + lax.broadcasted_iota(jnp.int32, (band, tq), 0)
    q_pos = i * tq + lax.broadcasted_iota(jnp.int32, (band, tq), 1)
    bias1 = jnp.where(jnp.abs(k_pos - q_pos) <= WINDOW, 0.0, NEG_BIG)
    bias = jnp.concatenate([bias1] * grp, axis=1)
    zeros = jnp.zeros((hd, grp * tq), BF16)
    outs = []
    for kv in range(WIN_KV_HEADS):
        qg = jnp.concatenate([qt[(kv * grp + g) * hd:(kv * grp + g + 1) * hd, :] for g in range(grp)], axis=1)
        qg = jnp.concatenate([qg, zeros] if kv % 2 == 0 else [zeros, qg], axis=0)
        col = (kv // 2) * LANES
        s_ctx = _dot(kc_ref[:, col:col + LANES], qg)
        s_loc = _dot(k_ref[pl.ds(start, band), col:col + LANES], qg) + bias
        sink = jnp.concatenate([jnp.full((1, tq), sink_ref[kv * grp + g], F32) for g in range(grp)], axis=1)
        m = jnp.maximum(jnp.maximum(jnp.max(s_loc, axis=0, keepdims=True),
                                    jnp.max(s_ctx, axis=0, keepdims=True)), sink)
        p_loc = jnp.exp2(s_loc - m)
        p_ctx = jnp.exp2(s_ctx - m)
        den = (jnp.sum(p_loc, axis=0, keepdims=True) + jnp.sum(p_ctx, axis=0, keepdims=True)
               + jnp.exp2(sink - m))
        ot = (_dot(vct_ref[kv * hd:(kv + 1) * hd, :], p_ctx.astype(BF16))
              + _dot(vt_ref[kv * hd:(kv + 1) * hd, pl.ds(start, band)], p_loc.astype(BF16))) / den
        outs += [ot[:, g * tq:(g + 1) * tq] for g in range(grp)]
    o_ref[...] = jnp.concatenate(outs, axis=0).T.astype(BF16)


def _win_attention(q, k, vt, k_ctx, vt_ctx, sink, *, tq):
    b, n, qw = q.shape
    n_c = k_ctx.shape[1]
    kw = k.shape[2]
    band = tq + 2 * WINDOW
    full = lambda r, w: pl.BlockSpec((None, r, w), lambda bi, i: (bi, 0, 0))
    return pl.pallas_call(
        functools.partial(_win_kernel, tq=tq, band=band, n_lat=n),
        grid=(b, n // tq),
        in_specs=[pl.BlockSpec(memory_space=pltpu.SMEM),
                  pl.BlockSpec((None, tq, qw), lambda bi, i: (bi, i, 0)),
                  full(n, kw), full(kw, n), full(n_c, kw), full(kw, n_c)],
        out_specs=pl.BlockSpec((None, tq, qw), lambda bi, i: (bi, i, 0)),
        out_shape=jax.ShapeDtypeStruct((b, n, qw), BF16),
        compiler_params=_params("parallel", "arbitrary"),
        name="win_attention",
    )(sink.astype(F32) * LOG2_E, q, k, vt, k_ctx, vt_ctx)


def _post_kernel(*refs, s5_width, n_chunks):
    if s5_width:
        (h_ref, g_ref, sh_ref, sc_ref, g2_ref, gn_ref, yg_ref, o_ref, wglu_ref, bglu_ref, wo_ref,
         wg_ref, wu_ref, wd_ref, out_ref, a_ref, acc_ref) = refs
        yg = yg_ref[...]
        s5 = yg * jax.nn.sigmoid(_dot(yg.astype(BF16), wglu_ref[...]) + bglu_ref[...])
        mix = _dot(s5.astype(BF16), wo_ref[:s5_width, :]) + _dot(o_ref[...], wo_ref[s5_width:, :])
    else:
        (h_ref, g_ref, sh_ref, sc_ref, g2_ref, gn_ref, o_ref, wo_ref,
         wg_ref, wu_ref, wd_ref, out_ref, a_ref, acc_ref) = refs
        mix = _dot(o_ref[...], wo_ref[...])
    h1 = h_ref[...] + g_ref[...] * mix
    a_ref[...] = (_rms(h1, gn_ref[...]) * (1.0 + sc_ref[...]) + sh_ref[...]).astype(BF16)
    acc_ref[...] = jnp.zeros_like(acc_ref)

    def body(c, _):
        a = a_ref[...]
        cols = pl.ds(pl.multiple_of(c * MXU_TILE, MXU_TILE), MXU_TILE)
        act = _silu(_dot(a, wg_ref[:, cols])) * _dot(a, wu_ref[:, cols])
        acc_ref[...] += _dot(act.astype(BF16), wd_ref[cols, :])
        return 0

    lax.fori_loop(0, n_chunks, body, 0)
    out_ref[...] = h1 + g2_ref[...] * acc_ref[...]


def _post(h, mods, ctx_row, tm, wts, ffn, layer, o, yg=None):
    b, t, d = h.shape
    tile = lambda w: pl.BlockSpec((None, tm, w), lambda bi, i: (bi, i, 0))
    s5_width = 0 if yg is None else yg.shape[2]
    in_specs = [tile(d)] + _mod_specs(d, (2, 3, 4, 5), ctx_row) + [_const_spec(wts["gn"].shape)]
    args = [h, mods, mods, mods, mods, wts["gn"]]
    if yg is not None:
        in_specs += [tile(s5_width), tile(o.shape[2]), _const_spec(wts["wglu"].shape),
                     _const_spec(wts["bglu"].shape)]
        args += [yg, o, wts["wglu"], wts["bglu"]]
    else:
        in_specs += [tile(o.shape[2])]
        args += [o]
    in_specs.append(_const_spec(wts["wo"].shape))
    args.append(wts["wo"])
    for k in ("wg", "wu", "wd"):
        in_specs.append(_const_spec(ffn[k].shape, layer))
        args.append(ffn[k])
    return pl.pallas_call(
        functools.partial(_post_kernel, s5_width=s5_width, n_chunks=ffn["wg"].shape[2] // MXU_TILE),
        grid=(b, t // tm),
        in_specs=in_specs,
        out_specs=tile(d),
        out_shape=jax.ShapeDtypeStruct((b, t, d), F32),
        scratch_shapes=[pltpu.VMEM((tm, d), BF16), pltpu.VMEM((tm, d), F32)],
        compiler_params=_params("parallel", "parallel"),
        name="post_ffn",
    )(*args)


def _prep_post(norm_ffn, w_out, w_glu=None, b_glu=None):
    wts = dict(gn=norm_ffn[None, :], wo=w_out.astype(BF16))
    if w_glu is not None:
        wts.update(wglu=w_glu.astype(BF16), bglu=b_glu[None, :])
    return wts


def kernel(x, c, ctx, c_ctx, ada_w, ada_b, norm_mix, norm_ffn, ffn_w_gate, ffn_w_up, ffn_w_down,
           a_w_in, a_w_out, s5_lam_re, s5_lam_im, s5_log_step, s5_b_re, s5_b_im, s5_c_re, s5_c_im,
           s5_d, s5_w_glu, s5_b_glu, mla_qa_norm, mla_w_q_b, mla_kva_norm, mla_w_kv_b,
           mla_q_norm, mla_k_norm, c_w_in, c_w_out, c_q_norm, c_k_norm, c_sink):
    b, n, d = x.shape
    n_c = ctx.shape[1]
    depth = ada_w.shape[0]
    assert b + 1 <= MOD_ROWS and n % 1024 == 0 and n_c % (S5_CHUNK * SUBLANES) == 0
    rows = n // GRID_W
    tm_lat, tm_ctx = 512, n_c

    cond = jnp.zeros((MOD_ROWS, d), F32).at[:b].set(c).at[b].set(c_ctx)
    mods = _ada_modulation(cond, ada_w, ada_b)
    mods = mods.reshape(depth, MOD_ROWS, N_MOD, 1, d)

    cos_a, sin_a = _grid_rope_tables(rows, MLA_ROPE)
    cs_a_lat = jnp.concatenate([cos_a, sin_a], axis=1)
    cs_a_ctx = jnp.concatenate([jnp.ones((n_c, MLA_ROPE), F32), jnp.zeros((n_c, MLA_ROPE), F32)], axis=1)
    cos_c, sin_c = _grid_rope_tables(rows, WIN_HEAD_DIM)
    cos_c2, sin_c2 = jnp.tile(cos_c, (1, 2)), jnp.tile(sin_c, (1, 2))
    one_c, zero_c = jnp.ones((n_c, LANES), F32), jnp.zeros((n_c, LANES), F32)
    assert ffn_w_gate.shape[2] % MXU_TILE == 0
    ffn = dict(wg=ffn_w_gate.astype(BF16), wu=ffn_w_up.astype(BF16), wd=ffn_w_down.astype(BF16))

    h_ctx, h_lat = ctx, x
    for i in range(depth):
        need_ctx = i < depth - 1
        j = i // 2
        m_i = mods[i]
        if i % 2 == 0:
            pw = _prep_proj_a(norm_mix[i], a_w_in[j], mla_qa_norm[j], mla_w_q_b[j], mla_kva_norm[j],
                              mla_w_kv_b[j], mla_q_norm[j], mla_k_norm[j])
            u_l, q_l, k_l, vt_l = _proj_a(h_lat, m_i, None, tm_lat, pw, cs_a_lat)
            u_c, q_c, k_c, vt_c = _proj_a(h_ctx, m_i, b, tm_ctx, pw, cs_a_ctx)
            tables = _s5_tables(s5_lam_re[j], s5_lam_im[j], s5_log_step[j], s5_b_re[j], s5_b_im[j],
                                s5_c_re[j], s5_c_im[j], s5_d[j])
            yg_c, yg_l = _s5(u_c, u_l, tables)
            o_l = _mla_attention(q_l, k_c, vt_c, k_l, vt_l, tq=min(n, 4096))
            post_w = _prep_post(norm_ffn[i], a_w_out[j], s5_w_glu[j], s5_b_glu[j])
            h_lat_new = _post(h_lat, m_i, None, tm_lat, post_w, ffn, i, o_l, yg_l)
            if need_ctx:
                o_c = _mla_attention(q_c, k_c, vt_c, tq=n_c)
                h_ctx = _post(h_ctx, m_i, b, tm_ctx, post_w, ffn, i, o_c, yg_c)
            h_lat = h_lat_new
        else:
            pw = _prep_proj_c(norm_mix[i], c_w_in[j], c_q_norm[j], c_k_norm[j])
            q_l, k_l, vt_l = _proj_c(h_lat, m_i, None, tm_lat, pw, cos_c2, sin_c2)
            q_c, k_c, vt_c = _proj_c(h_ctx, m_i, b, tm_ctx, pw, one_c, zero_c)
            o_l = _win_attention(q_l, k_l, vt_l, k_c, vt_c, c_sink[j], tq=256)
            post_w = _prep_post(norm_ffn[i], c_w_out[j])
            h_lat_new = _post(h_lat, m_i, None, tm_lat, post_w, ffn, i, o_l)
            if need_ctx:
                raise NotImplementedError("context queries of a windowed layer")
            h_lat = h_lat_new
    return h_lat
```

```python
import functools
import math

import jax
import jax.numpy as jnp
from jax import lax
from jax.experimental import pallas as pl
from jax.experimental.pallas import tpu as pltpu

F32 = jnp.float32
BF16 = jnp.bfloat16

GRID_W = 64
NORM_EPS = 1e-6
ROPE_THETA = 10000.0
N_MOD = 6
S5_GROUP_DIM = 16
S5_STATE = 64
S5_CHUNK = 16
MLA_HEADS = 4
MLA_NOPE = 128
MLA_ROPE = 64
MLA_QK_DIM = MLA_NOPE + MLA_ROPE
MLA_V = 128
MLA_Q_RANK = 384
MLA_KV_RANK = 256
MLA_SCALE = MLA_QK_DIM ** -0.5
MLA_HEAD_PAD = 256
WIN_HEADS = 16
WIN_KV_HEADS = 4
WIN_GROUP = WIN_HEADS // WIN_KV_HEADS
WIN_HEAD_DIM = 64
WINDOW = 128
WIN_SCALE = WIN_HEAD_DIM ** -0.5
LANES = 128
SUBLANES = 8
MXU_TILE = 256
VMEM_LIMIT_BYTES = 48 * 1024 * 1024
NEG_BIG = -1e30
LOG2_E = math.log2(math.e)
S5_BLOCK_PAIRS = 4
MLA_Q_BLOCK = 256
MLA_K_PIECE = 256
MOD_ROWS = 8


def _dot(a, b):
    return jnp.dot(a, b, preferred_element_type=F32)


def _dot_nt(a, b):
    return lax.dot_general(a, b, (((1,), (1,)), ((), ())), preferred_element_type=F32)


def _split_bf16(x):
    hi = x.astype(BF16)
    lo = (x - hi.astype(F32)).astype(BF16)
    return hi, lo


def _rms(x, gain):
    return x * lax.rsqrt(jnp.mean(x * x, axis=-1, keepdims=True) + NORM_EPS) * gain


def _silu(x):
    return x * jax.nn.sigmoid(x)


def _params(*sem):
    return pltpu.CompilerParams(dimension_semantics=sem, vmem_limit_bytes=VMEM_LIMIT_BYTES)


def _const_spec(shape, layer=None):
    if layer is None:
        nd = len(shape)
        return pl.BlockSpec(shape, lambda *_: (0,) * nd, pipeline_mode=pl.Buffered(1))
    nd = len(shape) - 1
    return pl.BlockSpec((None,) + tuple(shape[1:]), lambda *_: (layer,) + (0,) * nd,
                        pipeline_mode=pl.Buffered(1))


def _ada_kernel(cond_ref, w_ref, b_ref, o_ref):
    s = _silu(cond_ref[...])
    s_hi, s_lo = _split_bf16(s)
    w_hi, w_lo = _split_bf16(w_ref[...])
    o_ref[...] = _dot(s_hi, w_hi) + _dot(s_lo, w_hi) + _dot(s_hi, w_lo) + b_ref[...]


def _ada_modulation(cond, ada_w, ada_b):
    depth, d, n = ada_w.shape
    tn = 1024
    return pl.pallas_call(
        _ada_kernel,
        grid=(depth, n // tn),
        in_specs=[pl.BlockSpec((MOD_ROWS, d), lambda i, j: (0, 0)),
                  pl.BlockSpec((None, d, tn), lambda i, j: (i, 0, j)),
                  pl.BlockSpec((None, 1, tn), lambda i, j: (i, 0, j))],
        out_specs=pl.BlockSpec((None, MOD_ROWS, tn), lambda i, j: (i, 0, j)),
        out_shape=jax.ShapeDtypeStruct((depth, MOD_ROWS, n), F32),
        compiler_params=_params("arbitrary", "arbitrary"),
        name="ada_modulation",
    )(cond, ada_w, ada_b.reshape(depth, 1, n))


def _mod_specs(d, slots, ctx_row):
    def make(slot):
        if ctx_row is None:
            return pl.BlockSpec((None, None, 1, d), lambda b, i: (b, slot, 0, 0))
        return pl.BlockSpec((None, None, 1, d), lambda b, i: (ctx_row, slot, 0, 0))
    return [make(s) for s in slots]


def _grid_rope_tables(rows, rot_dim):
    n_freq = rot_dim // 4
    inv_freq = ROPE_THETA ** (-jnp.arange(n_freq, dtype=F32) / n_freq)
    ang_r = jnp.arange(rows, dtype=jnp.int32).astype(F32)[:, None] * inv_freq
    ang_c = jnp.arange(GRID_W, dtype=jnp.int32).astype(F32)[:, None] * inv_freq

    def expand(r, c):
        r = jnp.broadcast_to(r[:, None, :], (rows, GRID_W, n_freq))
        c = jnp.broadcast_to(c[None, :, :], (rows, GRID_W, n_freq))
        return jnp.concatenate([r, r, c, c], axis=-1).reshape(rows * GRID_W, rot_dim)

    return expand(jnp.cos(ang_r), jnp.cos(ang_c)), expand(jnp.sin(ang_r), jnp.sin(ang_c))


def _rot_perm_sign(rot_dim):
    q = rot_dim // 4
    idx = jnp.arange(rot_dim)
    perm = jnp.where((idx // q) % 2 == 0, idx + q, idx - q)
    sign = jnp.where((idx // q) % 2 == 0, -1.0, 1.0).astype(F32)
    return perm, sign


def _proj_a_kernel(h_ref, sh_ref, sc_ref, gn_ref, w1_ref, gqa_ref, wq_ref, gkv_ref, wkv_ref, wvt_ref,
                   gq_ref, gk_ref, cs_ref, msk_ref, u_ref, q_ref, k_ref, vt_ref):
    a = _rms(h_ref[...], gn_ref[...]) * (1.0 + sc_ref[...]) + sh_ref[...]
    p1 = _dot(a.astype(BF16), w1_ref[...])
    u_ref[...] = p1[:, :512]
    cq = p1[:, 512:512 + MLA_Q_RANK]
    ckv = p1[:, 896:896 + MLA_KV_RANK]
    krr = p1[:, 1152:1280]
    qb = _dot(_rms(cq, gqa_ref[...]).astype(BF16), wq_ref[...])
    ckv_n = _rms(ckv, gkv_ref[...]).astype(BF16)
    kv = _dot(ckv_n, wkv_ref[...])
    vt_ref[...] = _dot_nt(wvt_ref[...], ckv_n).astype(BF16)
    cs = cs_ref[...]
    msk = msk_ref[...]
    low_half = lax.broadcasted_iota(jnp.int32, cs.shape, 1) < MLA_ROPE

    def finish(xh, gain, out_ref, h):
        ssq = _dot((xh * xh).astype(BF16), msk)
        xn = xh * lax.rsqrt(ssq * (1.0 / MLA_QK_DIM) + NORM_EPS) * gain
        rr = xn[:, LANES:] * cs
        rot = rr + pltpu.roll(rr, MLA_ROPE, axis=1)
        base = h * MLA_HEAD_PAD
        out_ref[:, base:base + LANES] = xn[:, :LANES].astype(BF16)
        out_ref[:, base + LANES:base + 2 * LANES] = jnp.where(low_half, rot, 0.0).astype(BF16)

    for h in range(MLA_HEADS):
        base = h * MLA_HEAD_PAD
        finish(qb[:, base:base + MLA_HEAD_PAD], gq_ref[:, base:base + MLA_HEAD_PAD], q_ref, h)
        kh = jnp.concatenate([kv[:, h * MLA_NOPE:(h + 1) * MLA_NOPE], krr], axis=1)
        finish(kh, gk_ref[:, base:base + MLA_HEAD_PAD], k_ref, h)


def _proj_a(h, mods, ctx_row, tm, wts, cs):
    b, t, d = h.shape
    qw = MLA_HEADS * MLA_HEAD_PAD
    grid = (b, t // tm)
    tile = lambda w: pl.BlockSpec((None, tm, w), lambda bi, i: (bi, i, 0))
    in_specs = ([tile(d)] + _mod_specs(d, (0, 1), ctx_row)
                + [_const_spec(wts[k].shape) for k in
                   ("gn", "w1", "gqa", "wq", "gkv", "wkv", "wvt", "gq", "gk")]
                + [pl.BlockSpec((tm, LANES), lambda bi, i: (i, 0)), _const_spec(wts["msk"].shape)])
    vw = MLA_HEADS * MLA_V
    return pl.pallas_call(
        _proj_a_kernel,
        grid=grid,
        in_specs=in_specs,
        out_specs=[tile(512), tile(qw), tile(qw), pl.BlockSpec((None, vw, tm), lambda bi, i: (bi, 0, i))],
        out_shape=[jax.ShapeDtypeStruct((b, t, 512), F32),
                   jax.ShapeDtypeStruct((b, t, qw), BF16),
                   jax.ShapeDtypeStruct((b, t, qw), BF16),
                   jax.ShapeDtypeStruct((b, vw, t), BF16)],
        compiler_params=_params("parallel", "parallel"),
        name="proj_a",
    )(h, mods, mods, wts["gn"], wts["w1"], wts["gqa"], wts["wq"], wts["gkv"], wts["wkv"], wts["wvt"],
      wts["gq"], wts["gk"], cs, wts["msk"])


def _prep_proj_a(norm_mix, a_w_in, qa_norm, w_q_b, kva_norm, w_kv_b, q_norm, k_norm):
    perm, sign = _rot_perm_sign(MLA_ROPE)
    s5w = a_w_in.shape[1] - (MLA_Q_RANK + MLA_KV_RANK + MLA_ROPE)
    assert s5w == 512
    kr = a_w_in[:, -MLA_ROPE:]
    w1 = jnp.concatenate([a_w_in, kr[:, perm] * sign], axis=1).astype(BF16)
    wq = w_q_b.reshape(MLA_Q_RANK, MLA_HEADS, MLA_QK_DIM)
    rope = wq[:, :, MLA_NOPE:]
    wq = jnp.concatenate([wq, rope[:, :, perm] * sign], axis=2)
    wq = wq.reshape(MLA_Q_RANK, MLA_HEADS * MLA_HEAD_PAD).astype(BF16)
    wkv3 = w_kv_b.reshape(MLA_KV_RANK, MLA_HEADS, MLA_NOPE + MLA_V)
    wkv = wkv3[:, :, :MLA_NOPE].reshape(MLA_KV_RANK, -1).astype(BF16)
    wvt = wkv3[:, :, MLA_NOPE:].reshape(MLA_KV_RANK, -1).T.astype(BF16)

    def head_gain(g, scale):
        gb = jnp.concatenate([g, g[MLA_NOPE:][perm]]) * scale
        return jnp.tile(gb, MLA_HEADS)[None, :]

    rows = jnp.arange(MLA_HEAD_PAD)[:, None] < MLA_QK_DIM
    msk = jnp.broadcast_to(rows, (MLA_HEAD_PAD, MLA_HEAD_PAD)).astype(BF16)
    return dict(gn=norm_mix[None, :], w1=w1, gqa=qa_norm[None, :], wq=wq, gkv=kva_norm[None, :],
                wkv=wkv, wvt=wvt, gq=head_gain(q_norm, MLA_SCALE * LOG2_E), gk=head_gain(k_norm, 1.0),
                msk=msk)


def _s5_kernel(uc_ref, ul_ref, sel_in_ref, sel_out_ref, toep_ref, bst_ref, cst_ref, a_ref, d_ref,
               yc_ref, yl_ref, uhi_ref, ulo_ref, x_ref, z_ref, sin_ref, yb_ref, *, n_ctx, n_all):
    L, S = S5_CHUNK, S5_GROUP_DIM
    n_lat = n_all - n_ctx
    npair = S5_BLOCK_PAIRS
    half = SUBLANES * LANES
    for tl in range(L):
        u = jnp.concatenate([uc_ref[pl.ds(tl, n_ctx, stride=L), :], ul_ref[pl.ds(tl, n_lat, stride=L), :]], axis=0)
        hi, lo = _split_bf16(u)
        uhi_ref[:, tl * LANES:(tl + 1) * LANES] = hi
        ulo_ref[:, tl * LANES:(tl + 1) * LANES] = lo
    for pp in range(npair):
        for hh in range(2):
            xs = (_dot(uhi_ref[:, hh * half:(hh + 1) * half], sel_in_ref[pp])
                  + _dot(ulo_ref[:, hh * half:(hh + 1) * half], sel_in_ref[pp]))
            x_ref[pp, :, hh * LANES:(hh + 1) * LANES] = xs[:, :LANES]
            x_ref[pp, :, MXU_TILE + hh * LANES:MXU_TILE + (hh + 1) * LANES] = xs[:, LANES:]
    for pp in range(npair):
        z_ref[pp] = _dot(x_ref[pp].astype(BF16), bst_ref[pp])
    t_ctx, t_all = n_ctx // SUBLANES, n_all // SUBLANES
    row = lax.broadcasted_iota(jnp.int32, (SUBLANES, LANES), 0)

    def cmul(a_re, a_im, b_re, b_im):
        return a_re * b_re - a_im * b_im, a_re * b_im + a_im * b_re

    def tile_scan(z_re, z_im, c_re, c_im, a_re, a_im, fwd):
        for sft in (1, 2, 4):
            k = sft - 1 if fwd else SUBLANES - sft
            p_re, p_im = a_re[k:k + 1, :], a_im[k:k + 1, :]
            amt = sft if fwd else SUBLANES - sft
            keep = (row >= sft) if fwd else (row < SUBLANES - sft)
            s_re = jnp.where(keep, pltpu.roll(z_re, amt, axis=0), 0.0)
            s_im = jnp.where(keep, pltpu.roll(z_im, amt, axis=0), 0.0)
            m_re, m_im = cmul(p_re, p_im, s_re, s_im)
            z_re, z_im = z_re + m_re, z_im + m_im
        m_re, m_im = cmul(a_re, a_im, c_re, c_im)
        s_re, s_im = z_re + m_re, z_im + m_im
        edge = (row == 0) if fwd else (row == SUBLANES - 1)
        amt = 1 if fwd else SUBLANES - 1
        in_re = jnp.where(edge, c_re, pltpu.roll(s_re, amt, axis=0))
        in_im = jnp.where(edge, c_im, pltpu.roll(s_im, amt, axis=0))
        last = SUBLANES - 1 if fwd else 0
        return in_re, in_im, s_re[last:last + 1, :], s_im[last:last + 1, :]

    def body(it, carry):
        jt = jnp.where(it < t_ctx, t_ctx - 1 - it, t_all + t_ctx - 1 - it)
        rf = pl.multiple_of(it * SUBLANES, SUBLANES)
        rb = pl.multiple_of(jt * SUBLANES, SUBLANES)
        new = []
        for pp in range(npair):
            cf_re, cf_im, cb_re, cb_im = carry[4 * pp:4 * pp + 4]
            zf = z_ref[pp, pl.ds(rf, SUBLANES), 0:2 * LANES]
            zb = z_ref[pp, pl.ds(rb, SUBLANES), 2 * LANES:4 * LANES]
            f_re, f_im, cf_re, cf_im = tile_scan(zf[:, :LANES], zf[:, LANES:], cf_re, cf_im,
                                                 a_ref[pp, 0], a_ref[pp, 1], True)
            b_re, b_im, cb_re, cb_im = tile_scan(zb[:, :LANES], zb[:, LANES:], cb_re, cb_im,
                                                 a_ref[pp, 2], a_ref[pp, 3], False)
            sin_ref[pp, pl.ds(rf, SUBLANES), 0:2 * LANES] = jnp.concatenate([f_re, f_im], axis=1)
            sin_ref[pp, pl.ds(rb, SUBLANES), 2 * LANES:4 * LANES] = jnp.concatenate([b_re, b_im], axis=1)
            new += [cf_re, cf_im, cb_re, cb_im]
        return tuple(new)

    zero = jnp.zeros((1, LANES), F32)
    lax.fori_loop(0, t_all, body, (zero,) * (4 * npair))
    for pp in range(npair):
        x = x_ref[pp]
        xb = x.astype(BF16)
        y = jnp.concatenate([_dot(xb[:, :MXU_TILE], toep_ref[pp, 0]), _dot(xb[:, MXU_TILE:], toep_ref[pp, 1])],
                            axis=1)
        y = y + _dot(sin_ref[pp].astype(BF16), cst_ref[pp]) + x * d_ref[pp]
        yg = jax.nn.gelu(y).astype(BF16)
        for gl in range(2):
            for hh in range(2):
                g8 = 2 * pp + gl
                col = gl * MXU_TILE + hh * LANES
                yb_ref[hh, :, g8 * LANES:(g8 + 1) * LANES] = yg[:, col:col + LANES]
    for hh in range(2):
        for kk in range(SUBLANES // 2):
            two = _dot(yb_ref[hh], sel_out_ref[kk])
            for e in range(2):
                tl = hh * SUBLANES + 2 * kk + e
                yc_ref[pl.ds(tl, n_ctx, stride=L), :] = two[:n_ctx, e * LANES:(e + 1) * LANES]
                yl_ref[pl.ds(tl, n_lat, stride=L), :] = two[n_ctx:, e * LANES:(e + 1) * LANES]


def _s5_selectors():
    S = S5_GROUP_DIM
    r = jnp.arange(SUBLANES * LANES)[:, None]
    c = jnp.arange(2 * LANES)[None, :]
    k, l = r // LANES, r % LANES
    pp = jnp.arange(S5_BLOCK_PAIRS)[:, None, None]
    sel_in = (k == (c % LANES) // S) & (l == 2 * S * pp + S * (c // LANES) + c % S)
    kk = jnp.arange(SUBLANES // 2)[:, None, None]
    sel_out = (k == (c % LANES) // S) & (l == S * (2 * kk + c // LANES) + c % S)
    return sel_in.astype(BF16), sel_out.astype(BF16)


def _dot_nt_f32(a, b):
    a_hi, a_lo = _split_bf16(a)
    b_hi, b_lo = _split_bf16(b)
    return _dot_nt(a_hi, b_hi) + _dot_nt(a_lo, b_hi) + _dot_nt(a_hi, b_lo)


def _s5_table_kernel(par_ref, bre_ref, bim_ref, cre_ref, cim_ref, toep_ref, bst_ref, cst_ref, a_ref):
    L, S, P = S5_CHUNK, S5_GROUP_DIM, S5_STATE
    kk = lax.broadcasted_iota(jnp.int32, (3 * SUBLANES, LANES), 0).astype(F32)
    lane = lax.broadcasted_iota(jnp.int32, (L, LANES), 1)
    rows512 = lax.broadcasted_iota(jnp.int32, (2 * L * S, LANES), 0)
    lanes512 = lax.broadcasted_iota(jnp.int32, (2 * L * S, LANES), 1)
    own_group = (rows512 // (L * S)) == (lanes512 // P)
    lane_pad = jnp.zeros((2 * P, LANES - S), F32)

    def rows_of_powers(pw, ks, groups):
        one = jnp.concatenate([jnp.broadcast_to(pw[k:k + 1, :], (S, LANES)) for k in ks], axis=0)
        return jnp.concatenate([one] * groups, axis=0) if groups > 1 else one

    def cmul(a_re, a_im, b_re, b_im):
        return a_re * b_re - a_im * b_im, a_re * b_im + a_im * b_re

    lag_tables = []
    for d in range(2):
        lam_re, lam_im = par_ref[d, 0:1, :], par_ref[d, 1:2, :]
        step = jnp.exp(par_ref[d, 2:3, :])
        ar, ai = lam_re * step, lam_im * step
        mag = jnp.exp(kk * ar)
        pw_re, pw_im = mag * jnp.cos(kk * ai), mag * jnp.sin(kk * ai)
        th = jnp.tanh(0.5 * ar)
        em1 = 2.0 * th / (1.0 - th)
        sh = jnp.sin(0.5 * ai)
        n_re = em1 * jnp.cos(ai) - 2.0 * sh * sh
        n_im = (em1 + 1.0) * jnp.sin(ai)
        den = lam_re * lam_re + lam_im * lam_im
        co_re = (n_re * lam_re + n_im * lam_im) / den
        co_im = (n_im * lam_re - n_re * lam_im) / den
        bt_re = jnp.concatenate([bre_ref[d], lane_pad], axis=1).T[:S]
        bt_im = jnp.concatenate([bim_ref[d], lane_pad], axis=1).T[:S]
        bb_re, bb_im = cmul(co_re, co_im, bt_re, bt_im)
        cc_re = jnp.concatenate([cre_ref[d, 0], cre_ref[d, 1]], axis=1)
        cc_im = jnp.concatenate([cim_ref[d, 0], cim_ref[d, 1]], axis=1)
        ks = [L - 1 - t for t in range(L)] if d == 0 else list(range(L))
        r_re, r_im = rows_of_powers(pw_re, ks, 2), rows_of_powers(pw_im, ks, 2)
        bbt_re, bbt_im = jnp.concatenate([bb_re] * (2 * L), axis=0), jnp.concatenate([bb_im] * (2 * L), axis=0)
        v_re, v_im = cmul(r_re, r_im, bbt_re, bbt_im)
        bst_ref[:, (2 * d) * LANES:(2 * d + 1) * LANES] = jnp.where(own_group, v_re, 0.0).astype(BF16)
        bst_ref[:, (2 * d + 1) * LANES:(2 * d + 2) * LANES] = jnp.where(own_group, v_im, 0.0).astype(BF16)
        ks = [t + 1 for t in range(L)] if d == 0 else [L - t for t in range(L)]
        r_re, r_im = rows_of_powers(pw_re, ks, 2), rows_of_powers(pw_im, ks, 2)
        cct_re, cct_im = jnp.concatenate([cc_re] * (2 * L), axis=0), jnp.concatenate([cc_im] * (2 * L), axis=0)
        v_re, v_im = cmul(cct_re, cct_im, r_re, r_im)
        cst_ref[(2 * d) * LANES:(2 * d + 1) * LANES, :] = jnp.where(own_group, v_re, 0.0).T.astype(BF16)
        cst_ref[(2 * d + 1) * LANES:(2 * d + 2) * LANES, :] = jnp.where(own_group, -v_im, 0.0).T.astype(BF16)
        ks = list(range(L)) if d == 0 else [L - 1 - j for j in range(L)]
        r_re, r_im = rows_of_powers(pw_re, ks, 1), rows_of_powers(pw_im, ks, 1)
        cl_re, cl_im = cmul(jnp.concatenate([cc_re] * L, axis=0), jnp.concatenate([cc_im] * L, axis=0), r_re, r_im)
        per_group = []
        for g in range(2):
            mine = (lane // P) == g
            per_group.append(_dot_nt_f32(jnp.where(mine, bb_re, 0.0), cl_re)
                             - _dot_nt_f32(jnp.where(mine, bb_im, 0.0), cl_im))
        lag_tables.append(per_group)
        row8 = lax.broadcasted_iota(jnp.int32, (SUBLANES, LANES), 0)
        n_chunks = ((row8 + 1) if d == 0 else (SUBLANES - row8)).astype(F32) * float(L)
        mag8 = jnp.exp(n_chunks * ar)
        a_ref[2 * d] = mag8 * jnp.cos(n_chunks * ai)
        a_ref[2 * d + 1] = mag8 * jnp.sin(n_chunks * ai)

    def shift_right(x, s):
        x0, x1 = x[:, :LANES], x[:, LANES:]
        a, r = divmod(s, LANES)
        r0 = pltpu.roll(x0, r, axis=1) if r else x0
        r1 = pltpu.roll(x1, r, axis=1) if r else x1
        if a == 0:
            return jnp.concatenate([jnp.where(lane >= r, r0, 0.0), jnp.where(lane >= r, r1, r0)], axis=1)
        return jnp.concatenate([jnp.zeros_like(x0), jnp.where(lane >= r, r0, 0.0)], axis=1)

    def shift_left(x, s):
        x0, x1 = x[:, :LANES], x[:, LANES:]
        a, r = divmod(s, LANES)
        r0 = pltpu.roll(x0, LANES - r, axis=1) if r else x0
        r1 = pltpu.roll(x1, LANES - r, axis=1) if r else x1
        if a == 0:
            return jnp.concatenate([jnp.where(lane < LANES - r, r0, r1), jnp.where(lane < LANES - r, r1, 0.0)], axis=1)
        return jnp.concatenate([jnp.where(lane < LANES - r, r1, 0.0), jnp.zeros_like(x0)], axis=1)

    for g in range(2):
        kf, kb = lag_tables[0][g], lag_tables[1][g]
        for tau in range(L):
            blk = shift_right(kf, S * tau) + shift_left(kb, S * (L - 1 - tau))
            toep_ref[g, tau * S:(tau + 1) * S, :] = blk.astype(BF16)


def _s5_tables(lam_re, lam_im, log_step, b_re, b_im, c_re, c_im, d_skip):
    _, G, P = lam_re.shape
    S, L = S5_GROUP_DIM, S5_CHUNK
    assert P == S5_STATE and 2 * P == LANES and 2 * L * S == 2 * MXU_TILE
    pairs = G // 2
    par = jnp.stack([lam_re.reshape(2, pairs, 2 * P), lam_im.reshape(2, pairs, 2 * P),
                     jnp.repeat(log_step, P, axis=-1).reshape(2, pairs, 2 * P)], axis=2)
    par = jnp.transpose(par, (1, 0, 2, 3)).astype(F32)
    bshape = (2, pairs, 2 * P, S)
    cshape = (2, pairs, 2, S, P)
    pw = 2 * L * S
    bspec = pl.BlockSpec((2, None, 2 * P, S), lambda g: (0, g, 0, 0))
    cspec = pl.BlockSpec((2, None, 2, S, P), lambda g: (0, g, 0, 0, 0))
    toep, bst, cst, a_chunk = pl.pallas_call(
        _s5_table_kernel,
        grid=(pairs,),
        in_specs=[pl.BlockSpec((None, 2, 3, 2 * P), lambda g: (g, 0, 0, 0)), bspec, bspec, cspec, cspec],
        out_specs=[pl.BlockSpec((None, 2, MXU_TILE, MXU_TILE), lambda g: (g, 0, 0, 0)),
                   pl.BlockSpec((None, pw, pw), lambda g: (g, 0, 0)),
                   pl.BlockSpec((None, pw, pw), lambda g: (g, 0, 0)),
                   pl.BlockSpec((None, 4, SUBLANES, LANES), lambda g: (g, 0, 0, 0))],
        out_shape=[jax.ShapeDtypeStruct((pairs, 2, MXU_TILE, MXU_TILE), BF16),
                   jax.ShapeDtypeStruct((pairs, pw, pw), BF16),
                   jax.ShapeDtypeStruct((pairs, pw, pw), BF16),
                   jax.ShapeDtypeStruct((pairs, 4, SUBLANES, LANES), F32)],
        compiler_params=_params("parallel"),
        name="s5_tables",
    )(par, b_re.reshape(bshape), b_im.reshape(bshape), c_re.reshape(cshape), c_im.reshape(cshape))
    d_pair = jnp.broadcast_to(d_skip.astype(F32).reshape(pairs, 2, 1, S), (pairs, 2, L, S))
    return toep, bst, cst, a_chunk, d_pair.reshape(pairs, 1, pw)


def _s5(u_ctx, u_lat, tables):
    toep, bst, cst, a_pow, d_pair = tables
    b, n_c, w = u_ctx.shape
    n_l = u_lat.shape[1]
    L = S5_CHUNK
    n_ctx, n_all = n_c // L, (n_c + n_l) // L
    nblk = w // LANES
    pw = 2 * L * S5_GROUP_DIM
    npair = S5_BLOCK_PAIRS
    sel_in, sel_out = _s5_selectors()
    wspec = lambda shape: pl.BlockSpec((npair,) + shape, lambda g, bi: (g,) + (0,) * len(shape),
                                       pipeline_mode=pl.Buffered(1))
    return pl.pallas_call(
        functools.partial(_s5_kernel, n_ctx=n_ctx, n_all=n_all),
        grid=(nblk, b),
        in_specs=[pl.BlockSpec((None, n_c, LANES), lambda g, bi: (bi, 0, g)),
                  pl.BlockSpec((None, n_l, LANES), lambda g, bi: (bi, 0, g)),
                  _const_spec(sel_in.shape), _const_spec(sel_out.shape),
                  wspec((2, MXU_TILE, MXU_TILE)), wspec((pw, pw)), wspec((pw, pw)),
                  wspec((4, SUBLANES, LANES)), wspec((1, pw))],
        out_specs=[pl.BlockSpec((None, n_c, LANES), lambda g, bi: (bi, 0, g)),
                   pl.BlockSpec((None, n_l, LANES), lambda g, bi: (bi, 0, g))],
        out_shape=[jax.ShapeDtypeStruct((b, n_c, w), F32), jax.ShapeDtypeStruct((b, n_l, w), F32)],
        scratch_shapes=[pltpu.VMEM((n_all, L * LANES), BF16), pltpu.VMEM((n_all, L * LANES), BF16),
                        pltpu.VMEM((npair, n_all, pw), F32), pltpu.VMEM((npair, n_all, pw), F32),
                        pltpu.VMEM((npair, n_all, pw), F32), pltpu.VMEM((2, n_all, SUBLANES * LANES), BF16)],
        compiler_params=_params("parallel", "arbitrary"),
        name="s5_scan",
    )(u_ctx, u_lat, sel_in, sel_out, toep, bst, cst, a_pow, d_pair)


def _mla_kernel(*refs, tk, n_steps):
    if n_steps:
        q_ref, kc_ref, vc_ref, k_ref, v_ref, o_ref, qt_ref, s_ref, m_ref, l_ref, acc_ref = refs
    else:
        q_ref, kc_ref, vc_ref, o_ref, qt_ref, s_ref, m_ref, l_ref, acc_ref = refs
    tq = q_ref.shape[0]
    ncb = tq // MLA_Q_BLOCK
    kp = MLA_K_PIECE
    qt_ref[...] = q_ref[...].astype(F32).T.astype(BF16)

    def chunk(load_k, load_vt, nkeys, first):
        nr = nkeys // kp

        def score_piece(c, r):
            st = _dot(load_k(r), qt_ref[:, c * MLA_Q_BLOCK:(c + 1) * MLA_Q_BLOCK])
            s_ref[c % 2, r * kp:(r + 1) * kp, :] = st
            return jnp.max(st, axis=0, keepdims=True)

        def block_stats(c, mx):
            if first:
                return mx, None
            m_old = m_ref[:, c * MLA_Q_BLOCK:(c + 1) * MLA_Q_BLOCK]
            m_new = jnp.maximum(m_old, mx)
            return m_new, jnp.exp2(m_old - m_new)

        def prob_piece(c, r, m_new):
            p = jnp.exp2(s_ref[c % 2, r * kp:(r + 1) * kp, :] - m_new)
            return jnp.sum(p, axis=0, keepdims=True), _dot(load_vt(r), p.astype(BF16))

        def finish(c, m_new, alpha, lsum, pv):
            cols = slice(c * MLA_Q_BLOCK, (c + 1) * MLA_Q_BLOCK)
            if first:
                l_ref[:, cols] = lsum
                acc_ref[:, cols] = pv
            else:
                l_ref[:, cols] = alpha * l_ref[:, cols] + lsum
                acc_ref[:, cols] = alpha * acc_ref[:, cols] + pv
            m_ref[:, cols] = m_new

        mx = None
        for r in range(nr):
            pm = score_piece(0, r)
            mx = pm if mx is None else jnp.maximum(mx, pm)
        for c in range(ncb):
            m_new, alpha = block_stats(c, mx)
            mx = lsum = pv = None
            for r in range(nr):
                if c + 1 < ncb:
                    pm = score_piece(c + 1, r)
                    mx = pm if mx is None else jnp.maximum(mx, pm)
                ls, pvr = prob_piece(c, r, m_new)
                lsum = ls if lsum is None else lsum + ls
                pv = pvr if pv is None else pv + pvr
            finish(c, m_new, alpha, lsum, pv)

    chunk(lambda r: kc_ref[r * kp:(r + 1) * kp, :], lambda r: vc_ref[:, r * kp:(r + 1) * kp],
          kc_ref.shape[0], True)
    if n_steps:
        def body(j, _):
            off = pl.multiple_of(j * tk, tk)
            chunk(lambda r: k_ref[pl.ds(off + r * kp, kp), :], lambda r: v_ref[:, pl.ds(off + r * kp, kp)],
                  tk, False)
            return 0
        lax.fori_loop(0, n_steps, body, 0)
    o_ref[...] = (acc_ref[...] / l_ref[...]).T.astype(BF16)


def _mla_attention(q, k_ctx, vt_ctx, k_lat=None, vt_lat=None, *, tq, tk=1024):
    b, t, _ = q.shape
    n_c = k_ctx.shape[1]
    in_specs = [pl.BlockSpec((None, tq, MLA_HEAD_PAD), lambda bi, h, i: (bi, i, h)),
                pl.BlockSpec((None, n_c, MLA_HEAD_PAD), lambda bi, h, i: (bi, 0, h)),
                pl.BlockSpec((None, MLA_V, n_c), lambda bi, h, i: (bi, h, 0))]
    args = [q, k_ctx, vt_ctx]
    n_steps = 0
    if k_lat is not None:
        n_l = k_lat.shape[1]
        n_steps = n_l // tk
        in_specs += [pl.BlockSpec((None, n_l, MLA_HEAD_PAD), lambda bi, h, i: (bi, 0, h)),
                     pl.BlockSpec((None, MLA_V, n_l), lambda bi, h, i: (bi, h, 0))]
        args += [k_lat, vt_lat]
    return pl.pallas_call(
        functools.partial(_mla_kernel, tk=tk, n_steps=n_steps),
        grid=(b, MLA_HEADS, t // tq),
        in_specs=in_specs,
        out_specs=pl.BlockSpec((None, tq, MLA_V), lambda bi, h, i: (bi, i, h)),
        out_shape=jax.ShapeDtypeStruct((b, t, MLA_HEADS * MLA_V), BF16),
        scratch_shapes=[pltpu.VMEM((MLA_HEAD_PAD, tq), BF16), pltpu.VMEM((2, max(tk, n_c), MLA_Q_BLOCK), F32),
                        pltpu.VMEM((1, tq), F32), pltpu.VMEM((1, tq), F32), pltpu.VMEM((MLA_V, tq), F32)],
        compiler_params=_params("parallel", "parallel", "arbitrary"),
        name="mla_attention",
    )(*args)


def _proj_c_kernel(h_ref, sh_ref, sc_ref, gn_ref, wc_ref, wvt_ref, gqk_ref, cos_ref, sin_ref, bd_ref,
                   q_ref, k_ref, vt_ref):
    a = (_rms(h_ref[...], gn_ref[...]) * (1.0 + sc_ref[...]) + sh_ref[...]).astype(BF16)
    p = _dot(a, wc_ref[...])
    qw = WIN_HEADS * WIN_HEAD_DIM
    kw = WIN_KV_HEADS * WIN_HEAD_DIM
    vt_ref[...] = _dot_nt(wvt_ref[...], a).astype(BF16)
    bd = bd_ref[...]
    cos, sin = cos_ref[...], sin_ref[...]
    lane = lax.broadcasted_iota(jnp.int32, cos.shape, 1)
    first_quarter = (lane % (WIN_HEAD_DIM // 2)) < (WIN_HEAD_DIM // 4)
    for j in range((qw + kw) // MXU_TILE):
        xh = p[:, j * MXU_TILE:(j + 1) * MXU_TILE]
        ssq = _dot((xh * xh).astype(BF16), bd)
        xn = xh * lax.rsqrt(ssq * (1.0 / WIN_HEAD_DIM) + NORM_EPS) * gqk_ref[:, j * MXU_TILE:(j + 1) * MXU_TILE]
        for c in range(MXU_TILE // LANES):
            xc = xn[:, c * LANES:(c + 1) * LANES]
            fwd = pltpu.roll(xc, WIN_HEAD_DIM // 4, axis=1)
            bwd = pltpu.roll(xc, LANES - WIN_HEAD_DIM // 4, axis=1)
            y = (xc * cos + jnp.where(first_quarter, -bwd, fwd) * sin).astype(BF16)
            col = j * MXU_TILE + c * LANES
            if col < qw:
                q_ref[:, col:col + LANES] = y
            else:
                k_ref[:, col - qw:col - qw + LANES] = y


def _proj_c(h, mods, ctx_row, tm, wts, cos2, sin2):
    b, t, d = h.shape
    qw = WIN_HEADS * WIN_HEAD_DIM
    kw = WIN_KV_HEADS * WIN_HEAD_DIM
    tile = lambda w: pl.BlockSpec((None, tm, w), lambda bi, i: (bi, i, 0))
    tab = pl.BlockSpec((tm, LANES), lambda bi, i: (i, 0))
    in_specs = ([tile(d)] + _mod_specs(d, (0, 1), ctx_row)
                + [_const_spec(wts[k].shape) for k in ("gn", "wc", "wvt", "gqk")]
                + [tab, tab, _const_spec(wts["bd"].shape)])
    return pl.pallas_call(
        _proj_c_kernel,
        grid=(b, t // tm),
        in_specs=in_specs,
        out_specs=[tile(qw), tile(kw), pl.BlockSpec((None, kw, tm), lambda bi, i: (bi, 0, i))],
        out_shape=[jax.ShapeDtypeStruct((b, t, qw), BF16),
                   jax.ShapeDtypeStruct((b, t, kw), BF16),
                   jax.ShapeDtypeStruct((b, kw, t), BF16)],
        compiler_params=_params("parallel", "parallel"),
        name="proj_c",
    )(h, mods, mods, wts["gn"], wts["wc"], wts["wvt"], wts["gqk"], cos2, sin2, wts["bd"])


def _prep_proj_c(norm_mix, c_w_in, q_norm, k_norm):
    gqk = jnp.concatenate([jnp.tile(q_norm * (WIN_SCALE * LOG2_E), WIN_HEADS), jnp.tile(k_norm, WIN_KV_HEADS)])
    idx = jnp.arange(MXU_TILE) // WIN_HEAD_DIM
    bd = (idx[:, None] == idx[None, :]).astype(BF16)
    qk = (WIN_HEADS + WIN_KV_HEADS) * WIN_HEAD_DIM
    return dict(gn=norm_mix[None, :], wc=c_w_in[:, :qk].astype(BF16), wvt=c_w_in[:, qk:].T.astype(BF16),
                gqk=gqk[None, :], bd=bd)


def _win_kernel(sink_ref, q_ref, k_ref, vt_ref, kc_ref, vct_ref, o_ref, *, tq, band, n_lat):
    i = pl.program_id(1)
    start = pl.multiple_of(jnp.clip(i * tq - WINDOW, 0, n_lat - band), WINDOW)
    hd, grp = WIN_HEAD_DIM, WIN_GROUP
    qt = q_ref[...].astype(F32).T.astype(BF16)
    k_pos = start + lax.broadcasted_iota(jnp.int32, (band, tq), 0)
    q_pos = i * tq + lax.broadcasted_iota(jnp.int32, (band, tq), 1)
    bias1 = jnp.where(jnp.abs(k_pos - q_pos) <= WINDOW, 0.0, NEG_BIG)
    bias = jnp.concatenate([bias1] * grp, axis=1)
    zeros = jnp.zeros((hd, grp * tq), BF16)
    outs = []
    for kv in range(WIN_KV_HEADS):
        qg = jnp.concatenate([qt[(kv * grp + g) * hd:(kv * grp + g + 1) * hd, :] for g in range(grp)], axis=1)
        qg = jnp.concatenate([qg, zeros] if kv % 2 == 0 else [zeros, qg], axis=0)
        col = (kv // 2) * LANES
        s_ctx = _dot(kc_ref[:, col:col + LANES], qg)
        s_loc = _dot(k_ref[pl.ds(start, band), col:col + LANES], qg) + bias
        sink = jnp.concatenate([jnp.full((1, tq), sink_ref[kv * grp + g], F32) for g in range(grp)], axis=1)
        m = jnp.maximum(jnp.maximum(jnp.max(s_loc, axis=0, keepdims=True),
                                    jnp.max(s_ctx, axis=0, keepdims=True)), sink)
        p_loc = jnp.exp2(s_loc - m)
        p_ctx = jnp.exp2(s_ctx - m)
        den = (jnp.sum(p_loc, axis=0, keepdims=True) + jnp.sum(p_ctx, axis=0, keepdims=True)
               + jnp.exp2(sink - m))
        ot = (_dot(vct_ref[kv * hd:(kv + 1) * hd, :], p_ctx.astype(BF16))
              + _dot(vt_ref[kv * hd:(kv + 1) * hd, pl.ds(start, band)], p_loc.astype(BF16))) / den
        outs += [ot[:, g * tq:(g + 1) * tq] for g in range(grp)]
    o_ref[...] = jnp.concatenate(outs, axis=0).T.astype(BF16)


def _win_attention(q, k, vt, k_ctx, vt_ctx, sink, *, tq):
    b, n, qw = q.shape
    n_c = k_ctx.shape[1]
    kw = k.shape[2]
    band = tq + 2 * WINDOW
    full = lambda r, w: pl.BlockSpec((None, r, w), lambda bi, i: (bi, 0, 0))
    return pl.pallas_call(
        functools.partial(_win_kernel, tq=tq, band=band, n_lat=n),
        grid=(b, n // tq),
        in_specs=[pl.BlockSpec(memory_space=pltpu.SMEM),
                  pl.BlockSpec((None, tq, qw), lambda bi, i: (bi, i, 0)),
                  full(n, kw), full(kw, n), full(n_c, kw), full(kw, n_c)],
        out_specs=pl.BlockSpec((None, tq, qw), lambda bi, i: (bi, i, 0)),
        out_shape=jax.ShapeDtypeStruct((b, n, qw), BF16),
        compiler_params=_params("parallel", "arbitrary"),
        name="win_attention",
    )(sink.astype(F32) * LOG2_E, q, k, vt, k_ctx, vt_ctx)


def _post_kernel(*refs, s5_width, n_chunks):
    if s5_width:
        (h_ref, g_ref, sh_ref, sc_ref, g2_ref, gn_ref, yg_ref, o_ref, wglu_ref, bglu_ref, wo_ref,
         wg_ref, wu_ref, wd_ref, out_ref, a_ref, acc_ref) = refs
        yg = yg_ref[...]
        s5 = yg * jax.nn.sigmoid(_dot(yg.astype(BF16), wglu_ref[...]) + bglu_ref[...])
        mix = _dot(s5.astype(BF16), wo_ref[:s5_width, :]) + _dot(o_ref[...], wo_ref[s5_width:, :])
    else:
        (h_ref, g_ref, sh_ref, sc_ref, g2_ref, gn_ref, o_ref, wo_ref,
         wg_ref, wu_ref, wd_ref, out_ref, a_ref, acc_ref) = refs
        mix = _dot(o_ref[...], wo_ref[...])
    h1 = h_ref[...] + g_ref[...] * mix
    a_ref[...] = (_rms(h1, gn_ref[...]) * (1.0 + sc_ref[...]) + sh_ref[...]).astype(BF16)
    acc_ref[...] = jnp.zeros_like(acc_ref)

    def body(c, _):
        a = a_ref[...]
        cols = pl.ds(pl.multiple_of(c * MXU_TILE, MXU_TILE), MXU_TILE)
        act = _silu(_dot(a, wg_ref[:, cols])) * _dot(a, wu_ref[:, cols])
        acc_ref[...] += _dot(act.astype(BF16), wd_ref[cols, :])
        return 0

    lax.fori_loop(0, n_chunks, body, 0, unroll=True)
    out_ref[...] = h1 + g2_ref[...] * acc_ref[...]


def _post(h, mods, ctx_row, tm, wts, ffn, layer, o, yg=None):
    b, t, d = h.shape
    tile = lambda w: pl.BlockSpec((None, tm, w), lambda bi, i: (bi, i, 0))
    s5_width = 0 if yg is None else yg.shape[2]
    in_specs = [tile(d)] + _mod_specs(d, (2, 3, 4, 5), ctx_row) + [_const_spec(wts["gn"].shape)]
    args = [h, mods, mods, mods, mods, wts["gn"]]
    if yg is not None:
        in_specs += [tile(s5_width), tile(o.shape[2]), _const_spec(wts["wglu"].shape),
                     _const_spec(wts["bglu"].shape)]
        args += [yg, o, wts["wglu"], wts["bglu"]]
    else:
        in_specs += [tile(o.shape[2])]
        args += [o]
    in_specs.append(_const_spec(wts["wo"].shape))
    args.append(wts["wo"])
    for k in ("wg", "wu", "wd"):
        in_specs.append(_const_spec(ffn[k].shape, layer))
        args.append(ffn[k])
    return pl.pallas_call(
        functools.partial(_post_kernel, s5_width=s5_width, n_chunks=ffn["wg"].shape[2] // MXU_TILE),
        grid=(b, t // tm),
        in_specs=in_specs,
        out_specs=tile(d),
        out_shape=jax.ShapeDtypeStruct((b, t, d), F32),
        scratch_shapes=[pltpu.VMEM((tm, d), BF16), pltpu.VMEM((tm, d), F32)],
        compiler_params=_params("parallel", "parallel"),
        name="post_ffn",
    )(*args)


def _prep_post(norm_ffn, w_out, w_glu=None, b_glu=None):
    wts = dict(gn=norm_ffn[None, :], wo=w_out.astype(BF16))
    if w_glu is not None:
        wts.update(wglu=w_glu.astype(BF16), bglu=b_glu[None, :])
    return wts


def kernel(x, c, ctx, c_ctx, ada_w, ada_b, norm_mix, norm_ffn, ffn_w_gate, ffn_w_up, ffn_w_down,
           a_w_in, a_w_out, s5_lam_re, s5_lam_im, s5_log_step, s5_b_re, s5_b_im, s5_c_re, s5_c_im,
           s5_d, s5_w_glu, s5_b_glu, mla_qa_norm, mla_w_q_b, mla_kva_norm, mla_w_kv_b,
           mla_q_norm, mla_k_norm, c_w_in, c_w_out, c_q_norm, c_k_norm, c_sink):
    b, n, d = x.shape
    n_c = ctx.shape[1]
    depth = ada_w.shape[0]
    assert b + 1 <= MOD_ROWS and n % 1024 == 0 and n_c % (S5_CHUNK * SUBLANES) == 0
    rows = n // GRID_W
    tm_lat, tm_ctx = 512, n_c

    cond = jnp.zeros((MOD_ROWS, d), F32).at[:b].set(c).at[b].set(c_ctx)
    mods = _ada_modulation(cond, ada_w, ada_b)
    mods = mods.reshape(depth, MOD_ROWS, N_MOD, 1, d)

    cos_a, sin_a = _grid_rope_tables(rows, MLA_ROPE)
    cs_a_lat = jnp.concatenate([cos_a, sin_a], axis=1)
    cs_a_ctx = jnp.concatenate([jnp.ones((n_c, MLA_ROPE), F32), jnp.zeros((n_c, MLA_ROPE), F32)], axis=1)
    cos_c, sin_c = _grid_rope_tables(rows, WIN_HEAD_DIM)
    cos_c2, sin_c2 = jnp.tile(cos_c, (1, 2)), jnp.tile(sin_c, (1, 2))
    one_c, zero_c = jnp.ones((n_c, LANES), F32), jnp.zeros((n_c, LANES), F32)
    assert ffn_w_gate.shape[2] % MXU_TILE == 0
    ffn = dict(wg=ffn_w_gate.astype(BF16), wu=ffn_w_up.astype(BF16), wd=ffn_w_down.astype(BF16))

    h_ctx, h_lat = ctx, x
    for i in range(depth):
        need_ctx = i < depth - 1
        j = i // 2
        m_i = mods[i]
        if i % 2 == 0:
            pw = _prep_proj_a(norm_mix[i], a_w_in[j], mla_qa_norm[j], mla_w_q_b[j], mla_kva_norm[j],
                              mla_w_kv_b[j], mla_q_norm[j], mla_k_norm[j])
            u_l, q_l, k_l, vt_l = _proj_a(h_lat, m_i, None, tm_lat, pw, cs_a_lat)
            u_c, q_c, k_c, vt_c = _proj_a(h_ctx, m_i, b, tm_ctx, pw, cs_a_ctx)
            tables = _s5_tables(s5_lam_re[j], s5_lam_im[j], s5_log_step[j], s5_b_re[j], s5_b_im[j],
                                s5_c_re[j], s5_c_im[j], s5_d[j])
            yg_c, yg_l = _s5(u_c, u_l, tables)
            o_l = _mla_attention(q_l, k_c, vt_c, k_l, vt_l, tq=min(n, 4096))
            post_w = _prep_post(norm_ffn[i], a_w_out[j], s5_w_glu[j], s5_b_glu[j])
            h_lat_new = _post(h_lat, m_i, None, tm_lat, post_w, ffn, i, o_l, yg_l)
            if need_ctx:
                o_c = _mla_attention(q_c, k_c, vt_c, tq=n_c)
                h_ctx = _post(h_ctx, m_i, b, tm_ctx, post_w, ffn, i, o_c, yg_c)
            h_lat = h_lat_new
        else:
            pw = _prep_proj_c(norm_mix[i], c_w_in[j], c_q_norm[j], c_k_norm[j])
            q_l, k_l, vt_l = _proj_c(h_lat, m_i, None, tm_lat, pw, cos_c2, sin_c2)
            q_c, k_c, vt_c = _proj_c(h_ctx, m_i, b, tm_ctx, pw, one_c, zero_c)
            o_l = _win_attention(q_l, k_l, vt_l, k_c, vt_c, c_sink[j], tq=256)
            post_w = _prep_post(norm_ffn[i], c_w_out[j])
            h_lat_new = _post(h_lat, m_i, None, tm_lat, post_w, ffn, i, o_l)
            if need_ctx:
                raise NotImplementedError("context queries of a windowed layer")
            h_lat = h_lat_new
    return h_lat
```

```python
import functools
import math

import jax
import jax.numpy as jnp
from jax import lax
from jax.experimental import pallas as pl
from jax.experimental.pallas import tpu as pltpu

F32 = jnp.float32
BF16 = jnp.bfloat16

GRID_W = 64
NORM_EPS = 1e-6
ROPE_THETA = 10000.0
N_MOD = 6
S5_GROUP_DIM = 16
S5_STATE = 64
S5_CHUNK = 16
MLA_HEADS = 4
MLA_NOPE = 128
MLA_ROPE = 64
MLA_QK_DIM = MLA_NOPE + MLA_ROPE
MLA_V = 128
MLA_Q_RANK = 384
MLA_KV_RANK = 256
MLA_SCALE = MLA_QK_DIM ** -0.5
MLA_HEAD_PAD = 256
WIN_HEADS = 16
WIN_KV_HEADS = 4
WIN_GROUP = WIN_HEADS // WIN_KV_HEADS
WIN_HEAD_DIM = 64
WINDOW = 128
WIN_SCALE = WIN_HEAD_DIM ** -0.5
LANES = 128
SUBLANES = 8
MXU_TILE = 256
VMEM_LIMIT_BYTES = 48 * 1024 * 1024
NEG_BIG = -1e30
LOG2_E = math.log2(math.e)
PROJ_ROWS = 256
SOFTMAX_DEN_MIN = 2.0 ** -60
SOFTMAX_DEN_MAX = 2.0 ** 60
S5_BLOCK_PAIRS = 4
MLA_Q_BLOCK = 256
MLA_K_PIECE = 256
MOD_ROWS = 8


def _dot(a, b):
    return jnp.dot(a, b, preferred_element_type=F32)


def _dot_nt(a, b):
    return lax.dot_general(a, b, (((1,), (1,)), ((), ())), preferred_element_type=F32)


def _split_bf16(x):
    hi = x.astype(BF16)
    lo = (x - hi.astype(F32)).astype(BF16)
    return hi, lo


def _rms(x, gain):
    return x * lax.rsqrt(jnp.mean(x * x, axis=-1, keepdims=True) + NORM_EPS) * gain


def _silu(x):
    return x * jax.nn.sigmoid(x)


def _params(*sem):
    return pltpu.CompilerParams(dimension_semantics=sem, vmem_limit_bytes=VMEM_LIMIT_BYTES)


def _const_spec(shape, layer=None):
    if layer is None:
        nd = len(shape)
        return pl.BlockSpec(shape, lambda *_: (0,) * nd, pipeline_mode=pl.Buffered(1))
    nd = len(shape) - 1
    return pl.BlockSpec((None,) + tuple(shape[1:]), lambda *_: (layer,) + (0,) * nd,
                        pipeline_mode=pl.Buffered(1))


def _ada_kernel(cond_ref, w_ref, b_ref, o_ref):
    s = _silu(cond_ref[...])
    s_hi, s_lo = _split_bf16(s)
    w_hi, w_lo = _split_bf16(w_ref[...])
    o_ref[...] = _dot(s_hi, w_hi) + _dot(s_lo, w_hi) + _dot(s_hi, w_lo) + b_ref[...]


def _ada_modulation(cond, ada_w, ada_b):
    depth, d, n = ada_w.shape
    tn = 1024
    return pl.pallas_call(
        _ada_kernel,
        grid=(depth, n // tn),
        in_specs=[pl.BlockSpec((MOD_ROWS, d), lambda i, j: (0, 0)),
                  pl.BlockSpec((None, d, tn), lambda i, j: (i, 0, j)),
                  pl.BlockSpec((None, 1, tn), lambda i, j: (i, 0, j))],
        out_specs=pl.BlockSpec((None, MOD_ROWS, tn), lambda i, j: (i, 0, j)),
        out_shape=jax.ShapeDtypeStruct((depth, MOD_ROWS, n), F32),
        compiler_params=_params("arbitrary", "arbitrary"),
        name="ada_modulation",
    )(cond, ada_w, ada_b.reshape(depth, 1, n))


def _mod_specs(d, slots, ctx_row):
    def make(slot):
        if ctx_row is None:
            return pl.BlockSpec((None, None, 1, d), lambda b, i: (b, slot, 0, 0))
        return pl.BlockSpec((None, None, 1, d), lambda b, i: (ctx_row, slot, 0, 0))
    return [make(s) for s in slots]


def _grid_rope_tables(rows, rot_dim):
    n_freq = rot_dim // 4
    inv_freq = ROPE_THETA ** (-jnp.arange(n_freq, dtype=F32) / n_freq)
    ang_r = jnp.arange(rows, dtype=jnp.int32).astype(F32)[:, None] * inv_freq
    ang_c = jnp.arange(GRID_W, dtype=jnp.int32).astype(F32)[:, None] * inv_freq

    def expand(r, c):
        r = jnp.broadcast_to(r[:, None, :], (rows, GRID_W, n_freq))
        c = jnp.broadcast_to(c[None, :, :], (rows, GRID_W, n_freq))
        return jnp.concatenate([r, r, c, c], axis=-1).reshape(rows * GRID_W, rot_dim)

    return expand(jnp.cos(ang_r), jnp.cos(ang_c)), expand(jnp.sin(ang_r), jnp.sin(ang_c))


def _rot_perm_sign(rot_dim):
    q = rot_dim // 4
    idx = jnp.arange(rot_dim)
    perm = jnp.where((idx // q) % 2 == 0, idx + q, idx - q)
    sign = jnp.where((idx // q) % 2 == 0, -1.0, 1.0).astype(F32)
    return perm, sign


def _proj_a_kernel(h_ref, sh_ref, sc_ref, gn_ref, w1_ref, gqa_ref, wq_ref, gkv_ref, wkv_ref, wvt_ref,
                   gq_ref, gk_ref, cs_ref, msk_ref, u_ref, q_ref, k_ref, vt_ref):
    tm = h_ref.shape[0]
    rb = min(PROJ_ROWS, tm)
    msk = msk_ref[...]
    low_half = lax.broadcasted_iota(jnp.int32, (rb, LANES), 1) < MLA_ROPE

    def finish(xh, gain, cs, out_ref, rows, h):
        ssq = _dot((xh * xh).astype(BF16), msk)
        xn = xh * lax.rsqrt(ssq * (1.0 / MLA_QK_DIM) + NORM_EPS) * gain
        rr = xn[:, LANES:] * cs
        rot = rr + pltpu.roll(rr, MLA_ROPE, axis=1)
        base = h * MLA_HEAD_PAD
        out_ref[rows, base:base + LANES] = xn[:, :LANES].astype(BF16)
        out_ref[rows, base + LANES:base + 2 * LANES] = jnp.where(low_half, rot, 0.0).astype(BF16)

    for r in range(tm // rb):
        rows = slice(r * rb, (r + 1) * rb)
        a = _rms(h_ref[rows, :], gn_ref[...]) * (1.0 + sc_ref[...]) + sh_ref[...]
        p1 = _dot(a.astype(BF16), w1_ref[...])
        u_ref[rows, :] = p1[:, :512]
        cq = p1[:, 512:512 + MLA_Q_RANK]
        ckv = p1[:, 896:896 + MLA_KV_RANK]
        krr = p1[:, 1152:1280]
        qb = _dot(_rms(cq, gqa_ref[...]).astype(BF16), wq_ref[...])
        ckv_n = _rms(ckv, gkv_ref[...]).astype(BF16)
        kv = _dot(ckv_n, wkv_ref[...])
        vt_ref[:, rows] = _dot_nt(wvt_ref[...], ckv_n).astype(BF16)
        cs = cs_ref[rows, :]
        for h in range(MLA_HEADS):
            base = h * MLA_HEAD_PAD
            finish(qb[:, base:base + MLA_HEAD_PAD], gq_ref[:, base:base + MLA_HEAD_PAD], cs, q_ref, rows, h)
            kh = jnp.concatenate([kv[:, h * MLA_NOPE:(h + 1) * MLA_NOPE], krr], axis=1)
            finish(kh, gk_ref[:, base:base + MLA_HEAD_PAD], cs, k_ref, rows, h)


def _proj_a(h, mods, ctx_row, tm, wts, cs):
    b, t, d = h.shape
    qw = MLA_HEADS * MLA_HEAD_PAD
    grid = (b, t // tm)
    tile = lambda w: pl.BlockSpec((None, tm, w), lambda bi, i: (bi, i, 0))
    in_specs = ([tile(d)] + _mod_specs(d, (0, 1), ctx_row)
                + [_const_spec(wts[k].shape) for k in
                   ("gn", "w1", "gqa", "wq", "gkv", "wkv", "wvt", "gq", "gk")]
                + [pl.BlockSpec((tm, LANES), lambda bi, i: (i, 0)), _const_spec(wts["msk"].shape)])
    vw = MLA_HEADS * MLA_V
    return pl.pallas_call(
        _proj_a_kernel,
        grid=grid,
        in_specs=in_specs,
        out_specs=[tile(512), tile(qw), tile(qw), pl.BlockSpec((None, vw, tm), lambda bi, i: (bi, 0, i))],
        out_shape=[jax.ShapeDtypeStruct((b, t, 512), F32),
                   jax.ShapeDtypeStruct((b, t, qw), BF16),
                   jax.ShapeDtypeStruct((b, t, qw), BF16),
                   jax.ShapeDtypeStruct((b, vw, t), BF16)],
        compiler_params=_params("parallel", "parallel"),
        name="proj_a",
    )(h, mods, mods, wts["gn"], wts["w1"], wts["gqa"], wts["wq"], wts["gkv"], wts["wkv"], wts["wvt"],
      wts["gq"], wts["gk"], cs, wts["msk"])


def _prep_proj_a(norm_mix, a_w_in, qa_norm, w_q_b, kva_norm, w_kv_b, q_norm, k_norm):
    perm, sign = _rot_perm_sign(MLA_ROPE)
    s5w = a_w_in.shape[1] - (MLA_Q_RANK + MLA_KV_RANK + MLA_ROPE)
    assert s5w == 512
    kr = a_w_in[:, -MLA_ROPE:]
    w1 = jnp.concatenate([a_w_in, kr[:, perm] * sign], axis=1).astype(BF16)
    wq = w_q_b.reshape(MLA_Q_RANK, MLA_HEADS, MLA_QK_DIM)
    rope = wq[:, :, MLA_NOPE:]
    wq = jnp.concatenate([wq, rope[:, :, perm] * sign], axis=2)
    wq = wq.reshape(MLA_Q_RANK, MLA_HEADS * MLA_HEAD_PAD).astype(BF16)
    wkv3 = w_kv_b.reshape(MLA_KV_RANK, MLA_HEADS, MLA_NOPE + MLA_V)
    wkv = wkv3[:, :, :MLA_NOPE].reshape(MLA_KV_RANK, -1).astype(BF16)
    wvt = wkv3[:, :, MLA_NOPE:].reshape(MLA_KV_RANK, -1).T.astype(BF16)

    def head_gain(g, scale):
        gb = jnp.concatenate([g, g[MLA_NOPE:][perm]]) * scale
        return jnp.tile(gb, MLA_HEADS)[None, :]

    rows = jnp.arange(MLA_HEAD_PAD)[:, None] < MLA_QK_DIM
    msk = jnp.broadcast_to(rows, (MLA_HEAD_PAD, MLA_HEAD_PAD)).astype(BF16)
    return dict(gn=norm_mix[None, :], w1=w1, gqa=qa_norm[None, :], wq=wq, gkv=kva_norm[None, :],
                wkv=wkv, wvt=wvt, gq=head_gain(q_norm, MLA_SCALE * LOG2_E), gk=head_gain(k_norm, 1.0),
                msk=msk)


def _s5_kernel(uc_ref, ul_ref, sel_in_ref, sel_out_ref, toep_ref, bst_ref, cst_ref, a_ref, d_ref,
               yc_ref, yl_ref, uhi_ref, ulo_ref, x_ref, z_ref, sin_ref, yb_ref, *, n_ctx, n_all):
    L, S = S5_CHUNK, S5_GROUP_DIM
    n_lat = n_all - n_ctx
    npair = S5_BLOCK_PAIRS
    half = SUBLANES * LANES
    for tl in range(L):
        u = jnp.concatenate([uc_ref[pl.ds(tl, n_ctx, stride=L), :], ul_ref[pl.ds(tl, n_lat, stride=L), :]], axis=0)
        hi, lo = _split_bf16(u)
        uhi_ref[:, tl * LANES:(tl + 1) * LANES] = hi
        ulo_ref[:, tl * LANES:(tl + 1) * LANES] = lo
    for pp in range(npair):
        for hh in range(2):
            xs = (_dot(uhi_ref[:, hh * half:(hh + 1) * half], sel_in_ref[pp])
                  + _dot(ulo_ref[:, hh * half:(hh + 1) * half], sel_in_ref[pp]))
            x_ref[pp, :, hh * LANES:(hh + 1) * LANES] = xs[:, :LANES]
            x_ref[pp, :, MXU_TILE + hh * LANES:MXU_TILE + (hh + 1) * LANES] = xs[:, LANES:]
    for pp in range(npair):
        z_ref[pp] = _dot(x_ref[pp].astype(BF16), bst_ref[pp])
    t_ctx, t_all = n_ctx // SUBLANES, n_all // SUBLANES
    row = lax.broadcasted_iota(jnp.int32, (SUBLANES, LANES), 0)

    def cmul(a_re, a_im, b_re, b_im):
        return a_re * b_re - a_im * b_im, a_re * b_im + a_im * b_re

    def tile_scan(z_re, z_im, c_re, c_im, a_re, a_im, fwd):
        for sft in (1, 2, 4):
            k = sft - 1 if fwd else SUBLANES - sft
            p_re, p_im = a_re[k:k + 1, :], a_im[k:k + 1, :]
            amt = sft if fwd else SUBLANES - sft
            keep = (row >= sft) if fwd else (row < SUBLANES - sft)
            s_re = jnp.where(keep, pltpu.roll(z_re, amt, axis=0), 0.0)
            s_im = jnp.where(keep, pltpu.roll(z_im, amt, axis=0), 0.0)
            m_re, m_im = cmul(p_re, p_im, s_re, s_im)
            z_re, z_im = z_re + m_re, z_im + m_im
        m_re, m_im = cmul(a_re, a_im, c_re, c_im)
        s_re, s_im = z_re + m_re, z_im + m_im
        edge = (row == 0) if fwd else (row == SUBLANES - 1)
        amt = 1 if fwd else SUBLANES - 1
        in_re = jnp.where(edge, c_re, pltpu.roll(s_re, amt, axis=0))
        in_im = jnp.where(edge, c_im, pltpu.roll(s_im, amt, axis=0))
        last = SUBLANES - 1 if fwd else 0
        return in_re, in_im, s_re[last:last + 1, :], s_im[last:last + 1, :]

    def body(it, carry):
        jt = jnp.where(it < t_ctx, t_ctx - 1 - it, t_all + t_ctx - 1 - it)
        rf = pl.multiple_of(it * SUBLANES, SUBLANES)
        rb = pl.multiple_of(jt * SUBLANES, SUBLANES)
        new = []
        for pp in range(npair):
            cf_re, cf_im, cb_re, cb_im = carry[4 * pp:4 * pp + 4]
            zf = z_ref[pp, pl.ds(rf, SUBLANES), 0:2 * LANES]
            zb = z_ref[pp, pl.ds(rb, SUBLANES), 2 * LANES:4 * LANES]
            f_re, f_im, cf_re, cf_im = tile_scan(zf[:, :LANES], zf[:, LANES:], cf_re, cf_im,
                                                 a_ref[pp, 0], a_ref[pp, 1], True)
            b_re, b_im, cb_re, cb_im = tile_scan(zb[:, :LANES], zb[:, LANES:], cb_re, cb_im,
                                                 a_ref[pp, 2], a_ref[pp, 3], False)
            sin_ref[pp, pl.ds(rf, SUBLANES), 0:2 * LANES] = jnp.concatenate([f_re, f_im], axis=1)
            sin_ref[pp, pl.ds(rb, SUBLANES), 2 * LANES:4 * LANES] = jnp.concatenate([b_re, b_im], axis=1)
            new += [cf_re, cf_im, cb_re, cb_im]
        return tuple(new)

    zero = jnp.zeros((1, LANES), F32)
    lax.fori_loop(0, t_all, body, (zero,) * (4 * npair))
    for pp in range(npair):
        x = x_ref[pp]
        xb = x.astype(BF16)
        y = jnp.concatenate([_dot(xb[:, :MXU_TILE], toep_ref[pp, 0]), _dot(xb[:, MXU_TILE:], toep_ref[pp, 1])],
                            axis=1)
        y = y + _dot(sin_ref[pp].astype(BF16), cst_ref[pp]) + x * d_ref[pp]
        yg = jax.nn.gelu(y).astype(BF16)
        for gl in range(2):
            for hh in range(2):
                g8 = 2 * pp + gl
                col = gl * MXU_TILE + hh * LANES
                yb_ref[hh, :, g8 * LANES:(g8 + 1) * LANES] = yg[:, col:col + LANES]
    for hh in range(2):
        for kk in range(SUBLANES // 2):
            two = _dot(yb_ref[hh], sel_out_ref[kk])
            for e in range(2):
                tl = hh * SUBLANES + 2 * kk + e
                yc_ref[pl.ds(tl, n_ctx, stride=L), :] = two[:n_ctx, e * LANES:(e + 1) * LANES]
                yl_ref[pl.ds(tl, n_lat, stride=L), :] = two[n_ctx:, e * LANES:(e + 1) * LANES]


def _s5_selectors():
    S = S5_GROUP_DIM
    r = jnp.arange(SUBLANES * LANES)[:, None]
    c = jnp.arange(2 * LANES)[None, :]
    k, l = r // LANES, r % LANES
    pp = jnp.arange(S5_BLOCK_PAIRS)[:, None, None]
    sel_in = (k == (c % LANES) // S) & (l == 2 * S * pp + S * (c // LANES) + c % S)
    kk = jnp.arange(SUBLANES // 2)[:, None, None]
    sel_out = (k == (c % LANES) // S) & (l == S * (2 * kk + c // LANES) + c % S)
    return sel_in.astype(BF16), sel_out.astype(BF16)


def _dot_nt_f32(a, b):
    a_hi, a_lo = _split_bf16(a)
    b_hi, b_lo = _split_bf16(b)
    return _dot_nt(a_hi, b_hi) + _dot_nt(a_lo, b_hi) + _dot_nt(a_hi, b_lo)


def _s5_table_kernel(par_ref, bre_ref, bim_ref, cre_ref, cim_ref, toep_ref, bst_ref, cst_ref, a_ref):
    L, S, P = S5_CHUNK, S5_GROUP_DIM, S5_STATE
    kk = lax.broadcasted_iota(jnp.int32, (3 * SUBLANES, LANES), 0).astype(F32)
    lane = lax.broadcasted_iota(jnp.int32, (L, LANES), 1)
    rows512 = lax.broadcasted_iota(jnp.int32, (2 * L * S, LANES), 0)
    lanes512 = lax.broadcasted_iota(jnp.int32, (2 * L * S, LANES), 1)
    own_group = (rows512 // (L * S)) == (lanes512 // P)
    lane_pad = jnp.zeros((2 * P, LANES - S), F32)

    def rows_of_powers(pw, ks, groups):
        one = jnp.concatenate([jnp.broadcast_to(pw[k:k + 1, :], (S, LANES)) for k in ks], axis=0)
        return jnp.concatenate([one] * groups, axis=0) if groups > 1 else one

    def cmul(a_re, a_im, b_re, b_im):
        return a_re * b_re - a_im * b_im, a_re * b_im + a_im * b_re

    lag_tables = []
    for d in range(2):
        lam_re, lam_im = par_ref[d, 0:1, :], par_ref[d, 1:2, :]
        step = jnp.exp(par_ref[d, 2:3, :])
        ar, ai = lam_re * step, lam_im * step
        mag = jnp.exp(kk * ar)
        pw_re, pw_im = mag * jnp.cos(kk * ai), mag * jnp.sin(kk * ai)
        th = jnp.tanh(0.5 * ar)
        em1 = 2.0 * th / (1.0 - th)
        sh = jnp.sin(0.5 * ai)
        n_re = em1 * jnp.cos(ai) - 2.0 * sh * sh
        n_im = (em1 + 1.0) * jnp.sin(ai)
        den = lam_re * lam_re + lam_im * lam_im
        co_re = (n_re * lam_re + n_im * lam_im) / den
        co_im = (n_im * lam_re - n_re * lam_im) / den
        bt_re = jnp.concatenate([bre_ref[d], lane_pad], axis=1).T[:S]
        bt_im = jnp.concatenate([bim_ref[d], lane_pad], axis=1).T[:S]
        bb_re, bb_im = cmul(co_re, co_im, bt_re, bt_im)
        cc_re = jnp.concatenate([cre_ref[d, 0], cre_ref[d, 1]], axis=1)
        cc_im = jnp.concatenate([cim_ref[d, 0], cim_ref[d, 1]], axis=1)
        ks = [L - 1 - t for t in range(L)] if d == 0 else list(range(L))
        r_re, r_im = rows_of_powers(pw_re, ks, 2), rows_of_powers(pw_im, ks, 2)
        bbt_re, bbt_im = jnp.concatenate([bb_re] * (2 * L), axis=0), jnp.concatenate([bb_im] * (2 * L), axis=0)
        v_re, v_im = cmul(r_re, r_im, bbt_re, bbt_im)
        bst_ref[:, (2 * d) * LANES:(2 * d + 1) * LANES] = jnp.where(own_group, v_re, 0.0).astype(BF16)
        bst_ref[:, (2 * d + 1) * LANES:(2 * d + 2) * LANES] = jnp.where(own_group, v_im, 0.0).astype(BF16)
        ks = [t + 1 for t in range(L)] if d == 0 else [L - t for t in range(L)]
        r_re, r_im = rows_of_powers(pw_re, ks, 2), rows_of_powers(pw_im, ks, 2)
        cct_re, cct_im = jnp.concatenate([cc_re] * (2 * L), axis=0), jnp.concatenate([cc_im] * (2 * L), axis=0)
        v_re, v_im = cmul(cct_re, cct_im, r_re, r_im)
        cst_ref[(2 * d) * LANES:(2 * d + 1) * LANES, :] = jnp.where(own_group, v_re, 0.0).T.astype(BF16)
        cst_ref[(2 * d + 1) * LANES:(2 * d + 2) * LANES, :] = jnp.where(own_group, -v_im, 0.0).T.astype(BF16)
        ks = list(range(L)) if d == 0 else [L - 1 - j for j in range(L)]
        r_re, r_im = rows_of_powers(pw_re, ks, 1), rows_of_powers(pw_im, ks, 1)
        cl_re, cl_im = cmul(jnp.concatenate([cc_re] * L, axis=0), jnp.concatenate([cc_im] * L, axis=0), r_re, r_im)
        per_group = []
        for g in range(2):
            mine = (lane // P) == g
            per_group.append(_dot_nt_f32(jnp.where(mine, bb_re, 0.0), cl_re)
                             - _dot_nt_f32(jnp.where(mine, bb_im, 0.0), cl_im))
        lag_tables.append(per_group)
        row8 = lax.broadcasted_iota(jnp.int32, (SUBLANES, LANES), 0)
        n_chunks = ((row8 + 1) if d == 0 else (SUBLANES - row8)).astype(F32) * float(L)
        mag8 = jnp.exp(n_chunks * ar)
        a_ref[2 * d] = mag8 * jnp.cos(n_chunks * ai)
        a_ref[2 * d + 1] = mag8 * jnp.sin(n_chunks * ai)

    def shift_right(x, s):
        x0, x1 = x[:, :LANES], x[:, LANES:]
        a, r = divmod(s, LANES)
        r0 = pltpu.roll(x0, r, axis=1) if r else x0
        r1 = pltpu.roll(x1, r, axis=1) if r else x1
        if a == 0:
            return jnp.concatenate([jnp.where(lane >= r, r0, 0.0), jnp.where(lane >= r, r1, r0)], axis=1)
        return jnp.concatenate([jnp.zeros_like(x0), jnp.where(lane >= r, r0, 0.0)], axis=1)

    def shift_left(x, s):
        x0, x1 = x[:, :LANES], x[:, LANES:]
        a, r = divmod(s, LANES)
        r0 = pltpu.roll(x0, LANES - r, axis=1) if r else x0
        r1 = pltpu.roll(x1, LANES - r, axis=1) if r else x1
        if a == 0:
            return jnp.concatenate([jnp.where(lane < LANES - r, r0, r1), jnp.where(lane < LANES - r, r1, 0.0)], axis=1)
        return jnp.concatenate([jnp.where(lane < LANES - r, r1, 0.0), jnp.zeros_like(x0)], axis=1)

    for g in range(2):
        kf, kb = lag_tables[0][g], lag_tables[1][g]
        for tau in range(L):
            blk = shift_right(kf, S * tau) + shift_left(kb, S * (L - 1 - tau))
            toep_ref[g, tau * S:(tau + 1) * S, :] = blk.astype(BF16)


def _s5_tables(lam_re, lam_im, log_step, b_re, b_im, c_re, c_im, d_skip):
    _, G, P = lam_re.shape
    S, L = S5_GROUP_DIM, S5_CHUNK
    assert P == S5_STATE and 2 * P == LANES and 2 * L * S == 2 * MXU_TILE
    pairs = G // 2
    par = jnp.stack([lam_re.reshape(2, pairs, 2 * P), lam_im.reshape(2, pairs, 2 * P),
                     jnp.repeat(log_step, P, axis=-1).reshape(2, pairs, 2 * P)], axis=2)
    par = jnp.transpose(par, (1, 0, 2, 3)).astype(F32)
    bshape = (2, pairs, 2 * P, S)
    cshape = (2, pairs, 2, S, P)
    pw = 2 * L * S
    bspec = pl.BlockSpec((2, None, 2 * P, S), lambda g: (0, g, 0, 0))
    cspec = pl.BlockSpec((2, None, 2, S, P), lambda g: (0, g, 0, 0, 0))
    toep, bst, cst, a_chunk = pl.pallas_call(
        _s5_table_kernel,
        grid=(pairs,),
        in_specs=[pl.BlockSpec((None, 2, 3, 2 * P), lambda g: (g, 0, 0, 0)), bspec, bspec, cspec, cspec],
        out_specs=[pl.BlockSpec((None, 2, MXU_TILE, MXU_TILE), lambda g: (g, 0, 0, 0)),
                   pl.BlockSpec((None, pw, pw), lambda g: (g, 0, 0)),
                   pl.BlockSpec((None, pw, pw), lambda g: (g, 0, 0)),
                   pl.BlockSpec((None, 4, SUBLANES, LANES), lambda g: (g, 0, 0, 0))],
        out_shape=[jax.ShapeDtypeStruct((pairs, 2, MXU_TILE, MXU_TILE), BF16),
                   jax.ShapeDtypeStruct((pairs, pw, pw), BF16),
                   jax.ShapeDtypeStruct((pairs, pw, pw), BF16),
                   jax.ShapeDtypeStruct((pairs, 4, SUBLANES, LANES), F32)],
        compiler_params=_params("parallel"),
        name="s5_tables",
    )(par, b_re.reshape(bshape), b_im.reshape(bshape), c_re.reshape(cshape), c_im.reshape(cshape))
    d_pair = jnp.broadcast_to(d_skip.astype(F32).reshape(pairs, 2, 1, S), (pairs, 2, L, S))
    return toep, bst, cst, a_chunk, d_pair.reshape(pairs, 1, pw)


def _s5(u_ctx, u_lat, tables):
    toep, bst, cst, a_pow, d_pair = tables
    b, n_c, w = u_ctx.shape
    n_l = u_lat.shape[1]
    L = S5_CHUNK
    n_ctx, n_all = n_c // L, (n_c + n_l) // L
    nblk = w // LANES
    pw = 2 * L * S5_GROUP_DIM
    npair = S5_BLOCK_PAIRS
    sel_in, sel_out = _s5_selectors()
    wspec = lambda shape: pl.BlockSpec((npair,) + shape, lambda g, bi: (g,) + (0,) * len(shape),
                                       pipeline_mode=pl.Buffered(1))
    return pl.pallas_call(
        functools.partial(_s5_kernel, n_ctx=n_ctx, n_all=n_all),
        grid=(nblk, b),
        in_specs=[pl.BlockSpec((None, n_c, LANES), lambda g, bi: (bi, 0, g)),
                  pl.BlockSpec((None, n_l, LANES), lambda g, bi: (bi, 0, g)),
                  _const_spec(sel_in.shape), _const_spec(sel_out.shape),
                  wspec((2, MXU_TILE, MXU_TILE)), wspec((pw, pw)), wspec((pw, pw)),
                  wspec((4, SUBLANES, LANES)), wspec((1, pw))],
        out_specs=[pl.BlockSpec((None, n_c, LANES), lambda g, bi: (bi, 0, g)),
                   pl.BlockSpec((None, n_l, LANES), lambda g, bi: (bi, 0, g))],
        out_shape=[jax.ShapeDtypeStruct((b, n_c, w), F32), jax.ShapeDtypeStruct((b, n_l, w), F32)],
        scratch_shapes=[pltpu.VMEM((n_all, L * LANES), BF16), pltpu.VMEM((n_all, L * LANES), BF16),
                        pltpu.VMEM((npair, n_all, pw), F32), pltpu.VMEM((npair, n_all, pw), F32),
                        pltpu.VMEM((npair, n_all, pw), F32), pltpu.VMEM((2, n_all, SUBLANES * LANES), BF16)],
        compiler_params=_params("parallel", "arbitrary"),
        name="s5_scan",
    )(u_ctx, u_lat, sel_in, sel_out, toep, bst, cst, a_pow, d_pair)


def _mla_kernel(*refs, tk, n_steps):
    if n_steps:
        q_ref, kc_ref, vc_ref, k_ref, v_ref, o_ref, qt_ref, s_ref, m_ref, l_ref, acc_ref = refs
    else:
        q_ref, kc_ref, vc_ref, o_ref, qt_ref, s_ref, m_ref, l_ref, acc_ref = refs
    tq = q_ref.shape[0]
    ncb = tq // MLA_Q_BLOCK
    kp = MLA_K_PIECE
    qt_ref[...] = q_ref[...].astype(F32).T.astype(BF16)

    def chunk(load_k, load_vt, nkeys, first):
        nr = nkeys // kp

        def score_piece(c, r):
            st = _dot(load_k(r), qt_ref[:, c * MLA_Q_BLOCK:(c + 1) * MLA_Q_BLOCK])
            s_ref[c % 2, r * kp:(r + 1) * kp, :] = st
            return jnp.max(st, axis=0, keepdims=True)

        def block_stats(c, mx):
            if first:
                return mx, None
            m_old = m_ref[:, c * MLA_Q_BLOCK:(c + 1) * MLA_Q_BLOCK]
            m_new = jnp.maximum(m_old, mx)
            return m_new, jnp.exp2(m_old - m_new)

        def prob_piece(c, r, m_new):
            p = jnp.exp2(s_ref[c % 2, r * kp:(r + 1) * kp, :] - m_new)
            return jnp.sum(p, axis=0, keepdims=True), _dot(load_vt(r), p.astype(BF16))

        def finish(c, m_new, alpha, lsum, pv):
            cols = slice(c * MLA_Q_BLOCK, (c + 1) * MLA_Q_BLOCK)
            if first:
                l_ref[:, cols] = lsum
                acc_ref[:, cols] = pv
            else:
                l_ref[:, cols] = alpha * l_ref[:, cols] + lsum
                acc_ref[:, cols] = alpha * acc_ref[:, cols] + pv
            m_ref[:, cols] = m_new

        mx = None
        for r in range(nr):
            pm = score_piece(0, r)
            mx = pm if mx is None else jnp.maximum(mx, pm)
        for c in range(ncb):
            m_new, alpha = block_stats(c, mx)
            mx = lsum = pv = None
            for r in range(nr):
                if c + 1 < ncb:
                    pm = score_piece(c + 1, r)
                    mx = pm if mx is None else jnp.maximum(mx, pm)
                ls, pvr = prob_piece(c, r, m_new)
                lsum = ls if lsum is None else lsum + ls
                pv = pvr if pv is None else pv + pvr
            finish(c, m_new, alpha, lsum, pv)

    chunk(lambda r: kc_ref[r * kp:(r + 1) * kp, :], lambda r: vc_ref[:, r * kp:(r + 1) * kp],
          kc_ref.shape[0], True)
    if n_steps:
        def body(j, _):
            off = pl.multiple_of(j * tk, tk)
            chunk(lambda r: k_ref[pl.ds(off + r * kp, kp), :], lambda r: v_ref[:, pl.ds(off + r * kp, kp)],
                  tk, False)
            return 0
        lax.fori_loop(0, n_steps, body, 0)
    o_ref[...] = (acc_ref[...] / l_ref[...]).T.astype(BF16)


def _mla_attention(q, k_ctx, vt_ctx, k_lat=None, vt_lat=None, *, tq, tk=1024):
    b, t, _ = q.shape
    n_c = k_ctx.shape[1]
    in_specs = [pl.BlockSpec((None, tq, MLA_HEAD_PAD), lambda bi, h, i: (bi, i, h)),
                pl.BlockSpec((None, n_c, MLA_HEAD_PAD), lambda bi, h, i: (bi, 0, h)),
                pl.BlockSpec((None, MLA_V, n_c), lambda bi, h, i: (bi, h, 0))]
    args = [q, k_ctx, vt_ctx]
    n_steps = 0
    if k_lat is not None:
        n_l = k_lat.shape[1]
        n_steps = n_l // tk
        in_specs += [pl.BlockSpec((None, n_l, MLA_HEAD_PAD), lambda bi, h, i: (bi, 0, h)),
                     pl.BlockSpec((None, MLA_V, n_l), lambda bi, h, i: (bi, h, 0))]
        args += [k_lat, vt_lat]
    return pl.pallas_call(
        functools.partial(_mla_kernel, tk=tk, n_steps=n_steps),
        grid=(b, MLA_HEADS, t // tq),
        in_specs=in_specs,
        out_specs=pl.BlockSpec((None, tq, MLA_V), lambda bi, h, i: (bi, i, h)),
        out_shape=jax.ShapeDtypeStruct((b, t, MLA_HEADS * MLA_V), BF16),
        scratch_shapes=[pltpu.VMEM((MLA_HEAD_PAD, tq), BF16), pltpu.VMEM((2, max(tk, n_c), MLA_Q_BLOCK), F32),
                        pltpu.VMEM((1, tq), F32), pltpu.VMEM((1, tq), F32), pltpu.VMEM((MLA_V, tq), F32)],
        compiler_params=_params("parallel", "parallel", "arbitrary"),
        name="mla_attention",
    )(*args)


def _proj_c_kernel(h_ref, sh_ref, sc_ref, gn_ref, wc_ref, wvt_ref, gqk_ref, cos_ref, sin_ref, bd_ref,
                   q_ref, k_ref, vt_ref):
    qw = WIN_HEADS * WIN_HEAD_DIM
    kw = WIN_KV_HEADS * WIN_HEAD_DIM
    bd = bd_ref[...]
    tm = h_ref.shape[0]
    rb = min(PROJ_ROWS, tm)
    lane = lax.broadcasted_iota(jnp.int32, (rb, LANES), 1)
    first_quarter = (lane % (WIN_HEAD_DIM // 2)) < (WIN_HEAD_DIM // 4)
    for r in range(tm // rb):
        rows = slice(r * rb, (r + 1) * rb)
        a = (_rms(h_ref[rows, :], gn_ref[...]) * (1.0 + sc_ref[...]) + sh_ref[...]).astype(BF16)
        p = _dot(a, wc_ref[...])
        vt_ref[:, rows] = _dot_nt(wvt_ref[...], a).astype(BF16)
        cos, sin = cos_ref[rows, :], sin_ref[rows, :]
        for j in range((qw + kw) // MXU_TILE):
            xh = p[:, j * MXU_TILE:(j + 1) * MXU_TILE]
            ssq = _dot((xh * xh).astype(BF16), bd)
            xn = xh * lax.rsqrt(ssq * (1.0 / WIN_HEAD_DIM) + NORM_EPS) * gqk_ref[:, j * MXU_TILE:(j + 1) * MXU_TILE]
            for c in range(MXU_TILE // LANES):
                xc = xn[:, c * LANES:(c + 1) * LANES]
                fwd = pltpu.roll(xc, WIN_HEAD_DIM // 4, axis=1)
                bwd = pltpu.roll(xc, LANES - WIN_HEAD_DIM // 4, axis=1)
                y = (xc * cos + jnp.where(first_quarter, -bwd, fwd) * sin).astype(BF16)
                col = j * MXU_TILE + c * LANES
                if col < qw:
                    q_ref[rows, col:col + LANES] = y
                else:
                    k_ref[rows, col - qw:col - qw + LANES] = y


def _proj_c(h, mods, ctx_row, tm, wts, cos2, sin2):
    b, t, d = h.shape
    qw = WIN_HEADS * WIN_HEAD_DIM
    kw = WIN_KV_HEADS * WIN_HEAD_DIM
    tile = lambda w: pl.BlockSpec((None, tm, w), lambda bi, i: (bi, i, 0))
    tab = pl.BlockSpec((tm, LANES), lambda bi, i: (i, 0))
    in_specs = ([tile(d)] + _mod_specs(d, (0, 1), ctx_row)
                + [_const_spec(wts[k].shape) for k in ("gn", "wc", "wvt", "gqk")]
                + [tab, tab, _const_spec(wts["bd"].shape)])
    return pl.pallas_call(
        _proj_c_kernel,
        grid=(b, t // tm),
        in_specs=in_specs,
        out_specs=[tile(qw), tile(kw), pl.BlockSpec((None, kw, tm), lambda bi, i: (bi, 0, i))],
        out_shape=[jax.ShapeDtypeStruct((b, t, qw), BF16),
                   jax.ShapeDtypeStruct((b, t, kw), BF16),
                   jax.ShapeDtypeStruct((b, kw, t), BF16)],
        compiler_params=_params("parallel", "parallel"),
        name="proj_c",
    )(h, mods, mods, wts["gn"], wts["wc"], wts["wvt"], wts["gqk"], cos2, sin2, wts["bd"])


def _prep_proj_c(norm_mix, c_w_in, q_norm, k_norm):
    gqk = jnp.concatenate([jnp.tile(q_norm * (WIN_SCALE * LOG2_E), WIN_HEADS), jnp.tile(k_norm, WIN_KV_HEADS)])
    idx = jnp.arange(MXU_TILE) // WIN_HEAD_DIM
    bd = (idx[:, None] == idx[None, :]).astype(BF16)
    qk = (WIN_HEADS + WIN_KV_HEADS) * WIN_HEAD_DIM
    bound = 1.01 * WIN_HEAD_DIM * jnp.max(jnp.abs(q_norm * (WIN_SCALE * LOG2_E))) * jnp.max(jnp.abs(k_norm))
    return dict(gn=norm_mix[None, :], wc=c_w_in[:, :qk].astype(BF16), wvt=c_w_in[:, qk:].T.astype(BF16),
                gqk=gqk[None, :], bd=bd, bound=bound.astype(F32))


def _win_kernel(sc_ref, q_ref, k_ref, vt_ref, kc_ref, vct_ref, o_ref, ot_ref, *, tq, band, n_lat):
    i = pl.program_id(1)
    start = pl.multiple_of(jnp.clip(i * tq - WINDOW, 0, n_lat - band), WINDOW)
    hd, grp = WIN_HEAD_DIM, WIN_GROUP
    bound = sc_ref[WIN_HEADS]
    qt = q_ref[...].astype(F32).T.astype(BF16)
    k_pos = start + lax.broadcasted_iota(jnp.int32, (band, tq), 0)
    q_pos = i * tq + lax.broadcasted_iota(jnp.int32, (band, tq), 1)
    bias1 = jnp.where(jnp.abs(k_pos - q_pos) <= WINDOW, 0.0, NEG_BIG)
    bias = jnp.concatenate([bias1] * grp, axis=1)
    zeros = jnp.zeros((hd, grp * tq), BF16)

    def group_inputs(kv):
        qg = jnp.concatenate([qt[(kv * grp + g) * hd:(kv * grp + g + 1) * hd, :] for g in range(grp)], axis=1)
        qg = jnp.concatenate([qg, zeros] if kv % 2 == 0 else [zeros, qg], axis=0)
        col = (kv // 2) * LANES
        sink = jnp.concatenate([jnp.full((1, tq), sc_ref[kv * grp + g], F32) for g in range(grp)], axis=1)
        return qg, col, sink

    def weighted_values(kv, p_ctx, p_loc):
        return (_dot(vct_ref[kv * hd:(kv + 1) * hd, :], p_ctx.astype(BF16))
                + _dot(vt_ref[kv * hd:(kv + 1) * hd, pl.ds(start, band)], p_loc.astype(BF16)))

    def regroup(ot):
        return jnp.concatenate([ot[:, g * tq:(g + 1) * tq] for g in range(grp)], axis=0)

    def one_pass(kv):
        qg, col, sink = group_inputs(kv)
        p_ctx = jnp.exp2(_dot(kc_ref[:, col:col + LANES], qg) - bound)
        p_loc = jnp.exp2(_dot(k_ref[pl.ds(start, band), col:col + LANES], qg) + (bias - bound))
        den = (jnp.sum(p_loc, axis=0, keepdims=True) + jnp.sum(p_ctx, axis=0, keepdims=True)
               + jnp.exp2(sink - bound))
        ot_ref[kv * grp * hd:(kv + 1) * grp * hd, :] = regroup(weighted_values(kv, p_ctx, p_loc) / den)
        return den

    def two_pass(kv):
        qg, col, sink = group_inputs(kv)
        s_ctx = _dot(kc_ref[:, col:col + LANES], qg)
        s_loc = _dot(k_ref[pl.ds(start, band), col:col + LANES], qg) + bias
        m = jnp.maximum(jnp.maximum(jnp.max(s_loc, axis=0, keepdims=True),
                                    jnp.max(s_ctx, axis=0, keepdims=True)), sink)
        p_loc = jnp.exp2(s_loc - m)
        p_ctx = jnp.exp2(s_ctx - m)
        den = (jnp.sum(p_loc, axis=0, keepdims=True) + jnp.sum(p_ctx, axis=0, keepdims=True)
               + jnp.exp2(sink - m))
        ot_ref[kv * grp * hd:(kv + 1) * grp * hd, :] = regroup(weighted_values(kv, p_ctx, p_loc) / den)

    lo = hi = None
    for kv in range(WIN_KV_HEADS):
        den = one_pass(kv)
        lo = den if lo is None else jnp.minimum(lo, den)
        hi = den if hi is None else jnp.maximum(hi, den)
    unsafe = jnp.logical_or(jnp.min(lo) < SOFTMAX_DEN_MIN, jnp.max(hi) > SOFTMAX_DEN_MAX)

    @pl.when(unsafe)
    def _():
        for kv in range(WIN_KV_HEADS):
            two_pass(kv)

    o_ref[...] = ot_ref[...].T.astype(BF16)


def _win_attention(q, k, vt, k_ctx, vt_ctx, sink, score_bound, *, tq):
    b, n, qw = q.shape
    n_c = k_ctx.shape[1]
    kw = k.shape[2]
    band = tq + 2 * WINDOW
    full = lambda r, w: pl.BlockSpec((None, r, w), lambda bi, i: (bi, 0, 0))
    return pl.pallas_call(
        functools.partial(_win_kernel, tq=tq, band=band, n_lat=n),
        grid=(b, n // tq),
        in_specs=[pl.BlockSpec(memory_space=pltpu.SMEM),
                  pl.BlockSpec((None, tq, qw), lambda bi, i: (bi, i, 0)),
                  full(n, kw), full(kw, n), full(n_c, kw), full(kw, n_c)],
        out_specs=pl.BlockSpec((None, tq, qw), lambda bi, i: (bi, i, 0)),
        out_shape=jax.ShapeDtypeStruct((b, n, qw), BF16),
        scratch_shapes=[pltpu.VMEM((qw, tq), F32)],
        compiler_params=_params("parallel", "arbitrary"),
        name="win_attention",
    )(jnp.concatenate([sink.astype(F32) * LOG2_E, score_bound.reshape(1)]), q, k, vt, k_ctx, vt_ctx)


def _post_kernel(*refs, s5_width, n_chunks):
    if s5_width:
        (h_ref, g_ref, sh_ref, sc_ref, g2_ref, gn_ref, yg_ref, o_ref, wglu_ref, bglu_ref, wo_ref,
         wg_ref, wu_ref, wd_ref, out_ref, a_ref, acc_ref) = refs
        yg = yg_ref[...]
        s5 = yg * jax.nn.sigmoid(_dot(yg.astype(BF16), wglu_ref[...]) + bglu_ref[...])
        mix = _dot(s5.astype(BF16), wo_ref[:s5_width, :]) + _dot(o_ref[...], wo_ref[s5_width:, :])
    else:
        (h_ref, g_ref, sh_ref, sc_ref, g2_ref, gn_ref, o_ref, wo_ref,
         wg_ref, wu_ref, wd_ref, out_ref, a_ref, acc_ref) = refs
        mix = _dot(o_ref[...], wo_ref[...])
    h1 = h_ref[...] + g_ref[...] * mix
    a_ref[...] = (_rms(h1, gn_ref[...]) * (1.0 + sc_ref[...]) + sh_ref[...]).astype(BF16)
    acc_ref[...] = jnp.zeros_like(acc_ref)

    def body(c, _):
        a = a_ref[...]
        cols = pl.ds(pl.multiple_of(c * MXU_TILE, MXU_TILE), MXU_TILE)
        act = _silu(_dot(a, wg_ref[:, cols])) * _dot(a, wu_ref[:, cols])
        acc_ref[...] += _dot(act.astype(BF16), wd_ref[cols, :])
        return 0

    lax.fori_loop(0, n_chunks, body, 0, unroll=True)
    out_ref[...] = h1 + g2_ref[...] * acc_ref[...]


def _post(h, mods, ctx_row, tm, wts, ffn, layer, o, yg=None):
    b, t, d = h.shape
    tile = lambda w: pl.BlockSpec((None, tm, w), lambda bi, i: (bi, i, 0))
    s5_width = 0 if yg is None else yg.shape[2]
    in_specs = [tile(d)] + _mod_specs(d, (2, 3, 4, 5), ctx_row) + [_const_spec(wts["gn"].shape)]
    args = [h, mods, mods, mods, mods, wts["gn"]]
    if yg is not None:
        in_specs += [tile(s5_width), tile(o.shape[2]), _const_spec(wts["wglu"].shape),
                     _const_spec(wts["bglu"].shape)]
        args += [yg, o, wts["wglu"], wts["bglu"]]
    else:
        in_specs += [tile(o.shape[2])]
        args += [o]
    in_specs.append(_const_spec(wts["wo"].shape))
    args.append(wts["wo"])
    for k in ("wg", "wu", "wd"):
        in_specs.append(_const_spec(ffn[k].shape, layer))
        args.append(ffn[k])
    return pl.pallas_call(
        functools.partial(_post_kernel, s5_width=s5_width, n_chunks=ffn["wg"].shape[2] // MXU_TILE),
        grid=(b, t // tm),
        in_specs=in_specs,
        out_specs=tile(d),
        out_shape=jax.ShapeDtypeStruct((b, t, d), F32),
        scratch_shapes=[pltpu.VMEM((tm, d), BF16), pltpu.VMEM((tm, d), F32)],
        compiler_params=_params("parallel", "parallel"),
        name="post_ffn",
    )(*args)


def _prep_post(norm_ffn, w_out, w_glu=None, b_glu=None):
    wts = dict(gn=norm_ffn[None, :], wo=w_out.astype(BF16))
    if w_glu is not None:
        wts.update(wglu=w_glu.astype(BF16), bglu=b_glu[None, :])
    return wts


def kernel(x, c, ctx, c_ctx, ada_w, ada_b, norm_mix, norm_ffn, ffn_w_gate, ffn_w_up, ffn_w_down,
           a_w_in, a_w_out, s5_lam_re, s5_lam_im, s5_log_step, s5_b_re, s5_b_im, s5_c_re, s5_c_im,
           s5_d, s5_w_glu, s5_b_glu, mla_qa_norm, mla_w_q_b, mla_kva_norm, mla_w_kv_b,
           mla_q_norm, mla_k_norm, c_w_in, c_w_out, c_q_norm, c_k_norm, c_sink):
    b, n, d = x.shape
    n_c = ctx.shape[1]
    depth = ada_w.shape[0]
    assert b + 1 <= MOD_ROWS and n % 1024 == 0 and n_c % (S5_CHUNK * SUBLANES) == 0
    rows = n // GRID_W
    tm_lat, tm_ctx, tm_proj = 512, n_c, min(n, 1024)

    cond = jnp.zeros((MOD_ROWS, d), F32).at[:b].set(c).at[b].set(c_ctx)
    mods = _ada_modulation(cond, ada_w, ada_b)
    mods = mods.reshape(depth, MOD_ROWS, N_MOD, 1, d)

    cos_a, sin_a = _grid_rope_tables(rows, MLA_ROPE)
    cs_a_lat = jnp.concatenate([cos_a, sin_a], axis=1)
    cs_a_ctx = jnp.concatenate([jnp.ones((n_c, MLA_ROPE), F32), jnp.zeros((n_c, MLA_ROPE), F32)], axis=1)
    cos_c, sin_c = _grid_rope_tables(rows, WIN_HEAD_DIM)
    cos_c2, sin_c2 = jnp.tile(cos_c, (1, 2)), jnp.tile(sin_c, (1, 2))
    one_c, zero_c = jnp.ones((n_c, LANES), F32), jnp.zeros((n_c, LANES), F32)
    assert ffn_w_gate.shape[2] % MXU_TILE == 0
    ffn = dict(wg=ffn_w_gate.astype(BF16), wu=ffn_w_up.astype(BF16), wd=ffn_w_down.astype(BF16))

    h_ctx, h_lat = ctx, x
    for i in range(depth):
        need_ctx = i < depth - 1
        j = i // 2
        m_i = mods[i]
        if i % 2 == 0:
            pw = _prep_proj_a(norm_mix[i], a_w_in[j], mla_qa_norm[j], mla_w_q_b[j], mla_kva_norm[j],
                              mla_w_kv_b[j], mla_q_norm[j], mla_k_norm[j])
            u_l, q_l, k_l, vt_l = _proj_a(h_lat, m_i, None, tm_proj, pw, cs_a_lat)
            u_c, q_c, k_c, vt_c = _proj_a(h_ctx, m_i, b, tm_ctx, pw, cs_a_ctx)
            tables = _s5_tables(s5_lam_re[j], s5_lam_im[j], s5_log_step[j], s5_b_re[j], s5_b_im[j],
                                s5_c_re[j], s5_c_im[j], s5_d[j])
            yg_c, yg_l = _s5(u_c, u_l, tables)
            o_l = _mla_attention(q_l, k_c, vt_c, k_l, vt_l, tq=min(n, 4096))
            post_w = _prep_post(norm_ffn[i], a_w_out[j], s5_w_glu[j], s5_b_glu[j])
            h_lat_new = _post(h_lat, m_i, None, tm_lat, post_w, ffn, i, o_l, yg_l)
            if need_ctx:
                o_c = _mla_attention(q_c, k_c, vt_c, tq=n_c)
                h_ctx = _post(h_ctx, m_i, b, tm_ctx, post_w, ffn, i, o_c, yg_c)
            h_lat = h_lat_new
        else:
            pw = _prep_proj_c(norm_mix[i], c_w_in[j], c_q_norm[j], c_k_norm[j])
            q_l, k_l, vt_l = _proj_c(h_lat, m_i, None, tm_proj, pw, cos_c2, sin_c2)
            q_c, k_c, vt_c = _proj_c(h_ctx, m_i, b, tm_ctx, pw, one_c, zero_c)
            o_l = _win_attention(q_l, k_l, vt_l, k_c, vt_c, c_sink[j], pw["bound"], tq=256)
            post_w = _prep_post(norm_ffn[i], c_w_out[j])
            h_lat_new = _post(h_lat, m_i, None, tm_lat, post_w, ffn, i, o_l)
            if need_ctx:
                raise NotImplementedError("context queries of a windowed layer")
            h_lat = h_lat_new
    return h_lat
```

```python
import functools
import math

import jax
import jax.numpy as jnp
import numpy as np
from jax import lax
from jax.experimental import pallas as pl
from jax.experimental.pallas import tpu as pltpu

F32 = jnp.float32
BF16 = jnp.bfloat16

GRID_W = 64
NORM_EPS = 1e-6
ROPE_THETA = 10000.0
N_MOD = 6
S5_GROUP_DIM = 16
S5_STATE = 64
S5_CHUNK = 16
MLA_HEADS = 4
MLA_NOPE = 128
MLA_ROPE = 64
MLA_QK_DIM = MLA_NOPE + MLA_ROPE
MLA_V = 128
MLA_Q_RANK = 384
MLA_KV_RANK = 256
MLA_SCALE = MLA_QK_DIM ** -0.5
MLA_HEAD_PAD = 256
WIN_HEADS = 16
WIN_KV_HEADS = 4
WIN_GROUP = WIN_HEADS // WIN_KV_HEADS
WIN_HEAD_DIM = 64
WINDOW = 128
WIN_SCALE = WIN_HEAD_DIM ** -0.5
LANES = 128
SUBLANES = 8
MXU_TILE = 256
VMEM_LIMIT_BYTES = 48 * 1024 * 1024
NEG_BIG = -1e30
LOG2_E = math.log2(math.e)
PROJ_ROWS = 256
SOFTMAX_DEN_MIN = 2.0 ** -60
SOFTMAX_DEN_MAX = 2.0 ** 60
S5_BLOCK_PAIRS = 4
MLA_Q_BLOCK = 256
MLA_K_PIECE = 256
MOD_ROWS = 8


def _dot(a, b):
    return jnp.dot(a, b, preferred_element_type=F32)


def _dot_nt(a, b):
    return lax.dot_general(a, b, (((1,), (1,)), ((), ())), preferred_element_type=F32)


def _split_bf16(x):
    hi = x.astype(BF16)
    lo = (x - hi.astype(F32)).astype(BF16)
    return hi, lo


def _rms(x, gain):
    return x * lax.rsqrt(jnp.mean(x * x, axis=-1, keepdims=True) + NORM_EPS) * gain


def _silu(x):
    return x * jax.nn.sigmoid(x)


def _params(*sem):
    return pltpu.CompilerParams(dimension_semantics=sem, vmem_limit_bytes=VMEM_LIMIT_BYTES)


def _const_spec(shape, layer=None):
    if layer is None:
        nd = len(shape)
        return pl.BlockSpec(shape, lambda *_: (0,) * nd, pipeline_mode=pl.Buffered(1))
    nd = len(shape) - 1
    return pl.BlockSpec((None,) + tuple(shape[1:]), lambda *_: (layer,) + (0,) * nd,
                        pipeline_mode=pl.Buffered(1))


def _ada_kernel(cond_ref, w_ref, b_ref, o_ref):
    s = _silu(cond_ref[...])
    s_hi, s_lo = _split_bf16(s)
    w_hi, w_lo = _split_bf16(w_ref[...])
    o_ref[...] = _dot(s_hi, w_hi) + _dot(s_lo, w_hi) + _dot(s_hi, w_lo) + b_ref[...]


def _ada_modulation(cond, ada_w, ada_b):
    depth, d, n = ada_w.shape
    tn = 1024
    return pl.pallas_call(
        _ada_kernel,
        grid=(depth, n // tn),
        in_specs=[pl.BlockSpec((MOD_ROWS, d), lambda i, j: (0, 0)),
                  pl.BlockSpec((None, d, tn), lambda i, j: (i, 0, j)),
                  pl.BlockSpec((None, 1, tn), lambda i, j: (i, 0, j))],
        out_specs=pl.BlockSpec((None, MOD_ROWS, tn), lambda i, j: (i, 0, j)),
        out_shape=jax.ShapeDtypeStruct((depth, MOD_ROWS, n), F32),
        compiler_params=_params("arbitrary", "arbitrary"),
        name="ada_modulation",
    )(cond, ada_w, ada_b.reshape(depth, 1, n))


def _mod_specs(d, slots, ctx_row):
    def make(slot):
        if ctx_row is None:
            return pl.BlockSpec((None, None, 1, d), lambda b, i: (b, slot, 0, 0))
        return pl.BlockSpec((None, None, 1, d), lambda b, i: (ctx_row, slot, 0, 0))
    return [make(s) for s in slots]


def _grid_rope_tables(rows, rot_dim):
    n_freq = rot_dim // 4
    inv_freq = np.power(np.float32(ROPE_THETA), -np.arange(n_freq, dtype=np.float32) / np.float32(n_freq))
    ang_r = np.arange(rows, dtype=np.float32)[:, None] * inv_freq.astype(np.float32)
    ang_c = np.arange(GRID_W, dtype=np.float32)[:, None] * inv_freq.astype(np.float32)

    def expand(r, c):
        r = np.broadcast_to(r[:, None, :], (rows, GRID_W, n_freq))
        c = np.broadcast_to(c[None, :, :], (rows, GRID_W, n_freq))
        return np.concatenate([r, r, c, c], axis=-1).reshape(rows * GRID_W, rot_dim)

    return expand(np.cos(ang_r), np.cos(ang_c)), expand(np.sin(ang_r), np.sin(ang_c))


def _rot_perm_sign(rot_dim):
    q = rot_dim // 4
    idx = np.arange(rot_dim)
    perm = np.where((idx // q) % 2 == 0, idx + q, idx - q)
    sign = np.where((idx // q) % 2 == 0, -1.0, 1.0).astype(np.float32)
    return perm, sign


def _proj_a_kernel(h_ref, sh_ref, sc_ref, gn_ref, w1_ref, gqa_ref, wq_ref, gkv_ref, wkv_ref, wvt_ref,
                   gq_ref, gk_ref, cs_ref, msk_ref, u_ref, q_ref, k_ref, vt_ref):
    tm = h_ref.shape[0]
    rb = min(PROJ_ROWS, tm)
    msk = msk_ref[...]
    low_half = lax.broadcasted_iota(jnp.int32, (rb, LANES), 1) < MLA_ROPE

    def finish(xh, gain, cs, out_ref, rows, h):
        ssq = _dot((xh * xh).astype(BF16), msk)
        xn = xh * lax.rsqrt(ssq * (1.0 / MLA_QK_DIM) + NORM_EPS) * gain
        rr = xn[:, LANES:] * cs
        rot = rr + pltpu.roll(rr, MLA_ROPE, axis=1)
        base = h * MLA_HEAD_PAD
        out_ref[rows, base:base + LANES] = xn[:, :LANES].astype(BF16)
        out_ref[rows, base + LANES:base + 2 * LANES] = jnp.where(low_half, rot, 0.0).astype(BF16)

    for r in range(tm // rb):
        rows = slice(r * rb, (r + 1) * rb)
        a = _rms(h_ref[rows, :], gn_ref[...]) * (1.0 + sc_ref[...]) + sh_ref[...]
        p1 = _dot(a.astype(BF16), w1_ref[...])
        u_ref[rows, :] = p1[:, :512]
        cq = p1[:, 512:512 + MLA_Q_RANK]
        ckv = p1[:, 896:896 + MLA_KV_RANK]
        krr = p1[:, 1152:1280]
        qb = _dot(_rms(cq, gqa_ref[...]).astype(BF16), wq_ref[...])
        ckv_n = _rms(ckv, gkv_ref[...]).astype(BF16)
        kv = _dot(ckv_n, wkv_ref[...])
        vt_ref[:, rows] = _dot_nt(wvt_ref[...], ckv_n).astype(BF16)
        cs = cs_ref[rows, :]
        for h in range(MLA_HEADS):
            base = h * MLA_HEAD_PAD
            finish(qb[:, base:base + MLA_HEAD_PAD], gq_ref[:, base:base + MLA_HEAD_PAD], cs, q_ref, rows, h)
            kh = jnp.concatenate([kv[:, h * MLA_NOPE:(h + 1) * MLA_NOPE], krr], axis=1)
            finish(kh, gk_ref[:, base:base + MLA_HEAD_PAD], cs, k_ref, rows, h)


def _proj_a(h, mods, ctx_row, tm, wts, cs):
    b, t, d = h.shape
    qw = MLA_HEADS * MLA_HEAD_PAD
    grid = (b, t // tm)
    tile = lambda w: pl.BlockSpec((None, tm, w), lambda bi, i: (bi, i, 0))
    in_specs = ([tile(d)] + _mod_specs(d, (0, 1), ctx_row)
                + [_const_spec(wts[k].shape) for k in
                   ("gn", "w1", "gqa", "wq", "gkv", "wkv", "wvt", "gq", "gk")]
                + [pl.BlockSpec((tm, LANES), lambda bi, i: (i, 0)), _const_spec(wts["msk"].shape)])
    vw = MLA_HEADS * MLA_V
    return pl.pallas_call(
        _proj_a_kernel,
        grid=grid,
        in_specs=in_specs,
        out_specs=[tile(512), tile(qw), tile(qw), pl.BlockSpec((None, vw, tm), lambda bi, i: (bi, 0, i))],
        out_shape=[jax.ShapeDtypeStruct((b, t, 512), F32),
                   jax.ShapeDtypeStruct((b, t, qw), BF16),
                   jax.ShapeDtypeStruct((b, t, qw), BF16),
                   jax.ShapeDtypeStruct((b, vw, t), BF16)],
        compiler_params=_params("parallel", "parallel"),
        name="proj_a",
    )(h, mods, mods, wts["gn"], wts["w1"], wts["gqa"], wts["wq"], wts["gkv"], wts["wkv"], wts["wvt"],
      wts["gq"], wts["gk"], cs, wts["msk"])


def _prep_proj_a(norm_mix, a_w_in, qa_norm, w_q_b, kva_norm, w_kv_b, q_norm, k_norm):
    perm, sign = _rot_perm_sign(MLA_ROPE)
    s5w = a_w_in.shape[1] - (MLA_Q_RANK + MLA_KV_RANK + MLA_ROPE)
    assert s5w == 512
    kr = a_w_in[:, -MLA_ROPE:]
    w1 = jnp.concatenate([a_w_in, kr[:, perm] * sign], axis=1).astype(BF16)
    wq = w_q_b.reshape(MLA_Q_RANK, MLA_HEADS, MLA_QK_DIM)
    rope = wq[:, :, MLA_NOPE:]
    wq = jnp.concatenate([wq, rope[:, :, perm] * sign], axis=2)
    wq = wq.reshape(MLA_Q_RANK, MLA_HEADS * MLA_HEAD_PAD).astype(BF16)
    wkv3 = w_kv_b.reshape(MLA_KV_RANK, MLA_HEADS, MLA_NOPE + MLA_V)
    wkv = wkv3[:, :, :MLA_NOPE].reshape(MLA_KV_RANK, -1).astype(BF16)
    wvt = wkv3[:, :, MLA_NOPE:].reshape(MLA_KV_RANK, -1).T.astype(BF16)

    def head_gain(g, scale):
        gb = jnp.concatenate([g, g[MLA_NOPE:][perm]]) * scale
        return jnp.tile(gb, MLA_HEADS)[None, :]

    rows = np.arange(MLA_HEAD_PAD)[:, None] < MLA_QK_DIM
    msk = jnp.asarray(np.broadcast_to(rows, (MLA_HEAD_PAD, MLA_HEAD_PAD)), BF16)
    return dict(gn=norm_mix[None, :], w1=w1, gqa=qa_norm[None, :], wq=wq, gkv=kva_norm[None, :],
                wkv=wkv, wvt=wvt, gq=head_gain(q_norm, MLA_SCALE * LOG2_E), gk=head_gain(k_norm, 1.0),
                msk=msk)


def _s5_kernel(uc_ref, ul_ref, sel_in_ref, sel_out_ref, toep_ref, bst_ref, cst_ref, a_ref, d_ref,
               yc_ref, yl_ref, uhi_ref, ulo_ref, x_ref, z_ref, sin_ref, yb_ref, *, n_ctx, n_all):
    L, S = S5_CHUNK, S5_GROUP_DIM
    n_lat = n_all - n_ctx
    npair = S5_BLOCK_PAIRS
    half = SUBLANES * LANES
    for tl in range(L):
        u = jnp.concatenate([uc_ref[pl.ds(tl, n_ctx, stride=L), :], ul_ref[pl.ds(tl, n_lat, stride=L), :]], axis=0)
        hi, lo = _split_bf16(u)
        uhi_ref[:, tl * LANES:(tl + 1) * LANES] = hi
        ulo_ref[:, tl * LANES:(tl + 1) * LANES] = lo
    for pp in range(npair):
        for hh in range(2):
            xs = (_dot(uhi_ref[:, hh * half:(hh + 1) * half], sel_in_ref[pp])
                  + _dot(ulo_ref[:, hh * half:(hh + 1) * half], sel_in_ref[pp]))
            x_ref[pp, :, hh * LANES:(hh + 1) * LANES] = xs[:, :LANES]
            x_ref[pp, :, MXU_TILE + hh * LANES:MXU_TILE + (hh + 1) * LANES] = xs[:, LANES:]
    for pp in range(npair):
        z_ref[pp] = _dot(x_ref[pp].astype(BF16), bst_ref[pp])
    t_ctx, t_all = n_ctx // SUBLANES, n_all // SUBLANES
    row = lax.broadcasted_iota(jnp.int32, (SUBLANES, LANES), 0)

    def cmul(a_re, a_im, b_re, b_im):
        return a_re * b_re - a_im * b_im, a_re * b_im + a_im * b_re

    def tile_scan(z_re, z_im, c_re, c_im, a_re, a_im, fwd):
        for sft in (1, 2, 4):
            k = sft - 1 if fwd else SUBLANES - sft
            p_re, p_im = a_re[k:k + 1, :], a_im[k:k + 1, :]
            amt = sft if fwd else SUBLANES - sft
            keep = (row >= sft) if fwd else (row < SUBLANES - sft)
            s_re = jnp.where(keep, pltpu.roll(z_re, amt, axis=0), 0.0)
            s_im = jnp.where(keep, pltpu.roll(z_im, amt, axis=0), 0.0)
            m_re, m_im = cmul(p_re, p_im, s_re, s_im)
            z_re, z_im = z_re + m_re, z_im + m_im
        m_re, m_im = cmul(a_re, a_im, c_re, c_im)
        s_re, s_im = z_re + m_re, z_im + m_im
        edge = (row == 0) if fwd else (row == SUBLANES - 1)
        amt = 1 if fwd else SUBLANES - 1
        in_re = jnp.where(edge, c_re, pltpu.roll(s_re, amt, axis=0))
        in_im = jnp.where(edge, c_im, pltpu.roll(s_im, amt, axis=0))
        last = SUBLANES - 1 if fwd else 0
        return in_re, in_im, s_re[last:last + 1, :], s_im[last:last + 1, :]

    def body(it, carry):
        jt = jnp.where(it < t_ctx, t_ctx - 1 - it, t_all + t_ctx - 1 - it)
        rf = pl.multiple_of(it * SUBLANES, SUBLANES)
        rb = pl.multiple_of(jt * SUBLANES, SUBLANES)
        new = []
        for pp in range(npair):
            cf_re, cf_im, cb_re, cb_im = carry[4 * pp:4 * pp + 4]
            zf = z_ref[pp, pl.ds(rf, SUBLANES), 0:2 * LANES]
            zb = z_ref[pp, pl.ds(rb, SUBLANES), 2 * LANES:4 * LANES]
            f_re, f_im, cf_re, cf_im = tile_scan(zf[:, :LANES], zf[:, LANES:], cf_re, cf_im,
                                                 a_ref[pp, 0], a_ref[pp, 1], True)
            b_re, b_im, cb_re, cb_im = tile_scan(zb[:, :LANES], zb[:, LANES:], cb_re, cb_im,
                                                 a_ref[pp, 2], a_ref[pp, 3], False)
            sin_ref[pp, pl.ds(rf, SUBLANES), 0:2 * LANES] = jnp.concatenate([f_re, f_im], axis=1)
            sin_ref[pp, pl.ds(rb, SUBLANES), 2 * LANES:4 * LANES] = jnp.concatenate([b_re, b_im], axis=1)
            new += [cf_re, cf_im, cb_re, cb_im]
        return tuple(new)

    zero = jnp.zeros((1, LANES), F32)
    lax.fori_loop(0, t_all, body, (zero,) * (4 * npair))
    for pp in range(npair):
        x = x_ref[pp]
        xb = x.astype(BF16)
        y = jnp.concatenate([_dot(xb[:, :MXU_TILE], toep_ref[pp, 0]), _dot(xb[:, MXU_TILE:], toep_ref[pp, 1])],
                            axis=1)
        y = y + _dot(sin_ref[pp].astype(BF16), cst_ref[pp]) + x * d_ref[pp]
        yg = jax.nn.gelu(y).astype(BF16)
        for gl in range(2):
            for hh in range(2):
                g8 = 2 * pp + gl
                col = gl * MXU_TILE + hh * LANES
                yb_ref[hh, :, g8 * LANES:(g8 + 1) * LANES] = yg[:, col:col + LANES]
    for hh in range(2):
        for kk in range(SUBLANES // 2):
            two = _dot(yb_ref[hh], sel_out_ref[kk])
            for e in range(2):
                tl = hh * SUBLANES + 2 * kk + e
                yc_ref[pl.ds(tl, n_ctx, stride=L), :] = two[:n_ctx, e * LANES:(e + 1) * LANES]
                yl_ref[pl.ds(tl, n_lat, stride=L), :] = two[n_ctx:, e * LANES:(e + 1) * LANES]


def _s5_selectors():
    S = S5_GROUP_DIM
    r = np.arange(SUBLANES * LANES)[:, None]
    c = np.arange(2 * LANES)[None, :]
    k, l = r // LANES, r % LANES
    pp = np.arange(S5_BLOCK_PAIRS)[:, None, None]
    sel_in = (k == (c % LANES) // S) & (l == 2 * S * pp + S * (c // LANES) + c % S)
    kk = np.arange(SUBLANES // 2)[:, None, None]
    sel_out = (k == (c % LANES) // S) & (l == S * (2 * kk + c // LANES) + c % S)
    return jnp.asarray(sel_in, BF16), jnp.asarray(sel_out, BF16)


def _dot_nt_f32(a, b):
    a_hi, a_lo = _split_bf16(a)
    b_hi, b_lo = _split_bf16(b)
    return _dot_nt(a_hi, b_hi) + _dot_nt(a_lo, b_hi) + _dot_nt(a_hi, b_lo)


def _s5_table_kernel(par_ref, bre_ref, bim_ref, cre_ref, cim_ref, toep_ref, bst_ref, cst_ref, a_ref):
    L, S, P = S5_CHUNK, S5_GROUP_DIM, S5_STATE
    kk = lax.broadcasted_iota(jnp.int32, (3 * SUBLANES, LANES), 0).astype(F32)
    lane = lax.broadcasted_iota(jnp.int32, (L, LANES), 1)
    rows512 = lax.broadcasted_iota(jnp.int32, (2 * L * S, LANES), 0)
    lanes512 = lax.broadcasted_iota(jnp.int32, (2 * L * S, LANES), 1)
    own_group = (rows512 // (L * S)) == (lanes512 // P)
    lane_pad = jnp.zeros((2 * P, LANES - S), F32)

    def rows_of_powers(pw, ks, groups):
        one = jnp.concatenate([jnp.broadcast_to(pw[k:k + 1, :], (S, LANES)) for k in ks], axis=0)
        return jnp.concatenate([one] * groups, axis=0) if groups > 1 else one

    def cmul(a_re, a_im, b_re, b_im):
        return a_re * b_re - a_im * b_im, a_re * b_im + a_im * b_re

    lag_tables = []
    for d in range(2):
        lam_re, lam_im = par_ref[d, 0:1, :], par_ref[d, 1:2, :]
        step = jnp.exp(par_ref[d, 2:3, :])
        ar, ai = lam_re * step, lam_im * step
        mag = jnp.exp(kk * ar)
        pw_re, pw_im = mag * jnp.cos(kk * ai), mag * jnp.sin(kk * ai)
        th = jnp.tanh(0.5 * ar)
        em1 = 2.0 * th / (1.0 - th)
        sh = jnp.sin(0.5 * ai)
        n_re = em1 * jnp.cos(ai) - 2.0 * sh * sh
        n_im = (em1 + 1.0) * jnp.sin(ai)
        den = lam_re * lam_re + lam_im * lam_im
        co_re = (n_re * lam_re + n_im * lam_im) / den
        co_im = (n_im * lam_re - n_re * lam_im) / den
        bt_re = jnp.concatenate([bre_ref[d], lane_pad], axis=1).T[:S]
        bt_im = jnp.concatenate([bim_ref[d], lane_pad], axis=1).T[:S]
        bb_re, bb_im = cmul(co_re, co_im, bt_re, bt_im)
        cc_re = jnp.concatenate([cre_ref[d, 0], cre_ref[d, 1]], axis=1)
        cc_im = jnp.concatenate([cim_ref[d, 0], cim_ref[d, 1]], axis=1)
        ks = [L - 1 - t for t in range(L)] if d == 0 else list(range(L))
        r_re, r_im = rows_of_powers(pw_re, ks, 2), rows_of_powers(pw_im, ks, 2)
        bbt_re, bbt_im = jnp.concatenate([bb_re] * (2 * L), axis=0), jnp.concatenate([bb_im] * (2 * L), axis=0)
        v_re, v_im = cmul(r_re, r_im, bbt_re, bbt_im)
        bst_ref[:, (2 * d) * LANES:(2 * d + 1) * LANES] = jnp.where(own_group, v_re, 0.0).astype(BF16)
        bst_ref[:, (2 * d + 1) * LANES:(2 * d + 2) * LANES] = jnp.where(own_group, v_im, 0.0).astype(BF16)
        ks = [t + 1 for t in range(L)] if d == 0 else [L - t for t in range(L)]
        r_re, r_im = rows_of_powers(pw_re, ks, 2), rows_of_powers(pw_im, ks, 2)
        cct_re, cct_im = jnp.concatenate([cc_re] * (2 * L), axis=0), jnp.concatenate([cc_im] * (2 * L), axis=0)
        v_re, v_im = cmul(cct_re, cct_im, r_re, r_im)
        cst_ref[(2 * d) * LANES:(2 * d + 1) * LANES, :] = jnp.where(own_group, v_re, 0.0).T.astype(BF16)
        cst_ref[(2 * d + 1) * LANES:(2 * d + 2) * LANES, :] = jnp.where(own_group, -v_im, 0.0).T.astype(BF16)
        ks = list(range(L)) if d == 0 else [L - 1 - j for j in range(L)]
        r_re, r_im = rows_of_powers(pw_re, ks, 1), rows_of_powers(pw_im, ks, 1)
        cl_re, cl_im = cmul(jnp.concatenate([cc_re] * L, axis=0), jnp.concatenate([cc_im] * L, axis=0), r_re, r_im)
        per_group = []
        for g in range(2):
            mine = (lane // P) == g
            per_group.append(_dot_nt_f32(jnp.where(mine, bb_re, 0.0), cl_re)
                             - _dot_nt_f32(jnp.where(mine, bb_im, 0.0), cl_im))
        lag_tables.append(per_group)
        row8 = lax.broadcasted_iota(jnp.int32, (SUBLANES, LANES), 0)
        n_chunks = ((row8 + 1) if d == 0 else (SUBLANES - row8)).astype(F32) * float(L)
        mag8 = jnp.exp(n_chunks * ar)
        a_ref[2 * d] = mag8 * jnp.cos(n_chunks * ai)
        a_ref[2 * d + 1] = mag8 * jnp.sin(n_chunks * ai)

    def shift_right(x, s):
        x0, x1 = x[:, :LANES], x[:, LANES:]
        a, r = divmod(s, LANES)
        r0 = pltpu.roll(x0, r, axis=1) if r else x0
        r1 = pltpu.roll(x1, r, axis=1) if r else x1
        if a == 0:
            return jnp.concatenate([jnp.where(lane >= r, r0, 0.0), jnp.where(lane >= r, r1, r0)], axis=1)
        return jnp.concatenate([jnp.zeros_like(x0), jnp.where(lane >= r, r0, 0.0)], axis=1)

    def shift_left(x, s):
        x0, x1 = x[:, :LANES], x[:, LANES:]
        a, r = divmod(s, LANES)
        r0 = pltpu.roll(x0, LANES - r, axis=1) if r else x0
        r1 = pltpu.roll(x1, LANES - r, axis=1) if r else x1
        if a == 0:
            return jnp.concatenate([jnp.where(lane < LANES - r, r0, r1), jnp.where(lane < LANES - r, r1, 0.0)], axis=1)
        return jnp.concatenate([jnp.where(lane < LANES - r, r1, 0.0), jnp.zeros_like(x0)], axis=1)

    for g in range(2):
        kf, kb = lag_tables[0][g], lag_tables[1][g]
        for tau in range(L):
            blk = shift_right(kf, S * tau) + shift_left(kb, S * (L - 1 - tau))
            toep_ref[g, tau * S:(tau + 1) * S, :] = blk.astype(BF16)


def _s5_tables(lam_re, lam_im, log_step, b_re, b_im, c_re, c_im, d_skip):
    _, G, P = lam_re.shape
    S, L = S5_GROUP_DIM, S5_CHUNK
    assert P == S5_STATE and 2 * P == LANES and 2 * L * S == 2 * MXU_TILE
    pairs = G // 2
    par = jnp.stack([lam_re.reshape(2, pairs, 2 * P), lam_im.reshape(2, pairs, 2 * P),
                     jnp.repeat(log_step, P, axis=-1).reshape(2, pairs, 2 * P)], axis=2)
    par = jnp.transpose(par, (1, 0, 2, 3)).astype(F32)
    bshape = (2, pairs, 2 * P, S)
    cshape = (2, pairs, 2, S, P)
    pw = 2 * L * S
    bspec = pl.BlockSpec((2, None, 2 * P, S), lambda g: (0, g, 0, 0))
    cspec = pl.BlockSpec((2, None, 2, S, P), lambda g: (0, g, 0, 0, 0))
    toep, bst, cst, a_chunk = pl.pallas_call(
        _s5_table_kernel,
        grid=(pairs,),
        in_specs=[pl.BlockSpec((None, 2, 3, 2 * P), lambda g: (g, 0, 0, 0)), bspec, bspec, cspec, cspec],
        out_specs=[pl.BlockSpec((None, 2, MXU_TILE, MXU_TILE), lambda g: (g, 0, 0, 0)),
                   pl.BlockSpec((None, pw, pw), lambda g: (g, 0, 0)),
                   pl.BlockSpec((None, pw, pw), lambda g: (g, 0, 0)),
                   pl.BlockSpec((None, 4, SUBLANES, LANES), lambda g: (g, 0, 0, 0))],
        out_shape=[jax.ShapeDtypeStruct((pairs, 2, MXU_TILE, MXU_TILE), BF16),
                   jax.ShapeDtypeStruct((pairs, pw, pw), BF16),
                   jax.ShapeDtypeStruct((pairs, pw, pw), BF16),
                   jax.ShapeDtypeStruct((pairs, 4, SUBLANES, LANES), F32)],
        compiler_params=_params("parallel"),
        name="s5_tables",
    )(par, b_re.reshape(bshape), b_im.reshape(bshape), c_re.reshape(cshape), c_im.reshape(cshape))
    d_pair = jnp.broadcast_to(d_skip.astype(F32).reshape(pairs, 2, 1, S), (pairs, 2, L, S))
    return toep, bst, cst, a_chunk, d_pair.reshape(pairs, 1, pw)


def _s5(u_ctx, u_lat, tables):
    toep, bst, cst, a_pow, d_pair = tables
    b, n_c, w = u_ctx.shape
    n_l = u_lat.shape[1]
    L = S5_CHUNK
    n_ctx, n_all = n_c // L, (n_c + n_l) // L
    nblk = w // LANES
    pw = 2 * L * S5_GROUP_DIM
    npair = S5_BLOCK_PAIRS
    sel_in, sel_out = _s5_selectors()
    wspec = lambda shape: pl.BlockSpec((npair,) + shape, lambda g, bi: (g,) + (0,) * len(shape),
                                       pipeline_mode=pl.Buffered(1))
    return pl.pallas_call(
        functools.partial(_s5_kernel, n_ctx=n_ctx, n_all=n_all),
        grid=(nblk, b),
        in_specs=[pl.BlockSpec((None, n_c, LANES), lambda g, bi: (bi, 0, g)),
                  pl.BlockSpec((None, n_l, LANES), lambda g, bi: (bi, 0, g)),
                  _const_spec(sel_in.shape), _const_spec(sel_out.shape),
                  wspec((2, MXU_TILE, MXU_TILE)), wspec((pw, pw)), wspec((pw, pw)),
                  wspec((4, SUBLANES, LANES)), wspec((1, pw))],
        out_specs=[pl.BlockSpec((None, n_c, LANES), lambda g, bi: (bi, 0, g)),
                   pl.BlockSpec((None, n_l, LANES), lambda g, bi: (bi, 0, g))],
        out_shape=[jax.ShapeDtypeStruct((b, n_c, w), F32), jax.ShapeDtypeStruct((b, n_l, w), F32)],
        scratch_shapes=[pltpu.VMEM((n_all, L * LANES), BF16), pltpu.VMEM((n_all, L * LANES), BF16),
                        pltpu.VMEM((npair, n_all, pw), F32), pltpu.VMEM((npair, n_all, pw), F32),
                        pltpu.VMEM((npair, n_all, pw), F32), pltpu.VMEM((2, n_all, SUBLANES * LANES), BF16)],
        compiler_params=_params("parallel", "arbitrary"),
        name="s5_scan",
    )(u_ctx, u_lat, sel_in, sel_out, toep, bst, cst, a_pow, d_pair)


def _mla_kernel(*refs, tk, n_steps):
    if n_steps:
        q_ref, kc_ref, vc_ref, k_ref, v_ref, o_ref, qt_ref, s_ref, m_ref, l_ref, acc_ref = refs
    else:
        q_ref, kc_ref, vc_ref, o_ref, qt_ref, s_ref, m_ref, l_ref, acc_ref = refs
    tq = q_ref.shape[0]
    ncb = tq // MLA_Q_BLOCK
    kp = MLA_K_PIECE
    qt_ref[...] = q_ref[...].astype(F32).T.astype(BF16)

    def chunk(load_k, load_vt, nkeys, first):
        nr = nkeys // kp

        def score_piece(c, r):
            st = _dot(load_k(r), qt_ref[:, c * MLA_Q_BLOCK:(c + 1) * MLA_Q_BLOCK])
            s_ref[c % 2, r * kp:(r + 1) * kp, :] = st
            return jnp.max(st, axis=0, keepdims=True)

        def block_stats(c, mx):
            if first:
                return mx, None
            m_old = m_ref[:, c * MLA_Q_BLOCK:(c + 1) * MLA_Q_BLOCK]
            m_new = jnp.maximum(m_old, mx)
            return m_new, jnp.exp2(m_old - m_new)

        def prob_piece(c, r, m_new):
            p = jnp.exp2(s_ref[c % 2, r * kp:(r + 1) * kp, :] - m_new)
            return jnp.sum(p, axis=0, keepdims=True), _dot(load_vt(r), p.astype(BF16))

        def finish(c, m_new, alpha, lsum, pv):
            cols = slice(c * MLA_Q_BLOCK, (c + 1) * MLA_Q_BLOCK)
            if first:
                l_ref[:, cols] = lsum
                acc_ref[:, cols] = pv
            else:
                l_ref[:, cols] = alpha * l_ref[:, cols] + lsum
                acc_ref[:, cols] = alpha * acc_ref[:, cols] + pv
            m_ref[:, cols] = m_new

        mx = None
        for r in range(nr):
            pm = score_piece(0, r)
            mx = pm if mx is None else jnp.maximum(mx, pm)
        for c in range(ncb):
            m_new, alpha = block_stats(c, mx)
            mx = lsum = pv = None
            for r in range(nr):
                if c + 1 < ncb:
                    pm = score_piece(c + 1, r)
                    mx = pm if mx is None else jnp.maximum(mx, pm)
                ls, pvr = prob_piece(c, r, m_new)
                lsum = ls if lsum is None else lsum + ls
                pv = pvr if pv is None else pv + pvr
            finish(c, m_new, alpha, lsum, pv)

    chunk(lambda r: kc_ref[r * kp:(r + 1) * kp, :], lambda r: vc_ref[:, r * kp:(r + 1) * kp],
          kc_ref.shape[0], True)
    if n_steps:
        def body(j, _):
            off = pl.multiple_of(j * tk, tk)
            chunk(lambda r: k_ref[pl.ds(off + r * kp, kp), :], lambda r: v_ref[:, pl.ds(off + r * kp, kp)],
                  tk, False)
            return 0
        lax.fori_loop(0, n_steps, body, 0)
    o_ref[...] = (acc_ref[...] / l_ref[...]).T.astype(BF16)


def _mla_attention(q, k_ctx, vt_ctx, k_lat=None, vt_lat=None, *, tq, tk=1024):
    b, t, _ = q.shape
    n_c = k_ctx.shape[1]
    in_specs = [pl.BlockSpec((None, tq, MLA_HEAD_PAD), lambda bi, h, i: (bi, i, h)),
                pl.BlockSpec((None, n_c, MLA_HEAD_PAD), lambda bi, h, i: (bi, 0, h)),
                pl.BlockSpec((None, MLA_V, n_c), lambda bi, h, i: (bi, h, 0))]
    args = [q, k_ctx, vt_ctx]
    n_steps = 0
    if k_lat is not None:
        n_l = k_lat.shape[1]
        n_steps = n_l // tk
        in_specs += [pl.BlockSpec((None, n_l, MLA_HEAD_PAD), lambda bi, h, i: (bi, 0, h)),
                     pl.BlockSpec((None, MLA_V, n_l), lambda bi, h, i: (bi, h, 0))]
        args += [k_lat, vt_lat]
    return pl.pallas_call(
        functools.partial(_mla_kernel, tk=tk, n_steps=n_steps),
        grid=(b, MLA_HEADS, t // tq),
        in_specs=in_specs,
        out_specs=pl.BlockSpec((None, tq, MLA_V), lambda bi, h, i: (bi, i, h)),
        out_shape=jax.ShapeDtypeStruct((b, t, MLA_HEADS * MLA_V), BF16),
        scratch_shapes=[pltpu.VMEM((MLA_HEAD_PAD, tq), BF16), pltpu.VMEM((2, max(tk, n_c), MLA_Q_BLOCK), F32),
                        pltpu.VMEM((1, tq), F32), pltpu.VMEM((1, tq), F32), pltpu.VMEM((MLA_V, tq), F32)],
        compiler_params=_params("parallel", "parallel", "arbitrary"),
        name="mla_attention",
    )(*args)


def _proj_c_kernel(h_ref, sh_ref, sc_ref, gn_ref, wc_ref, wvt_ref, gqk_ref, cos_ref, sin_ref, bd_ref,
                   q_ref, k_ref, vt_ref):
    qw = WIN_HEADS * WIN_HEAD_DIM
    kw = WIN_KV_HEADS * WIN_HEAD_DIM
    bd = bd_ref[...]
    tm = h_ref.shape[0]
    rb = min(PROJ_ROWS, tm)
    lane = lax.broadcasted_iota(jnp.int32, (rb, LANES), 1)
    first_quarter = (lane % (WIN_HEAD_DIM // 2)) < (WIN_HEAD_DIM // 4)
    for r in range(tm // rb):
        rows = slice(r * rb, (r + 1) * rb)
        a = (_rms(h_ref[rows, :], gn_ref[...]) * (1.0 + sc_ref[...]) + sh_ref[...]).astype(BF16)
        p = _dot(a, wc_ref[...])
        vt_ref[:, rows] = _dot_nt(wvt_ref[...], a).astype(BF16)
        cos, sin = cos_ref[rows, :], sin_ref[rows, :]
        for j in range((qw + kw) // MXU_TILE):
            xh = p[:, j * MXU_TILE:(j + 1) * MXU_TILE]
            ssq = _dot((xh * xh).astype(BF16), bd)
            xn = xh * lax.rsqrt(ssq * (1.0 / WIN_HEAD_DIM) + NORM_EPS) * gqk_ref[:, j * MXU_TILE:(j + 1) * MXU_TILE]
            for c in range(MXU_TILE // LANES):
                xc = xn[:, c * LANES:(c + 1) * LANES]
                fwd = pltpu.roll(xc, WIN_HEAD_DIM // 4, axis=1)
                bwd = pltpu.roll(xc, LANES - WIN_HEAD_DIM // 4, axis=1)
                y = (xc * cos + jnp.where(first_quarter, -bwd, fwd) * sin).astype(BF16)
                col = j * MXU_TILE + c * LANES
                if col < qw:
                    q_ref[rows, col:col + LANES] = y
                else:
                    k_ref[rows, col - qw:col - qw + LANES] = y


def _proj_c(h, mods, ctx_row, tm, wts, cos2, sin2):
    b, t, d = h.shape
    qw = WIN_HEADS * WIN_HEAD_DIM
    kw = WIN_KV_HEADS * WIN_HEAD_DIM
    tile = lambda w: pl.BlockSpec((None, tm, w), lambda bi, i: (bi, i, 0))
    tab = pl.BlockSpec((tm, LANES), lambda bi, i: (i, 0))
    in_specs = ([tile(d)] + _mod_specs(d, (0, 1), ctx_row)
                + [_const_spec(wts[k].shape) for k in ("gn", "wc", "wvt", "gqk")]
                + [tab, tab, _const_spec(wts["bd"].shape)])
    return pl.pallas_call(
        _proj_c_kernel,
        grid=(b, t // tm),
        in_specs=in_specs,
        out_specs=[tile(qw), tile(kw), pl.BlockSpec((None, kw, tm), lambda bi, i: (bi, 0, i))],
        out_shape=[jax.ShapeDtypeStruct((b, t, qw), BF16),
                   jax.ShapeDtypeStruct((b, t, kw), BF16),
                   jax.ShapeDtypeStruct((b, kw, t), BF16)],
        compiler_params=_params("parallel", "parallel"),
        name="proj_c",
    )(h, mods, mods, wts["gn"], wts["wc"], wts["wvt"], wts["gqk"], cos2, sin2, wts["bd"])


def _prep_proj_c(norm_mix, c_w_in, q_norm, k_norm):
    gqk = jnp.concatenate([jnp.tile(q_norm * (WIN_SCALE * LOG2_E), WIN_HEADS), jnp.tile(k_norm, WIN_KV_HEADS)])
    idx = np.arange(MXU_TILE) // WIN_HEAD_DIM
    bd = jnp.asarray(idx[:, None] == idx[None, :], BF16)
    qk = (WIN_HEADS + WIN_KV_HEADS) * WIN_HEAD_DIM
    bound = 1.01 * WIN_HEAD_DIM * jnp.max(jnp.abs(q_norm * (WIN_SCALE * LOG2_E))) * jnp.max(jnp.abs(k_norm))
    return dict(gn=norm_mix[None, :], wc=c_w_in[:, :qk].astype(BF16), wvt=c_w_in[:, qk:].T.astype(BF16),
                gqk=gqk[None, :], bd=bd, bound=bound.astype(F32))


def _win_kernel(sc_ref, q_ref, k_ref, vt_ref, kc_ref, vct_ref, o_ref, ot_ref, *, tq, band, n_lat):
    i = pl.program_id(1)
    start = pl.multiple_of(jnp.clip(i * tq - WINDOW, 0, n_lat - band), WINDOW)
    hd, grp = WIN_HEAD_DIM, WIN_GROUP
    bound = sc_ref[WIN_HEADS]
    qt = q_ref[...].astype(F32).T.astype(BF16)
    k_pos = start + lax.broadcasted_iota(jnp.int32, (band, tq), 0)
    q_pos = i * tq + lax.broadcasted_iota(jnp.int32, (band, tq), 1)
    bias1 = jnp.where(jnp.abs(k_pos - q_pos) <= WINDOW, 0.0, NEG_BIG)
    bias = jnp.concatenate([bias1] * grp, axis=1)
    zeros = jnp.zeros((hd, grp * tq), BF16)

    def group_inputs(kv):
        qg = jnp.concatenate([qt[(kv * grp + g) * hd:(kv * grp + g + 1) * hd, :] for g in range(grp)], axis=1)
        qg = jnp.concatenate([qg, zeros] if kv % 2 == 0 else [zeros, qg], axis=0)
        col = (kv // 2) * LANES
        sink = jnp.concatenate([jnp.full((1, tq), sc_ref[kv * grp + g], F32) for g in range(grp)], axis=1)
        return qg, col, sink

    def weighted_values(kv, p_ctx, p_loc):
        return (_dot(vct_ref[kv * hd:(kv + 1) * hd, :], p_ctx.astype(BF16))
                + _dot(vt_ref[kv * hd:(kv + 1) * hd, pl.ds(start, band)], p_loc.astype(BF16)))

    def regroup(ot):
        return jnp.concatenate([ot[:, g * tq:(g + 1) * tq] for g in range(grp)], axis=0)

    def one_pass(kv):
        qg, col, sink = group_inputs(kv)
        p_ctx = jnp.exp2(_dot(kc_ref[:, col:col + LANES], qg) - bound)
        p_loc = jnp.exp2(_dot(k_ref[pl.ds(start, band), col:col + LANES], qg) + (bias - bound))
        den = (jnp.sum(p_loc, axis=0, keepdims=True) + jnp.sum(p_ctx, axis=0, keepdims=True)
               + jnp.exp2(sink - bound))
        ot_ref[kv * grp * hd:(kv + 1) * grp * hd, :] = regroup(weighted_values(kv, p_ctx, p_loc) / den)
        return den

    def two_pass(kv):
        qg, col, sink = group_inputs(kv)
        s_ctx = _dot(kc_ref[:, col:col + LANES], qg)
        s_loc = _dot(k_ref[pl.ds(start, band), col:col + LANES], qg) + bias
        m = jnp.maximum(jnp.maximum(jnp.max(s_loc, axis=0, keepdims=True),
                                    jnp.max(s_ctx, axis=0, keepdims=True)), sink)
        p_loc = jnp.exp2(s_loc - m)
        p_ctx = jnp.exp2(s_ctx - m)
        den = (jnp.sum(p_loc, axis=0, keepdims=True) + jnp.sum(p_ctx, axis=0, keepdims=True)
               + jnp.exp2(sink - m))
        ot_ref[kv * grp * hd:(kv + 1) * grp * hd, :] = regroup(weighted_values(kv, p_ctx, p_loc) / den)

    lo = hi = None
    for kv in range(WIN_KV_HEADS):
        den = one_pass(kv)
        lo = den if lo is None else jnp.minimum(lo, den)
        hi = den if hi is None else jnp.maximum(hi, den)
    unsafe = jnp.logical_or(jnp.min(lo) < SOFTMAX_DEN_MIN, jnp.max(hi) > SOFTMAX_DEN_MAX)

    @pl.when(unsafe)
    def _():
        for kv in range(WIN_KV_HEADS):
            two_pass(kv)

    o_ref[...] = ot_ref[...].T.astype(BF16)


def _win_attention(q, k, vt, k_ctx, vt_ctx, sink, score_bound, *, tq):
    b, n, qw = q.shape
    n_c = k_ctx.shape[1]
    kw = k.shape[2]
    band = tq + 2 * WINDOW
    full = lambda r, w: pl.BlockSpec((None, r, w), lambda bi, i: (bi, 0, 0))
    return pl.pallas_call(
        functools.partial(_win_kernel, tq=tq, band=band, n_lat=n),
        grid=(b, n // tq),
        in_specs=[pl.BlockSpec(memory_space=pltpu.SMEM),
                  pl.BlockSpec((None, tq, qw), lambda bi, i: (bi, i, 0)),
                  full(n, kw), full(kw, n), full(n_c, kw), full(kw, n_c)],
        out_specs=pl.BlockSpec((None, tq, qw), lambda bi, i: (bi, i, 0)),
        out_shape=jax.ShapeDtypeStruct((b, n, qw), BF16),
        scratch_shapes=[pltpu.VMEM((qw, tq), F32)],
        compiler_params=_params("parallel", "arbitrary"),
        name="win_attention",
    )(jnp.concatenate([sink.astype(F32) * LOG2_E, score_bound.reshape(1)]), q, k, vt, k_ctx, vt_ctx)


def _post_kernel(*refs, s5_width, n_chunks):
    if s5_width:
        (h_ref, g_ref, sh_ref, sc_ref, g2_ref, gn_ref, yg_ref, o_ref, wglu_ref, bglu_ref, wo_ref,
         wg_ref, wu_ref, wd_ref, out_ref, a_ref, acc_ref) = refs
        yg = yg_ref[...]
        s5 = yg * jax.nn.sigmoid(_dot(yg.astype(BF16), wglu_ref[...]) + bglu_ref[...])
        mix = _dot(s5.astype(BF16), wo_ref[:s5_width, :]) + _dot(o_ref[...], wo_ref[s5_width:, :])
    else:
        (h_ref, g_ref, sh_ref, sc_ref, g2_ref, gn_ref, o_ref, wo_ref,
         wg_ref, wu_ref, wd_ref, out_ref, a_ref, acc_ref) = refs
        mix = _dot(o_ref[...], wo_ref[...])
    h1 = h_ref[...] + g_ref[...] * mix
    a_ref[...] = (_rms(h1, gn_ref[...]) * (1.0 + sc_ref[...]) + sh_ref[...]).astype(BF16)
    acc_ref[...] = jnp.zeros_like(acc_ref)

    def body(c, _):
        a = a_ref[...]
        cols = pl.ds(pl.multiple_of(c * MXU_TILE, MXU_TILE), MXU_TILE)
        act = _silu(_dot(a, wg_ref[:, cols])) * _dot(a, wu_ref[:, cols])
        acc_ref[...] += _dot(act.astype(BF16), wd_ref[cols, :])
        return 0

    lax.fori_loop(0, n_chunks, body, 0, unroll=True)
    out_ref[...] = h1 + g2_ref[...] * acc_ref[...]


def _post(h, mods, ctx_row, tm, wts, ffn, layer, o, yg=None):
    b, t, d = h.shape
    tile = lambda w: pl.BlockSpec((None, tm, w), lambda bi, i: (bi, i, 0))
    s5_width = 0 if yg is None else yg.shape[2]
    in_specs = [tile(d)] + _mod_specs(d, (2, 3, 4, 5), ctx_row) + [_const_spec(wts["gn"].shape)]
    args = [h, mods, mods, mods, mods, wts["gn"]]
    if yg is not None:
        in_specs += [tile(s5_width), tile(o.shape[2]), _const_spec(wts["wglu"].shape),
                     _const_spec(wts["bglu"].shape)]
        args += [yg, o, wts["wglu"], wts["bglu"]]
    else:
        in_specs += [tile(o.shape[2])]
        args += [o]
    in_specs.append(_const_spec(wts["wo"].shape))
    args.append(wts["wo"])
    for k in ("wg", "wu", "wd"):
        in_specs.append(_const_spec(ffn[k].shape, layer))
        args.append(ffn[k])
    return pl.pallas_call(
        functools.partial(_post_kernel, s5_width=s5_width, n_chunks=ffn["wg"].shape[2] // MXU_TILE),
        grid=(b, t // tm),
        in_specs=in_specs,
        out_specs=tile(d),
        out_shape=jax.ShapeDtypeStruct((b, t, d), F32),
        scratch_shapes=[pltpu.VMEM((tm, d), BF16), pltpu.VMEM((tm, d), F32)],
        compiler_params=_params("parallel", "parallel"),
        name="post_ffn",
    )(*args)


def _prep_post(norm_ffn, w_out, w_glu=None, b_glu=None):
    wts = dict(gn=norm_ffn[None, :], wo=w_out.astype(BF16))
    if w_glu is not None:
        wts.update(wglu=w_glu.astype(BF16), bglu=b_glu[None, :])
    return wts


def kernel(x, c, ctx, c_ctx, ada_w, ada_b, norm_mix, norm_ffn, ffn_w_gate, ffn_w_up, ffn_w_down,
           a_w_in, a_w_out, s5_lam_re, s5_lam_im, s5_log_step, s5_b_re, s5_b_im, s5_c_re, s5_c_im,
           s5_d, s5_w_glu, s5_b_glu, mla_qa_norm, mla_w_q_b, mla_kva_norm, mla_w_kv_b,
           mla_q_norm, mla_k_norm, c_w_in, c_w_out, c_q_norm, c_k_norm, c_sink):
    b, n, d = x.shape
    n_c = ctx.shape[1]
    depth = ada_w.shape[0]
    assert b + 1 <= MOD_ROWS and n % 1024 == 0 and n_c % (S5_CHUNK * SUBLANES) == 0
    rows = n // GRID_W
    tm_lat, tm_ctx, tm_proj = 512, n_c, min(n, 1024)

    cond = jnp.zeros((MOD_ROWS, d), F32).at[:b].set(c).at[b].set(c_ctx)
    mods = _ada_modulation(cond, ada_w, ada_b)
    mods = mods.reshape(depth, MOD_ROWS, N_MOD, 1, d)

    cos_a, sin_a = _grid_rope_tables(rows, MLA_ROPE)
    cs_a_lat = np.concatenate([cos_a, sin_a], axis=1)
    cs_a_ctx = np.concatenate([np.ones((n_c, MLA_ROPE), np.float32), np.zeros((n_c, MLA_ROPE), np.float32)], axis=1)
    cos_c, sin_c = _grid_rope_tables(rows, WIN_HEAD_DIM)
    cos_c2, sin_c2 = np.tile(cos_c, (1, 2)), np.tile(sin_c, (1, 2))
    one_c, zero_c = np.ones((n_c, LANES), np.float32), np.zeros((n_c, LANES), np.float32)
    assert ffn_w_gate.shape[2] % MXU_TILE == 0
    ffn = dict(wg=ffn_w_gate.astype(BF16), wu=ffn_w_up.astype(BF16), wd=ffn_w_down.astype(BF16))

    h_ctx, h_lat = ctx, x
    for i in range(depth):
        need_ctx = i < depth - 1
        j = i // 2
        m_i = mods[i]
        if i % 2 == 0:
            pw = _prep_proj_a(norm_mix[i], a_w_in[j], mla_qa_norm[j], mla_w_q_b[j], mla_kva_norm[j],
                              mla_w_kv_b[j], mla_q_norm[j], mla_k_norm[j])
            u_l, q_l, k_l, vt_l = _proj_a(h_lat, m_i, None, tm_lat, pw, cs_a_lat)
            u_c, q_c, k_c, vt_c = _proj_a(h_ctx, m_i, b, tm_ctx, pw, cs_a_ctx)
            tables = _s5_tables(s5_lam_re[j], s5_lam_im[j], s5_log_step[j], s5_b_re[j], s5_b_im[j],
                                s5_c_re[j], s5_c_im[j], s5_d[j])
            yg_c, yg_l = _s5(u_c, u_l, tables)
            o_l = _mla_attention(q_l, k_c, vt_c, k_l, vt_l, tq=n)
            post_w = _prep_post(norm_ffn[i], a_w_out[j], s5_w_glu[j], s5_b_glu[j])
            h_lat_new = _post(h_lat, m_i, None, tm_lat, post_w, ffn, i, o_l, yg_l)
            if need_ctx:
                o_c = _mla_attention(q_c, k_c, vt_c, tq=n_c)
                h_ctx = _post(h_ctx, m_i, b, tm_ctx, post_w, ffn, i, o_c, yg_c)
            h_lat = h_lat_new
        else:
            pw = _prep_proj_c(norm_mix[i], c_w_in[j], c_q_norm[j], c_k_norm[j])
            q_l, k_l, vt_l = _proj_c(h_lat, m_i, None, tm_proj, pw, cos_c2, sin_c2)
            q_c, k_c, vt_c = _proj_c(h_ctx, m_i, b, tm_ctx, pw, one_c, zero_c)
            o_l = _win_attention(q_l, k_l, vt_l, k_c, vt_c, c_sink[j], pw["bound"], tq=256)
            post_w = _prep_post(norm_ffn[i], c_w_out[j])
            h_lat_new = _post(h_lat, m_i, None, tm_lat, post_w, ffn, i, o_l)
            if need_ctx:
                raise NotImplementedError("context queries of a windowed layer")
            h_lat = h_lat_new
    return h_lat
```

```python
import functools
import math

import jax
import jax.numpy as jnp
import numpy as np
from jax import lax
from jax.experimental import pallas as pl
from jax.experimental.pallas import tpu as pltpu

F32 = jnp.float32
BF16 = jnp.bfloat16

GRID_W = 64
NORM_EPS = 1e-6
ROPE_THETA = 10000.0
N_MOD = 6
S5_GROUP_DIM = 16
S5_STATE = 64
S5_CHUNK = 16
MLA_HEADS = 4
MLA_NOPE = 128
MLA_ROPE = 64
MLA_QK_DIM = MLA_NOPE + MLA_ROPE
MLA_V = 128
MLA_Q_RANK = 384
MLA_KV_RANK = 256
MLA_SCALE = MLA_QK_DIM ** -0.5
MLA_HEAD_PAD = 256
WIN_HEADS = 16
WIN_KV_HEADS = 4
WIN_GROUP = WIN_HEADS // WIN_KV_HEADS
WIN_HEAD_DIM = 64
WINDOW = 128
WIN_SCALE = WIN_HEAD_DIM ** -0.5
LANES = 128
SUBLANES = 8
MXU_TILE = 256
VMEM_LIMIT_BYTES = 48 * 1024 * 1024
NEG_BIG = -1e30
LOG2_E = math.log2(math.e)
PROJ_ROWS = 256
SOFTMAX_DEN_MIN = 2.0 ** -60
SOFTMAX_DEN_MAX = 2.0 ** 60
S5_BLOCK_PAIRS = 4
MLA_Q_BLOCK = 256
MLA_K_PIECE = 256
MOD_ROWS = 8


def _dot(a, b):
    return jnp.dot(a, b, preferred_element_type=F32)


def _dot_nt(a, b):
    return lax.dot_general(a, b, (((1,), (1,)), ((), ())), preferred_element_type=F32)


def _split_bf16(x):
    hi = x.astype(BF16)
    lo = (x - hi.astype(F32)).astype(BF16)
    return hi, lo


def _rms(x, gain):
    return x * lax.rsqrt(jnp.mean(x * x, axis=-1, keepdims=True) + NORM_EPS) * gain


def _silu(x):
    return x * jax.nn.sigmoid(x)


def _params(*sem):
    return pltpu.CompilerParams(dimension_semantics=sem, vmem_limit_bytes=VMEM_LIMIT_BYTES)


def _const_spec(shape, layer=None):
    if layer is None:
        nd = len(shape)
        return pl.BlockSpec(shape, lambda *_: (0,) * nd, pipeline_mode=pl.Buffered(1))
    nd = len(shape) - 1
    return pl.BlockSpec((None,) + tuple(shape[1:]), lambda *_: (layer,) + (0,) * nd,
                        pipeline_mode=pl.Buffered(1))


def _ada_kernel(cond_ref, w_ref, b_ref, o_ref):
    s = _silu(cond_ref[...])
    s_hi, s_lo = _split_bf16(s)
    w_hi, w_lo = _split_bf16(w_ref[...])
    o_ref[...] = _dot(s_hi, w_hi) + _dot(s_lo, w_hi) + _dot(s_hi, w_lo) + b_ref[...]


def _ada_modulation(cond, ada_w, ada_b):
    depth, d, n = ada_w.shape
    tn = 1024
    return pl.pallas_call(
        _ada_kernel,
        grid=(depth, n // tn),
        in_specs=[pl.BlockSpec((MOD_ROWS, d), lambda i, j: (0, 0)),
                  pl.BlockSpec((None, d, tn), lambda i, j: (i, 0, j)),
                  pl.BlockSpec((None, 1, tn), lambda i, j: (i, 0, j))],
        out_specs=pl.BlockSpec((None, MOD_ROWS, tn), lambda i, j: (i, 0, j)),
        out_shape=jax.ShapeDtypeStruct((depth, MOD_ROWS, n), F32),
        compiler_params=_params("arbitrary", "arbitrary"),
        name="ada_modulation",
    )(cond, ada_w, ada_b.reshape(depth, 1, n))


def _mod_specs(d, slots, ctx_row):
    def make(slot):
        if ctx_row is None:
            return pl.BlockSpec((None, None, 1, d), lambda b, i: (b, slot, 0, 0))
        return pl.BlockSpec((None, None, 1, d), lambda b, i: (ctx_row, slot, 0, 0))
    return [make(s) for s in slots]


def _grid_rope_tables(rows, rot_dim):
    n_freq = rot_dim // 4
    inv_freq = np.power(np.float32(ROPE_THETA), -np.arange(n_freq, dtype=np.float32) / np.float32(n_freq))
    ang_r = np.arange(rows, dtype=np.float32)[:, None] * inv_freq.astype(np.float32)
    ang_c = np.arange(GRID_W, dtype=np.float32)[:, None] * inv_freq.astype(np.float32)

    def expand(r, c):
        r = np.broadcast_to(r[:, None, :], (rows, GRID_W, n_freq))
        c = np.broadcast_to(c[None, :, :], (rows, GRID_W, n_freq))
        return np.concatenate([r, r, c, c], axis=-1).reshape(rows * GRID_W, rot_dim)

    return expand(np.cos(ang_r), np.cos(ang_c)), expand(np.sin(ang_r), np.sin(ang_c))


def _rot_perm_sign(rot_dim):
    q = rot_dim // 4
    idx = np.arange(rot_dim)
    perm = np.where((idx // q) % 2 == 0, idx + q, idx - q)
    sign = np.where((idx // q) % 2 == 0, -1.0, 1.0).astype(np.float32)
    return perm, sign


def _proj_a_kernel(h_ref, sh_ref, sc_ref, gn_ref, w1_ref, gqa_ref, wq_ref, gkv_ref, wkv_ref, wvt_ref,
                   gq_ref, gk_ref, cs_ref, msk_ref, u_ref, q_ref, k_ref, vt_ref):
    tm = h_ref.shape[0]
    rb = min(PROJ_ROWS, tm)
    msk = msk_ref[...]
    low_half = lax.broadcasted_iota(jnp.int32, (rb, LANES), 1) < MLA_ROPE

    def finish(xh, gain, cs, out_ref, rows, h):
        ssq = _dot((xh * xh).astype(BF16), msk)
        xn = xh * lax.rsqrt(ssq * (1.0 / MLA_QK_DIM) + NORM_EPS) * gain
        rr = xn[:, LANES:] * cs
        rot = rr + pltpu.roll(rr, MLA_ROPE, axis=1)
        base = h * MLA_HEAD_PAD
        out_ref[rows, base:base + LANES] = xn[:, :LANES].astype(BF16)
        out_ref[rows, base + LANES:base + 2 * LANES] = jnp.where(low_half, rot, 0.0).astype(BF16)

    for r in range(tm // rb):
        rows = slice(r * rb, (r + 1) * rb)
        a = _rms(h_ref[rows, :], gn_ref[...]) * (1.0 + sc_ref[...]) + sh_ref[...]
        p1 = _dot(a.astype(BF16), w1_ref[...])
        u_ref[rows, :] = p1[:, :512]
        cq = p1[:, 512:512 + MLA_Q_RANK]
        ckv = p1[:, 896:896 + MLA_KV_RANK]
        krr = p1[:, 1152:1280]
        qb = _dot(_rms(cq, gqa_ref[...]).astype(BF16), wq_ref[...])
        ckv_n = _rms(ckv, gkv_ref[...]).astype(BF16)
        kv = _dot(ckv_n, wkv_ref[...])
        vt_ref[:, rows] = _dot_nt(wvt_ref[...], ckv_n).astype(BF16)
        cs = cs_ref[rows, :]
        for h in range(MLA_HEADS):
            base = h * MLA_HEAD_PAD
            finish(qb[:, base:base + MLA_HEAD_PAD], gq_ref[:, base:base + MLA_HEAD_PAD], cs, q_ref, rows, h)
            kh = jnp.concatenate([kv[:, h * MLA_NOPE:(h + 1) * MLA_NOPE], krr], axis=1)
            finish(kh, gk_ref[:, base:base + MLA_HEAD_PAD], cs, k_ref, rows, h)


def _proj_a(h, mods, ctx_row, tm, wts, cs):
    b, t, d = h.shape
    qw = MLA_HEADS * MLA_HEAD_PAD
    grid = (b, t // tm)
    tile = lambda w: pl.BlockSpec((None, tm, w), lambda bi, i: (bi, i, 0))
    in_specs = ([tile(d)] + _mod_specs(d, (0, 1), ctx_row)
                + [_const_spec(wts[k].shape) for k in
                   ("gn", "w1", "gqa", "wq", "gkv", "wkv", "wvt", "gq", "gk")]
                + [pl.BlockSpec((tm, LANES), lambda bi, i: (i, 0)), _const_spec(wts["msk"].shape)])
    vw = MLA_HEADS * MLA_V
    return pl.pallas_call(
        _proj_a_kernel,
        grid=grid,
        in_specs=in_specs,
        out_specs=[tile(512), tile(qw), tile(qw), pl.BlockSpec((None, vw, tm), lambda bi, i: (bi, 0, i))],
        out_shape=[jax.ShapeDtypeStruct((b, t, 512), F32),
                   jax.ShapeDtypeStruct((b, t, qw), BF16),
                   jax.ShapeDtypeStruct((b, t, qw), BF16),
                   jax.ShapeDtypeStruct((b, vw, t), BF16)],
        compiler_params=_params("parallel", "parallel"),
        name="proj_a",
    )(h, mods, mods, wts["gn"], wts["w1"], wts["gqa"], wts["wq"], wts["gkv"], wts["wkv"], wts["wvt"],
      wts["gq"], wts["gk"], cs, wts["msk"])


def _prep_proj_a(norm_mix, a_w_in, qa_norm, w_q_b, kva_norm, w_kv_b, q_norm, k_norm):
    perm, sign = _rot_perm_sign(MLA_ROPE)
    s5w = a_w_in.shape[1] - (MLA_Q_RANK + MLA_KV_RANK + MLA_ROPE)
    assert s5w == 512
    kr = a_w_in[:, -MLA_ROPE:]
    w1 = jnp.concatenate([a_w_in, kr[:, perm] * sign], axis=1).astype(BF16)
    wq = w_q_b.reshape(MLA_Q_RANK, MLA_HEADS, MLA_QK_DIM)
    rope = wq[:, :, MLA_NOPE:]
    wq = jnp.concatenate([wq, rope[:, :, perm] * sign], axis=2)
    wq = wq.reshape(MLA_Q_RANK, MLA_HEADS * MLA_HEAD_PAD).astype(BF16)
    wkv3 = w_kv_b.reshape(MLA_KV_RANK, MLA_HEADS, MLA_NOPE + MLA_V)
    wkv = wkv3[:, :, :MLA_NOPE].reshape(MLA_KV_RANK, -1).astype(BF16)
    wvt = wkv3[:, :, MLA_NOPE:].reshape(MLA_KV_RANK, -1).T.astype(BF16)

    def head_gain(g, scale):
        gb = jnp.concatenate([g, g[MLA_NOPE:][perm]]) * scale
        return jnp.tile(gb, MLA_HEADS)[None, :]

    rows = np.arange(MLA_HEAD_PAD)[:, None] < MLA_QK_DIM
    msk = jnp.asarray(np.broadcast_to(rows, (MLA_HEAD_PAD, MLA_HEAD_PAD)), BF16)
    return dict(gn=norm_mix[None, :], w1=w1, gqa=qa_norm[None, :], wq=wq, gkv=kva_norm[None, :],
                wkv=wkv, wvt=wvt, gq=head_gain(q_norm, MLA_SCALE * LOG2_E), gk=head_gain(k_norm, 1.0),
                msk=msk)


def _s5_kernel(uc_ref, ul_ref, sel_in_ref, sel_out_ref, toep_ref, bst_ref, cst_ref, a_ref, d_ref,
               yc_ref, yl_ref, ub_ref, x_ref, z_ref, sin_ref, yb_ref, *, n_ctx, n_all):
    L, S = S5_CHUNK, S5_GROUP_DIM
    n_lat = n_all - n_ctx
    npair = S5_BLOCK_PAIRS
    half = SUBLANES * LANES
    for tl in range(L):
        u = jnp.concatenate([uc_ref[pl.ds(tl, n_ctx, stride=L), :], ul_ref[pl.ds(tl, n_lat, stride=L), :]], axis=0)
        ub_ref[:, tl * LANES:(tl + 1) * LANES] = u.astype(BF16)
    for pp in range(npair):
        for hh in range(2):
            xs = _dot(ub_ref[:, hh * half:(hh + 1) * half], sel_in_ref[pp]).astype(BF16)
            x_ref[pp, :, hh * LANES:(hh + 1) * LANES] = xs[:, :LANES]
            x_ref[pp, :, MXU_TILE + hh * LANES:MXU_TILE + (hh + 1) * LANES] = xs[:, LANES:]
    for pp in range(npair):
        z_ref[pp] = _dot(x_ref[pp], bst_ref[pp])
    t_ctx, t_all = n_ctx // SUBLANES, n_all // SUBLANES
    row = lax.broadcasted_iota(jnp.int32, (SUBLANES, LANES), 0)

    def cmul(a_re, a_im, b_re, b_im):
        return a_re * b_re - a_im * b_im, a_re * b_im + a_im * b_re

    def tile_scan(z_re, z_im, c_re, c_im, a_re, a_im, fwd):
        for sft in (1, 2, 4):
            k = sft - 1 if fwd else SUBLANES - sft
            p_re, p_im = a_re[k:k + 1, :], a_im[k:k + 1, :]
            amt = sft if fwd else SUBLANES - sft
            keep = (row >= sft) if fwd else (row < SUBLANES - sft)
            s_re = jnp.where(keep, pltpu.roll(z_re, amt, axis=0), 0.0)
            s_im = jnp.where(keep, pltpu.roll(z_im, amt, axis=0), 0.0)
            m_re, m_im = cmul(p_re, p_im, s_re, s_im)
            z_re, z_im = z_re + m_re, z_im + m_im
        m_re, m_im = cmul(a_re, a_im, c_re, c_im)
        s_re, s_im = z_re + m_re, z_im + m_im
        edge = (row == 0) if fwd else (row == SUBLANES - 1)
        amt = 1 if fwd else SUBLANES - 1
        in_re = jnp.where(edge, c_re, pltpu.roll(s_re, amt, axis=0))
        in_im = jnp.where(edge, c_im, pltpu.roll(s_im, amt, axis=0))
        last = SUBLANES - 1 if fwd else 0
        return in_re, in_im, s_re[last:last + 1, :], s_im[last:last + 1, :]

    def body(it, carry):
        jt = jnp.where(it < t_ctx, t_ctx - 1 - it, t_all + t_ctx - 1 - it)
        rf = pl.multiple_of(it * SUBLANES, SUBLANES)
        rb = pl.multiple_of(jt * SUBLANES, SUBLANES)
        new = []
        for pp in range(npair):
            cf_re, cf_im, cb_re, cb_im = carry[4 * pp:4 * pp + 4]
            zf = z_ref[pp, pl.ds(rf, SUBLANES), 0:2 * LANES]
            zb = z_ref[pp, pl.ds(rb, SUBLANES), 2 * LANES:4 * LANES]
            f_re, f_im, cf_re, cf_im = tile_scan(zf[:, :LANES], zf[:, LANES:], cf_re, cf_im,
                                                 a_ref[pp, 0], a_ref[pp, 1], True)
            b_re, b_im, cb_re, cb_im = tile_scan(zb[:, :LANES], zb[:, LANES:], cb_re, cb_im,
                                                 a_ref[pp, 2], a_ref[pp, 3], False)
            sin_ref[pp, pl.ds(rf, SUBLANES), 0:2 * LANES] = jnp.concatenate([f_re, f_im], axis=1)
            sin_ref[pp, pl.ds(rb, SUBLANES), 2 * LANES:4 * LANES] = jnp.concatenate([b_re, b_im], axis=1)
            new += [cf_re, cf_im, cb_re, cb_im]
        return tuple(new)

    zero = jnp.zeros((1, LANES), F32)
    lax.fori_loop(0, t_all, body, (zero,) * (4 * npair))
    for pp in range(npair):
        x = x_ref[pp]
        y = jnp.concatenate([_dot(x[:, :MXU_TILE], toep_ref[pp, 0]), _dot(x[:, MXU_TILE:], toep_ref[pp, 1])],
                            axis=1)
        y = y + _dot(sin_ref[pp].astype(BF16), cst_ref[pp]) + x.astype(F32) * d_ref[pp]
        yg = jax.nn.gelu(y).astype(BF16)
        for gl in range(2):
            for hh in range(2):
                g8 = 2 * pp + gl
                col = gl * MXU_TILE + hh * LANES
                yb_ref[hh, :, g8 * LANES:(g8 + 1) * LANES] = yg[:, col:col + LANES]
    for hh in range(2):
        for kk in range(SUBLANES // 2):
            two = _dot(yb_ref[hh], sel_out_ref[kk])
            for e in range(2):
                tl = hh * SUBLANES + 2 * kk + e
                yc_ref[pl.ds(tl, n_ctx, stride=L), :] = two[:n_ctx, e * LANES:(e + 1) * LANES]
                yl_ref[pl.ds(tl, n_lat, stride=L), :] = two[n_ctx:, e * LANES:(e + 1) * LANES]


def _s5_selectors():
    S = S5_GROUP_DIM
    r = np.arange(SUBLANES * LANES)[:, None]
    c = np.arange(2 * LANES)[None, :]
    k, l = r // LANES, r % LANES
    pp = np.arange(S5_BLOCK_PAIRS)[:, None, None]
    sel_in = (k == (c % LANES) // S) & (l == 2 * S * pp + S * (c // LANES) + c % S)
    kk = np.arange(SUBLANES // 2)[:, None, None]
    sel_out = (k == (c % LANES) // S) & (l == S * (2 * kk + c // LANES) + c % S)
    return jnp.asarray(sel_in, BF16), jnp.asarray(sel_out, BF16)


def _dot_nt_f32(a, b):
    a_hi, a_lo = _split_bf16(a)
    b_hi, b_lo = _split_bf16(b)
    return _dot_nt(a_hi, b_hi) + _dot_nt(a_lo, b_hi) + _dot_nt(a_hi, b_lo)


def _s5_table_kernel(par_ref, bre_ref, bim_ref, cre_ref, cim_ref, toep_ref, bst_ref, cst_ref, a_ref):
    L, S, P = S5_CHUNK, S5_GROUP_DIM, S5_STATE
    kk = lax.broadcasted_iota(jnp.int32, (3 * SUBLANES, LANES), 0).astype(F32)
    lane = lax.broadcasted_iota(jnp.int32, (L, LANES), 1)
    rows512 = lax.broadcasted_iota(jnp.int32, (2 * L * S, LANES), 0)
    lanes512 = lax.broadcasted_iota(jnp.int32, (2 * L * S, LANES), 1)
    own_group = (rows512 // (L * S)) == (lanes512 // P)
    lane_pad = jnp.zeros((2 * P, LANES - S), F32)

    def rows_of_powers(pw, ks, groups):
        one = jnp.concatenate([jnp.broadcast_to(pw[k:k + 1, :], (S, LANES)) for k in ks], axis=0)
        return jnp.concatenate([one] * groups, axis=0) if groups > 1 else one

    def cmul(a_re, a_im, b_re, b_im):
        return a_re * b_re - a_im * b_im, a_re * b_im + a_im * b_re

    lag_tables = []
    for d in range(2):
        lam_re, lam_im = par_ref[d, 0:1, :], par_ref[d, 1:2, :]
        step = jnp.exp(par_ref[d, 2:3, :])
        ar, ai = lam_re * step, lam_im * step
        mag = jnp.exp(kk * ar)
        pw_re, pw_im = mag * jnp.cos(kk * ai), mag * jnp.sin(kk * ai)
        th = jnp.tanh(0.5 * ar)
        em1 = 2.0 * th / (1.0 - th)
        sh = jnp.sin(0.5 * ai)
        n_re = em1 * jnp.cos(ai) - 2.0 * sh * sh
        n_im = (em1 + 1.0) * jnp.sin(ai)
        den = lam_re * lam_re + lam_im * lam_im
        co_re = (n_re * lam_re + n_im * lam_im) / den
        co_im = (n_im * lam_re - n_re * lam_im) / den
        bt_re = jnp.concatenate([bre_ref[d], lane_pad], axis=1).T[:S]
        bt_im = jnp.concatenate([bim_ref[d], lane_pad], axis=1).T[:S]
        bb_re, bb_im = cmul(co_re, co_im, bt_re, bt_im)
        cc_re = jnp.concatenate([cre_ref[d, 0], cre_ref[d, 1]], axis=1)
        cc_im = jnp.concatenate([cim_ref[d, 0], cim_ref[d, 1]], axis=1)
        ks = [L - 1 - t for t in range(L)] if d == 0 else list(range(L))
        r_re, r_im = rows_of_powers(pw_re, ks, 2), rows_of_powers(pw_im, ks, 2)
        bbt_re, bbt_im = jnp.concatenate([bb_re] * (2 * L), axis=0), jnp.concatenate([bb_im] * (2 * L), axis=0)
        v_re, v_im = cmul(r_re, r_im, bbt_re, bbt_im)
        bst_ref[:, (2 * d) * LANES:(2 * d + 1) * LANES] = jnp.where(own_group, v_re, 0.0).astype(BF16)
        bst_ref[:, (2 * d + 1) * LANES:(2 * d + 2) * LANES] = jnp.where(own_group, v_im, 0.0).astype(BF16)
        ks = [t + 1 for t in range(L)] if d == 0 else [L - t for t in range(L)]
        r_re, r_im = rows_of_powers(pw_re, ks, 2), rows_of_powers(pw_im, ks, 2)
        cct_re, cct_im = jnp.concatenate([cc_re] * (2 * L), axis=0), jnp.concatenate([cc_im] * (2 * L), axis=0)
        v_re, v_im = cmul(cct_re, cct_im, r_re, r_im)
        cst_ref[(2 * d) * LANES:(2 * d + 1) * LANES, :] = jnp.where(own_group, v_re, 0.0).T.astype(BF16)
        cst_ref[(2 * d + 1) * LANES:(2 * d + 2) * LANES, :] = jnp.where(own_group, -v_im, 0.0).T.astype(BF16)
        ks = list(range(L)) if d == 0 else [L - 1 - j for j in range(L)]
        r_re, r_im = rows_of_powers(pw_re, ks, 1), rows_of_powers(pw_im, ks, 1)
        cl_re, cl_im = cmul(jnp.concatenate([cc_re] * L, axis=0), jnp.concatenate([cc_im] * L, axis=0), r_re, r_im)
        per_group = []
        for g in range(2):
            mine = (lane // P) == g
            per_group.append(_dot_nt_f32(jnp.where(mine, bb_re, 0.0), cl_re)
                             - _dot_nt_f32(jnp.where(mine, bb_im, 0.0), cl_im))
        lag_tables.append(per_group)
        row8 = lax.broadcasted_iota(jnp.int32, (SUBLANES, LANES), 0)
        n_chunks = ((row8 + 1) if d == 0 else (SUBLANES - row8)).astype(F32) * float(L)
        mag8 = jnp.exp(n_chunks * ar)
        a_ref[2 * d] = mag8 * jnp.cos(n_chunks * ai)
        a_ref[2 * d + 1] = mag8 * jnp.sin(n_chunks * ai)

    def shift_right(x, s):
        x0, x1 = x[:, :LANES], x[:, LANES:]
        a, r = divmod(s, LANES)
        r0 = pltpu.roll(x0, r, axis=1) if r else x0
        r1 = pltpu.roll(x1, r, axis=1) if r else x1
        if a == 0:
            return jnp.concatenate([jnp.where(lane >= r, r0, 0.0), jnp.where(lane >= r, r1, r0)], axis=1)
        return jnp.concatenate([jnp.zeros_like(x0), jnp.where(lane >= r, r0, 0.0)], axis=1)

    def shift_left(x, s):
        x0, x1 = x[:, :LANES], x[:, LANES:]
        a, r = divmod(s, LANES)
        r0 = pltpu.roll(x0, LANES - r, axis=1) if r else x0
        r1 = pltpu.roll(x1, LANES - r, axis=1) if r else x1
        if a == 0:
            return jnp.concatenate([jnp.where(lane < LANES - r, r0, r1), jnp.where(lane < LANES - r, r1, 0.0)], axis=1)
        return jnp.concatenate([jnp.where(lane < LANES - r, r1, 0.0), jnp.zeros_like(x0)], axis=1)

    for g in range(2):
        kf, kb = lag_tables[0][g], lag_tables[1][g]
        for tau in range(L):
            blk = shift_right(kf, S * tau) + shift_left(kb, S * (L - 1 - tau))
            toep_ref[g, tau * S:(tau + 1) * S, :] = blk.astype(BF16)


def _s5_tables(lam_re, lam_im, log_step, b_re, b_im, c_re, c_im, d_skip):
    _, G, P = lam_re.shape
    S, L = S5_GROUP_DIM, S5_CHUNK
    assert P == S5_STATE and 2 * P == LANES and 2 * L * S == 2 * MXU_TILE
    pairs = G // 2
    par = jnp.stack([lam_re.reshape(2, pairs, 2 * P), lam_im.reshape(2, pairs, 2 * P),
                     jnp.repeat(log_step, P, axis=-1).reshape(2, pairs, 2 * P)], axis=2)
    par = jnp.transpose(par, (1, 0, 2, 3)).astype(F32)
    bshape = (2, pairs, 2 * P, S)
    cshape = (2, pairs, 2, S, P)
    pw = 2 * L * S
    bspec = pl.BlockSpec((2, None, 2 * P, S), lambda g: (0, g, 0, 0))
    cspec = pl.BlockSpec((2, None, 2, S, P), lambda g: (0, g, 0, 0, 0))
    toep, bst, cst, a_chunk = pl.pallas_call(
        _s5_table_kernel,
        grid=(pairs,),
        in_specs=[pl.BlockSpec((None, 2, 3, 2 * P), lambda g: (g, 0, 0, 0)), bspec, bspec, cspec, cspec],
        out_specs=[pl.BlockSpec((None, 2, MXU_TILE, MXU_TILE), lambda g: (g, 0, 0, 0)),
                   pl.BlockSpec((None, pw, pw), lambda g: (g, 0, 0)),
                   pl.BlockSpec((None, pw, pw), lambda g: (g, 0, 0)),
                   pl.BlockSpec((None, 4, SUBLANES, LANES), lambda g: (g, 0, 0, 0))],
        out_shape=[jax.ShapeDtypeStruct((pairs, 2, MXU_TILE, MXU_TILE), BF16),
                   jax.ShapeDtypeStruct((pairs, pw, pw), BF16),
                   jax.ShapeDtypeStruct((pairs, pw, pw), BF16),
                   jax.ShapeDtypeStruct((pairs, 4, SUBLANES, LANES), F32)],
        compiler_params=_params("parallel"),
        name="s5_tables",
    )(par, b_re.reshape(bshape), b_im.reshape(bshape), c_re.reshape(cshape), c_im.reshape(cshape))
    d_pair = jnp.broadcast_to(d_skip.astype(F32).reshape(pairs, 2, 1, S), (pairs, 2, L, S))
    return toep, bst, cst, a_chunk, d_pair.reshape(pairs, 1, pw)


def _s5(u_ctx, u_lat, tables):
    toep, bst, cst, a_pow, d_pair = tables
    b, n_c, w = u_ctx.shape
    n_l = u_lat.shape[1]
    L = S5_CHUNK
    n_ctx, n_all = n_c // L, (n_c + n_l) // L
    nblk = w // LANES
    pw = 2 * L * S5_GROUP_DIM
    npair = S5_BLOCK_PAIRS
    sel_in, sel_out = _s5_selectors()
    wspec = lambda shape: pl.BlockSpec((npair,) + shape, lambda g, bi: (g,) + (0,) * len(shape),
                                       pipeline_mode=pl.Buffered(1))
    return pl.pallas_call(
        functools.partial(_s5_kernel, n_ctx=n_ctx, n_all=n_all),
        grid=(nblk, b),
        in_specs=[pl.BlockSpec((None, n_c, LANES), lambda g, bi: (bi, 0, g)),
                  pl.BlockSpec((None, n_l, LANES), lambda g, bi: (bi, 0, g)),
                  _const_spec(sel_in.shape), _const_spec(sel_out.shape),
                  wspec((2, MXU_TILE, MXU_TILE)), wspec((pw, pw)), wspec((pw, pw)),
                  wspec((4, SUBLANES, LANES)), wspec((1, pw))],
        out_specs=[pl.BlockSpec((None, n_c, LANES), lambda g, bi: (bi, 0, g)),
                   pl.BlockSpec((None, n_l, LANES), lambda g, bi: (bi, 0, g))],
        out_shape=[jax.ShapeDtypeStruct((b, n_c, w), F32), jax.ShapeDtypeStruct((b, n_l, w), F32)],
        scratch_shapes=[pltpu.VMEM((n_all, L * LANES), BF16), pltpu.VMEM((npair, n_all, pw), BF16),
                        pltpu.VMEM((npair, n_all, pw), F32), pltpu.VMEM((npair, n_all, pw), F32),
                        pltpu.VMEM((2, n_all, SUBLANES * LANES), BF16)],
        compiler_params=_params("parallel", "arbitrary"),
        name="s5_scan",
    )(u_ctx, u_lat, sel_in, sel_out, toep, bst, cst, a_pow, d_pair)


def _mla_kernel(*refs, tk, n_steps):
    if n_steps:
        q_ref, kc_ref, vc_ref, k_ref, v_ref, o_ref, qt_ref, s_ref, m_ref, l_ref, acc_ref = refs
    else:
        q_ref, kc_ref, vc_ref, o_ref, qt_ref, s_ref, m_ref, l_ref, acc_ref = refs
    tq = q_ref.shape[0]
    ncb = tq // MLA_Q_BLOCK
    kp = MLA_K_PIECE
    qt_ref[...] = q_ref[...].astype(F32).T.astype(BF16)

    def chunk(load_k, load_vt, nkeys, first):
        nr = nkeys // kp

        def score_piece(c, r):
            st = _dot(load_k(r), qt_ref[:, c * MLA_Q_BLOCK:(c + 1) * MLA_Q_BLOCK])
            s_ref[c % 2, r * kp:(r + 1) * kp, :] = st
            return jnp.max(st, axis=0, keepdims=True)

        def block_stats(c, mx):
            if first:
                return mx, None
            m_old = m_ref[:, c * MLA_Q_BLOCK:(c + 1) * MLA_Q_BLOCK]
            m_new = jnp.maximum(m_old, mx)
            return m_new, jnp.exp2(m_old - m_new)

        def prob_piece(c, r, m_new):
            p = jnp.exp2(s_ref[c % 2, r * kp:(r + 1) * kp, :] - m_new)
            return jnp.sum(p, axis=0, keepdims=True), _dot(load_vt(r), p.astype(BF16))

        def finish(c, m_new, alpha, lsum, pv):
            cols = slice(c * MLA_Q_BLOCK, (c + 1) * MLA_Q_BLOCK)
            if first:
                l_ref[:, cols] = lsum
                acc_ref[:, cols] = pv
            else:
                l_ref[:, cols] = alpha * l_ref[:, cols] + lsum
                acc_ref[:, cols] = alpha * acc_ref[:, cols] + pv
            m_ref[:, cols] = m_new

        mx = None
        for r in range(nr):
            pm = score_piece(0, r)
            mx = pm if mx is None else jnp.maximum(mx, pm)
        for c in range(ncb):
            m_new, alpha = block_stats(c, mx)
            mx = lsum = pv = None
            for r in range(nr):
                if c + 1 < ncb:
                    pm = score_piece(c + 1, r)
                    mx = pm if mx is None else jnp.maximum(mx, pm)
                ls, pvr = prob_piece(c, r, m_new)
                lsum = ls if lsum is None else lsum + ls
                pv = pvr if pv is None else pv + pvr
            finish(c, m_new, alpha, lsum, pv)

    chunk(lambda r: kc_ref[r * kp:(r + 1) * kp, :], lambda r: vc_ref[:, r * kp:(r + 1) * kp],
          kc_ref.shape[0], True)
    if n_steps:
        def body(j, _):
            off = pl.multiple_of(j * tk, tk)
            chunk(lambda r: k_ref[pl.ds(off + r * kp, kp), :], lambda r: v_ref[:, pl.ds(off + r * kp, kp)],
                  tk, False)
            return 0
        lax.fori_loop(0, n_steps, body, 0)
    o_ref[...] = (acc_ref[...] / l_ref[...]).T.astype(BF16)


def _mla_attention(q, k_ctx, vt_ctx, k_lat=None, vt_lat=None, *, tq, tk=1024):
    b, t, _ = q.shape
    n_c = k_ctx.shape[1]
    in_specs = [pl.BlockSpec((None, tq, MLA_HEAD_PAD), lambda bi, h, i: (bi, i, h)),
                pl.BlockSpec((None, n_c, MLA_HEAD_PAD), lambda bi, h, i: (bi, 0, h)),
                pl.BlockSpec((None, MLA_V, n_c), lambda bi, h, i: (bi, h, 0))]
    args = [q, k_ctx, vt_ctx]
    n_steps = 0
    if k_lat is not None:
        n_l = k_lat.shape[1]
        n_steps = n_l // tk
        in_specs += [pl.BlockSpec((None, n_l, MLA_HEAD_PAD), lambda bi, h, i: (bi, 0, h)),
                     pl.BlockSpec((None, MLA_V, n_l), lambda bi, h, i: (bi, h, 0))]
        args += [k_lat, vt_lat]
    return pl.pallas_call(
        functools.partial(_mla_kernel, tk=tk, n_steps=n_steps),
        grid=(b, MLA_HEADS, t // tq),
        in_specs=in_specs,
        out_specs=pl.BlockSpec((None, tq, MLA_V), lambda bi, h, i: (bi, i, h)),
        out_shape=jax.ShapeDtypeStruct((b, t, MLA_HEADS * MLA_V), BF16),
        scratch_shapes=[pltpu.VMEM((MLA_HEAD_PAD, tq), BF16), pltpu.VMEM((2, max(tk, n_c), MLA_Q_BLOCK), F32),
                        pltpu.VMEM((1, tq), F32), pltpu.VMEM((1, tq), F32), pltpu.VMEM((MLA_V, tq), F32)],
        compiler_params=_params("parallel", "parallel", "arbitrary"),
        name="mla_attention",
    )(*args)


def _proj_c_kernel(h_ref, sh_ref, sc_ref, gn_ref, wc_ref, wvt_ref, gqk_ref, cos_ref, sin_ref, bd_ref,
                   q_ref, k_ref, vt_ref):
    qw = WIN_HEADS * WIN_HEAD_DIM
    kw = WIN_KV_HEADS * WIN_HEAD_DIM
    bd = bd_ref[...]
    tm = h_ref.shape[0]
    rb = min(PROJ_ROWS, tm)
    lane = lax.broadcasted_iota(jnp.int32, (rb, LANES), 1)
    first_quarter = (lane % (WIN_HEAD_DIM // 2)) < (WIN_HEAD_DIM // 4)
    for r in range(tm // rb):
        rows = slice(r * rb, (r + 1) * rb)
        a = (_rms(h_ref[rows, :], gn_ref[...]) * (1.0 + sc_ref[...]) + sh_ref[...]).astype(BF16)
        p = _dot(a, wc_ref[...])
        vt_ref[:, rows] = _dot_nt(wvt_ref[...], a).astype(BF16)
        cos, sin = cos_ref[rows, :], sin_ref[rows, :]
        for j in range((qw + kw) // MXU_TILE):
            xh = p[:, j * MXU_TILE:(j + 1) * MXU_TILE]
            ssq = _dot((xh * xh).astype(BF16), bd)
            xn = xh * lax.rsqrt(ssq * (1.0 / WIN_HEAD_DIM) + NORM_EPS) * gqk_ref[:, j * MXU_TILE:(j + 1) * MXU_TILE]
            for c in range(MXU_TILE // LANES):
                xc = xn[:, c * LANES:(c + 1) * LANES]
                fwd = pltpu.roll(xc, WIN_HEAD_DIM // 4, axis=1)
                bwd = pltpu.roll(xc, LANES - WIN_HEAD_DIM // 4, axis=1)
                y = (xc * cos + jnp.where(first_quarter, -bwd, fwd) * sin).astype(BF16)
                col = j * MXU_TILE + c * LANES
                if col < qw:
                    q_ref[rows, col:col + LANES] = y
                else:
                    k_ref[rows, col - qw:col - qw + LANES] = y


def _proj_c(h, mods, ctx_row, tm, wts, cos2, sin2):
    b, t, d = h.shape
    qw = WIN_HEADS * WIN_HEAD_DIM
    kw = WIN_KV_HEADS * WIN_HEAD_DIM
    tile = lambda w: pl.BlockSpec((None, tm, w), lambda bi, i: (bi, i, 0))
    tab = pl.BlockSpec((tm, LANES), lambda bi, i: (i, 0))
    in_specs = ([tile(d)] + _mod_specs(d, (0, 1), ctx_row)
                + [_const_spec(wts[k].shape) for k in ("gn", "wc", "wvt", "gqk")]
                + [tab, tab, _const_spec(wts["bd"].shape)])
    return pl.pallas_call(
        _proj_c_kernel,
        grid=(b, t // tm),
        in_specs=in_specs,
        out_specs=[tile(qw), tile(kw), pl.BlockSpec((None, kw, tm), lambda bi, i: (bi, 0, i))],
        out_shape=[jax.ShapeDtypeStruct((b, t, qw), BF16),
                   jax.ShapeDtypeStruct((b, t, kw), BF16),
                   jax.ShapeDtypeStruct((b, kw, t), BF16)],
        compiler_params=_params("parallel", "parallel"),
        name="proj_c",
    )(h, mods, mods, wts["gn"], wts["wc"], wts["wvt"], wts["gqk"], cos2, sin2, wts["bd"])


def _prep_proj_c(norm_mix, c_w_in, q_norm, k_norm):
    gqk = jnp.concatenate([jnp.tile(q_norm * (WIN_SCALE * LOG2_E), WIN_HEADS), jnp.tile(k_norm, WIN_KV_HEADS)])
    idx = np.arange(MXU_TILE) // WIN_HEAD_DIM
    bd = jnp.asarray(idx[:, None] == idx[None, :], BF16)
    qk = (WIN_HEADS + WIN_KV_HEADS) * WIN_HEAD_DIM
    bound = 1.01 * WIN_HEAD_DIM * jnp.max(jnp.abs(q_norm * (WIN_SCALE * LOG2_E))) * jnp.max(jnp.abs(k_norm))
    return dict(gn=norm_mix[None, :], wc=c_w_in[:, :qk].astype(BF16), wvt=c_w_in[:, qk:].T.astype(BF16),
                gqk=gqk[None, :], bd=bd, bound=bound.astype(F32))


def _win_kernel(sc_ref, q_ref, k_ref, vt_ref, kc_ref, vct_ref, o_ref, ot_ref, *, tq, band, n_lat):
    i = pl.program_id(1)
    start = pl.multiple_of(jnp.clip(i * tq - WINDOW, 0, n_lat - band), WINDOW)
    hd, grp = WIN_HEAD_DIM, WIN_GROUP
    bound = sc_ref[WIN_HEADS]
    qt = q_ref[...].astype(F32).T.astype(BF16)
    k_pos = start + lax.broadcasted_iota(jnp.int32, (band, tq), 0)
    q_pos = i * tq + lax.broadcasted_iota(jnp.int32, (band, tq), 1)
    bias1 = jnp.where(jnp.abs(k_pos - q_pos) <= WINDOW, 0.0, NEG_BIG)
    bias = jnp.concatenate([bias1] * grp, axis=1)
    zeros = jnp.zeros((hd, grp * tq), BF16)

    def group_inputs(kv):
        qg = jnp.concatenate([qt[(kv * grp + g) * hd:(kv * grp + g + 1) * hd, :] for g in range(grp)], axis=1)
        qg = jnp.concatenate([qg, zeros] if kv % 2 == 0 else [zeros, qg], axis=0)
        col = (kv // 2) * LANES
        sink = jnp.concatenate([jnp.full((1, tq), sc_ref[kv * grp + g], F32) for g in range(grp)], axis=1)
        return qg, col, sink

    def weighted_values(kv, p_ctx, p_loc):
        return (_dot(vct_ref[kv * hd:(kv + 1) * hd, :], p_ctx.astype(BF16))
                + _dot(vt_ref[kv * hd:(kv + 1) * hd, pl.ds(start, band)], p_loc.astype(BF16)))

    def regroup(ot):
        return jnp.concatenate([ot[:, g * tq:(g + 1) * tq] for g in range(grp)], axis=0)

    def one_pass(kv):
        qg, col, sink = group_inputs(kv)
        p_ctx = jnp.exp2(_dot(kc_ref[:, col:col + LANES], qg) - bound)
        p_loc = jnp.exp2(_dot(k_ref[pl.ds(start, band), col:col + LANES], qg) + (bias - bound))
        den = (jnp.sum(p_loc, axis=0, keepdims=True) + jnp.sum(p_ctx, axis=0, keepdims=True)
               + jnp.exp2(sink - bound))
        ot_ref[kv * grp * hd:(kv + 1) * grp * hd, :] = regroup(weighted_values(kv, p_ctx, p_loc) / den)
        return den

    def two_pass(kv):
        qg, col, sink = group_inputs(kv)
        s_ctx = _dot(kc_ref[:, col:col + LANES], qg)
        s_loc = _dot(k_ref[pl.ds(start, band), col:col + LANES], qg) + bias
        m = jnp.maximum(jnp.maximum(jnp.max(s_loc, axis=0, keepdims=True),
                                    jnp.max(s_ctx, axis=0, keepdims=True)), sink)
        p_loc = jnp.exp2(s_loc - m)
        p_ctx = jnp.exp2(s_ctx - m)
        den = (jnp.sum(p_loc, axis=0, keepdims=True) + jnp.sum(p_ctx, axis=0, keepdims=True)
               + jnp.exp2(sink - m))
        ot_ref[kv * grp * hd:(kv + 1) * grp * hd, :] = regroup(weighted_values(kv, p_ctx, p_loc) / den)

    lo = hi = None
    for kv in range(WIN_KV_HEADS):
        den = one_pass(kv)
        lo = den if lo is None else jnp.minimum(lo, den)
        hi = den if hi is None else jnp.maximum(hi, den)
    unsafe = jnp.logical_or(jnp.min(lo) < SOFTMAX_DEN_MIN, jnp.max(hi) > SOFTMAX_DEN_MAX)

    @pl.when(unsafe)
    def _():
        for kv in range(WIN_KV_HEADS):
            two_pass(kv)

    o_ref[...] = ot_ref[...].T.astype(BF16)


def _win_attention(q, k, vt, k_ctx, vt_ctx, sink, score_bound, *, tq):
    b, n, qw = q.shape
    n_c = k_ctx.shape[1]
    kw = k.shape[2]
    band = tq + 2 * WINDOW
    full = lambda r, w: pl.BlockSpec((None, r, w), lambda bi, i: (bi, 0, 0))
    return pl.pallas_call(
        functools.partial(_win_kernel, tq=tq, band=band, n_lat=n),
        grid=(b, n // tq),
        in_specs=[pl.BlockSpec(memory_space=pltpu.SMEM),
                  pl.BlockSpec((None, tq, qw), lambda bi, i: (bi, i, 0)),
                  full(n, kw), full(kw, n), full(n_c, kw), full(kw, n_c)],
        out_specs=pl.BlockSpec((None, tq, qw), lambda bi, i: (bi, i, 0)),
        out_shape=jax.ShapeDtypeStruct((b, n, qw), BF16),
        scratch_shapes=[pltpu.VMEM((qw, tq), F32)],
        compiler_params=_params("parallel", "arbitrary"),
        name="win_attention",
    )(jnp.concatenate([sink.astype(F32) * LOG2_E, score_bound.reshape(1)]), q, k, vt, k_ctx, vt_ctx)


def _post_kernel(*refs, s5_width, n_chunks):
    if s5_width:
        (h_ref, g_ref, sh_ref, sc_ref, g2_ref, gn_ref, yg_ref, o_ref, wglu_ref, bglu_ref, wo_ref,
         wg_ref, wu_ref, wd_ref, out_ref, a_ref, acc_ref) = refs
        yg = yg_ref[...]
        s5 = yg * jax.nn.sigmoid(_dot(yg.astype(BF16), wglu_ref[...]) + bglu_ref[...])
        mix = _dot(s5.astype(BF16), wo_ref[:s5_width, :]) + _dot(o_ref[...], wo_ref[s5_width:, :])
    else:
        (h_ref, g_ref, sh_ref, sc_ref, g2_ref, gn_ref, o_ref, wo_ref,
         wg_ref, wu_ref, wd_ref, out_ref, a_ref, acc_ref) = refs
        mix = _dot(o_ref[...], wo_ref[...])
    h1 = h_ref[...] + g_ref[...] * mix
    a_ref[...] = (_rms(h1, gn_ref[...]) * (1.0 + sc_ref[...]) + sh_ref[...]).astype(BF16)
    acc_ref[...] = jnp.zeros_like(acc_ref)

    def body(c, _):
        a = a_ref[...]
        cols = pl.ds(pl.multiple_of(c * MXU_TILE, MXU_TILE), MXU_TILE)
        act = _silu(_dot(a, wg_ref[:, cols])) * _dot(a, wu_ref[:, cols])
        acc_ref[...] += _dot(act.astype(BF16), wd_ref[cols, :])
        return 0

    lax.fori_loop(0, n_chunks, body, 0, unroll=True)
    out_ref[...] = h1 + g2_ref[...] * acc_ref[...]


def _post(h, mods, ctx_row, tm, wts, ffn, layer, o, yg=None):
    b, t, d = h.shape
    tile = lambda w: pl.BlockSpec((None, tm, w), lambda bi, i: (bi, i, 0))
    s5_width = 0 if yg is None else yg.shape[2]
    in_specs = [tile(d)] + _mod_specs(d, (2, 3, 4, 5), ctx_row) + [_const_spec(wts["gn"].shape)]
    args = [h, mods, mods, mods, mods, wts["gn"]]
    if yg is not None:
        in_specs += [tile(s5_width), tile(o.shape[2]), _const_spec(wts["wglu"].shape),
                     _const_spec(wts["bglu"].shape)]
        args += [yg, o, wts["wglu"], wts["bglu"]]
    else:
        in_specs += [tile(o.shape[2])]
        args += [o]
    in_specs.append(_const_spec(wts["wo"].shape))
    args.append(wts["wo"])
    for k in ("wg", "wu", "wd"):
        in_specs.append(_const_spec(ffn[k].shape, layer))
        args.append(ffn[k])
    return pl.pallas_call(
        functools.partial(_post_kernel, s5_width=s5_width, n_chunks=ffn["wg"].shape[2] // MXU_TILE),
        grid=(b, t // tm),
        in_specs=in_specs,
        out_specs=tile(d),
        out_shape=jax.ShapeDtypeStruct((b, t, d), F32),
        scratch_shapes=[pltpu.VMEM((tm, d), BF16), pltpu.VMEM((tm, d), F32)],
        compiler_params=_params("parallel", "parallel"),
        name="post_ffn",
    )(*args)


def _prep_post(norm_ffn, w_out, w_glu=None, b_glu=None):
    wts = dict(gn=norm_ffn[None, :], wo=w_out.astype(BF16))
    if w_glu is not None:
        wts.update(wglu=w_glu.astype(BF16), bglu=b_glu[None, :])
    return wts


def kernel(x, c, ctx, c_ctx, ada_w, ada_b, norm_mix, norm_ffn, ffn_w_gate, ffn_w_up, ffn_w_down,
           a_w_in, a_w_out, s5_lam_re, s5_lam_im, s5_log_step, s5_b_re, s5_b_im, s5_c_re, s5_c_im,
           s5_d, s5_w_glu, s5_b_glu, mla_qa_norm, mla_w_q_b, mla_kva_norm, mla_w_kv_b,
           mla_q_norm, mla_k_norm, c_w_in, c_w_out, c_q_norm, c_k_norm, c_sink):
    b, n, d = x.shape
    n_c = ctx.shape[1]
    depth = ada_w.shape[0]
    assert b + 1 <= MOD_ROWS and n % 1024 == 0 and n_c % (S5_CHUNK * SUBLANES) == 0
    rows = n // GRID_W
    tm_lat, tm_ctx, tm_proj = 512, n_c, min(n, 1024)

    cond = jnp.zeros((MOD_ROWS, d), F32).at[:b].set(c).at[b].set(c_ctx)
    mods = _ada_modulation(cond, ada_w, ada_b)
    mods = mods.reshape(depth, MOD_ROWS, N_MOD, 1, d)

    cos_a, sin_a = _grid_rope_tables(rows, MLA_ROPE)
    cs_a_lat = np.concatenate([cos_a, sin_a], axis=1)
    cs_a_ctx = np.concatenate([np.ones((n_c, MLA_ROPE), np.float32), np.zeros((n_c, MLA_ROPE), np.float32)], axis=1)
    cos_c, sin_c = _grid_rope_tables(rows, WIN_HEAD_DIM)
    cos_c2, sin_c2 = np.tile(cos_c, (1, 2)), np.tile(sin_c, (1, 2))
    one_c, zero_c = np.ones((n_c, LANES), np.float32), np.zeros((n_c, LANES), np.float32)
    assert ffn_w_gate.shape[2] % MXU_TILE == 0
    ffn = dict(wg=ffn_w_gate.astype(BF16), wu=ffn_w_up.astype(BF16), wd=ffn_w_down.astype(BF16))

    h_ctx, h_lat = ctx, x
    for i in range(depth):
        need_ctx = i < depth - 1
        j = i // 2
        m_i = mods[i]
        if i % 2 == 0:
            pw = _prep_proj_a(norm_mix[i], a_w_in[j], mla_qa_norm[j], mla_w_q_b[j], mla_kva_norm[j],
                              mla_w_kv_b[j], mla_q_norm[j], mla_k_norm[j])
            u_l, q_l, k_l, vt_l = _proj_a(h_lat, m_i, None, tm_lat, pw, cs_a_lat)
            u_c, q_c, k_c, vt_c = _proj_a(h_ctx, m_i, b, tm_ctx, pw, cs_a_ctx)
            tables = _s5_tables(s5_lam_re[j], s5_lam_im[j], s5_log_step[j], s5_b_re[j], s5_b_im[j],
                                s5_c_re[j], s5_c_im[j], s5_d[j])
            yg_c, yg_l = _s5(u_c, u_l, tables)
            o_l = _mla_attention(q_l, k_c, vt_c, k_l, vt_l, tq=n)
            post_w = _prep_post(norm_ffn[i], a_w_out[j], s5_w_glu[j], s5_b_glu[j])
            h_lat_new = _post(h_lat, m_i, None, tm_lat, post_w, ffn, i, o_l, yg_l)
            if need_ctx:
                o_c = _mla_attention(q_c, k_c, vt_c, tq=n_c)
                h_ctx = _post(h_ctx, m_i, b, tm_ctx, post_w, ffn, i, o_c, yg_c)
            h_lat = h_lat_new
        else:
            pw = _prep_proj_c(norm_mix[i], c_w_in[j], c_q_norm[j], c_k_norm[j])
            q_l, k_l, vt_l = _proj_c(h_lat, m_i, None, tm_proj, pw, cos_c2, sin_c2)
            q_c, k_c, vt_c = _proj_c(h_ctx, m_i, b, tm_ctx, pw, one_c, zero_c)
            o_l = _win_attention(q_l, k_l, vt_l, k_c, vt_c, c_sink[j], pw["bound"], tq=256)
            post_w = _prep_post(norm_ffn[i], c_w_out[j])
            h_lat_new = _post(h_lat, m_i, None, tm_lat, post_w, ffn, i, o_l)
            if need_ctx:
                raise NotImplementedError("context queries of a windowed layer")
            h_lat = h_lat_new
    return h_lat
```

```python
import functools
import math

import jax
import jax.numpy as jnp
import numpy as np
from jax import lax
from jax.experimental import pallas as pl
from jax.experimental.pallas import tpu as pltpu

F32 = jnp.float32
BF16 = jnp.bfloat16

GRID_W = 64
NORM_EPS = 1e-6
ROPE_THETA = 10000.0
N_MOD = 6
S5_GROUP_DIM = 16
S5_STATE = 64
S5_CHUNK = 16
MLA_HEADS = 4
MLA_NOPE = 128
MLA_ROPE = 64
MLA_QK_DIM = MLA_NOPE + MLA_ROPE
MLA_V = 128
MLA_Q_RANK = 384
MLA_KV_RANK = 256
MLA_SCALE = MLA_QK_DIM ** -0.5
MLA_HEAD_PAD = 256
WIN_HEADS = 16
WIN_KV_HEADS = 4
WIN_GROUP = WIN_HEADS // WIN_KV_HEADS
WIN_HEAD_DIM = 64
WINDOW = 128
WIN_SCALE = WIN_HEAD_DIM ** -0.5
LANES = 128
SUBLANES = 8
MXU_TILE = 256
VMEM_BYTES_V7X = 64 * 1024 * 1024
VMEM_LIMIT_BYTES = (VMEM_BYTES_V7X * 3) // 4
NEG_BIG = -1e30
LOG2_E = math.log2(math.e)
TM_FFN = 512
TM_PROJ_A = 512
TM_PROJ_C = 1024
TQ_WIN = 256
TK_MLA = 1024
TN_ADA = 1024
PROJ_ROWS = 256
SOFTMAX_DEN_MIN = 2.0 ** -60
SOFTMAX_DEN_MAX = 2.0 ** 60
S5_BLOCK_PAIRS = 4
MLA_Q_BLOCK = 256
MLA_K_PIECE = 256
MOD_ROWS = 8


def _dot(a, b):
    return jnp.dot(a, b, preferred_element_type=F32)


def _dot_nt(a, b):
    return lax.dot_general(a, b, (((1,), (1,)), ((), ())), preferred_element_type=F32)


def _split_bf16(x):
    hi = x.astype(BF16)
    lo = (x - hi.astype(F32)).astype(BF16)
    return hi, lo


def _rms(x, gain):
    return x * lax.rsqrt(jnp.mean(x * x, axis=-1, keepdims=True) + NORM_EPS) * gain


def _silu(x):
    return x * jax.nn.sigmoid(x)


def _params(*sem):
    return pltpu.CompilerParams(dimension_semantics=sem, vmem_limit_bytes=VMEM_LIMIT_BYTES)


def _const_spec(shape, layer=None):
    if layer is None:
        nd = len(shape)
        return pl.BlockSpec(shape, lambda *_: (0,) * nd, pipeline_mode=pl.Buffered(1))
    nd = len(shape) - 1
    return pl.BlockSpec((None,) + tuple(shape[1:]), lambda *_: (layer,) + (0,) * nd,
                        pipeline_mode=pl.Buffered(1))


def _ada_kernel(cond_ref, w_ref, b_ref, o_ref):
    s = _silu(cond_ref[...])
    s_hi, s_lo = _split_bf16(s)
    w_hi, w_lo = _split_bf16(w_ref[...])
    o_ref[...] = _dot(s_hi, w_hi) + _dot(s_lo, w_hi) + _dot(s_hi, w_lo) + b_ref[...]


def _ada_modulation(cond, ada_w, ada_b):
    depth, d, n = ada_w.shape
    tn = TN_ADA
    return pl.pallas_call(
        _ada_kernel,
        grid=(depth, n // tn),
        in_specs=[pl.BlockSpec((MOD_ROWS, d), lambda i, j: (0, 0)),
                  pl.BlockSpec((None, d, tn), lambda i, j: (i, 0, j)),
                  pl.BlockSpec((None, 1, tn), lambda i, j: (i, 0, j))],
        out_specs=pl.BlockSpec((None, MOD_ROWS, tn), lambda i, j: (i, 0, j)),
        out_shape=jax.ShapeDtypeStruct((depth, MOD_ROWS, n), F32),
        compiler_params=_params("arbitrary", "arbitrary"),
        name="ada_modulation",
    )(cond, ada_w, ada_b.reshape(depth, 1, n))


def _mod_specs(d, slots, ctx_row):
    def make(slot):
        if ctx_row is None:
            return pl.BlockSpec((None, None, 1, d), lambda b, i: (b, slot, 0, 0))
        return pl.BlockSpec((None, None, 1, d), lambda b, i: (ctx_row, slot, 0, 0))
    return [make(s) for s in slots]


def _grid_rope_tables(rows, rot_dim):
    n_freq = rot_dim // 4
    inv_freq = np.power(np.float32(ROPE_THETA), -np.arange(n_freq, dtype=np.float32) / np.float32(n_freq))
    ang_r = np.arange(rows, dtype=np.float32)[:, None] * inv_freq.astype(np.float32)
    ang_c = np.arange(GRID_W, dtype=np.float32)[:, None] * inv_freq.astype(np.float32)

    def expand(r, c):
        r = np.broadcast_to(r[:, None, :], (rows, GRID_W, n_freq))
        c = np.broadcast_to(c[None, :, :], (rows, GRID_W, n_freq))
        return np.concatenate([r, r, c, c], axis=-1).reshape(rows * GRID_W, rot_dim)

    return expand(np.cos(ang_r), np.cos(ang_c)), expand(np.sin(ang_r), np.sin(ang_c))


def _rot_perm_sign(rot_dim):
    q = rot_dim // 4
    idx = np.arange(rot_dim)
    perm = np.where((idx // q) % 2 == 0, idx + q, idx - q)
    sign = np.where((idx // q) % 2 == 0, -1.0, 1.0).astype(np.float32)
    return perm, sign


def _proj_a_kernel(h_ref, sh_ref, sc_ref, gn_ref, w1_ref, gqa_ref, wq_ref, gkv_ref, wkv_ref, wvt_ref,
                   gq_ref, gk_ref, cs_ref, msk_ref, u_ref, q_ref, k_ref, vt_ref):
    tm = h_ref.shape[0]
    rb = min(PROJ_ROWS, tm)
    msk = msk_ref[...]
    low_half = lax.broadcasted_iota(jnp.int32, (rb, LANES), 1) < MLA_ROPE

    def finish(xh, gain, cs, out_ref, rows, h):
        ssq = _dot((xh * xh).astype(BF16), msk)
        xn = xh * lax.rsqrt(ssq * (1.0 / MLA_QK_DIM) + NORM_EPS) * gain
        rr = xn[:, LANES:] * cs
        rot = rr + pltpu.roll(rr, MLA_ROPE, axis=1)
        base = h * MLA_HEAD_PAD
        out_ref[rows, base:base + LANES] = xn[:, :LANES].astype(BF16)
        out_ref[rows, base + LANES:base + 2 * LANES] = jnp.where(low_half, rot, 0.0).astype(BF16)

    for r in range(tm // rb):
        rows = slice(r * rb, (r + 1) * rb)
        a = _rms(h_ref[rows, :], gn_ref[...]) * (1.0 + sc_ref[...]) + sh_ref[...]
        p1 = _dot(a.astype(BF16), w1_ref[...])
        s5w = u_ref.shape[1]
        u_ref[rows, :] = p1[:, :s5w]
        cq = p1[:, s5w:s5w + MLA_Q_RANK]
        ckv = p1[:, s5w + MLA_Q_RANK:s5w + MLA_Q_RANK + MLA_KV_RANK]
        krr = p1[:, s5w + MLA_Q_RANK + MLA_KV_RANK:]
        qb = _dot(_rms(cq, gqa_ref[...]).astype(BF16), wq_ref[...])
        ckv_n = _rms(ckv, gkv_ref[...]).astype(BF16)
        kv = _dot(ckv_n, wkv_ref[...])
        vt_ref[:, rows] = _dot_nt(wvt_ref[...], ckv_n).astype(BF16)
        cs = cs_ref[rows, :]
        for h in range(MLA_HEADS):
            base = h * MLA_HEAD_PAD
            finish(qb[:, base:base + MLA_HEAD_PAD], gq_ref[:, base:base + MLA_HEAD_PAD], cs, q_ref, rows, h)
            kh = jnp.concatenate([kv[:, h * MLA_NOPE:(h + 1) * MLA_NOPE], krr], axis=1)
            finish(kh, gk_ref[:, base:base + MLA_HEAD_PAD], cs, k_ref, rows, h)


def _proj_a(h, mods, ctx_row, tm, wts, cs):
    b, t, d = h.shape
    qw = MLA_HEADS * MLA_HEAD_PAD
    grid = (b, t // tm)
    tile = lambda w: pl.BlockSpec((None, tm, w), lambda bi, i: (bi, i, 0))
    in_specs = ([tile(d)] + _mod_specs(d, (0, 1), ctx_row)
                + [_const_spec(wts[k].shape) for k in
                   ("gn", "w1", "gqa", "wq", "gkv", "wkv", "wvt", "gq", "gk")]
                + [pl.BlockSpec((tm, LANES), lambda bi, i: (i, 0)), _const_spec(wts["msk"].shape)])
    vw = MLA_HEADS * MLA_V
    s5w = wts["w1"].shape[1] - (MLA_Q_RANK + MLA_KV_RANK + 2 * MLA_ROPE)
    return pl.pallas_call(
        _proj_a_kernel,
        grid=grid,
        in_specs=in_specs,
        out_specs=[tile(s5w), tile(qw), tile(qw), pl.BlockSpec((None, vw, tm), lambda bi, i: (bi, 0, i))],
        out_shape=[jax.ShapeDtypeStruct((b, t, s5w), F32),
                   jax.ShapeDtypeStruct((b, t, qw), BF16),
                   jax.ShapeDtypeStruct((b, t, qw), BF16),
                   jax.ShapeDtypeStruct((b, vw, t), BF16)],
        compiler_params=_params("parallel", "parallel"),
        name="proj_a",
    )(h, mods, mods, wts["gn"], wts["w1"], wts["gqa"], wts["wq"], wts["gkv"], wts["wkv"], wts["wvt"],
      wts["gq"], wts["gk"], cs, wts["msk"])


def _prep_proj_a(norm_mix, a_w_in, qa_norm, w_q_b, kva_norm, w_kv_b, q_norm, k_norm):
    perm, sign = _rot_perm_sign(MLA_ROPE)
    s5w = a_w_in.shape[1] - (MLA_Q_RANK + MLA_KV_RANK + MLA_ROPE)
    assert s5w % LANES == 0 and MLA_Q_RANK % LANES == 0 and MLA_KV_RANK % LANES == 0
    kr = a_w_in[:, -MLA_ROPE:]
    w1 = jnp.concatenate([a_w_in, kr[:, perm] * sign], axis=1).astype(BF16)
    wq = w_q_b.reshape(MLA_Q_RANK, MLA_HEADS, MLA_QK_DIM)
    rope = wq[:, :, MLA_NOPE:]
    wq = jnp.concatenate([wq, rope[:, :, perm] * sign], axis=2)
    wq = wq.reshape(MLA_Q_RANK, MLA_HEADS * MLA_HEAD_PAD).astype(BF16)
    wkv3 = w_kv_b.reshape(MLA_KV_RANK, MLA_HEADS, MLA_NOPE + MLA_V)
    wkv = wkv3[:, :, :MLA_NOPE].reshape(MLA_KV_RANK, -1).astype(BF16)
    wvt = wkv3[:, :, MLA_NOPE:].reshape(MLA_KV_RANK, -1).T.astype(BF16)

    def head_gain(g, scale):
        gb = jnp.concatenate([g, g[MLA_NOPE:][perm]]) * scale
        return jnp.tile(gb, MLA_HEADS)[None, :]

    rows = np.arange(MLA_HEAD_PAD)[:, None] < MLA_QK_DIM
    msk = jnp.asarray(np.broadcast_to(rows, (MLA_HEAD_PAD, MLA_HEAD_PAD)), BF16)
    return dict(gn=norm_mix[None, :], w1=w1, gqa=qa_norm[None, :], wq=wq, gkv=kva_norm[None, :],
                wkv=wkv, wvt=wvt, gq=head_gain(q_norm, MLA_SCALE * LOG2_E), gk=head_gain(k_norm, 1.0),
                msk=msk)


def _s5_kernel(uc_ref, ul_ref, sel_in_ref, sel_out_ref, toep_ref, bst_ref, cst_ref, a_ref, d_ref,
               yc_ref, yl_ref, ub_ref, x_ref, z_ref, sin_ref, yb_ref, *, n_ctx, n_all):
    L, S = S5_CHUNK, S5_GROUP_DIM
    n_lat = n_all - n_ctx
    npair = S5_BLOCK_PAIRS
    half = SUBLANES * LANES
    for tl in range(L):
        u = jnp.concatenate([uc_ref[pl.ds(tl, n_ctx, stride=L), :], ul_ref[pl.ds(tl, n_lat, stride=L), :]], axis=0)
        ub_ref[:, tl * LANES:(tl + 1) * LANES] = u.astype(BF16)
    for pp in range(npair):
        for hh in range(2):
            xs = _dot(ub_ref[:, hh * half:(hh + 1) * half], sel_in_ref[pp]).astype(BF16)
            x_ref[pp, :, hh * LANES:(hh + 1) * LANES] = xs[:, :LANES]
            x_ref[pp, :, MXU_TILE + hh * LANES:MXU_TILE + (hh + 1) * LANES] = xs[:, LANES:]
    for pp in range(npair):
        z_ref[pp] = _dot(x_ref[pp], bst_ref[pp])
    t_ctx, t_all = n_ctx // SUBLANES, n_all // SUBLANES
    row = lax.broadcasted_iota(jnp.int32, (SUBLANES, LANES), 0)

    def cmul(a_re, a_im, b_re, b_im):
        return a_re * b_re - a_im * b_im, a_re * b_im + a_im * b_re

    def tile_scan(z_re, z_im, c_re, c_im, a_re, a_im, fwd):
        for sft in (1, 2, 4):
            k = sft - 1 if fwd else SUBLANES - sft
            p_re, p_im = a_re[k:k + 1, :], a_im[k:k + 1, :]
            amt = sft if fwd else SUBLANES - sft
            keep = (row >= sft) if fwd else (row < SUBLANES - sft)
            s_re = jnp.where(keep, pltpu.roll(z_re, amt, axis=0), 0.0)
            s_im = jnp.where(keep, pltpu.roll(z_im, amt, axis=0), 0.0)
            m_re, m_im = cmul(p_re, p_im, s_re, s_im)
            z_re, z_im = z_re + m_re, z_im + m_im
        m_re, m_im = cmul(a_re, a_im, c_re, c_im)
        s_re, s_im = z_re + m_re, z_im + m_im
        edge = (row == 0) if fwd else (row == SUBLANES - 1)
        amt = 1 if fwd else SUBLANES - 1
        in_re = jnp.where(edge, c_re, pltpu.roll(s_re, amt, axis=0))
        in_im = jnp.where(edge, c_im, pltpu.roll(s_im, amt, axis=0))
        last = SUBLANES - 1 if fwd else 0
        return in_re, in_im, s_re[last:last + 1, :], s_im[last:last + 1, :]

    def body(it, carry):
        jt = jnp.where(it < t_ctx, t_ctx - 1 - it, t_all + t_ctx - 1 - it)
        rf = pl.multiple_of(it * SUBLANES, SUBLANES)
        rb = pl.multiple_of(jt * SUBLANES, SUBLANES)
        new = []
        for pp in range(npair):
            cf_re, cf_im, cb_re, cb_im = carry[4 * pp:4 * pp + 4]
            zf = z_ref[pp, pl.ds(rf, SUBLANES), 0:2 * LANES]
            zb = z_ref[pp, pl.ds(rb, SUBLANES), 2 * LANES:4 * LANES]
            f_re, f_im, cf_re, cf_im = tile_scan(zf[:, :LANES], zf[:, LANES:], cf_re, cf_im,
                                                 a_ref[pp, 0], a_ref[pp, 1], True)
            b_re, b_im, cb_re, cb_im = tile_scan(zb[:, :LANES], zb[:, LANES:], cb_re, cb_im,
                                                 a_ref[pp, 2], a_ref[pp, 3], False)
            sin_ref[pp, pl.ds(rf, SUBLANES), 0:2 * LANES] = jnp.concatenate([f_re, f_im], axis=1)
            sin_ref[pp, pl.ds(rb, SUBLANES), 2 * LANES:4 * LANES] = jnp.concatenate([b_re, b_im], axis=1)
            new += [cf_re, cf_im, cb_re, cb_im]
        return tuple(new)

    zero = jnp.zeros((1, LANES), F32)
    lax.fori_loop(0, t_all, body, (zero,) * (4 * npair))
    for pp in range(npair):
        x = x_ref[pp]
        y = jnp.concatenate([_dot(x[:, :MXU_TILE], toep_ref[pp, 0]), _dot(x[:, MXU_TILE:], toep_ref[pp, 1])],
                            axis=1)
        y = y + _dot(sin_ref[pp].astype(BF16), cst_ref[pp]) + x.astype(F32) * d_ref[pp]
        yg = jax.nn.gelu(y).astype(BF16)
        for gl in range(2):
            for hh in range(2):
                g8 = 2 * pp + gl
                col = gl * MXU_TILE + hh * LANES
                yb_ref[hh, :, g8 * LANES:(g8 + 1) * LANES] = yg[:, col:col + LANES]
    for hh in range(2):
        for kk in range(SUBLANES // 2):
            two = _dot(yb_ref[hh], sel_out_ref[kk])
            for e in range(2):
                tl = hh * SUBLANES + 2 * kk + e
                yc_ref[pl.ds(tl, n_ctx, stride=L), :] = two[:n_ctx, e * LANES:(e + 1) * LANES]
                yl_ref[pl.ds(tl, n_lat, stride=L), :] = two[n_ctx:, e * LANES:(e + 1) * LANES]


def _s5_selectors():
    S = S5_GROUP_DIM
    r = np.arange(SUBLANES * LANES)[:, None]
    c = np.arange(2 * LANES)[None, :]
    k, l = r // LANES, r % LANES
    pp = np.arange(S5_BLOCK_PAIRS)[:, None, None]
    sel_in = (k == (c % LANES) // S) & (l == 2 * S * pp + S * (c // LANES) + c % S)
    kk = np.arange(SUBLANES // 2)[:, None, None]
    sel_out = (k == (c % LANES) // S) & (l == S * (2 * kk + c // LANES) + c % S)
    return jnp.asarray(sel_in, BF16), jnp.asarray(sel_out, BF16)


def _dot_nt_f32(a, b):
    a_hi, a_lo = _split_bf16(a)
    b_hi, b_lo = _split_bf16(b)
    return _dot_nt(a_hi, b_hi) + _dot_nt(a_lo, b_hi) + _dot_nt(a_hi, b_lo)


def _s5_table_kernel(par_ref, bre_ref, bim_ref, cre_ref, cim_ref, toep_ref, bst_ref, cst_ref, a_ref):
    L, S, P = S5_CHUNK, S5_GROUP_DIM, S5_STATE
    kk = lax.broadcasted_iota(jnp.int32, (3 * SUBLANES, LANES), 0).astype(F32)
    lane = lax.broadcasted_iota(jnp.int32, (L, LANES), 1)
    pair_rows = lax.broadcasted_iota(jnp.int32, (2 * L * S, LANES), 0)
    pair_lanes = lax.broadcasted_iota(jnp.int32, (2 * L * S, LANES), 1)
    own_group = (pair_rows // (L * S)) == (pair_lanes // P)
    lane_pad = jnp.zeros((2 * P, LANES - S), F32)

    def rows_of_powers(pw, ks, groups):
        one = jnp.concatenate([jnp.broadcast_to(pw[k:k + 1, :], (S, LANES)) for k in ks], axis=0)
        return jnp.concatenate([one] * groups, axis=0) if groups > 1 else one

    def cmul(a_re, a_im, b_re, b_im):
        return a_re * b_re - a_im * b_im, a_re * b_im + a_im * b_re

    lag_tables = []
    for d in range(2):
        lam_re, lam_im = par_ref[d, 0:1, :], par_ref[d, 1:2, :]
        step = jnp.exp(par_ref[d, 2:3, :])
        ar, ai = lam_re * step, lam_im * step
        mag = jnp.exp(kk * ar)
        pw_re, pw_im = mag * jnp.cos(kk * ai), mag * jnp.sin(kk * ai)
        th = jnp.tanh(0.5 * ar)
        em1 = 2.0 * th / (1.0 - th)
        sh = jnp.sin(0.5 * ai)
        n_re = em1 * jnp.cos(ai) - 2.0 * sh * sh
        n_im = (em1 + 1.0) * jnp.sin(ai)
        den = lam_re * lam_re + lam_im * lam_im
        co_re = (n_re * lam_re + n_im * lam_im) / den
        co_im = (n_im * lam_re - n_re * lam_im) / den
        bt_re = jnp.concatenate([bre_ref[d], lane_pad], axis=1).T[:S]
        bt_im = jnp.concatenate([bim_ref[d], lane_pad], axis=1).T[:S]
        bb_re, bb_im = cmul(co_re, co_im, bt_re, bt_im)
        cc_re = jnp.concatenate([cre_ref[d, 0], cre_ref[d, 1]], axis=1)
        cc_im = jnp.concatenate([cim_ref[d, 0], cim_ref[d, 1]], axis=1)
        ks = [L - 1 - t for t in range(L)] if d == 0 else list(range(L))
        r_re, r_im = rows_of_powers(pw_re, ks, 2), rows_of_powers(pw_im, ks, 2)
        bbt_re, bbt_im = jnp.concatenate([bb_re] * (2 * L), axis=0), jnp.concatenate([bb_im] * (2 * L), axis=0)
        v_re, v_im = cmul(r_re, r_im, bbt_re, bbt_im)
        bst_ref[:, (2 * d) * LANES:(2 * d + 1) * LANES] = jnp.where(own_group, v_re, 0.0).astype(BF16)
        bst_ref[:, (2 * d + 1) * LANES:(2 * d + 2) * LANES] = jnp.where(own_group, v_im, 0.0).astype(BF16)
        ks = [t + 1 for t in range(L)] if d == 0 else [L - t for t in range(L)]
        r_re, r_im = rows_of_powers(pw_re, ks, 2), rows_of_powers(pw_im, ks, 2)
        cct_re, cct_im = jnp.concatenate([cc_re] * (2 * L), axis=0), jnp.concatenate([cc_im] * (2 * L), axis=0)
        v_re, v_im = cmul(cct_re, cct_im, r_re, r_im)
        cst_ref[(2 * d) * LANES:(2 * d + 1) * LANES, :] = jnp.where(own_group, v_re, 0.0).T.astype(BF16)
        cst_ref[(2 * d + 1) * LANES:(2 * d + 2) * LANES, :] = jnp.where(own_group, -v_im, 0.0).T.astype(BF16)
        ks = list(range(L)) if d == 0 else [L - 1 - j for j in range(L)]
        r_re, r_im = rows_of_powers(pw_re, ks, 1), rows_of_powers(pw_im, ks, 1)
        cl_re, cl_im = cmul(jnp.concatenate([cc_re] * L, axis=0), jnp.concatenate([cc_im] * L, axis=0), r_re, r_im)
        per_group = []
        for g in range(2):
            mine = (lane // P) == g
            per_group.append(_dot_nt_f32(jnp.where(mine, bb_re, 0.0), cl_re)
                             - _dot_nt_f32(jnp.where(mine, bb_im, 0.0), cl_im))
        lag_tables.append(per_group)
        row8 = lax.broadcasted_iota(jnp.int32, (SUBLANES, LANES), 0)
        n_chunks = ((row8 + 1) if d == 0 else (SUBLANES - row8)).astype(F32) * float(L)
        mag8 = jnp.exp(n_chunks * ar)
        a_ref[2 * d] = mag8 * jnp.cos(n_chunks * ai)
        a_ref[2 * d + 1] = mag8 * jnp.sin(n_chunks * ai)

    def shift_right(x, s):
        x0, x1 = x[:, :LANES], x[:, LANES:]
        a, r = divmod(s, LANES)
        r0 = pltpu.roll(x0, r, axis=1) if r else x0
        r1 = pltpu.roll(x1, r, axis=1) if r else x1
        if a == 0:
            return jnp.concatenate([jnp.where(lane >= r, r0, 0.0), jnp.where(lane >= r, r1, r0)], axis=1)
        return jnp.concatenate([jnp.zeros_like(x0), jnp.where(lane >= r, r0, 0.0)], axis=1)

    def shift_left(x, s):
        x0, x1 = x[:, :LANES], x[:, LANES:]
        a, r = divmod(s, LANES)
        r0 = pltpu.roll(x0, LANES - r, axis=1) if r else x0
        r1 = pltpu.roll(x1, LANES - r, axis=1) if r else x1
        if a == 0:
            return jnp.concatenate([jnp.where(lane < LANES - r, r0, r1), jnp.where(lane < LANES - r, r1, 0.0)], axis=1)
        return jnp.concatenate([jnp.where(lane < LANES - r, r1, 0.0), jnp.zeros_like(x0)], axis=1)

    for g in range(2):
        kf, kb = lag_tables[0][g], lag_tables[1][g]
        for tau in range(L):
            blk = shift_right(kf, S * tau) + shift_left(kb, S * (L - 1 - tau))
            toep_ref[g, tau * S:(tau + 1) * S, :] = blk.astype(BF16)


def _s5_tables(lam_re, lam_im, log_step, b_re, b_im, c_re, c_im, d_skip):
    _, G, P = lam_re.shape
    S, L = S5_GROUP_DIM, S5_CHUNK
    assert P == S5_STATE and 2 * P == LANES and 2 * L * S == 2 * MXU_TILE
    pairs = G // 2
    par = jnp.stack([lam_re.reshape(2, pairs, 2 * P), lam_im.reshape(2, pairs, 2 * P),
                     jnp.repeat(log_step, P, axis=-1).reshape(2, pairs, 2 * P)], axis=2)
    par = jnp.transpose(par, (1, 0, 2, 3)).astype(F32)
    bshape = (2, pairs, 2 * P, S)
    cshape = (2, pairs, 2, S, P)
    pw = 2 * L * S
    bspec = pl.BlockSpec((2, None, 2 * P, S), lambda g: (0, g, 0, 0))
    cspec = pl.BlockSpec((2, None, 2, S, P), lambda g: (0, g, 0, 0, 0))
    toep, bst, cst, a_chunk = pl.pallas_call(
        _s5_table_kernel,
        grid=(pairs,),
        in_specs=[pl.BlockSpec((None, 2, 3, 2 * P), lambda g: (g, 0, 0, 0)), bspec, bspec, cspec, cspec],
        out_specs=[pl.BlockSpec((None, 2, MXU_TILE, MXU_TILE), lambda g: (g, 0, 0, 0)),
                   pl.BlockSpec((None, pw, pw), lambda g: (g, 0, 0)),
                   pl.BlockSpec((None, pw, pw), lambda g: (g, 0, 0)),
                   pl.BlockSpec((None, 4, SUBLANES, LANES), lambda g: (g, 0, 0, 0))],
        out_shape=[jax.ShapeDtypeStruct((pairs, 2, MXU_TILE, MXU_TILE), BF16),
                   jax.ShapeDtypeStruct((pairs, pw, pw), BF16),
                   jax.ShapeDtypeStruct((pairs, pw, pw), BF16),
                   jax.ShapeDtypeStruct((pairs, 4, SUBLANES, LANES), F32)],
        compiler_params=_params("parallel"),
        name="s5_tables",
    )(par, b_re.reshape(bshape), b_im.reshape(bshape), c_re.reshape(cshape), c_im.reshape(cshape))
    d_pair = jnp.broadcast_to(d_skip.astype(F32).reshape(pairs, 2, 1, S), (pairs, 2, L, S))
    return toep, bst, cst, a_chunk, d_pair.reshape(pairs, 1, pw)


def _s5(u_ctx, u_lat, tables):
    toep, bst, cst, a_pow, d_pair = tables
    b, n_c, w = u_ctx.shape
    n_l = u_lat.shape[1]
    L = S5_CHUNK
    n_ctx, n_all = n_c // L, (n_c + n_l) // L
    nblk = w // LANES
    pw = 2 * L * S5_GROUP_DIM
    npair = S5_BLOCK_PAIRS
    sel_in, sel_out = _s5_selectors()
    wspec = lambda shape: pl.BlockSpec((npair,) + shape, lambda g, bi: (g,) + (0,) * len(shape),
                                       pipeline_mode=pl.Buffered(1))
    return pl.pallas_call(
        functools.partial(_s5_kernel, n_ctx=n_ctx, n_all=n_all),
        grid=(nblk, b),
        in_specs=[pl.BlockSpec((None, n_c, LANES), lambda g, bi: (bi, 0, g)),
                  pl.BlockSpec((None, n_l, LANES), lambda g, bi: (bi, 0, g)),
                  _const_spec(sel_in.shape), _const_spec(sel_out.shape),
                  wspec((2, MXU_TILE, MXU_TILE)), wspec((pw, pw)), wspec((pw, pw)),
                  wspec((4, SUBLANES, LANES)), wspec((1, pw))],
        out_specs=[pl.BlockSpec((None, n_c, LANES), lambda g, bi: (bi, 0, g)),
                   pl.BlockSpec((None, n_l, LANES), lambda g, bi: (bi, 0, g))],
        out_shape=[jax.ShapeDtypeStruct((b, n_c, w), F32), jax.ShapeDtypeStruct((b, n_l, w), F32)],
        scratch_shapes=[pltpu.VMEM((n_all, L * LANES), BF16), pltpu.VMEM((npair, n_all, pw), BF16),
                        pltpu.VMEM((npair, n_all, pw), F32), pltpu.VMEM((npair, n_all, pw), F32),
                        pltpu.VMEM((2, n_all, SUBLANES * LANES), BF16)],
        compiler_params=_params("parallel", "arbitrary"),
        name="s5_scan",
    )(u_ctx, u_lat, sel_in, sel_out, toep, bst, cst, a_pow, d_pair)


def _mla_kernel(*refs, tk, n_steps):
    if n_steps:
        q_ref, kc_ref, vc_ref, k_ref, v_ref, o_ref, qt_ref, s_ref, m_ref, l_ref, acc_ref = refs
    else:
        q_ref, kc_ref, vc_ref, o_ref, qt_ref, s_ref, m_ref, l_ref, acc_ref = refs
    tq = q_ref.shape[0]
    ncb = tq // MLA_Q_BLOCK
    kp = MLA_K_PIECE
    qt_ref[...] = q_ref[...].astype(F32).T.astype(BF16)

    def chunk(load_k, load_vt, nkeys, first):
        nr = nkeys // kp

        def score_piece(c, r):
            st = _dot(load_k(r), qt_ref[:, c * MLA_Q_BLOCK:(c + 1) * MLA_Q_BLOCK])
            s_ref[c % 2, r * kp:(r + 1) * kp, :] = st
            return jnp.max(st, axis=0, keepdims=True)

        def block_stats(c, mx):
            if first:
                return mx, None
            m_old = m_ref[:, c * MLA_Q_BLOCK:(c + 1) * MLA_Q_BLOCK]
            m_new = jnp.maximum(m_old, mx)
            return m_new, jnp.exp2(m_old - m_new)

        def prob_piece(c, r, m_new):
            p = jnp.exp2(s_ref[c % 2, r * kp:(r + 1) * kp, :] - m_new)
            return jnp.sum(p, axis=0, keepdims=True), _dot(load_vt(r), p.astype(BF16))

        def finish(c, m_new, alpha, lsum, pv):
            cols = slice(c * MLA_Q_BLOCK, (c + 1) * MLA_Q_BLOCK)
            if first:
                l_ref[:, cols] = lsum
                acc_ref[:, cols] = pv
            else:
                l_ref[:, cols] = alpha * l_ref[:, cols] + lsum
                acc_ref[:, cols] = alpha * acc_ref[:, cols] + pv
            m_ref[:, cols] = m_new

        mx = None
        for r in range(nr):
            pm = score_piece(0, r)
            mx = pm if mx is None else jnp.maximum(mx, pm)
        for c in range(ncb):
            m_new, alpha = block_stats(c, mx)
            mx = lsum = pv = None
            for r in range(nr):
                if c + 1 < ncb:
                    pm = score_piece(c + 1, r)
                    mx = pm if mx is None else jnp.maximum(mx, pm)
                ls, pvr = prob_piece(c, r, m_new)
                lsum = ls if lsum is None else lsum + ls
                pv = pvr if pv is None else pv + pvr
            finish(c, m_new, alpha, lsum, pv)

    chunk(lambda r: kc_ref[r * kp:(r + 1) * kp, :], lambda r: vc_ref[:, r * kp:(r + 1) * kp],
          kc_ref.shape[0], True)
    if n_steps:
        def body(j, _):
            off = pl.multiple_of(j * tk, tk)
            chunk(lambda r: k_ref[pl.ds(off + r * kp, kp), :], lambda r: v_ref[:, pl.ds(off + r * kp, kp)],
                  tk, False)
            return 0
        lax.fori_loop(0, n_steps, body, 0)
    o_ref[...] = (acc_ref[...] / l_ref[...]).T.astype(BF16)


def _mla_attention(q, k_ctx, vt_ctx, k_lat=None, vt_lat=None, *, tq, tk=TK_MLA):
    b, t, _ = q.shape
    n_c = k_ctx.shape[1]
    in_specs = [pl.BlockSpec((None, tq, MLA_HEAD_PAD), lambda bi, h, i: (bi, i, h)),
                pl.BlockSpec((None, n_c, MLA_HEAD_PAD), lambda bi, h, i: (bi, 0, h)),
                pl.BlockSpec((None, MLA_V, n_c), lambda bi, h, i: (bi, h, 0))]
    args = [q, k_ctx, vt_ctx]
    n_steps = 0
    if k_lat is not None:
        n_l = k_lat.shape[1]
        n_steps = n_l // tk
        in_specs += [pl.BlockSpec((None, n_l, MLA_HEAD_PAD), lambda bi, h, i: (bi, 0, h)),
                     pl.BlockSpec((None, MLA_V, n_l), lambda bi, h, i: (bi, h, 0))]
        args += [k_lat, vt_lat]
    return pl.pallas_call(
        functools.partial(_mla_kernel, tk=tk, n_steps=n_steps),
        grid=(b, MLA_HEADS, t // tq),
        in_specs=in_specs,
        out_specs=pl.BlockSpec((None, tq, MLA_V), lambda bi, h, i: (bi, i, h)),
        out_shape=jax.ShapeDtypeStruct((b, t, MLA_HEADS * MLA_V), BF16),
        scratch_shapes=[pltpu.VMEM((MLA_HEAD_PAD, tq), BF16), pltpu.VMEM((2, max(tk, n_c), MLA_Q_BLOCK), F32),
                        pltpu.VMEM((1, tq), F32), pltpu.VMEM((1, tq), F32), pltpu.VMEM((MLA_V, tq), F32)],
        compiler_params=_params("parallel", "parallel", "arbitrary"),
        name="mla_attention",
    )(*args)


def _proj_c_kernel(h_ref, sh_ref, sc_ref, gn_ref, wc_ref, wvt_ref, gqk_ref, cos_ref, sin_ref, bd_ref,
                   q_ref, k_ref, vt_ref):
    qw = WIN_HEADS * WIN_HEAD_DIM
    kw = WIN_KV_HEADS * WIN_HEAD_DIM
    bd = bd_ref[...]
    tm = h_ref.shape[0]
    rb = min(PROJ_ROWS, tm)
    lane = lax.broadcasted_iota(jnp.int32, (rb, LANES), 1)
    first_quarter = (lane % (WIN_HEAD_DIM // 2)) < (WIN_HEAD_DIM // 4)
    for r in range(tm // rb):
        rows = slice(r * rb, (r + 1) * rb)
        a = (_rms(h_ref[rows, :], gn_ref[...]) * (1.0 + sc_ref[...]) + sh_ref[...]).astype(BF16)
        p = _dot(a, wc_ref[...])
        vt_ref[:, rows] = _dot_nt(wvt_ref[...], a).astype(BF16)
        cos, sin = cos_ref[rows, :], sin_ref[rows, :]
        for j in range((qw + kw) // MXU_TILE):
            xh = p[:, j * MXU_TILE:(j + 1) * MXU_TILE]
            ssq = _dot((xh * xh).astype(BF16), bd)
            xn = xh * lax.rsqrt(ssq * (1.0 / WIN_HEAD_DIM) + NORM_EPS) * gqk_ref[:, j * MXU_TILE:(j + 1) * MXU_TILE]
            for c in range(MXU_TILE // LANES):
                xc = xn[:, c * LANES:(c + 1) * LANES]
                fwd = pltpu.roll(xc, WIN_HEAD_DIM // 4, axis=1)
                bwd = pltpu.roll(xc, LANES - WIN_HEAD_DIM // 4, axis=1)
                y = (xc * cos + jnp.where(first_quarter, -bwd, fwd) * sin).astype(BF16)
                col = j * MXU_TILE + c * LANES
                if col < qw:
                    q_ref[rows, col:col + LANES] = y
                else:
                    k_ref[rows, col - qw:col - qw + LANES] = y


def _proj_c(h, mods, ctx_row, tm, wts, cos2, sin2):
    b, t, d = h.shape
    qw = WIN_HEADS * WIN_HEAD_DIM
    kw = WIN_KV_HEADS * WIN_HEAD_DIM
    tile = lambda w: pl.BlockSpec((None, tm, w), lambda bi, i: (bi, i, 0))
    tab = pl.BlockSpec((tm, LANES), lambda bi, i: (i, 0))
    in_specs = ([tile(d)] + _mod_specs(d, (0, 1), ctx_row)
                + [_const_spec(wts[k].shape) for k in ("gn", "wc", "wvt", "gqk")]
                + [tab, tab, _const_spec(wts["bd"].shape)])
    return pl.pallas_call(
        _proj_c_kernel,
        grid=(b, t // tm),
        in_specs=in_specs,
        out_specs=[tile(qw), tile(kw), pl.BlockSpec((None, kw, tm), lambda bi, i: (bi, 0, i))],
        out_shape=[jax.ShapeDtypeStruct((b, t, qw), BF16),
                   jax.ShapeDtypeStruct((b, t, kw), BF16),
                   jax.ShapeDtypeStruct((b, kw, t), BF16)],
        compiler_params=_params("parallel", "parallel"),
        name="proj_c",
    )(h, mods, mods, wts["gn"], wts["wc"], wts["wvt"], wts["gqk"], cos2, sin2, wts["bd"])


def _prep_proj_c(norm_mix, c_w_in, q_norm, k_norm):
    gqk = jnp.concatenate([jnp.tile(q_norm * (WIN_SCALE * LOG2_E), WIN_HEADS), jnp.tile(k_norm, WIN_KV_HEADS)])
    idx = np.arange(MXU_TILE) // WIN_HEAD_DIM
    bd = jnp.asarray(idx[:, None] == idx[None, :], BF16)
    qk = (WIN_HEADS + WIN_KV_HEADS) * WIN_HEAD_DIM
    bound = 1.01 * WIN_HEAD_DIM * jnp.max(jnp.abs(q_norm * (WIN_SCALE * LOG2_E))) * jnp.max(jnp.abs(k_norm))
    return dict(gn=norm_mix[None, :], wc=c_w_in[:, :qk].astype(BF16), wvt=c_w_in[:, qk:].T.astype(BF16),
                gqk=gqk[None, :], bd=bd, bound=bound.astype(F32))


def _win_kernel(sc_ref, q_ref, k_ref, vt_ref, kc_ref, vct_ref, o_ref, ot_ref, *, tq, band, n_lat):
    i = pl.program_id(1)
    start = pl.multiple_of(jnp.clip(i * tq - WINDOW, 0, n_lat - band), WINDOW)
    hd, grp = WIN_HEAD_DIM, WIN_GROUP
    bound = sc_ref[WIN_HEADS]
    qt = q_ref[...].astype(F32).T.astype(BF16)
    k_pos = start + lax.broadcasted_iota(jnp.int32, (band, tq), 0)
    q_pos = i * tq + lax.broadcasted_iota(jnp.int32, (band, tq), 1)
    bias1 = jnp.where(jnp.abs(k_pos - q_pos) <= WINDOW, 0.0, NEG_BIG)
    bias = jnp.concatenate([bias1] * grp, axis=1)
    zeros = jnp.zeros((hd, grp * tq), BF16)

    def group_inputs(kv):
        qg = jnp.concatenate([qt[(kv * grp + g) * hd:(kv * grp + g + 1) * hd, :] for g in range(grp)], axis=1)
        qg = jnp.concatenate([qg, zeros] if kv % 2 == 0 else [zeros, qg], axis=0)
        col = (kv // 2) * LANES
        sink = jnp.concatenate([jnp.full((1, tq), sc_ref[kv * grp + g], F32) for g in range(grp)], axis=1)
        return qg, col, sink

    def weighted_values(kv, p_ctx, p_loc):
        return (_dot(vct_ref[kv * hd:(kv + 1) * hd, :], p_ctx.astype(BF16))
                + _dot(vt_ref[kv * hd:(kv + 1) * hd, pl.ds(start, band)], p_loc.astype(BF16)))

    def regroup(ot):
        return jnp.concatenate([ot[:, g * tq:(g + 1) * tq] for g in range(grp)], axis=0)

    def one_pass(kv):
        qg, col, sink = group_inputs(kv)
        p_ctx = jnp.exp2(_dot(kc_ref[:, col:col + LANES], qg) - bound)
        p_loc = jnp.exp2(_dot(k_ref[pl.ds(start, band), col:col + LANES], qg) + (bias - bound))
        den = (jnp.sum(p_loc, axis=0, keepdims=True) + jnp.sum(p_ctx, axis=0, keepdims=True)
               + jnp.exp2(sink - bound))
        ot_ref[kv * grp * hd:(kv + 1) * grp * hd, :] = regroup(weighted_values(kv, p_ctx, p_loc) / den)
        return den

    def two_pass(kv):
        qg, col, sink = group_inputs(kv)
        s_ctx = _dot(kc_ref[:, col:col + LANES], qg)
        s_loc = _dot(k_ref[pl.ds(start, band), col:col + LANES], qg) + bias
        m = jnp.maximum(jnp.maximum(jnp.max(s_loc, axis=0, keepdims=True),
                                    jnp.max(s_ctx, axis=0, keepdims=True)), sink)
        p_loc = jnp.exp2(s_loc - m)
        p_ctx = jnp.exp2(s_ctx - m)
        den = (jnp.sum(p_loc, axis=0, keepdims=True) + jnp.sum(p_ctx, axis=0, keepdims=True)
               + jnp.exp2(sink - m))
        ot_ref[kv * grp * hd:(kv + 1) * grp * hd, :] = regroup(weighted_values(kv, p_ctx, p_loc) / den)

    lo = hi = None
    for kv in range(WIN_KV_HEADS):
        den = one_pass(kv)
        lo = den if lo is None else jnp.minimum(lo, den)
        hi = den if hi is None else jnp.maximum(hi, den)
    unsafe = jnp.logical_or(jnp.min(lo) < SOFTMAX_DEN_MIN, jnp.max(hi) > SOFTMAX_DEN_MAX)

    @pl.when(unsafe)
    def _():
        for kv in range(WIN_KV_HEADS):
            two_pass(kv)

    o_ref[...] = ot_ref[...].T.astype(BF16)


def _win_attention(q, k, vt, k_ctx, vt_ctx, sink, score_bound, *, tq):
    b, n, qw = q.shape
    n_c = k_ctx.shape[1]
    kw = k.shape[2]
    band = tq + 2 * WINDOW
    full = lambda r, w: pl.BlockSpec((None, r, w), lambda bi, i: (bi, 0, 0))
    return pl.pallas_call(
        functools.partial(_win_kernel, tq=tq, band=band, n_lat=n),
        grid=(b, n // tq),
        in_specs=[pl.BlockSpec(memory_space=pltpu.SMEM),
                  pl.BlockSpec((None, tq, qw), lambda bi, i: (bi, i, 0)),
                  full(n, kw), full(kw, n), full(n_c, kw), full(kw, n_c)],
        out_specs=pl.BlockSpec((None, tq, qw), lambda bi, i: (bi, i, 0)),
        out_shape=jax.ShapeDtypeStruct((b, n, qw), BF16),
        scratch_shapes=[pltpu.VMEM((qw, tq), F32)],
        compiler_params=_params("parallel", "arbitrary"),
        name="win_attention",
    )(jnp.concatenate([sink.astype(F32) * LOG2_E, score_bound.reshape(1)]), q, k, vt, k_ctx, vt_ctx)


def _post_kernel(*refs, s5_width, n_chunks):
    if s5_width:
        (h_ref, g_ref, sh_ref, sc_ref, g2_ref, gn_ref, yg_ref, o_ref, wglu_ref, bglu_ref, wo_ref,
         wg_ref, wu_ref, wd_ref, out_ref, a_ref, acc_ref) = refs
        yg = yg_ref[...]
        s5 = yg * jax.nn.sigmoid(_dot(yg.astype(BF16), wglu_ref[...]) + bglu_ref[...])
        mix = _dot(s5.astype(BF16), wo_ref[:s5_width, :]) + _dot(o_ref[...], wo_ref[s5_width:, :])
    else:
        (h_ref, g_ref, sh_ref, sc_ref, g2_ref, gn_ref, o_ref, wo_ref,
         wg_ref, wu_ref, wd_ref, out_ref, a_ref, acc_ref) = refs
        mix = _dot(o_ref[...], wo_ref[...])
    h1 = h_ref[...] + g_ref[...] * mix
    a_ref[...] = (_rms(h1, gn_ref[...]) * (1.0 + sc_ref[...]) + sh_ref[...]).astype(BF16)
    acc_ref[...] = jnp.zeros_like(acc_ref)

    def body(c, _):
        a = a_ref[...]
        cols = pl.ds(pl.multiple_of(c * MXU_TILE, MXU_TILE), MXU_TILE)
        act = _silu(_dot(a, wg_ref[:, cols])) * _dot(a, wu_ref[:, cols])
        acc_ref[...] += _dot(act.astype(BF16), wd_ref[cols, :])
        return 0

    lax.fori_loop(0, n_chunks, body, 0, unroll=True)
    out_ref[...] = h1 + g2_ref[...] * acc_ref[...]


def _post(h, mods, ctx_row, tm, wts, ffn, layer, o, yg=None):
    b, t, d = h.shape
    tile = lambda w: pl.BlockSpec((None, tm, w), lambda bi, i: (bi, i, 0))
    s5_width = 0 if yg is None else yg.shape[2]
    in_specs = [tile(d)] + _mod_specs(d, (2, 3, 4, 5), ctx_row) + [_const_spec(wts["gn"].shape)]
    args = [h, mods, mods, mods, mods, wts["gn"]]
    if yg is not None:
        in_specs += [tile(s5_width), tile(o.shape[2]), _const_spec(wts["wglu"].shape),
                     _const_spec(wts["bglu"].shape)]
        args += [yg, o, wts["wglu"], wts["bglu"]]
    else:
        in_specs += [tile(o.shape[2])]
        args += [o]
    in_specs.append(_const_spec(wts["wo"].shape))
    args.append(wts["wo"])
    for k in ("wg", "wu", "wd"):
        in_specs.append(_const_spec(ffn[k].shape, layer))
        args.append(ffn[k])
    return pl.pallas_call(
        functools.partial(_post_kernel, s5_width=s5_width, n_chunks=ffn["wg"].shape[2] // MXU_TILE),
        grid=(b, t // tm),
        in_specs=in_specs,
        out_specs=tile(d),
        out_shape=jax.ShapeDtypeStruct((b, t, d), F32),
        scratch_shapes=[pltpu.VMEM((tm, d), BF16), pltpu.VMEM((tm, d), F32)],
        compiler_params=_params("parallel", "parallel"),
        name="post_ffn",
    )(*args)


def _prep_post(norm_ffn, w_out, w_glu=None, b_glu=None):
    wts = dict(gn=norm_ffn[None, :], wo=w_out.astype(BF16))
    if w_glu is not None:
        wts.update(wglu=w_glu.astype(BF16), bglu=b_glu[None, :])
    return wts


def kernel(x, c, ctx, c_ctx, ada_w, ada_b, norm_mix, norm_ffn, ffn_w_gate, ffn_w_up, ffn_w_down,
           a_w_in, a_w_out, s5_lam_re, s5_lam_im, s5_log_step, s5_b_re, s5_b_im, s5_c_re, s5_c_im,
           s5_d, s5_w_glu, s5_b_glu, mla_qa_norm, mla_w_q_b, mla_kva_norm, mla_w_kv_b,
           mla_q_norm, mla_k_norm, c_w_in, c_w_out, c_q_norm, c_k_norm, c_sink):
    b, n, d = x.shape
    n_c = ctx.shape[1]
    depth = ada_w.shape[0]
    assert b + 1 <= MOD_ROWS and n % max(TM_PROJ_C, TK_MLA) == 0 and n_c % (S5_CHUNK * SUBLANES) == 0
    rows = n // GRID_W
    tm_ctx = n_c

    cond = jnp.zeros((MOD_ROWS, d), F32).at[:b].set(c).at[b].set(c_ctx)
    mods = _ada_modulation(cond, ada_w, ada_b)
    mods = mods.reshape(depth, MOD_ROWS, N_MOD, 1, d)

    cos_a, sin_a = _grid_rope_tables(rows, MLA_ROPE)
    cs_a_lat = np.concatenate([cos_a, sin_a], axis=1)
    cs_a_ctx = np.concatenate([np.ones((n_c, MLA_ROPE), np.float32), np.zeros((n_c, MLA_ROPE), np.float32)], axis=1)
    cos_c, sin_c = _grid_rope_tables(rows, WIN_HEAD_DIM)
    cos_c2, sin_c2 = np.tile(cos_c, (1, 2)), np.tile(sin_c, (1, 2))
    one_c, zero_c = np.ones((n_c, LANES), np.float32), np.zeros((n_c, LANES), np.float32)
    assert ffn_w_gate.shape[2] % MXU_TILE == 0
    ffn = dict(wg=ffn_w_gate.astype(BF16), wu=ffn_w_up.astype(BF16), wd=ffn_w_down.astype(BF16))

    h_ctx, h_lat = ctx, x
    for i in range(depth):
        need_ctx = i < depth - 1
        j = i // 2
        m_i = mods[i]
        if i % 2 == 0:
            pw = _prep_proj_a(norm_mix[i], a_w_in[j], mla_qa_norm[j], mla_w_q_b[j], mla_kva_norm[j],
                              mla_w_kv_b[j], mla_q_norm[j], mla_k_norm[j])
            u_l, q_l, k_l, vt_l = _proj_a(h_lat, m_i, None, TM_PROJ_A, pw, cs_a_lat)
            u_c, q_c, k_c, vt_c = _proj_a(h_ctx, m_i, b, tm_ctx, pw, cs_a_ctx)
            tables = _s5_tables(s5_lam_re[j], s5_lam_im[j], s5_log_step[j], s5_b_re[j], s5_b_im[j],
                                s5_c_re[j], s5_c_im[j], s5_d[j])
            yg_c, yg_l = _s5(u_c, u_l, tables)
            o_l = _mla_attention(q_l, k_c, vt_c, k_l, vt_l, tq=n)
            post_w = _prep_post(norm_ffn[i], a_w_out[j], s5_w_glu[j], s5_b_glu[j])
            h_lat_new = _post(h_lat, m_i, None, TM_FFN, post_w, ffn, i, o_l, yg_l)
            if need_ctx:
                o_c = _mla_attention(q_c, k_c, vt_c, tq=n_c)
                h_ctx = _post(h_ctx, m_i, b, tm_ctx, post_w, ffn, i, o_c, yg_c)
            h_lat = h_lat_new
        else:
            pw = _prep_proj_c(norm_mix[i], c_w_in[j], c_q_norm[j], c_k_norm[j])
            q_l, k_l, vt_l = _proj_c(h_lat, m_i, None, TM_PROJ_C, pw, cos_c2, sin_c2)
            q_c, k_c, vt_c = _proj_c(h_ctx, m_i, b, tm_ctx, pw, one_c, zero_c)
            o_l = _win_attention(q_l, k_l, vt_l, k_c, vt_c, c_sink[j], pw["bound"], tq=TQ_WIN)
            post_w = _prep_post(norm_ffn[i], c_w_out[j])
            h_lat_new = _post(h_lat, m_i, None, TM_FFN, post_w, ffn, i, o_l)
            if need_ctx:
                raise NotImplementedError("context queries of a windowed layer")
            h_lat = h_lat_new
    return h_lat
```

```python
import functools
import math

import jax
import jax.numpy as jnp
import numpy as np
from jax import lax
from jax.experimental import pallas as pl
from jax.experimental.pallas import tpu as pltpu

F32 = jnp.float32
BF16 = jnp.bfloat16

GRID_W = 64
NORM_EPS = 1e-6
ROPE_THETA = 10000.0
N_MOD = 6
S5_GROUP_DIM = 16
S5_STATE = 64
S5_CHUNK = 16
MLA_HEADS = 4
MLA_NOPE = 128
MLA_ROPE = 64
MLA_QK_DIM = MLA_NOPE + MLA_ROPE
MLA_V = 128
MLA_Q_RANK = 384
MLA_KV_RANK = 256
MLA_SCALE = MLA_QK_DIM ** -0.5
MLA_HEAD_PAD = 256
WIN_HEADS = 16
WIN_KV_HEADS = 4
WIN_GROUP = WIN_HEADS // WIN_KV_HEADS
WIN_HEAD_DIM = 64
WINDOW = 128
WIN_SCALE = WIN_HEAD_DIM ** -0.5
LANES = 128
SUBLANES = 8
MXU_TILE = 256
VMEM_BYTES_V7X = 64 * 1024 * 1024
VMEM_LIMIT_BYTES = (VMEM_BYTES_V7X * 3) // 4
NEG_BIG = -1e30
LOG2_E = math.log2(math.e)
TM_FFN = 512
TM_PROJ_A = 512
TM_PROJ_C = 1024
TQ_WIN = 256
TK_MLA = 2048
TN_ADA = 1024
PROJ_ROWS = 256
SOFTMAX_DEN_MIN = 2.0 ** -60
SOFTMAX_DEN_MAX = 2.0 ** 60
S5_BLOCK_PAIRS = 4
MLA_Q_BLOCK = 256
MLA_K_PIECE = 256
MOD_ROWS = 8


def _dot(a, b):
    return jnp.dot(a, b, preferred_element_type=F32)


def _dot_nt(a, b):
    return lax.dot_general(a, b, (((1,), (1,)), ((), ())), preferred_element_type=F32)


def _split_bf16(x):
    hi = x.astype(BF16)
    lo = (x - hi.astype(F32)).astype(BF16)
    return hi, lo


def _rms(x, gain):
    return x * lax.rsqrt(jnp.mean(x * x, axis=-1, keepdims=True) + NORM_EPS) * gain


def _silu(x):
    return x * jax.nn.sigmoid(x)


def _params(*sem):
    return pltpu.CompilerParams(dimension_semantics=sem, vmem_limit_bytes=VMEM_LIMIT_BYTES)


def _const_spec(shape, layer=None):
    if layer is None:
        nd = len(shape)
        return pl.BlockSpec(shape, lambda *_: (0,) * nd, pipeline_mode=pl.Buffered(1))
    nd = len(shape) - 1
    return pl.BlockSpec((None,) + tuple(shape[1:]), lambda *_: (layer,) + (0,) * nd,
                        pipeline_mode=pl.Buffered(1))


def _ada_kernel(cond_ref, w_ref, b_ref, o_ref):
    s = _silu(cond_ref[...])
    s_hi, s_lo = _split_bf16(s)
    w = w_ref[...].astype(BF16)
    o_ref[...] = _dot(s_hi, w) + _dot(s_lo, w) + b_ref[...]


def _ada_modulation(cond, ada_w, ada_b):
    depth, d, n = ada_w.shape
    tn = TN_ADA
    return pl.pallas_call(
        _ada_kernel,
        grid=(depth, n // tn),
        in_specs=[pl.BlockSpec((MOD_ROWS, d), lambda i, j: (0, 0)),
                  pl.BlockSpec((None, d, tn), lambda i, j: (i, 0, j)),
                  pl.BlockSpec((None, 1, tn), lambda i, j: (i, 0, j))],
        out_specs=pl.BlockSpec((None, MOD_ROWS, tn), lambda i, j: (i, 0, j)),
        out_shape=jax.ShapeDtypeStruct((depth, MOD_ROWS, n), F32),
        compiler_params=_params("arbitrary", "arbitrary"),
        name="ada_modulation",
    )(cond, ada_w, ada_b.reshape(depth, 1, n))


def _mod_specs(d, slots, ctx_row):
    def make(slot):
        if ctx_row is None:
            return pl.BlockSpec((None, None, 1, d), lambda b, i: (b, slot, 0, 0))
        return pl.BlockSpec((None, None, 1, d), lambda b, i: (ctx_row, slot, 0, 0))
    return [make(s) for s in slots]


def _grid_rope_tables(rows, rot_dim):
    n_freq = rot_dim // 4
    inv_freq = np.power(np.float32(ROPE_THETA), -np.arange(n_freq, dtype=np.float32) / np.float32(n_freq))
    ang_r = np.arange(rows, dtype=np.float32)[:, None] * inv_freq.astype(np.float32)
    ang_c = np.arange(GRID_W, dtype=np.float32)[:, None] * inv_freq.astype(np.float32)

    def expand(r, c):
        r = np.broadcast_to(r[:, None, :], (rows, GRID_W, n_freq))
        c = np.broadcast_to(c[None, :, :], (rows, GRID_W, n_freq))
        return np.concatenate([r, r, c, c], axis=-1).reshape(rows * GRID_W, rot_dim)

    return expand(np.cos(ang_r), np.cos(ang_c)), expand(np.sin(ang_r), np.sin(ang_c))


def _rot_perm_sign(rot_dim):
    q = rot_dim // 4
    idx = np.arange(rot_dim)
    perm = np.where((idx // q) % 2 == 0, idx + q, idx - q)
    sign = np.where((idx // q) % 2 == 0, -1.0, 1.0).astype(np.float32)
    return perm, sign


def _proj_a_kernel(h_ref, sh_ref, sc_ref, gn_ref, w1_ref, gqa_ref, wq_ref, gkv_ref, wkv_ref, wvt_ref,
                   gq_ref, gk_ref, cs_ref, msk_ref, u_ref, q_ref, k_ref, vt_ref):
    tm = h_ref.shape[0]
    rb = min(PROJ_ROWS, tm)
    msk = msk_ref[...]
    low_half = lax.broadcasted_iota(jnp.int32, (rb, LANES), 1) < MLA_ROPE

    def finish(xh, gain, cs, out_ref, rows, h):
        ssq = _dot((xh * xh).astype(BF16), msk)
        xn = xh * lax.rsqrt(ssq * (1.0 / MLA_QK_DIM) + NORM_EPS) * gain
        rr = xn[:, LANES:] * cs
        rot = rr + pltpu.roll(rr, MLA_ROPE, axis=1)
        base = h * MLA_HEAD_PAD
        out_ref[rows, base:base + LANES] = xn[:, :LANES].astype(BF16)
        out_ref[rows, base + LANES:base + 2 * LANES] = jnp.where(low_half, rot, 0.0).astype(BF16)

    for r in range(tm // rb):
        rows = slice(r * rb, (r + 1) * rb)
        a = _rms(h_ref[rows, :], gn_ref[...]) * (1.0 + sc_ref[...]) + sh_ref[...]
        p1 = _dot(a.astype(BF16), w1_ref[...])
        s5w = u_ref.shape[1]
        u_ref[rows, :] = p1[:, :s5w]
        cq = p1[:, s5w:s5w + MLA_Q_RANK]
        ckv = p1[:, s5w + MLA_Q_RANK:s5w + MLA_Q_RANK + MLA_KV_RANK]
        krr = p1[:, s5w + MLA_Q_RANK + MLA_KV_RANK:]
        qb = _dot(_rms(cq, gqa_ref[...]).astype(BF16), wq_ref[...])
        ckv_n = _rms(ckv, gkv_ref[...]).astype(BF16)
        kv = _dot(ckv_n, wkv_ref[...])
        vt_ref[:, rows] = _dot_nt(wvt_ref[...], ckv_n).astype(BF16)
        cs = cs_ref[rows, :]
        for h in range(MLA_HEADS):
            base = h * MLA_HEAD_PAD
            finish(qb[:, base:base + MLA_HEAD_PAD], gq_ref[:, base:base + MLA_HEAD_PAD], cs, q_ref, rows, h)
            kh = jnp.concatenate([kv[:, h * MLA_NOPE:(h + 1) * MLA_NOPE], krr], axis=1)
            finish(kh, gk_ref[:, base:base + MLA_HEAD_PAD], cs, k_ref, rows, h)


def _proj_a(h, mods, ctx_row, tm, wts, cs):
    b, t, d = h.shape
    qw = MLA_HEADS * MLA_HEAD_PAD
    grid = (b, t // tm)
    tile = lambda w: pl.BlockSpec((None, tm, w), lambda bi, i: (bi, i, 0))
    in_specs = ([tile(d)] + _mod_specs(d, (0, 1), ctx_row)
                + [_const_spec(wts[k].shape) for k in
                   ("gn", "w1", "gqa", "wq", "gkv", "wkv", "wvt", "gq", "gk")]
                + [pl.BlockSpec((tm, LANES), lambda bi, i: (i, 0)), _const_spec(wts["msk"].shape)])
    vw = MLA_HEADS * MLA_V
    s5w = wts["w1"].shape[1] - (MLA_Q_RANK + MLA_KV_RANK + 2 * MLA_ROPE)
    return pl.pallas_call(
        _proj_a_kernel,
        grid=grid,
        in_specs=in_specs,
        out_specs=[tile(s5w), tile(qw), tile(qw), pl.BlockSpec((None, vw, tm), lambda bi, i: (bi, 0, i))],
        out_shape=[jax.ShapeDtypeStruct((b, t, s5w), F32),
                   jax.ShapeDtypeStruct((b, t, qw), BF16),
                   jax.ShapeDtypeStruct((b, t, qw), BF16),
                   jax.ShapeDtypeStruct((b, vw, t), BF16)],
        compiler_params=_params("parallel", "parallel"),
        name="proj_a",
    )(h, mods, mods, wts["gn"], wts["w1"], wts["gqa"], wts["wq"], wts["gkv"], wts["wkv"], wts["wvt"],
      wts["gq"], wts["gk"], cs, wts["msk"])


def _prep_proj_a(norm_mix, a_w_in, qa_norm, w_q_b, kva_norm, w_kv_b, q_norm, k_norm):
    perm, sign = _rot_perm_sign(MLA_ROPE)
    s5w = a_w_in.shape[1] - (MLA_Q_RANK + MLA_KV_RANK + MLA_ROPE)
    assert s5w % LANES == 0 and MLA_Q_RANK % LANES == 0 and MLA_KV_RANK % LANES == 0
    kr = a_w_in[:, -MLA_ROPE:]
    w1 = jnp.concatenate([a_w_in, kr[:, perm] * sign], axis=1).astype(BF16)
    wq = w_q_b.reshape(MLA_Q_RANK, MLA_HEADS, MLA_QK_DIM)
    rope = wq[:, :, MLA_NOPE:]
    wq = jnp.concatenate([wq, rope[:, :, perm] * sign], axis=2)
    wq = wq.reshape(MLA_Q_RANK, MLA_HEADS * MLA_HEAD_PAD).astype(BF16)
    wkv3 = w_kv_b.reshape(MLA_KV_RANK, MLA_HEADS, MLA_NOPE + MLA_V)
    wkv = wkv3[:, :, :MLA_NOPE].reshape(MLA_KV_RANK, -1).astype(BF16)
    wvt = wkv3[:, :, MLA_NOPE:].reshape(MLA_KV_RANK, -1).T.astype(BF16)

    def head_gain(g, scale):
        gb = jnp.concatenate([g, g[MLA_NOPE:][perm]]) * scale
        return jnp.tile(gb, MLA_HEADS)[None, :]

    rows = np.arange(MLA_HEAD_PAD)[:, None] < MLA_QK_DIM
    msk = jnp.asarray(np.broadcast_to(rows, (MLA_HEAD_PAD, MLA_HEAD_PAD)), BF16)
    return dict(gn=norm_mix[None, :], w1=w1, gqa=qa_norm[None, :], wq=wq, gkv=kva_norm[None, :],
                wkv=wkv, wvt=wvt, gq=head_gain(q_norm, MLA_SCALE * LOG2_E), gk=head_gain(k_norm, 1.0),
                msk=msk)


def _s5_kernel(uc_ref, ul_ref, sel_in_ref, sel_out_ref, toep_ref, bst_ref, cst_ref, a_ref, d_ref,
               yc_ref, yl_ref, ub_ref, x_ref, z_ref, sin_ref, yb_ref, *, n_ctx, n_all):
    L, S = S5_CHUNK, S5_GROUP_DIM
    n_lat = n_all - n_ctx
    npair = S5_BLOCK_PAIRS
    half = SUBLANES * LANES
    for tl in range(L):
        u = jnp.concatenate([uc_ref[pl.ds(tl, n_ctx, stride=L), :], ul_ref[pl.ds(tl, n_lat, stride=L), :]], axis=0)
        ub_ref[:, tl * LANES:(tl + 1) * LANES] = u.astype(BF16)
    for pp in range(npair):
        for hh in range(2):
            xs = _dot(ub_ref[:, hh * half:(hh + 1) * half], sel_in_ref[pp]).astype(BF16)
            x_ref[pp, :, hh * LANES:(hh + 1) * LANES] = xs[:, :LANES]
            x_ref[pp, :, MXU_TILE + hh * LANES:MXU_TILE + (hh + 1) * LANES] = xs[:, LANES:]
    for pp in range(npair):
        z_ref[pp] = _dot(x_ref[pp], bst_ref[pp])
    t_ctx, t_all = n_ctx // SUBLANES, n_all // SUBLANES
    row = lax.broadcasted_iota(jnp.int32, (SUBLANES, LANES), 0)

    def cmul(a_re, a_im, b_re, b_im):
        return a_re * b_re - a_im * b_im, a_re * b_im + a_im * b_re

    def tile_scan(z_re, z_im, c_re, c_im, a_re, a_im, fwd):
        for sft in (1, 2, 4):
            k = sft - 1 if fwd else SUBLANES - sft
            p_re, p_im = a_re[k:k + 1, :], a_im[k:k + 1, :]
            amt = sft if fwd else SUBLANES - sft
            keep = (row >= sft) if fwd else (row < SUBLANES - sft)
            s_re = jnp.where(keep, pltpu.roll(z_re, amt, axis=0), 0.0)
            s_im = jnp.where(keep, pltpu.roll(z_im, amt, axis=0), 0.0)
            m_re, m_im = cmul(p_re, p_im, s_re, s_im)
            z_re, z_im = z_re + m_re, z_im + m_im
        m_re, m_im = cmul(a_re, a_im, c_re, c_im)
        s_re, s_im = z_re + m_re, z_im + m_im
        edge = (row == 0) if fwd else (row == SUBLANES - 1)
        amt = 1 if fwd else SUBLANES - 1
        in_re = jnp.where(edge, c_re, pltpu.roll(s_re, amt, axis=0))
        in_im = jnp.where(edge, c_im, pltpu.roll(s_im, amt, axis=0))
        last = SUBLANES - 1 if fwd else 0
        return in_re, in_im, s_re[last:last + 1, :], s_im[last:last + 1, :]

    def body(it, carry):
        jt = jnp.where(it < t_ctx, t_ctx - 1 - it, t_all + t_ctx - 1 - it)
        rf = pl.multiple_of(it * SUBLANES, SUBLANES)
        rb = pl.multiple_of(jt * SUBLANES, SUBLANES)
        new = []
        for pp in range(npair):
            cf_re, cf_im, cb_re, cb_im = carry[4 * pp:4 * pp + 4]
            zf = z_ref[pp, pl.ds(rf, SUBLANES), 0:2 * LANES]
            zb = z_ref[pp, pl.ds(rb, SUBLANES), 2 * LANES:4 * LANES]
            f_re, f_im, cf_re, cf_im = tile_scan(zf[:, :LANES], zf[:, LANES:], cf_re, cf_im,
                                                 a_ref[pp, 0], a_ref[pp, 1], True)
            b_re, b_im, cb_re, cb_im = tile_scan(zb[:, :LANES], zb[:, LANES:], cb_re, cb_im,
                                                 a_ref[pp, 2], a_ref[pp, 3], False)
            sin_ref[pp, pl.ds(rf, SUBLANES), 0:2 * LANES] = jnp.concatenate([f_re, f_im], axis=1)
            sin_ref[pp, pl.ds(rb, SUBLANES), 2 * LANES:4 * LANES] = jnp.concatenate([b_re, b_im], axis=1)
            new += [cf_re, cf_im, cb_re, cb_im]
        return tuple(new)

    zero = jnp.zeros((1, LANES), F32)
    lax.fori_loop(0, t_all, body, (zero,) * (4 * npair))
    for pp in range(npair):
        x = x_ref[pp]
        y = jnp.concatenate([_dot(x[:, :MXU_TILE], toep_ref[pp, 0]), _dot(x[:, MXU_TILE:], toep_ref[pp, 1])],
                            axis=1)
        y = y + _dot(sin_ref[pp].astype(BF16), cst_ref[pp]) + x.astype(F32) * d_ref[pp]
        yg = jax.nn.gelu(y).astype(BF16)
        for gl in range(2):
            for hh in range(2):
                g8 = 2 * pp + gl
                col = gl * MXU_TILE + hh * LANES
                yb_ref[hh, :, g8 * LANES:(g8 + 1) * LANES] = yg[:, col:col + LANES]
    for hh in range(2):
        for kk in range(SUBLANES // 2):
            two = _dot(yb_ref[hh], sel_out_ref[kk])
            for e in range(2):
                tl = hh * SUBLANES + 2 * kk + e
                yc_ref[pl.ds(tl, n_ctx, stride=L), :] = two[:n_ctx, e * LANES:(e + 1) * LANES]
                yl_ref[pl.ds(tl, n_lat, stride=L), :] = two[n_ctx:, e * LANES:(e + 1) * LANES]


def _s5_selectors():
    S = S5_GROUP_DIM
    r = np.arange(SUBLANES * LANES)[:, None]
    c = np.arange(2 * LANES)[None, :]
    k, l = r // LANES, r % LANES
    pp = np.arange(S5_BLOCK_PAIRS)[:, None, None]
    sel_in = (k == (c % LANES) // S) & (l == 2 * S * pp + S * (c // LANES) + c % S)
    kk = np.arange(SUBLANES // 2)[:, None, None]
    sel_out = (k == (c % LANES) // S) & (l == S * (2 * kk + c // LANES) + c % S)
    return jnp.asarray(sel_in, BF16), jnp.asarray(sel_out, BF16)


def _dot_nt_f32(a, b):
    a_hi, a_lo = _split_bf16(a)
    b_hi, b_lo = _split_bf16(b)
    return _dot_nt(a_hi, b_hi) + _dot_nt(a_lo, b_hi) + _dot_nt(a_hi, b_lo)


def _s5_table_kernel(par_ref, bre_ref, bim_ref, cre_ref, cim_ref, toep_ref, bst_ref, cst_ref, a_ref):
    L, S, P = S5_CHUNK, S5_GROUP_DIM, S5_STATE
    kk = lax.broadcasted_iota(jnp.int32, (3 * SUBLANES, LANES), 0).astype(F32)
    lane = lax.broadcasted_iota(jnp.int32, (L, LANES), 1)
    pair_rows = lax.broadcasted_iota(jnp.int32, (2 * L * S, LANES), 0)
    pair_lanes = lax.broadcasted_iota(jnp.int32, (2 * L * S, LANES), 1)
    own_group = (pair_rows // (L * S)) == (pair_lanes // P)
    lane_pad = jnp.zeros((2 * P, LANES - S), F32)

    def rows_of_powers(pw, ks, groups):
        one = jnp.concatenate([jnp.broadcast_to(pw[k:k + 1, :], (S, LANES)) for k in ks], axis=0)
        return jnp.concatenate([one] * groups, axis=0) if groups > 1 else one

    def cmul(a_re, a_im, b_re, b_im):
        return a_re * b_re - a_im * b_im, a_re * b_im + a_im * b_re

    lag_tables = []
    for d in range(2):
        lam_re, lam_im = par_ref[d, 0:1, :], par_ref[d, 1:2, :]
        step = jnp.exp(par_ref[d, 2:3, :])
        ar, ai = lam_re * step, lam_im * step
        mag = jnp.exp(kk * ar)
        pw_re, pw_im = mag * jnp.cos(kk * ai), mag * jnp.sin(kk * ai)
        th = jnp.tanh(0.5 * ar)
        em1 = 2.0 * th / (1.0 - th)
        sh = jnp.sin(0.5 * ai)
        n_re = em1 * jnp.cos(ai) - 2.0 * sh * sh
        n_im = (em1 + 1.0) * jnp.sin(ai)
        den = lam_re * lam_re + lam_im * lam_im
        co_re = (n_re * lam_re + n_im * lam_im) / den
        co_im = (n_im * lam_re - n_re * lam_im) / den
        bt_re = jnp.concatenate([bre_ref[d], lane_pad], axis=1).T[:S]
        bt_im = jnp.concatenate([bim_ref[d], lane_pad], axis=1).T[:S]
        bb_re, bb_im = cmul(co_re, co_im, bt_re, bt_im)
        cc_re = jnp.concatenate([cre_ref[d, 0], cre_ref[d, 1]], axis=1)
        cc_im = jnp.concatenate([cim_ref[d, 0], cim_ref[d, 1]], axis=1)
        ks = [L - 1 - t for t in range(L)] if d == 0 else list(range(L))
        r_re, r_im = rows_of_powers(pw_re, ks, 2), rows_of_powers(pw_im, ks, 2)
        bbt_re, bbt_im = jnp.concatenate([bb_re] * (2 * L), axis=0), jnp.concatenate([bb_im] * (2 * L), axis=0)
        v_re, v_im = cmul(r_re, r_im, bbt_re, bbt_im)
        bst_ref[:, (2 * d) * LANES:(2 * d + 1) * LANES] = jnp.where(own_group, v_re, 0.0).astype(BF16)
        bst_ref[:, (2 * d + 1) * LANES:(2 * d + 2) * LANES] = jnp.where(own_group, v_im, 0.0).astype(BF16)
        ks = [t + 1 for t in range(L)] if d == 0 else [L - t for t in range(L)]
        r_re, r_im = rows_of_powers(pw_re, ks, 2), rows_of_powers(pw_im, ks, 2)
        cct_re, cct_im = jnp.concatenate([cc_re] * (2 * L), axis=0), jnp.concatenate([cc_im] * (2 * L), axis=0)
        v_re, v_im = cmul(cct_re, cct_im, r_re, r_im)
        cst_ref[(2 * d) * LANES:(2 * d + 1) * LANES, :] = jnp.where(own_group, v_re, 0.0).T.astype(BF16)
        cst_ref[(2 * d + 1) * LANES:(2 * d + 2) * LANES, :] = jnp.where(own_group, -v_im, 0.0).T.astype(BF16)
        ks = list(range(L)) if d == 0 else [L - 1 - j for j in range(L)]
        r_re, r_im = rows_of_powers(pw_re, ks, 1), rows_of_powers(pw_im, ks, 1)
        cl_re, cl_im = cmul(jnp.concatenate([cc_re] * L, axis=0), jnp.concatenate([cc_im] * L, axis=0), r_re, r_im)
        per_group = []
        for g in range(2):
            mine = (lane // P) == g
            per_group.append(_dot_nt_f32(jnp.where(mine, bb_re, 0.0), cl_re)
                             - _dot_nt_f32(jnp.where(mine, bb_im, 0.0), cl_im))
        lag_tables.append(per_group)
        row8 = lax.broadcasted_iota(jnp.int32, (SUBLANES, LANES), 0)
        n_chunks = ((row8 + 1) if d == 0 else (SUBLANES - row8)).astype(F32) * float(L)
        mag8 = jnp.exp(n_chunks * ar)
        a_ref[2 * d] = mag8 * jnp.cos(n_chunks * ai)
        a_ref[2 * d + 1] = mag8 * jnp.sin(n_chunks * ai)

    def shift_right(x, s):
        x0, x1 = x[:, :LANES], x[:, LANES:]
        a, r = divmod(s, LANES)
        r0 = pltpu.roll(x0, r, axis=1) if r else x0
        r1 = pltpu.roll(x1, r, axis=1) if r else x1
        if a == 0:
            return jnp.concatenate([jnp.where(lane >= r, r0, 0.0), jnp.where(lane >= r, r1, r0)], axis=1)
        return jnp.concatenate([jnp.zeros_like(x0), jnp.where(lane >= r, r0, 0.0)], axis=1)

    def shift_left(x, s):
        x0, x1 = x[:, :LANES], x[:, LANES:]
        a, r = divmod(s, LANES)
        r0 = pltpu.roll(x0, LANES - r, axis=1) if r else x0
        r1 = pltpu.roll(x1, LANES - r, axis=1) if r else x1
        if a == 0:
            return jnp.concatenate([jnp.where(lane < LANES - r, r0, r1), jnp.where(lane < LANES - r, r1, 0.0)], axis=1)
        return jnp.concatenate([jnp.where(lane < LANES - r, r1, 0.0), jnp.zeros_like(x0)], axis=1)

    for g in range(2):
        kf, kb = lag_tables[0][g], lag_tables[1][g]
        for tau in range(L):
            blk = shift_right(kf, S * tau) + shift_left(kb, S * (L - 1 - tau))
            toep_ref[g, tau * S:(tau + 1) * S, :] = blk.astype(BF16)


def _s5_tables(lam_re, lam_im, log_step, b_re, b_im, c_re, c_im, d_skip):
    _, G, P = lam_re.shape
    S, L = S5_GROUP_DIM, S5_CHUNK
    assert P == S5_STATE and 2 * P == LANES and 2 * L * S == 2 * MXU_TILE
    pairs = G // 2
    par = jnp.stack([lam_re.reshape(2, pairs, 2 * P), lam_im.reshape(2, pairs, 2 * P),
                     jnp.repeat(log_step, P, axis=-1).reshape(2, pairs, 2 * P)], axis=2)
    par = jnp.transpose(par, (1, 0, 2, 3)).astype(F32)
    bshape = (2, pairs, 2 * P, S)
    cshape = (2, pairs, 2, S, P)
    pw = 2 * L * S
    bspec = pl.BlockSpec((2, None, 2 * P, S), lambda g: (0, g, 0, 0))
    cspec = pl.BlockSpec((2, None, 2, S, P), lambda g: (0, g, 0, 0, 0))
    toep, bst, cst, a_chunk = pl.pallas_call(
        _s5_table_kernel,
        grid=(pairs,),
        in_specs=[pl.BlockSpec((None, 2, 3, 2 * P), lambda g: (g, 0, 0, 0)), bspec, bspec, cspec, cspec],
        out_specs=[pl.BlockSpec((None, 2, MXU_TILE, MXU_TILE), lambda g: (g, 0, 0, 0)),
                   pl.BlockSpec((None, pw, pw), lambda g: (g, 0, 0)),
                   pl.BlockSpec((None, pw, pw), lambda g: (g, 0, 0)),
                   pl.BlockSpec((None, 4, SUBLANES, LANES), lambda g: (g, 0, 0, 0))],
        out_shape=[jax.ShapeDtypeStruct((pairs, 2, MXU_TILE, MXU_TILE), BF16),
                   jax.ShapeDtypeStruct((pairs, pw, pw), BF16),
                   jax.ShapeDtypeStruct((pairs, pw, pw), BF16),
                   jax.ShapeDtypeStruct((pairs, 4, SUBLANES, LANES), F32)],
        compiler_params=_params("parallel"),
        name="s5_tables",
    )(par, b_re.reshape(bshape), b_im.reshape(bshape), c_re.reshape(cshape), c_im.reshape(cshape))
    d_pair = jnp.broadcast_to(d_skip.astype(F32).reshape(pairs, 2, 1, S), (pairs, 2, L, S))
    return toep, bst, cst, a_chunk, d_pair.reshape(pairs, 1, pw)


def _s5(u_ctx, u_lat, tables):
    toep, bst, cst, a_pow, d_pair = tables
    b, n_c, w = u_ctx.shape
    n_l = u_lat.shape[1]
    L = S5_CHUNK
    n_ctx, n_all = n_c // L, (n_c + n_l) // L
    nblk = w // LANES
    pw = 2 * L * S5_GROUP_DIM
    npair = S5_BLOCK_PAIRS
    sel_in, sel_out = _s5_selectors()
    wspec = lambda shape: pl.BlockSpec((npair,) + shape, lambda g, bi: (g,) + (0,) * len(shape),
                                       pipeline_mode=pl.Buffered(1))
    return pl.pallas_call(
        functools.partial(_s5_kernel, n_ctx=n_ctx, n_all=n_all),
        grid=(nblk, b),
        in_specs=[pl.BlockSpec((None, n_c, LANES), lambda g, bi: (bi, 0, g)),
                  pl.BlockSpec((None, n_l, LANES), lambda g, bi: (bi, 0, g)),
                  _const_spec(sel_in.shape), _const_spec(sel_out.shape),
                  wspec((2, MXU_TILE, MXU_TILE)), wspec((pw, pw)), wspec((pw, pw)),
                  wspec((4, SUBLANES, LANES)), wspec((1, pw))],
        out_specs=[pl.BlockSpec((None, n_c, LANES), lambda g, bi: (bi, 0, g)),
                   pl.BlockSpec((None, n_l, LANES), lambda g, bi: (bi, 0, g))],
        out_shape=[jax.ShapeDtypeStruct((b, n_c, w), F32), jax.ShapeDtypeStruct((b, n_l, w), F32)],
        scratch_shapes=[pltpu.VMEM((n_all, L * LANES), BF16), pltpu.VMEM((npair, n_all, pw), BF16),
                        pltpu.VMEM((npair, n_all, pw), F32), pltpu.VMEM((npair, n_all, pw), F32),
                        pltpu.VMEM((2, n_all, SUBLANES * LANES), BF16)],
        compiler_params=_params("parallel", "arbitrary"),
        name="s5_scan",
    )(u_ctx, u_lat, sel_in, sel_out, toep, bst, cst, a_pow, d_pair)


def _mla_kernel(*refs, tk, n_steps):
    if n_steps:
        q_ref, kc_ref, vc_ref, k_ref, v_ref, o_ref, qt_ref, s_ref, m_ref, l_ref, acc_ref = refs
    else:
        q_ref, kc_ref, vc_ref, o_ref, qt_ref, s_ref, m_ref, l_ref, acc_ref = refs
    tq = q_ref.shape[0]
    ncb = tq // MLA_Q_BLOCK
    kp = MLA_K_PIECE
    qt_ref[...] = q_ref[...].astype(F32).T.astype(BF16)

    def chunk(load_k, load_vt, nkeys, first):
        nr = nkeys // kp

        def score_piece(c, r):
            st = _dot(load_k(r), qt_ref[:, c * MLA_Q_BLOCK:(c + 1) * MLA_Q_BLOCK])
            s_ref[c % 2, r * kp:(r + 1) * kp, :] = st
            return jnp.max(st, axis=0, keepdims=True)

        def block_stats(c, mx):
            if first:
                return mx, None
            m_old = m_ref[:, c * MLA_Q_BLOCK:(c + 1) * MLA_Q_BLOCK]
            m_new = jnp.maximum(m_old, mx)
            return m_new, jnp.exp2(m_old - m_new)

        def prob_piece(c, r, m_new):
            p = jnp.exp2(s_ref[c % 2, r * kp:(r + 1) * kp, :] - m_new)
            return jnp.sum(p, axis=0, keepdims=True), _dot(load_vt(r), p.astype(BF16))

        def finish(c, m_new, alpha, lsum, pv):
            cols = slice(c * MLA_Q_BLOCK, (c + 1) * MLA_Q_BLOCK)
            if first:
                l_ref[:, cols] = lsum
                acc_ref[:, cols] = pv
            else:
                l_ref[:, cols] = alpha * l_ref[:, cols] + lsum
                acc_ref[:, cols] = alpha * acc_ref[:, cols] + pv
            m_ref[:, cols] = m_new

        mx = None
        for r in range(nr):
            pm = score_piece(0, r)
            mx = pm if mx is None else jnp.maximum(mx, pm)
        for c in range(ncb):
            m_new, alpha = block_stats(c, mx)
            mx = lsum = pv = None
            for r in range(nr):
                if c + 1 < ncb:
                    pm = score_piece(c + 1, r)
                    mx = pm if mx is None else jnp.maximum(mx, pm)
                ls, pvr = prob_piece(c, r, m_new)
                lsum = ls if lsum is None else lsum + ls
                pv = pvr if pv is None else pv + pvr
            finish(c, m_new, alpha, lsum, pv)

    chunk(lambda r: kc_ref[r * kp:(r + 1) * kp, :], lambda r: vc_ref[:, r * kp:(r + 1) * kp],
          kc_ref.shape[0], True)
    if n_steps:
        def body(j, _):
            off = pl.multiple_of(j * tk, tk)
            chunk(lambda r: k_ref[pl.ds(off + r * kp, kp), :], lambda r: v_ref[:, pl.ds(off + r * kp, kp)],
                  tk, False)
            return 0
        lax.fori_loop(0, n_steps, body, 0)
    o_ref[...] = (acc_ref[...] / l_ref[...]).T.astype(BF16)


def _mla_attention(q, k_ctx, vt_ctx, k_lat=None, vt_lat=None, *, tq, tk=TK_MLA):
    b, t, _ = q.shape
    n_c = k_ctx.shape[1]
    in_specs = [pl.BlockSpec((None, tq, MLA_HEAD_PAD), lambda bi, h, i: (bi, i, h)),
                pl.BlockSpec((None, n_c, MLA_HEAD_PAD), lambda bi, h, i: (bi, 0, h)),
                pl.BlockSpec((None, MLA_V, n_c), lambda bi, h, i: (bi, h, 0))]
    args = [q, k_ctx, vt_ctx]
    n_steps = 0
    if k_lat is not None:
        n_l = k_lat.shape[1]
        tk = min(tk, n_l)
        assert n_l % tk == 0
        n_steps = n_l // tk
        in_specs += [pl.BlockSpec((None, n_l, MLA_HEAD_PAD), lambda bi, h, i: (bi, 0, h)),
                     pl.BlockSpec((None, MLA_V, n_l), lambda bi, h, i: (bi, h, 0))]
        args += [k_lat, vt_lat]
    return pl.pallas_call(
        functools.partial(_mla_kernel, tk=tk, n_steps=n_steps),
        grid=(b, MLA_HEADS, t // tq),
        in_specs=in_specs,
        out_specs=pl.BlockSpec((None, tq, MLA_V), lambda bi, h, i: (bi, i, h)),
        out_shape=jax.ShapeDtypeStruct((b, t, MLA_HEADS * MLA_V), BF16),
        scratch_shapes=[pltpu.VMEM((MLA_HEAD_PAD, tq), BF16), pltpu.VMEM((2, max(tk, n_c), MLA_Q_BLOCK), F32),
                        pltpu.VMEM((1, tq), F32), pltpu.VMEM((1, tq), F32), pltpu.VMEM((MLA_V, tq), F32)],
        compiler_params=_params("parallel", "parallel", "arbitrary"),
        name="mla_attention",
    )(*args)


def _proj_c_kernel(h_ref, sh_ref, sc_ref, gn_ref, wc_ref, wvt_ref, gqk_ref, cos_ref, sin_ref, bd_ref,
                   q_ref, k_ref, vt_ref):
    qw = WIN_HEADS * WIN_HEAD_DIM
    kw = WIN_KV_HEADS * WIN_HEAD_DIM
    bd = bd_ref[...]
    tm = h_ref.shape[0]
    rb = min(PROJ_ROWS, tm)
    lane = lax.broadcasted_iota(jnp.int32, (rb, LANES), 1)
    first_quarter = (lane % (WIN_HEAD_DIM // 2)) < (WIN_HEAD_DIM // 4)
    for r in range(tm // rb):
        rows = slice(r * rb, (r + 1) * rb)
        a = (_rms(h_ref[rows, :], gn_ref[...]) * (1.0 + sc_ref[...]) + sh_ref[...]).astype(BF16)
        p = _dot(a, wc_ref[...])
        vt_ref[:, rows] = _dot_nt(wvt_ref[...], a).astype(BF16)
        cos, sin = cos_ref[rows, :], sin_ref[rows, :]
        for j in range((qw + kw) // MXU_TILE):
            xh = p[:, j * MXU_TILE:(j + 1) * MXU_TILE]
            ssq = _dot((xh * xh).astype(BF16), bd)
            xn = xh * lax.rsqrt(ssq * (1.0 / WIN_HEAD_DIM) + NORM_EPS) * gqk_ref[:, j * MXU_TILE:(j + 1) * MXU_TILE]
            for c in range(MXU_TILE // LANES):
                xc = xn[:, c * LANES:(c + 1) * LANES]
                fwd = pltpu.roll(xc, WIN_HEAD_DIM // 4, axis=1)
                bwd = pltpu.roll(xc, LANES - WIN_HEAD_DIM // 4, axis=1)
                y = (xc * cos + jnp.where(first_quarter, -bwd, fwd) * sin).astype(BF16)
                col = j * MXU_TILE + c * LANES
                if col < qw:
                    q_ref[rows, col:col + LANES] = y
                else:
                    k_ref[rows, col - qw:col - qw + LANES] = y


def _proj_c(h, mods, ctx_row, tm, wts, cos2, sin2):
    b, t, d = h.shape
    qw = WIN_HEADS * WIN_HEAD_DIM
    kw = WIN_KV_HEADS * WIN_HEAD_DIM
    tile = lambda w: pl.BlockSpec((None, tm, w), lambda bi, i: (bi, i, 0))
    tab = pl.BlockSpec((tm, LANES), lambda bi, i: (i, 0))
    in_specs = ([tile(d)] + _mod_specs(d, (0, 1), ctx_row)
                + [_const_spec(wts[k].shape) for k in ("gn", "wc", "wvt", "gqk")]
                + [tab, tab, _const_spec(wts["bd"].shape)])
    return pl.pallas_call(
        _proj_c_kernel,
        grid=(b, t // tm),
        in_specs=in_specs,
        out_specs=[tile(qw), tile(kw), pl.BlockSpec((None, kw, tm), lambda bi, i: (bi, 0, i))],
        out_shape=[jax.ShapeDtypeStruct((b, t, qw), BF16),
                   jax.ShapeDtypeStruct((b, t, kw), BF16),
                   jax.ShapeDtypeStruct((b, kw, t), BF16)],
        compiler_params=_params("parallel", "parallel"),
        name="proj_c",
    )(h, mods, mods, wts["gn"], wts["wc"], wts["wvt"], wts["gqk"], cos2, sin2, wts["bd"])


def _prep_proj_c(norm_mix, c_w_in, q_norm, k_norm):
    gqk = jnp.concatenate([jnp.tile(q_norm * (WIN_SCALE * LOG2_E), WIN_HEADS), jnp.tile(k_norm, WIN_KV_HEADS)])
    idx = np.arange(MXU_TILE) // WIN_HEAD_DIM
    bd = jnp.asarray(idx[:, None] == idx[None, :], BF16)
    qk = (WIN_HEADS + WIN_KV_HEADS) * WIN_HEAD_DIM
    bound = 1.01 * WIN_HEAD_DIM * jnp.max(jnp.abs(q_norm * (WIN_SCALE * LOG2_E))) * jnp.max(jnp.abs(k_norm))
    return dict(gn=norm_mix[None, :], wc=c_w_in[:, :qk].astype(BF16), wvt=c_w_in[:, qk:].T.astype(BF16),
                gqk=gqk[None, :], bd=bd, bound=bound.astype(F32))


def _win_kernel(sc_ref, q_ref, k_ref, vt_ref, kc_ref, vct_ref, o_ref, ot_ref, *, tq, band, n_lat):
    i = pl.program_id(1)
    start = pl.multiple_of(jnp.clip(i * tq - WINDOW, 0, n_lat - band), WINDOW)
    hd, grp = WIN_HEAD_DIM, WIN_GROUP
    bound = sc_ref[WIN_HEADS]
    qt = q_ref[...].astype(F32).T.astype(BF16)
    k_pos = start + lax.broadcasted_iota(jnp.int32, (band, tq), 0)
    q_pos = i * tq + lax.broadcasted_iota(jnp.int32, (band, tq), 1)
    bias1 = jnp.where(jnp.abs(k_pos - q_pos) <= WINDOW, 0.0, NEG_BIG)
    bias = jnp.concatenate([bias1] * grp, axis=1)
    zeros = jnp.zeros((hd, grp * tq), BF16)

    def group_inputs(kv):
        qg = jnp.concatenate([qt[(kv * grp + g) * hd:(kv * grp + g + 1) * hd, :] for g in range(grp)], axis=1)
        qg = jnp.concatenate([qg, zeros] if kv % 2 == 0 else [zeros, qg], axis=0)
        col = (kv // 2) * LANES
        sink = jnp.concatenate([jnp.full((1, tq), sc_ref[kv * grp + g], F32) for g in range(grp)], axis=1)
        return qg, col, sink

    def weighted_values(kv, p_ctx, p_loc):
        return (_dot(vct_ref[kv * hd:(kv + 1) * hd, :], p_ctx.astype(BF16))
                + _dot(vt_ref[kv * hd:(kv + 1) * hd, pl.ds(start, band)], p_loc.astype(BF16)))

    def regroup(ot):
        return jnp.concatenate([ot[:, g * tq:(g + 1) * tq] for g in range(grp)], axis=0)

    def one_pass(kv):
        qg, col, sink = group_inputs(kv)
        p_ctx = jnp.exp2(_dot(kc_ref[:, col:col + LANES], qg) - bound)
        p_loc = jnp.exp2(_dot(k_ref[pl.ds(start, band), col:col + LANES], qg) + (bias - bound))
        den = (jnp.sum(p_loc, axis=0, keepdims=True) + jnp.sum(p_ctx, axis=0, keepdims=True)
               + jnp.exp2(sink - bound))
        ot_ref[kv * grp * hd:(kv + 1) * grp * hd, :] = regroup(weighted_values(kv, p_ctx, p_loc) / den)
        return den

    def two_pass(kv):
        qg, col, sink = group_inputs(kv)
        s_ctx = _dot(kc_ref[:, col:col + LANES], qg)
        s_loc = _dot(k_ref[pl.ds(start, band), col:col + LANES], qg) + bias
        m = jnp.maximum(jnp.maximum(jnp.max(s_loc, axis=0, keepdims=True),
                                    jnp.max(s_ctx, axis=0, keepdims=True)), sink)
        p_loc = jnp.exp2(s_loc - m)
        p_ctx = jnp.exp2(s_ctx - m)
        den = (jnp.sum(p_loc, axis=0, keepdims=True) + jnp.sum(p_ctx, axis=0, keepdims=True)
               + jnp.exp2(sink - m))
        ot_ref[kv * grp * hd:(kv + 1) * grp * hd, :] = regroup(weighted_values(kv, p_ctx, p_loc) / den)

    lo = hi = None
    for kv in range(WIN_KV_HEADS):
        den = one_pass(kv)
        lo = den if lo is None else jnp.minimum(lo, den)
        hi = den if hi is None else jnp.maximum(hi, den)
    unsafe = jnp.logical_or(jnp.min(lo) < SOFTMAX_DEN_MIN, jnp.max(hi) > SOFTMAX_DEN_MAX)

    @pl.when(unsafe)
    def _():
        for kv in range(WIN_KV_HEADS):
            two_pass(kv)

    o_ref[...] = ot_ref[...].T.astype(BF16)


def _win_attention(q, k, vt, k_ctx, vt_ctx, sink, score_bound, *, tq):
    b, n, qw = q.shape
    n_c = k_ctx.shape[1]
    kw = k.shape[2]
    band = tq + 2 * WINDOW
    full = lambda r, w: pl.BlockSpec((None, r, w), lambda bi, i: (bi, 0, 0))
    return pl.pallas_call(
        functools.partial(_win_kernel, tq=tq, band=band, n_lat=n),
        grid=(b, n // tq),
        in_specs=[pl.BlockSpec(memory_space=pltpu.SMEM),
                  pl.BlockSpec((None, tq, qw), lambda bi, i: (bi, i, 0)),
                  full(n, kw), full(kw, n), full(n_c, kw), full(kw, n_c)],
        out_specs=pl.BlockSpec((None, tq, qw), lambda bi, i: (bi, i, 0)),
        out_shape=jax.ShapeDtypeStruct((b, n, qw), BF16),
        scratch_shapes=[pltpu.VMEM((qw, tq), F32)],
        compiler_params=_params("parallel", "arbitrary"),
        name="win_attention",
    )(jnp.concatenate([sink.astype(F32) * LOG2_E, score_bound.reshape(1)]), q, k, vt, k_ctx, vt_ctx)


def _post_kernel(*refs, s5_width, n_chunks):
    if s5_width:
        (h_ref, g_ref, sh_ref, sc_ref, g2_ref, gn_ref, yg_ref, o_ref, wglu_ref, bglu_ref, wo_ref,
         wg_ref, wu_ref, wd_ref, out_ref, a_ref, acc_ref) = refs
        yg = yg_ref[...]
        s5 = yg * jax.nn.sigmoid(_dot(yg.astype(BF16), wglu_ref[...]) + bglu_ref[...])
        mix = _dot(s5.astype(BF16), wo_ref[:s5_width, :]) + _dot(o_ref[...], wo_ref[s5_width:, :])
    else:
        (h_ref, g_ref, sh_ref, sc_ref, g2_ref, gn_ref, o_ref, wo_ref,
         wg_ref, wu_ref, wd_ref, out_ref, a_ref, acc_ref) = refs
        mix = _dot(o_ref[...], wo_ref[...])
    h1 = h_ref[...] + g_ref[...] * mix
    a_ref[...] = (_rms(h1, gn_ref[...]) * (1.0 + sc_ref[...]) + sh_ref[...]).astype(BF16)
    acc_ref[...] = jnp.zeros_like(acc_ref)

    def body(c, _):
        a = a_ref[...]
        cols = pl.ds(pl.multiple_of(c * MXU_TILE, MXU_TILE), MXU_TILE)
        act = _silu(_dot(a, wg_ref[:, cols])) * _dot(a, wu_ref[:, cols])
        acc_ref[...] += _dot(act.astype(BF16), wd_ref[cols, :])
        return 0

    lax.fori_loop(0, n_chunks, body, 0, unroll=True)
    out_ref[...] = h1 + g2_ref[...] * acc_ref[...]


def _post(h, mods, ctx_row, tm, wts, ffn, layer, o, yg=None):
    b, t, d = h.shape
    tile = lambda w: pl.BlockSpec((None, tm, w), lambda bi, i: (bi, i, 0))
    s5_width = 0 if yg is None else yg.shape[2]
    in_specs = [tile(d)] + _mod_specs(d, (2, 3, 4, 5), ctx_row) + [_const_spec(wts["gn"].shape)]
    args = [h, mods, mods, mods, mods, wts["gn"]]
    if yg is not None:
        in_specs += [tile(s5_width), tile(o.shape[2]), _const_spec(wts["wglu"].shape),
                     _const_spec(wts["bglu"].shape)]
        args += [yg, o, wts["wglu"], wts["bglu"]]
    else:
        in_specs += [tile(o.shape[2])]
        args += [o]
    in_specs.append(_const_spec(wts["wo"].shape))
    args.append(wts["wo"])
    for k in ("wg", "wu", "wd"):
        in_specs.append(_const_spec(ffn[k].shape, layer))
        args.append(ffn[k])
    return pl.pallas_call(
        functools.partial(_post_kernel, s5_width=s5_width, n_chunks=ffn["wg"].shape[2] // MXU_TILE),
        grid=(b, t // tm),
        in_specs=in_specs,
        out_specs=tile(d),
        out_shape=jax.ShapeDtypeStruct((b, t, d), F32),
        scratch_shapes=[pltpu.VMEM((tm, d), BF16), pltpu.VMEM((tm, d), F32)],
        compiler_params=_params("parallel", "parallel"),
        name="post_ffn",
    )(*args)


def _prep_post(norm_ffn, w_out, w_glu=None, b_glu=None):
    wts = dict(gn=norm_ffn[None, :], wo=w_out.astype(BF16))
    if w_glu is not None:
        wts.update(wglu=w_glu.astype(BF16), bglu=b_glu[None, :])
    return wts


def kernel(x, c, ctx, c_ctx, ada_w, ada_b, norm_mix, norm_ffn, ffn_w_gate, ffn_w_up, ffn_w_down,
           a_w_in, a_w_out, s5_lam_re, s5_lam_im, s5_log_step, s5_b_re, s5_b_im, s5_c_re, s5_c_im,
           s5_d, s5_w_glu, s5_b_glu, mla_qa_norm, mla_w_q_b, mla_kva_norm, mla_w_kv_b,
           mla_q_norm, mla_k_norm, c_w_in, c_w_out, c_q_norm, c_k_norm, c_sink):
    b, n, d = x.shape
    n_c = ctx.shape[1]
    depth = ada_w.shape[0]
    assert b + 1 <= MOD_ROWS and n % TM_PROJ_C == 0 and n_c % (S5_CHUNK * SUBLANES) == 0
    rows = n // GRID_W
    tm_ctx = n_c

    cond = jnp.zeros((MOD_ROWS, d), F32).at[:b].set(c).at[b].set(c_ctx)
    mods = _ada_modulation(cond, ada_w, ada_b)
    mods = mods.reshape(depth, MOD_ROWS, N_MOD, 1, d)

    cos_a, sin_a = _grid_rope_tables(rows, MLA_ROPE)
    cs_a_lat = np.concatenate([cos_a, sin_a], axis=1)
    cs_a_ctx = np.concatenate([np.ones((n_c, MLA_ROPE), np.float32), np.zeros((n_c, MLA_ROPE), np.float32)], axis=1)
    cos_c, sin_c = _grid_rope_tables(rows, WIN_HEAD_DIM)
    cos_c2, sin_c2 = np.tile(cos_c, (1, 2)), np.tile(sin_c, (1, 2))
    one_c, zero_c = np.ones((n_c, LANES), np.float32), np.zeros((n_c, LANES), np.float32)
    assert ffn_w_gate.shape[2] % MXU_TILE == 0
    ffn = dict(wg=ffn_w_gate.astype(BF16), wu=ffn_w_up.astype(BF16), wd=ffn_w_down.astype(BF16))

    h_ctx, h_lat = ctx, x
    for i in range(depth):
        need_ctx = i < depth - 1
        j = i // 2
        m_i = mods[i]
        if i % 2 == 0:
            pw = _prep_proj_a(norm_mix[i], a_w_in[j], mla_qa_norm[j], mla_w_q_b[j], mla_kva_norm[j],
                              mla_w_kv_b[j], mla_q_norm[j], mla_k_norm[j])
            u_l, q_l, k_l, vt_l = _proj_a(h_lat, m_i, None, TM_PROJ_A, pw, cs_a_lat)
            u_c, q_c, k_c, vt_c = _proj_a(h_ctx, m_i, b, tm_ctx, pw, cs_a_ctx)
            tables = _s5_tables(s5_lam_re[j], s5_lam_im[j], s5_log_step[j], s5_b_re[j], s5_b_im[j],
                                s5_c_re[j], s5_c_im[j], s5_d[j])
            yg_c, yg_l = _s5(u_c, u_l, tables)
            o_l = _mla_attention(q_l, k_c, vt_c, k_l, vt_l, tq=n)
            post_w = _prep_post(norm_ffn[i], a_w_out[j], s5_w_glu[j], s5_b_glu[j])
            h_lat_new = _post(h_lat, m_i, None, TM_FFN, post_w, ffn, i, o_l, yg_l)
            if need_ctx:
                o_c = _mla_attention(q_c, k_c, vt_c, tq=n_c)
                h_ctx = _post(h_ctx, m_i, b, tm_ctx, post_w, ffn, i, o_c, yg_c)
            h_lat = h_lat_new
        else:
            pw = _prep_proj_c(norm_mix[i], c_w_in[j], c_q_norm[j], c_k_norm[j])
            q_l, k_l, vt_l = _proj_c(h_lat, m_i, None, TM_PROJ_C, pw, cos_c2, sin_c2)
            q_c, k_c, vt_c = _proj_c(h_ctx, m_i, b, tm_ctx, pw, one_c, zero_c)
            o_l = _win_attention(q_l, k_l, vt_l, k_c, vt_c, c_sink[j], pw["bound"], tq=TQ_WIN)
            post_w = _prep_post(norm_ffn[i], c_w_out[j])
            h_lat_new = _post(h_lat, m_i, None, TM_FFN, post_w, ffn, i, o_l)
            if need_ctx:
                raise NotImplementedError("context queries of a windowed layer")
            h_lat = h_lat_new
    return h_lat
```

```python
import functools
import math

import jax
import jax.numpy as jnp
import numpy as np
from jax import lax
from jax.experimental import pallas as pl
from jax.experimental.pallas import tpu as pltpu

F32 = jnp.float32
BF16 = jnp.bfloat16

GRID_W = 64
NORM_EPS = 1e-6
ROPE_THETA = 10000.0
N_MOD = 6
S5_GROUP_DIM = 16
S5_STATE = 64
S5_CHUNK = 16
MLA_HEADS = 4
MLA_NOPE = 128
MLA_ROPE = 64
MLA_QK_DIM = MLA_NOPE + MLA_ROPE
MLA_V = 128
MLA_Q_RANK = 384
MLA_KV_RANK = 256
MLA_SCALE = MLA_QK_DIM ** -0.5
MLA_HEAD_PAD = 256
WIN_HEADS = 16
WIN_KV_HEADS = 4
WIN_GROUP = WIN_HEADS // WIN_KV_HEADS
WIN_HEAD_DIM = 64
WINDOW = 128
WIN_SCALE = WIN_HEAD_DIM ** -0.5
LANES = 128
SUBLANES = 8
MXU_TILE = 256
VMEM_BYTES_V7X = 64 * 1024 * 1024
VMEM_LIMIT_BYTES = (VMEM_BYTES_V7X * 3) // 4
NEG_BIG = -1e30
LOG2_E = math.log2(math.e)
TM_FFN = 512
TM_PROJ_A = 512
TM_PROJ_C = 1024
TQ_WIN = 256
TK_MLA = 1024
TN_ADA = 1024
PROJ_ROWS = 256
SOFTMAX_DEN_MIN = 2.0 ** -60
SOFTMAX_DEN_MAX = 2.0 ** 60
S5_BLOCK_PAIRS = 4
MLA_Q_BLOCK = 256
MLA_K_PIECE = 256
MOD_ROWS = 8


def _dot(a, b):
    return jnp.dot(a, b, preferred_element_type=F32)


def _dot_nt(a, b):
    return lax.dot_general(a, b, (((1,), (1,)), ((), ())), preferred_element_type=F32)


def _split_bf16(x):
    hi = x.astype(BF16)
    lo = (x - hi.astype(F32)).astype(BF16)
    return hi, lo


def _rms(x, gain):
    return x * lax.rsqrt(jnp.mean(x * x, axis=-1, keepdims=True) + NORM_EPS) * gain


def _silu(x):
    return x * jax.nn.sigmoid(x)


def _params(*sem):
    return pltpu.CompilerParams(dimension_semantics=sem, vmem_limit_bytes=VMEM_LIMIT_BYTES)


def _const_spec(shape, layer=None):
    if layer is None:
        nd = len(shape)
        return pl.BlockSpec(shape, lambda *_: (0,) * nd, pipeline_mode=pl.Buffered(1))
    nd = len(shape) - 1
    return pl.BlockSpec((None,) + tuple(shape[1:]), lambda *_: (layer,) + (0,) * nd,
                        pipeline_mode=pl.Buffered(1))


def _ada_kernel(cond_ref, w_ref, b_ref, o_ref):
    s = _silu(cond_ref[...])
    s_hi, s_lo = _split_bf16(s)
    w_hi, w_lo = _split_bf16(w_ref[...])
    o_ref[...] = _dot(s_hi, w_hi) + _dot(s_lo, w_hi) + _dot(s_hi, w_lo) + b_ref[...]


def _ada_modulation(cond, ada_w, ada_b):
    depth, d, n = ada_w.shape
    tn = TN_ADA
    return pl.pallas_call(
        _ada_kernel,
        grid=(depth, n // tn),
        in_specs=[pl.BlockSpec((MOD_ROWS, d), lambda i, j: (0, 0)),
                  pl.BlockSpec((None, d, tn), lambda i, j: (i, 0, j)),
                  pl.BlockSpec((None, 1, tn), lambda i, j: (i, 0, j))],
        out_specs=pl.BlockSpec((None, MOD_ROWS, tn), lambda i, j: (i, 0, j)),
        out_shape=jax.ShapeDtypeStruct((depth, MOD_ROWS, n), F32),
        compiler_params=_params("arbitrary", "arbitrary"),
        name="ada_modulation",
    )(cond, ada_w, ada_b.reshape(depth, 1, n))


def _mod_specs(d, slots, ctx_row):
    def make(slot):
        if ctx_row is None:
            return pl.BlockSpec((None, None, 1, d), lambda b, i: (b, slot, 0, 0))
        return pl.BlockSpec((None, None, 1, d), lambda b, i: (ctx_row, slot, 0, 0))
    return [make(s) for s in slots]


def _grid_rope_tables(rows, rot_dim):
    n_freq = rot_dim // 4
    inv_freq = np.power(np.float32(ROPE_THETA), -np.arange(n_freq, dtype=np.float32) / np.float32(n_freq))
    ang_r = np.arange(rows, dtype=np.float32)[:, None] * inv_freq.astype(np.float32)
    ang_c = np.arange(GRID_W, dtype=np.float32)[:, None] * inv_freq.astype(np.float32)

    def expand(r, c):
        r = np.broadcast_to(r[:, None, :], (rows, GRID_W, n_freq))
        c = np.broadcast_to(c[None, :, :], (rows, GRID_W, n_freq))
        return np.concatenate([r, r, c, c], axis=-1).reshape(rows * GRID_W, rot_dim)

    return expand(np.cos(ang_r), np.cos(ang_c)), expand(np.sin(ang_r), np.sin(ang_c))


def _rot_perm_sign(rot_dim):
    q = rot_dim // 4
    idx = np.arange(rot_dim)
    perm = np.where((idx // q) % 2 == 0, idx + q, idx - q)
    sign = np.where((idx // q) % 2 == 0, -1.0, 1.0).astype(np.float32)
    return perm, sign


def _proj_a_kernel(h_ref, sh_ref, sc_ref, gn_ref, w1_ref, gqa_ref, wq_ref, gkv_ref, wkv_ref, wvt_ref,
                   gq_ref, gk_ref, cs_ref, msk_ref, u_ref, q_ref, k_ref, vt_ref):
    tm = h_ref.shape[0]
    rb = min(PROJ_ROWS, tm)
    msk = msk_ref[...]
    low_half = lax.broadcasted_iota(jnp.int32, (rb, LANES), 1) < MLA_ROPE

    def finish(xh, gain, cs, out_ref, rows, h):
        ssq = _dot((xh * xh).astype(BF16), msk)
        xn = xh * lax.rsqrt(ssq * (1.0 / MLA_QK_DIM) + NORM_EPS) * gain
        rr = xn[:, LANES:] * cs
        rot = rr + pltpu.roll(rr, MLA_ROPE, axis=1)
        base = h * MLA_HEAD_PAD
        out_ref[rows, base:base + LANES] = xn[:, :LANES].astype(BF16)
        out_ref[rows, base + LANES:base + 2 * LANES] = jnp.where(low_half, rot, 0.0).astype(BF16)

    for r in range(tm // rb):
        rows = slice(r * rb, (r + 1) * rb)
        a = _rms(h_ref[rows, :], gn_ref[...]) * (1.0 + sc_ref[...]) + sh_ref[...]
        p1 = _dot(a.astype(BF16), w1_ref[...])
        s5w = u_ref.shape[1]
        u_ref[rows, :] = p1[:, :s5w]
        cq = p1[:, s5w:s5w + MLA_Q_RANK]
        ckv = p1[:, s5w + MLA_Q_RANK:s5w + MLA_Q_RANK + MLA_KV_RANK]
        krr = p1[:, s5w + MLA_Q_RANK + MLA_KV_RANK:]
        qb = _dot(_rms(cq, gqa_ref[...]).astype(BF16), wq_ref[...])
        ckv_n = _rms(ckv, gkv_ref[...]).astype(BF16)
        kv = _dot(ckv_n, wkv_ref[...])
        vt_ref[:, rows] = _dot_nt(wvt_ref[...], ckv_n).astype(BF16)
        cs = cs_ref[rows, :]
        for h in range(MLA_HEADS):
            base = h * MLA_HEAD_PAD
            finish(qb[:, base:base + MLA_HEAD_PAD], gq_ref[:, base:base + MLA_HEAD_PAD], cs, q_ref, rows, h)
            kh = jnp.concatenate([kv[:, h * MLA_NOPE:(h + 1) * MLA_NOPE], krr], axis=1)
            finish(kh, gk_ref[:, base:base + MLA_HEAD_PAD], cs, k_ref, rows, h)


def _proj_a(h, mods, ctx_row, tm, wts, cs):
    b, t, d = h.shape
    qw = MLA_HEADS * MLA_HEAD_PAD
    grid = (b, t // tm)
    tile = lambda w: pl.BlockSpec((None, tm, w), lambda bi, i: (bi, i, 0))
    in_specs = ([tile(d)] + _mod_specs(d, (0, 1), ctx_row)
                + [_const_spec(wts[k].shape) for k in
                   ("gn", "w1", "gqa", "wq", "gkv", "wkv", "wvt", "gq", "gk")]
                + [pl.BlockSpec((tm, LANES), lambda bi, i: (i, 0)), _const_spec(wts["msk"].shape)])
    vw = MLA_HEADS * MLA_V
    s5w = wts["w1"].shape[1] - (MLA_Q_RANK + MLA_KV_RANK + 2 * MLA_ROPE)
    return pl.pallas_call(
        _proj_a_kernel,
        grid=grid,
        in_specs=in_specs,
        out_specs=[tile(s5w), tile(qw), tile(qw), pl.BlockSpec((None, vw, tm), lambda bi, i: (bi, 0, i))],
        out_shape=[jax.ShapeDtypeStruct((b, t, s5w), F32),
                   jax.ShapeDtypeStruct((b, t, qw), BF16),
                   jax.ShapeDtypeStruct((b, t, qw), BF16),
                   jax.ShapeDtypeStruct((b, vw, t), BF16)],
        compiler_params=_params("parallel", "parallel"),
        name="proj_a",
    )(h, mods, mods, wts["gn"], wts["w1"], wts["gqa"], wts["wq"], wts["gkv"], wts["wkv"], wts["wvt"],
      wts["gq"], wts["gk"], cs, wts["msk"])


def _prep_proj_a(norm_mix, a_w_in, qa_norm, w_q_b, kva_norm, w_kv_b, q_norm, k_norm):
    perm, sign = _rot_perm_sign(MLA_ROPE)
    s5w = a_w_in.shape[1] - (MLA_Q_RANK + MLA_KV_RANK + MLA_ROPE)
    assert s5w % LANES == 0 and MLA_Q_RANK % LANES == 0 and MLA_KV_RANK % LANES == 0
    kr = a_w_in[:, -MLA_ROPE:]
    w1 = jnp.concatenate([a_w_in, kr[:, perm] * sign], axis=1).astype(BF16)
    wq = w_q_b.reshape(MLA_Q_RANK, MLA_HEADS, MLA_QK_DIM)
    rope = wq[:, :, MLA_NOPE:]
    wq = jnp.concatenate([wq, rope[:, :, perm] * sign], axis=2)
    wq = wq.reshape(MLA_Q_RANK, MLA_HEADS * MLA_HEAD_PAD).astype(BF16)
    wkv3 = w_kv_b.reshape(MLA_KV_RANK, MLA_HEADS, MLA_NOPE + MLA_V)
    wkv = wkv3[:, :, :MLA_NOPE].reshape(MLA_KV_RANK, -1).astype(BF16)
    wvt = wkv3[:, :, MLA_NOPE:].reshape(MLA_KV_RANK, -1).T.astype(BF16)

    def head_gain(g, scale):
        gb = jnp.concatenate([g, g[MLA_NOPE:][perm]]) * scale
        return jnp.tile(gb, MLA_HEADS)[None, :]

    rows = np.arange(MLA_HEAD_PAD)[:, None] < MLA_QK_DIM
    msk = jnp.asarray(np.broadcast_to(rows, (MLA_HEAD_PAD, MLA_HEAD_PAD)), BF16)
    return dict(gn=norm_mix[None, :], w1=w1, gqa=qa_norm[None, :], wq=wq, gkv=kva_norm[None, :],
                wkv=wkv, wvt=wvt, gq=head_gain(q_norm, MLA_SCALE * LOG2_E), gk=head_gain(k_norm, 1.0),
                msk=msk)


def _s5_kernel(uc_ref, ul_ref, sel_in_ref, sel_out_ref, toep_ref, bst_ref, cst_ref, a_ref, d_ref,
               yc_ref, yl_ref, ub_ref, x_ref, z_ref, sin_ref, yb_ref, *, n_ctx, n_all):
    L, S = S5_CHUNK, S5_GROUP_DIM
    n_lat = n_all - n_ctx
    npair = S5_BLOCK_PAIRS
    half = SUBLANES * LANES
    for tl in range(L):
        u = jnp.concatenate([uc_ref[pl.ds(tl, n_ctx, stride=L), :], ul_ref[pl.ds(tl, n_lat, stride=L), :]], axis=0)
        ub_ref[:, tl * LANES:(tl + 1) * LANES] = u.astype(BF16)
    t_ctx, t_all = n_ctx // SUBLANES, n_all // SUBLANES
    row = lax.broadcasted_iota(jnp.int32, (SUBLANES, LANES), 0)

    def cmul(a_re, a_im, b_re, b_im):
        return a_re * b_re - a_im * b_im, a_re * b_im + a_im * b_re

    def tile_scan(z_re, z_im, c_re, c_im, a_re, a_im, fwd):
        for sft in (1, 2, 4):
            k = sft - 1 if fwd else SUBLANES - sft
            p_re, p_im = a_re[k:k + 1, :], a_im[k:k + 1, :]
            amt = sft if fwd else SUBLANES - sft
            keep = (row >= sft) if fwd else (row < SUBLANES - sft)
            s_re = jnp.where(keep, pltpu.roll(z_re, amt, axis=0), 0.0)
            s_im = jnp.where(keep, pltpu.roll(z_im, amt, axis=0), 0.0)
            m_re, m_im = cmul(p_re, p_im, s_re, s_im)
            z_re, z_im = z_re + m_re, z_im + m_im
        m_re, m_im = cmul(a_re, a_im, c_re, c_im)
        s_re, s_im = z_re + m_re, z_im + m_im
        edge = (row == 0) if fwd else (row == SUBLANES - 1)
        amt = 1 if fwd else SUBLANES - 1
        in_re = jnp.where(edge, c_re, pltpu.roll(s_re, amt, axis=0))
        in_im = jnp.where(edge, c_im, pltpu.roll(s_im, amt, axis=0))
        last = SUBLANES - 1 if fwd else 0
        return in_re, in_im, s_re[last:last + 1, :], s_im[last:last + 1, :]

    def relayout_items(pairs):
        items = []
        for pp in pairs:
            for hh in range(2):
                def sel(pp=pp, hh=hh):
                    xs = _dot(ub_ref[:, hh * half:(hh + 1) * half], sel_in_ref[pp]).astype(BF16)
                    x_ref[pp, :, hh * LANES:(hh + 1) * LANES] = xs[:, :LANES]
                    x_ref[pp, :, MXU_TILE + hh * LANES:MXU_TILE + (hh + 1) * LANES] = xs[:, LANES:]
                items.append(sel)

            def drive(pp=pp):
                z_ref[pp] = _dot(x_ref[pp], bst_ref[pp])
            items.append(drive)
        return items

    def readout_items(pairs):
        items = []
        for pp in pairs:
            def readout(pp=pp):
                x = x_ref[pp]
                y = jnp.concatenate([_dot(x[:, :MXU_TILE], toep_ref[pp, 0]), _dot(x[:, MXU_TILE:], toep_ref[pp, 1])],
                                    axis=1)
                y = y + _dot(sin_ref[pp].astype(BF16), cst_ref[pp]) + x.astype(F32) * d_ref[pp]
                yg = jax.nn.gelu(y).astype(BF16)
                for gl in range(2):
                    for hh in range(2):
                        g8 = 2 * pp + gl
                        col = gl * MXU_TILE + hh * LANES
                        yb_ref[hh, :, g8 * LANES:(g8 + 1) * LANES] = yg[:, col:col + LANES]
            items.append(readout)
        return items

    def scan_with(pairs, items):
        zero = jnp.zeros((1, LANES), F32)
        carry = {pp: (zero, zero, zero, zero) for pp in pairs}
        every = max(1, t_all // max(1, len(items)))
        pending = list(items)
        for it in range(t_all):
            jt = t_ctx - 1 - it if it < t_ctx else t_all + t_ctx - 1 - it
            rf, rb = it * SUBLANES, jt * SUBLANES
            for pp in pairs:
                cf_re, cf_im, cb_re, cb_im = carry[pp]
                zf = z_ref[pp, rf:rf + SUBLANES, 0:2 * LANES]
                zb = z_ref[pp, rb:rb + SUBLANES, 2 * LANES:4 * LANES]
                f_re, f_im, cf_re, cf_im = tile_scan(zf[:, :LANES], zf[:, LANES:], cf_re, cf_im,
                                                     a_ref[pp, 0], a_ref[pp, 1], True)
                b_re, b_im, cb_re, cb_im = tile_scan(zb[:, :LANES], zb[:, LANES:], cb_re, cb_im,
                                                     a_ref[pp, 2], a_ref[pp, 3], False)
                sin_ref[pp, rf:rf + SUBLANES, 0:2 * LANES] = jnp.concatenate([f_re, f_im], axis=1)
                sin_ref[pp, rb:rb + SUBLANES, 2 * LANES:4 * LANES] = jnp.concatenate([b_re, b_im], axis=1)
                carry[pp] = (cf_re, cf_im, cb_re, cb_im)
            if pending and (it + 1) % every == 0:
                pending.pop(0)()
        for item in pending:
            item()

    first, second = tuple(range(npair // 2)), tuple(range(npair // 2, npair))
    for item in relayout_items(first):
        item()
    scan_with(first, relayout_items(second))
    scan_with(second, readout_items(first))
    for item in readout_items(second):
        item()
    for hh in range(2):
        for kk in range(SUBLANES // 2):
            two = _dot(yb_ref[hh], sel_out_ref[kk])
            for e in range(2):
                tl = hh * SUBLANES + 2 * kk + e
                yc_ref[pl.ds(tl, n_ctx, stride=L), :] = two[:n_ctx, e * LANES:(e + 1) * LANES]
                yl_ref[pl.ds(tl, n_lat, stride=L), :] = two[n_ctx:, e * LANES:(e + 1) * LANES]


def _s5_selectors():
    S = S5_GROUP_DIM
    r = np.arange(SUBLANES * LANES)[:, None]
    c = np.arange(2 * LANES)[None, :]
    k, l = r // LANES, r % LANES
    pp = np.arange(S5_BLOCK_PAIRS)[:, None, None]
    sel_in = (k == (c % LANES) // S) & (l == 2 * S * pp + S * (c // LANES) + c % S)
    kk = np.arange(SUBLANES // 2)[:, None, None]
    sel_out = (k == (c % LANES) // S) & (l == S * (2 * kk + c // LANES) + c % S)
    return jnp.asarray(sel_in, BF16), jnp.asarray(sel_out, BF16)


def _dot_nt_f32(a, b):
    a_hi, a_lo = _split_bf16(a)
    b_hi, b_lo = _split_bf16(b)
    return _dot_nt(a_hi, b_hi) + _dot_nt(a_lo, b_hi) + _dot_nt(a_hi, b_lo)


def _s5_table_kernel(par_ref, bre_ref, bim_ref, cre_ref, cim_ref, toep_ref, bst_ref, cst_ref, a_ref):
    L, S, P = S5_CHUNK, S5_GROUP_DIM, S5_STATE
    kk = lax.broadcasted_iota(jnp.int32, (3 * SUBLANES, LANES), 0).astype(F32)
    lane = lax.broadcasted_iota(jnp.int32, (L, LANES), 1)
    pair_rows = lax.broadcasted_iota(jnp.int32, (2 * L * S, LANES), 0)
    pair_lanes = lax.broadcasted_iota(jnp.int32, (2 * L * S, LANES), 1)
    own_group = (pair_rows // (L * S)) == (pair_lanes // P)
    lane_pad = jnp.zeros((2 * P, LANES - S), F32)

    def rows_of_powers(pw, ks, groups):
        one = jnp.concatenate([jnp.broadcast_to(pw[k:k + 1, :], (S, LANES)) for k in ks], axis=0)
        return jnp.concatenate([one] * groups, axis=0) if groups > 1 else one

    def cmul(a_re, a_im, b_re, b_im):
        return a_re * b_re - a_im * b_im, a_re * b_im + a_im * b_re

    lag_tables = []
    for d in range(2):
        lam_re, lam_im = par_ref[d, 0:1, :], par_ref[d, 1:2, :]
        step = jnp.exp(par_ref[d, 2:3, :])
        ar, ai = lam_re * step, lam_im * step
        mag = jnp.exp(kk * ar)
        pw_re, pw_im = mag * jnp.cos(kk * ai), mag * jnp.sin(kk * ai)
        th = jnp.tanh(0.5 * ar)
        em1 = 2.0 * th / (1.0 - th)
        sh = jnp.sin(0.5 * ai)
        n_re = em1 * jnp.cos(ai) - 2.0 * sh * sh
        n_im = (em1 + 1.0) * jnp.sin(ai)
        den = lam_re * lam_re + lam_im * lam_im
        co_re = (n_re * lam_re + n_im * lam_im) / den
        co_im = (n_im * lam_re - n_re * lam_im) / den
        bt_re = jnp.concatenate([bre_ref[d], lane_pad], axis=1).T[:S]
        bt_im = jnp.concatenate([bim_ref[d], lane_pad], axis=1).T[:S]
        bb_re, bb_im = cmul(co_re, co_im, bt_re, bt_im)
        cc_re = jnp.concatenate([cre_ref[d, 0], cre_ref[d, 1]], axis=1)
        cc_im = jnp.concatenate([cim_ref[d, 0], cim_ref[d, 1]], axis=1)
        ks = [L - 1 - t for t in range(L)] if d == 0 else list(range(L))
        r_re, r_im = rows_of_powers(pw_re, ks, 2), rows_of_powers(pw_im, ks, 2)
        bbt_re, bbt_im = jnp.concatenate([bb_re] * (2 * L), axis=0), jnp.concatenate([bb_im] * (2 * L), axis=0)
        v_re, v_im = cmul(r_re, r_im, bbt_re, bbt_im)
        bst_ref[:, (2 * d) * LANES:(2 * d + 1) * LANES] = jnp.where(own_group, v_re, 0.0).astype(BF16)
        bst_ref[:, (2 * d + 1) * LANES:(2 * d + 2) * LANES] = jnp.where(own_group, v_im, 0.0).astype(BF16)
        ks = [t + 1 for t in range(L)] if d == 0 else [L - t for t in range(L)]
        r_re, r_im = rows_of_powers(pw_re, ks, 2), rows_of_powers(pw_im, ks, 2)
        cct_re, cct_im = jnp.concatenate([cc_re] * (2 * L), axis=0), jnp.concatenate([cc_im] * (2 * L), axis=0)
        v_re, v_im = cmul(cct_re, cct_im, r_re, r_im)
        cst_ref[(2 * d) * LANES:(2 * d + 1) * LANES, :] = jnp.where(own_group, v_re, 0.0).T.astype(BF16)
        cst_ref[(2 * d + 1) * LANES:(2 * d + 2) * LANES, :] = jnp.where(own_group, -v_im, 0.0).T.astype(BF16)
        ks = list(range(L)) if d == 0 else [L - 1 - j for j in range(L)]
        r_re, r_im = rows_of_powers(pw_re, ks, 1), rows_of_powers(pw_im, ks, 1)
        cl_re, cl_im = cmul(jnp.concatenate([cc_re] * L, axis=0), jnp.concatenate([cc_im] * L, axis=0), r_re, r_im)
        per_group = []
        for g in range(2):
            mine = (lane // P) == g
            per_group.append(_dot_nt_f32(jnp.where(mine, bb_re, 0.0), cl_re)
                             - _dot_nt_f32(jnp.where(mine, bb_im, 0.0), cl_im))
        lag_tables.append(per_group)
        row8 = lax.broadcasted_iota(jnp.int32, (SUBLANES, LANES), 0)
        n_chunks = ((row8 + 1) if d == 0 else (SUBLANES - row8)).astype(F32) * float(L)
        mag8 = jnp.exp(n_chunks * ar)
        a_ref[2 * d] = mag8 * jnp.cos(n_chunks * ai)
        a_ref[2 * d + 1] = mag8 * jnp.sin(n_chunks * ai)

    def shift_right(x, s):
        x0, x1 = x[:, :LANES], x[:, LANES:]
        a, r = divmod(s, LANES)
        r0 = pltpu.roll(x0, r, axis=1) if r else x0
        r1 = pltpu.roll(x1, r, axis=1) if r else x1
        if a == 0:
            return jnp.concatenate([jnp.where(lane >= r, r0, 0.0), jnp.where(lane >= r, r1, r0)], axis=1)
        return jnp.concatenate([jnp.zeros_like(x0), jnp.where(lane >= r, r0, 0.0)], axis=1)

    def shift_left(x, s):
        x0, x1 = x[:, :LANES], x[:, LANES:]
        a, r = divmod(s, LANES)
        r0 = pltpu.roll(x0, LANES - r, axis=1) if r else x0
        r1 = pltpu.roll(x1, LANES - r, axis=1) if r else x1
        if a == 0:
            return jnp.concatenate([jnp.where(lane < LANES - r, r0, r1), jnp.where(lane < LANES - r, r1, 0.0)], axis=1)
        return jnp.concatenate([jnp.where(lane < LANES - r, r1, 0.0), jnp.zeros_like(x0)], axis=1)

    for g in range(2):
        kf, kb = lag_tables[0][g], lag_tables[1][g]
        for tau in range(L):
            blk = shift_right(kf, S * tau) + shift_left(kb, S * (L - 1 - tau))
            toep_ref[g, tau * S:(tau + 1) * S, :] = blk.astype(BF16)


def _s5_tables(lam_re, lam_im, log_step, b_re, b_im, c_re, c_im, d_skip):
    _, G, P = lam_re.shape
    S, L = S5_GROUP_DIM, S5_CHUNK
    assert P == S5_STATE and 2 * P == LANES and 2 * L * S == 2 * MXU_TILE
    pairs = G // 2
    par = jnp.stack([lam_re.reshape(2, pairs, 2 * P), lam_im.reshape(2, pairs, 2 * P),
                     jnp.repeat(log_step, P, axis=-1).reshape(2, pairs, 2 * P)], axis=2)
    par = jnp.transpose(par, (1, 0, 2, 3)).astype(F32)
    bshape = (2, pairs, 2 * P, S)
    cshape = (2, pairs, 2, S, P)
    pw = 2 * L * S
    bspec = pl.BlockSpec((2, None, 2 * P, S), lambda g: (0, g, 0, 0))
    cspec = pl.BlockSpec((2, None, 2, S, P), lambda g: (0, g, 0, 0, 0))
    toep, bst, cst, a_chunk = pl.pallas_call(
        _s5_table_kernel,
        grid=(pairs,),
        in_specs=[pl.BlockSpec((None, 2, 3, 2 * P), lambda g: (g, 0, 0, 0)), bspec, bspec, cspec, cspec],
        out_specs=[pl.BlockSpec((None, 2, MXU_TILE, MXU_TILE), lambda g: (g, 0, 0, 0)),
                   pl.BlockSpec((None, pw, pw), lambda g: (g, 0, 0)),
                   pl.BlockSpec((None, pw, pw), lambda g: (g, 0, 0)),
                   pl.BlockSpec((None, 4, SUBLANES, LANES), lambda g: (g, 0, 0, 0))],
        out_shape=[jax.ShapeDtypeStruct((pairs, 2, MXU_TILE, MXU_TILE), BF16),
                   jax.ShapeDtypeStruct((pairs, pw, pw), BF16),
                   jax.ShapeDtypeStruct((pairs, pw, pw), BF16),
                   jax.ShapeDtypeStruct((pairs, 4, SUBLANES, LANES), F32)],
        compiler_params=_params("parallel"),
        name="s5_tables",
    )(par, b_re.reshape(bshape), b_im.reshape(bshape), c_re.reshape(cshape), c_im.reshape(cshape))
    d_pair = jnp.broadcast_to(d_skip.astype(F32).reshape(pairs, 2, 1, S), (pairs, 2, L, S))
    return toep, bst, cst, a_chunk, d_pair.reshape(pairs, 1, pw)


def _s5(u_ctx, u_lat, tables):
    toep, bst, cst, a_pow, d_pair = tables
    b, n_c, w = u_ctx.shape
    n_l = u_lat.shape[1]
    L = S5_CHUNK
    n_ctx, n_all = n_c // L, (n_c + n_l) // L
    nblk = w // LANES
    pw = 2 * L * S5_GROUP_DIM
    npair = S5_BLOCK_PAIRS
    sel_in, sel_out = _s5_selectors()
    wspec = lambda shape: pl.BlockSpec((npair,) + shape, lambda g, bi: (g,) + (0,) * len(shape),
                                       pipeline_mode=pl.Buffered(1))
    return pl.pallas_call(
        functools.partial(_s5_kernel, n_ctx=n_ctx, n_all=n_all),
        grid=(nblk, b),
        in_specs=[pl.BlockSpec((None, n_c, LANES), lambda g, bi: (bi, 0, g)),
                  pl.BlockSpec((None, n_l, LANES), lambda g, bi: (bi, 0, g)),
                  _const_spec(sel_in.shape), _const_spec(sel_out.shape),
                  wspec((2, MXU_TILE, MXU_TILE)), wspec((pw, pw)), wspec((pw, pw)),
                  wspec((4, SUBLANES, LANES)), wspec((1, pw))],
        out_specs=[pl.BlockSpec((None, n_c, LANES), lambda g, bi: (bi, 0, g)),
                   pl.BlockSpec((None, n_l, LANES), lambda g, bi: (bi, 0, g))],
        out_shape=[jax.ShapeDtypeStruct((b, n_c, w), F32), jax.ShapeDtypeStruct((b, n_l, w), F32)],
        scratch_shapes=[pltpu.VMEM((n_all, L * LANES), BF16), pltpu.VMEM((npair, n_all, pw), BF16),
                        pltpu.VMEM((npair, n_all, pw), F32), pltpu.VMEM((npair, n_all, pw), F32),
                        pltpu.VMEM((2, n_all, SUBLANES * LANES), BF16)],
        compiler_params=_params("parallel", "arbitrary"),
        name="s5_scan",
    )(u_ctx, u_lat, sel_in, sel_out, toep, bst, cst, a_pow, d_pair)


def _mla_kernel(*refs, tk, n_steps):
    if n_steps:
        q_ref, kc_ref, vc_ref, k_ref, v_ref, o_ref, qt_ref, s_ref, m_ref, l_ref, acc_ref = refs
    else:
        q_ref, kc_ref, vc_ref, o_ref, qt_ref, s_ref, m_ref, l_ref, acc_ref = refs
    tq = q_ref.shape[0]
    ncb = tq // MLA_Q_BLOCK
    kp = MLA_K_PIECE
    qt_ref[...] = q_ref[...].astype(F32).T.astype(BF16)

    def chunk(load_k, load_vt, nkeys, first):
        nr = nkeys // kp

        def score_piece(c, r):
            st = _dot(load_k(r), qt_ref[:, c * MLA_Q_BLOCK:(c + 1) * MLA_Q_BLOCK])
            s_ref[c % 2, r * kp:(r + 1) * kp, :] = st
            return jnp.max(st, axis=0, keepdims=True)

        def block_stats(c, mx):
            if first:
                return mx, None
            m_old = m_ref[:, c * MLA_Q_BLOCK:(c + 1) * MLA_Q_BLOCK]
            m_new = jnp.maximum(m_old, mx)
            return m_new, jnp.exp2(m_old - m_new)

        def prob_piece(c, r, m_new):
            p = jnp.exp2(s_ref[c % 2, r * kp:(r + 1) * kp, :] - m_new)
            return jnp.sum(p, axis=0, keepdims=True), _dot(load_vt(r), p.astype(BF16))

        def finish(c, m_new, alpha, lsum, pv):
            cols = slice(c * MLA_Q_BLOCK, (c + 1) * MLA_Q_BLOCK)
            if first:
                l_ref[:, cols] = lsum
                acc_ref[:, cols] = pv
            else:
                l_ref[:, cols] = alpha * l_ref[:, cols] + lsum
                acc_ref[:, cols] = alpha * acc_ref[:, cols] + pv
            m_ref[:, cols] = m_new

        mx = None
        for r in range(nr):
            pm = score_piece(0, r)
            mx = pm if mx is None else jnp.maximum(mx, pm)
        for c in range(ncb):
            m_new, alpha = block_stats(c, mx)
            mx = lsum = pv = None
            for r in range(nr):
                if c + 1 < ncb:
                    pm = score_piece(c + 1, r)
                    mx = pm if mx is None else jnp.maximum(mx, pm)
                ls, pvr = prob_piece(c, r, m_new)
                lsum = ls if lsum is None else lsum + ls
                pv = pvr if pv is None else pv + pvr
            finish(c, m_new, alpha, lsum, pv)

    chunk(lambda r: kc_ref[r * kp:(r + 1) * kp, :], lambda r: vc_ref[:, r * kp:(r + 1) * kp],
          kc_ref.shape[0], True)
    if n_steps:
        def body(j, _):
            off = pl.multiple_of(j * tk, tk)
            chunk(lambda r: k_ref[pl.ds(off + r * kp, kp), :], lambda r: v_ref[:, pl.ds(off + r * kp, kp)],
                  tk, False)
            return 0
        lax.fori_loop(0, n_steps, body, 0)
    o_ref[...] = (acc_ref[...] / l_ref[...]).T.astype(BF16)


def _mla_attention(q, k_ctx, vt_ctx, k_lat=None, vt_lat=None, *, tq, tk=TK_MLA):
    b, t, _ = q.shape
    n_c = k_ctx.shape[1]
    in_specs = [pl.BlockSpec((None, tq, MLA_HEAD_PAD), lambda bi, h, i: (bi, i, h)),
                pl.BlockSpec((None, n_c, MLA_HEAD_PAD), lambda bi, h, i: (bi, 0, h)),
                pl.BlockSpec((None, MLA_V, n_c), lambda bi, h, i: (bi, h, 0))]
    args = [q, k_ctx, vt_ctx]
    n_steps = 0
    if k_lat is not None:
        n_l = k_lat.shape[1]
        n_steps = n_l // tk
        in_specs += [pl.BlockSpec((None, n_l, MLA_HEAD_PAD), lambda bi, h, i: (bi, 0, h)),
                     pl.BlockSpec((None, MLA_V, n_l), lambda bi, h, i: (bi, h, 0))]
        args += [k_lat, vt_lat]
    return pl.pallas_call(
        functools.partial(_mla_kernel, tk=tk, n_steps=n_steps),
        grid=(b, MLA_HEADS, t // tq),
        in_specs=in_specs,
        out_specs=pl.BlockSpec((None, tq, MLA_V), lambda bi, h, i: (bi, i, h)),
        out_shape=jax.ShapeDtypeStruct((b, t, MLA_HEADS * MLA_V), BF16),
        scratch_shapes=[pltpu.VMEM((MLA_HEAD_PAD, tq), BF16), pltpu.VMEM((2, max(tk, n_c), MLA_Q_BLOCK), F32),
                        pltpu.VMEM((1, tq), F32), pltpu.VMEM((1, tq), F32), pltpu.VMEM((MLA_V, tq), F32)],
        compiler_params=_params("parallel", "parallel", "arbitrary"),
        name="mla_attention",
    )(*args)


def _proj_c_kernel(h_ref, sh_ref, sc_ref, gn_ref, wc_ref, wvt_ref, gqk_ref, cos_ref, sin_ref, bd_ref,
                   q_ref, k_ref, vt_ref):
    qw = WIN_HEADS * WIN_HEAD_DIM
    kw = WIN_KV_HEADS * WIN_HEAD_DIM
    bd = bd_ref[...]
    tm = h_ref.shape[0]
    rb = min(PROJ_ROWS, tm)
    lane = lax.broadcasted_iota(jnp.int32, (rb, LANES), 1)
    first_quarter = (lane % (WIN_HEAD_DIM // 2)) < (WIN_HEAD_DIM // 4)
    for r in range(tm // rb):
        rows = slice(r * rb, (r + 1) * rb)
        a = (_rms(h_ref[rows, :], gn_ref[...]) * (1.0 + sc_ref[...]) + sh_ref[...]).astype(BF16)
        p = _dot(a, wc_ref[...])
        vt_ref[:, rows] = _dot_nt(wvt_ref[...], a).astype(BF16)
        cos, sin = cos_ref[rows, :], sin_ref[rows, :]
        for j in range((qw + kw) // MXU_TILE):
            xh = p[:, j * MXU_TILE:(j + 1) * MXU_TILE]
            ssq = _dot((xh * xh).astype(BF16), bd)
            xn = xh * lax.rsqrt(ssq * (1.0 / WIN_HEAD_DIM) + NORM_EPS) * gqk_ref[:, j * MXU_TILE:(j + 1) * MXU_TILE]
            for c in range(MXU_TILE // LANES):
                xc = xn[:, c * LANES:(c + 1) * LANES]
                fwd = pltpu.roll(xc, WIN_HEAD_DIM // 4, axis=1)
                bwd = pltpu.roll(xc, LANES - WIN_HEAD_DIM // 4, axis=1)
                y = (xc * cos + jnp.where(first_quarter, -bwd, fwd) * sin).astype(BF16)
                col = j * MXU_TILE + c * LANES
                if col < qw:
                    q_ref[rows, col:col + LANES] = y
                else:
                    k_ref[rows, col - qw:col - qw + LANES] = y


def _proj_c(h, mods, ctx_row, tm, wts, cos2, sin2):
    b, t, d = h.shape
    qw = WIN_HEADS * WIN_HEAD_DIM
    kw = WIN_KV_HEADS * WIN_HEAD_DIM
    tile = lambda w: pl.BlockSpec((None, tm, w), lambda bi, i: (bi, i, 0))
    tab = pl.BlockSpec((tm, LANES), lambda bi, i: (i, 0))
    in_specs = ([tile(d)] + _mod_specs(d, (0, 1), ctx_row)
                + [_const_spec(wts[k].shape) for k in ("gn", "wc", "wvt", "gqk")]
                + [tab, tab, _const_spec(wts["bd"].shape)])
    return pl.pallas_call(
        _proj_c_kernel,
        grid=(b, t // tm),
        in_specs=in_specs,
        out_specs=[tile(qw), tile(kw), pl.BlockSpec((None, kw, tm), lambda bi, i: (bi, 0, i))],
        out_shape=[jax.ShapeDtypeStruct((b, t, qw), BF16),
                   jax.ShapeDtypeStruct((b, t, kw), BF16),
                   jax.ShapeDtypeStruct((b, kw, t), BF16)],
        compiler_params=_params("parallel", "parallel"),
        name="proj_c",
    )(h, mods, mods, wts["gn"], wts["wc"], wts["wvt"], wts["gqk"], cos2, sin2, wts["bd"])


def _prep_proj_c(norm_mix, c_w_in, q_norm, k_norm):
    gqk = jnp.concatenate([jnp.tile(q_norm * (WIN_SCALE * LOG2_E), WIN_HEADS), jnp.tile(k_norm, WIN_KV_HEADS)])
    idx = np.arange(MXU_TILE) // WIN_HEAD_DIM
    bd = jnp.asarray(idx[:, None] == idx[None, :], BF16)
    qk = (WIN_HEADS + WIN_KV_HEADS) * WIN_HEAD_DIM
    bound = 1.01 * WIN_HEAD_DIM * jnp.max(jnp.abs(q_norm * (WIN_SCALE * LOG2_E))) * jnp.max(jnp.abs(k_norm))
    return dict(gn=norm_mix[None, :], wc=c_w_in[:, :qk].astype(BF16), wvt=c_w_in[:, qk:].T.astype(BF16),
                gqk=gqk[None, :], bd=bd, bound=bound.astype(F32))


def _win_kernel(sc_ref, q_ref, k_ref, vt_ref, kc_ref, vct_ref, o_ref, ot_ref, *, tq, band, n_lat):
    i = pl.program_id(1)
    start = pl.multiple_of(jnp.clip(i * tq - WINDOW, 0, n_lat - band), WINDOW)
    hd, grp = WIN_HEAD_DIM, WIN_GROUP
    bound = sc_ref[WIN_HEADS]
    qt = q_ref[...].astype(F32).T.astype(BF16)
    k_pos = start + lax.broadcasted_iota(jnp.int32, (band, tq), 0)
    q_pos = i * tq + lax.broadcasted_iota(jnp.int32, (band, tq), 1)
    bias1 = jnp.where(jnp.abs(k_pos - q_pos) <= WINDOW, 0.0, NEG_BIG)
    bias = jnp.concatenate([bias1] * grp, axis=1)
    zeros = jnp.zeros((hd, grp * tq), BF16)

    def group_inputs(kv):
        qg = jnp.concatenate([qt[(kv * grp + g) * hd:(kv * grp + g + 1) * hd, :] for g in range(grp)], axis=1)
        qg = jnp.concatenate([qg, zeros] if kv % 2 == 0 else [zeros, qg], axis=0)
        col = (kv // 2) * LANES
        sink = jnp.concatenate([jnp.full((1, tq), sc_ref[kv * grp + g], F32) for g in range(grp)], axis=1)
        return qg, col, sink

    def weighted_values(kv, p_ctx, p_loc):
        return (_dot(vct_ref[kv * hd:(kv + 1) * hd, :], p_ctx.astype(BF16))
                + _dot(vt_ref[kv * hd:(kv + 1) * hd, pl.ds(start, band)], p_loc.astype(BF16)))

    def regroup(ot):
        return jnp.concatenate([ot[:, g * tq:(g + 1) * tq] for g in range(grp)], axis=0)

    def one_pass(kv):
        qg, col, sink = group_inputs(kv)
        p_ctx = jnp.exp2(_dot(kc_ref[:, col:col + LANES], qg) - bound)
        p_loc = jnp.exp2(_dot(k_ref[pl.ds(start, band), col:col + LANES], qg) + (bias - bound))
        den = (jnp.sum(p_loc, axis=0, keepdims=True) + jnp.sum(p_ctx, axis=0, keepdims=True)
               + jnp.exp2(sink - bound))
        ot_ref[kv * grp * hd:(kv + 1) * grp * hd, :] = regroup(weighted_values(kv, p_ctx, p_loc) / den)
        return den

    def two_pass(kv):
        qg, col, sink = group_inputs(kv)
        s_ctx = _dot(kc_ref[:, col:col + LANES], qg)
        s_loc = _dot(k_ref[pl.ds(start, band), col:col + LANES], qg) + bias
        m = jnp.maximum(jnp.maximum(jnp.max(s_loc, axis=0, keepdims=True),
                                    jnp.max(s_ctx, axis=0, keepdims=True)), sink)
        p_loc = jnp.exp2(s_loc - m)
        p_ctx = jnp.exp2(s_ctx - m)
        den = (jnp.sum(p_loc, axis=0, keepdims=True) + jnp.sum(p_ctx, axis=0, keepdims=True)
               + jnp.exp2(sink - m))
        ot_ref[kv * grp * hd:(kv + 1) * grp * hd, :] = regroup(weighted_values(kv, p_ctx, p_loc) / den)

    lo = hi = None
    for kv in range(WIN_KV_HEADS):
        den = one_pass(kv)
        lo = den if lo is None else jnp.minimum(lo, den)
        hi = den if hi is None else jnp.maximum(hi, den)
    unsafe = jnp.logical_or(jnp.min(lo) < SOFTMAX_DEN_MIN, jnp.max(hi) > SOFTMAX_DEN_MAX)

    @pl.when(unsafe)
    def _():
        for kv in range(WIN_KV_HEADS):
            two_pass(kv)

    o_ref[...] = ot_ref[...].T.astype(BF16)


def _win_attention(q, k, vt, k_ctx, vt_ctx, sink, score_bound, *, tq):
    b, n, qw = q.shape
    n_c = k_ctx.shape[1]
    kw = k.shape[2]
    band = tq + 2 * WINDOW
    full = lambda r, w: pl.BlockSpec((None, r, w), lambda bi, i: (bi, 0, 0))
    return pl.pallas_call(
        functools.partial(_win_kernel, tq=tq, band=band, n_lat=n),
        grid=(b, n // tq),
        in_specs=[pl.BlockSpec(memory_space=pltpu.SMEM),
                  pl.BlockSpec((None, tq, qw), lambda bi, i: (bi, i, 0)),
                  full(n, kw), full(kw, n), full(n_c, kw), full(kw, n_c)],
        out_specs=pl.BlockSpec((None, tq, qw), lambda bi, i: (bi, i, 0)),
        out_shape=jax.ShapeDtypeStruct((b, n, qw), BF16),
        scratch_shapes=[pltpu.VMEM((qw, tq), F32)],
        compiler_params=_params("parallel", "arbitrary"),
        name="win_attention",
    )(jnp.concatenate([sink.astype(F32) * LOG2_E, score_bound.reshape(1)]), q, k, vt, k_ctx, vt_ctx)


def _post_kernel(*refs, s5_width, n_chunks):
    if s5_width:
        (h_ref, g_ref, sh_ref, sc_ref, g2_ref, gn_ref, yg_ref, o_ref, wglu_ref, bglu_ref, wo_ref,
         wg_ref, wu_ref, wd_ref, out_ref, a_ref, acc_ref) = refs
        yg = yg_ref[...]
        s5 = yg * jax.nn.sigmoid(_dot(yg.astype(BF16), wglu_ref[...]) + bglu_ref[...])
        mix = _dot(s5.astype(BF16), wo_ref[:s5_width, :]) + _dot(o_ref[...], wo_ref[s5_width:, :])
    else:
        (h_ref, g_ref, sh_ref, sc_ref, g2_ref, gn_ref, o_ref, wo_ref,
         wg_ref, wu_ref, wd_ref, out_ref, a_ref, acc_ref) = refs
        mix = _dot(o_ref[...], wo_ref[...])
    h1 = h_ref[...] + g_ref[...] * mix
    a_ref[...] = (_rms(h1, gn_ref[...]) * (1.0 + sc_ref[...]) + sh_ref[...]).astype(BF16)
    acc_ref[...] = jnp.zeros_like(acc_ref)

    def body(c, _):
        a = a_ref[...]
        cols = pl.ds(pl.multiple_of(c * MXU_TILE, MXU_TILE), MXU_TILE)
        act = _silu(_dot(a, wg_ref[:, cols])) * _dot(a, wu_ref[:, cols])
        acc_ref[...] += _dot(act.astype(BF16), wd_ref[cols, :])
        return 0

    lax.fori_loop(0, n_chunks, body, 0, unroll=True)
    out_ref[...] = h1 + g2_ref[...] * acc_ref[...]


def _post(h, mods, ctx_row, tm, wts, ffn, layer, o, yg=None):
    b, t, d = h.shape
    tile = lambda w: pl.BlockSpec((None, tm, w), lambda bi, i: (bi, i, 0))
    s5_width = 0 if yg is None else yg.shape[2]
    in_specs = [tile(d)] + _mod_specs(d, (2, 3, 4, 5), ctx_row) + [_const_spec(wts["gn"].shape)]
    args = [h, mods, mods, mods, mods, wts["gn"]]
    if yg is not None:
        in_specs += [tile(s5_width), tile(o.shape[2]), _const_spec(wts["wglu"].shape),
                     _const_spec(wts["bglu"].shape)]
        args += [yg, o, wts["wglu"], wts["bglu"]]
    else:
        in_specs += [tile(o.shape[2])]
        args += [o]
    in_specs.append(_const_spec(wts["wo"].shape))
    args.append(wts["wo"])
    for k in ("wg", "wu", "wd"):
        in_specs.append(_const_spec(ffn[k].shape, layer))
        args.append(ffn[k])
    return pl.pallas_call(
        functools.partial(_post_kernel, s5_width=s5_width, n_chunks=ffn["wg"].shape[2] // MXU_TILE),
        grid=(b, t // tm),
        in_specs=in_specs,
        out_specs=tile(d),
        out_shape=jax.ShapeDtypeStruct((b, t, d), F32),
        scratch_shapes=[pltpu.VMEM((tm, d), BF16), pltpu.VMEM((tm, d), F32)],
        compiler_params=_params("parallel", "parallel"),
        name="post_ffn",
    )(*args)


def _prep_post(norm_ffn, w_out, w_glu=None, b_glu=None):
    wts = dict(gn=norm_ffn[None, :], wo=w_out.astype(BF16))
    if w_glu is not None:
        wts.update(wglu=w_glu.astype(BF16), bglu=b_glu[None, :])
    return wts


def kernel(x, c, ctx, c_ctx, ada_w, ada_b, norm_mix, norm_ffn, ffn_w_gate, ffn_w_up, ffn_w_down,
           a_w_in, a_w_out, s5_lam_re, s5_lam_im, s5_log_step, s5_b_re, s5_b_im, s5_c_re, s5_c_im,
           s5_d, s5_w_glu, s5_b_glu, mla_qa_norm, mla_w_q_b, mla_kva_norm, mla_w_kv_b,
           mla_q_norm, mla_k_norm, c_w_in, c_w_out, c_q_norm, c_k_norm, c_sink):
    b, n, d = x.shape
    n_c = ctx.shape[1]
    depth = ada_w.shape[0]
    assert b + 1 <= MOD_ROWS and n % max(TM_PROJ_C, TK_MLA) == 0 and n_c % (S5_CHUNK * SUBLANES) == 0
    rows = n // GRID_W
    tm_ctx = n_c

    cond = jnp.zeros((MOD_ROWS, d), F32).at[:b].set(c).at[b].set(c_ctx)
    mods = _ada_modulation(cond, ada_w, ada_b)
    mods = mods.reshape(depth, MOD_ROWS, N_MOD, 1, d)

    cos_a, sin_a = _grid_rope_tables(rows, MLA_ROPE)
    cs_a_lat = np.concatenate([cos_a, sin_a], axis=1)
    cs_a_ctx = np.concatenate([np.ones((n_c, MLA_ROPE), np.float32), np.zeros((n_c, MLA_ROPE), np.float32)], axis=1)
    cos_c, sin_c = _grid_rope_tables(rows, WIN_HEAD_DIM)
    cos_c2, sin_c2 = np.tile(cos_c, (1, 2)), np.tile(sin_c, (1, 2))
    one_c, zero_c = np.ones((n_c, LANES), np.float32), np.zeros((n_c, LANES), np.float32)
    assert ffn_w_gate.shape[2] % MXU_TILE == 0
    ffn = dict(wg=ffn_w_gate.astype(BF16), wu=ffn_w_up.astype(BF16), wd=ffn_w_down.astype(BF16))

    h_ctx, h_lat = ctx, x
    for i in range(depth):
        need_ctx = i < depth - 1
        j = i // 2
        m_i = mods[i]
        if i % 2 == 0:
            pw = _prep_proj_a(norm_mix[i], a_w_in[j], mla_qa_norm[j], mla_w_q_b[j], mla_kva_norm[j],
                              mla_w_kv_b[j], mla_q_norm[j], mla_k_norm[j])
            u_l, q_l, k_l, vt_l = _proj_a(h_lat, m_i, None, TM_PROJ_A, pw, cs_a_lat)
            u_c, q_c, k_c, vt_c = _proj_a(h_ctx, m_i, b, tm_ctx, pw, cs_a_ctx)
            tables = _s5_tables(s5_lam_re[j], s5_lam_im[j], s5_log_step[j], s5_b_re[j], s5_b_im[j],
                                s5_c_re[j], s5_c_im[j], s5_d[j])
            yg_c, yg_l = _s5(u_c, u_l, tables)
            o_l = _mla_attention(q_l, k_c, vt_c, k_l, vt_l, tq=n)
            post_w = _prep_post(norm_ffn[i], a_w_out[j], s5_w_glu[j], s5_b_glu[j])
            h_lat_new = _post(h_lat, m_i, None, TM_FFN, post_w, ffn, i, o_l, yg_l)
            if need_ctx:
                o_c = _mla_attention(q_c, k_c, vt_c, tq=n_c)
                h_ctx = _post(h_ctx, m_i, b, tm_ctx, post_w, ffn, i, o_c, yg_c)
            h_lat = h_lat_new
        else:
            pw = _prep_proj_c(norm_mix[i], c_w_in[j], c_q_norm[j], c_k_norm[j])
            q_l, k_l, vt_l = _proj_c(h_lat, m_i, None, TM_PROJ_C, pw, cos_c2, sin_c2)
            q_c, k_c, vt_c = _proj_c(h_ctx, m_i, b, tm_ctx, pw, one_c, zero_c)
            o_l = _win_attention(q_l, k_l, vt_l, k_c, vt_c, c_sink[j], pw["bound"], tq=TQ_WIN)
            post_w = _prep_post(norm_ffn[i], c_w_out[j])
            h_lat_new = _post(h_lat, m_i, None, TM_FFN, post_w, ffn, i, o_l)
            if need_ctx:
                raise NotImplementedError("context queries of a windowed layer")
            h_lat = h_lat_new
    return h_lat
```

```python
import functools
import math

import jax
import jax.numpy as jnp
import numpy as np
from jax import lax
from jax.experimental import pallas as pl
from jax.experimental.pallas import tpu as pltpu

F32 = jnp.float32
BF16 = jnp.bfloat16

GRID_W = 64
NORM_EPS = 1e-6
ROPE_THETA = 10000.0
N_MOD = 6
S5_GROUP_DIM = 16
S5_STATE = 64
S5_CHUNK = 16
MLA_HEADS = 4
MLA_NOPE = 128
MLA_ROPE = 64
MLA_QK_DIM = MLA_NOPE + MLA_ROPE
MLA_V = 128
MLA_Q_RANK = 384
MLA_KV_RANK = 256
MLA_SCALE = MLA_QK_DIM ** -0.5
MLA_HEAD_PAD = 256
WIN_HEADS = 16
WIN_KV_HEADS = 4
WIN_GROUP = WIN_HEADS // WIN_KV_HEADS
WIN_HEAD_DIM = 64
WINDOW = 128
WIN_SCALE = WIN_HEAD_DIM ** -0.5
LANES = 128
SUBLANES = 8
MXU_TILE = 256
VMEM_BYTES_V7X = 64 * 1024 * 1024
VMEM_LIMIT_BYTES = (VMEM_BYTES_V7X * 3) // 4
NEG_BIG = -1e30
LOG2_E = math.log2(math.e)
TM_FFN = 512
TM_PROJ_A = 512
TM_PROJ_C = 1024
TQ_WIN = 256
TK_MLA = 1024
TN_ADA = 1024
PROJ_ROWS = 256
SOFTMAX_DEN_MIN = 2.0 ** -60
SOFTMAX_DEN_MAX = 2.0 ** 60
S5_BLOCK_PAIRS = 4
MLA_Q_BLOCK = 256
MLA_K_PIECE = 256
MOD_ROWS = 8


def _dot(a, b):
    return jnp.dot(a, b, preferred_element_type=F32)


def _dot_nt(a, b):
    return lax.dot_general(a, b, (((1,), (1,)), ((), ())), preferred_element_type=F32)


def _split_bf16(x):
    hi = x.astype(BF16)
    lo = (x - hi.astype(F32)).astype(BF16)
    return hi, lo


def _rms(x, gain):
    return x * lax.rsqrt(jnp.mean(x * x, axis=-1, keepdims=True) + NORM_EPS) * gain


def _silu(x):
    return x * jax.nn.sigmoid(x)


def _params(*sem):
    return pltpu.CompilerParams(dimension_semantics=sem, vmem_limit_bytes=VMEM_LIMIT_BYTES)


def _const_spec(shape, layer=None):
    if layer is None:
        nd = len(shape)
        return pl.BlockSpec(shape, lambda *_: (0,) * nd, pipeline_mode=pl.Buffered(1))
    nd = len(shape) - 1
    return pl.BlockSpec((None,) + tuple(shape[1:]), lambda *_: (layer,) + (0,) * nd,
                        pipeline_mode=pl.Buffered(1))


def _ada_kernel(cond_ref, w_ref, b_ref, o_ref):
    s = _silu(cond_ref[...])
    s_hi, s_lo = _split_bf16(s)
    w_hi, w_lo = _split_bf16(w_ref[...])
    o_ref[...] = _dot(s_hi, w_hi) + _dot(s_lo, w_hi) + _dot(s_hi, w_lo) + b_ref[...]


def _ada_modulation(cond, ada_w, ada_b):
    depth, d, n = ada_w.shape
    tn = TN_ADA
    return pl.pallas_call(
        _ada_kernel,
        grid=(depth, n // tn),
        in_specs=[pl.BlockSpec((MOD_ROWS, d), lambda i, j: (0, 0)),
                  pl.BlockSpec((None, d, tn), lambda i, j: (i, 0, j)),
                  pl.BlockSpec((None, 1, tn), lambda i, j: (i, 0, j))],
        out_specs=pl.BlockSpec((None, MOD_ROWS, tn), lambda i, j: (i, 0, j)),
        out_shape=jax.ShapeDtypeStruct((depth, MOD_ROWS, n), F32),
        compiler_params=_params("arbitrary", "arbitrary"),
        name="ada_modulation",
    )(cond, ada_w, ada_b.reshape(depth, 1, n))


def _mod_specs(d, slots, ctx_row):
    def make(slot):
        if ctx_row is None:
            return pl.BlockSpec((None, None, 1, d), lambda b, i: (b, slot, 0, 0))
        return pl.BlockSpec((None, None, 1, d), lambda b, i: (ctx_row, slot, 0, 0))
    return [make(s) for s in slots]


def _grid_rope_tables(rows, rot_dim):
    n_freq = rot_dim // 4
    inv_freq = np.power(np.float32(ROPE_THETA), -np.arange(n_freq, dtype=np.float32) / np.float32(n_freq))
    ang_r = np.arange(rows, dtype=np.float32)[:, None] * inv_freq.astype(np.float32)
    ang_c = np.arange(GRID_W, dtype=np.float32)[:, None] * inv_freq.astype(np.float32)

    def expand(r, c):
        r = np.broadcast_to(r[:, None, :], (rows, GRID_W, n_freq))
        c = np.broadcast_to(c[None, :, :], (rows, GRID_W, n_freq))
        return np.concatenate([r, r, c, c], axis=-1).reshape(rows * GRID_W, rot_dim)

    return expand(np.cos(ang_r), np.cos(ang_c)), expand(np.sin(ang_r), np.sin(ang_c))


def _rot_perm_sign(rot_dim):
    q = rot_dim // 4
    idx = np.arange(rot_dim)
    perm = np.where((idx // q) % 2 == 0, idx + q, idx - q)
    sign = np.where((idx // q) % 2 == 0, -1.0, 1.0).astype(np.float32)
    return perm, sign


def _proj_a_kernel(h_ref, sh_ref, sc_ref, gn_ref, w1_ref, gqa_ref, wq_ref, gkv_ref, wkv_ref, wvt_ref,
                   gq_ref, gk_ref, cs_ref, msk_ref, u_ref, q_ref, k_ref, vt_ref):
    tm = h_ref.shape[0]
    rb = min(PROJ_ROWS, tm)
    msk = msk_ref[...]
    low_half = lax.broadcasted_iota(jnp.int32, (rb, LANES), 1) < MLA_ROPE

    def finish(xh, gain, cs, out_ref, rows, h):
        ssq = _dot((xh * xh).astype(BF16), msk)
        xn = xh * lax.rsqrt(ssq * (1.0 / MLA_QK_DIM) + NORM_EPS) * gain
        rr = xn[:, LANES:] * cs
        rot = rr + pltpu.roll(rr, MLA_ROPE, axis=1)
        base = h * MLA_HEAD_PAD
        out_ref[rows, base:base + LANES] = xn[:, :LANES].astype(BF16)
        out_ref[rows, base + LANES:base + 2 * LANES] = jnp.where(low_half, rot, 0.0).astype(BF16)

    for r in range(tm // rb):
        rows = slice(r * rb, (r + 1) * rb)
        a = _rms(h_ref[rows, :], gn_ref[...]) * (1.0 + sc_ref[...]) + sh_ref[...]
        p1 = _dot(a.astype(BF16), w1_ref[...])
        s5w = u_ref.shape[1]
        u_ref[rows, :] = p1[:, :s5w]
        cq = p1[:, s5w:s5w + MLA_Q_RANK]
        ckv = p1[:, s5w + MLA_Q_RANK:s5w + MLA_Q_RANK + MLA_KV_RANK]
        krr = p1[:, s5w + MLA_Q_RANK + MLA_KV_RANK:]
        qb = _dot(_rms(cq, gqa_ref[...]).astype(BF16), wq_ref[...])
        ckv_n = _rms(ckv, gkv_ref[...]).astype(BF16)
        kv = _dot(ckv_n, wkv_ref[...])
        vt_ref[:, rows] = _dot_nt(wvt_ref[...], ckv_n).astype(BF16)
        cs = cs_ref[rows, :]
        for h in range(MLA_HEADS):
            base = h * MLA_HEAD_PAD
            finish(qb[:, base:base + MLA_HEAD_PAD], gq_ref[:, base:base + MLA_HEAD_PAD], cs, q_ref, rows, h)
            kh = jnp.concatenate([kv[:, h * MLA_NOPE:(h + 1) * MLA_NOPE], krr], axis=1)
            finish(kh, gk_ref[:, base:base + MLA_HEAD_PAD], cs, k_ref, rows, h)


def _proj_a(h, mods, ctx_row, tm, wts, cs):
    b, t, d = h.shape
    qw = MLA_HEADS * MLA_HEAD_PAD
    grid = (b, t // tm)
    tile = lambda w: pl.BlockSpec((None, tm, w), lambda bi, i: (bi, i, 0))
    in_specs = ([tile(d)] + _mod_specs(d, (0, 1), ctx_row)
                + [_const_spec(wts[k].shape) for k in
                   ("gn", "w1", "gqa", "wq", "gkv", "wkv", "wvt", "gq", "gk")]
                + [pl.BlockSpec((tm, LANES), lambda bi, i: (i, 0)), _const_spec(wts["msk"].shape)])
    vw = MLA_HEADS * MLA_V
    s5w = wts["w1"].shape[1] - (MLA_Q_RANK + MLA_KV_RANK + 2 * MLA_ROPE)
    return pl.pallas_call(
        _proj_a_kernel,
        grid=grid,
        in_specs=in_specs,
        out_specs=[tile(s5w), tile(qw), tile(qw), pl.BlockSpec((None, vw, tm), lambda bi, i: (bi, 0, i))],
        out_shape=[jax.ShapeDtypeStruct((b, t, s5w), F32),
                   jax.ShapeDtypeStruct((b, t, qw), BF16),
                   jax.ShapeDtypeStruct((b, t, qw), BF16),
                   jax.ShapeDtypeStruct((b, vw, t), BF16)],
        compiler_params=_params("parallel", "parallel"),
        name="proj_a",
    )(h, mods, mods, wts["gn"], wts["w1"], wts["gqa"], wts["wq"], wts["gkv"], wts["wkv"], wts["wvt"],
      wts["gq"], wts["gk"], cs, wts["msk"])


def _prep_proj_a(norm_mix, a_w_in, qa_norm, w_q_b, kva_norm, w_kv_b, q_norm, k_norm):
    perm, sign = _rot_perm_sign(MLA_ROPE)
    s5w = a_w_in.shape[1] - (MLA_Q_RANK + MLA_KV_RANK + MLA_ROPE)
    assert s5w % LANES == 0 and MLA_Q_RANK % LANES == 0 and MLA_KV_RANK % LANES == 0
    kr = a_w_in[:, -MLA_ROPE:]
    w1 = jnp.concatenate([a_w_in, kr[:, perm] * sign], axis=1).astype(BF16)
    wq = w_q_b.reshape(MLA_Q_RANK, MLA_HEADS, MLA_QK_DIM)
    rope = wq[:, :, MLA_NOPE:]
    wq = jnp.concatenate([wq, rope[:, :, perm] * sign], axis=2)
    wq = wq.reshape(MLA_Q_RANK, MLA_HEADS * MLA_HEAD_PAD).astype(BF16)
    wkv3 = w_kv_b.reshape(MLA_KV_RANK, MLA_HEADS, MLA_NOPE + MLA_V)
    wkv = wkv3[:, :, :MLA_NOPE].reshape(MLA_KV_RANK, -1).astype(BF16)
    wvt = wkv3[:, :, MLA_NOPE:].reshape(MLA_KV_RANK, -1).T.astype(BF16)

    def head_gain(g, scale):
        gb = jnp.concatenate([g, g[MLA_NOPE:][perm]]) * scale
        return jnp.tile(gb, MLA_HEADS)[None, :]

    rows = np.arange(MLA_HEAD_PAD)[:, None] < MLA_QK_DIM
    msk = jnp.asarray(np.broadcast_to(rows, (MLA_HEAD_PAD, MLA_HEAD_PAD)), BF16)
    return dict(gn=norm_mix[None, :], w1=w1, gqa=qa_norm[None, :], wq=wq, gkv=kva_norm[None, :],
                wkv=wkv, wvt=wvt, gq=head_gain(q_norm, MLA_SCALE * LOG2_E), gk=head_gain(k_norm, 1.0),
                msk=msk)


def _s5_kernel(uc_ref, ul_ref, sel_in_ref, sel_out_ref, toep_ref, bst_ref, cst_ref, a_ref, d_ref,
               yc_ref, yl_ref, ub_ref, x_ref, z_ref, sin_ref, yb_ref, *, n_ctx, n_all):
    L, S = S5_CHUNK, S5_GROUP_DIM
    n_lat = n_all - n_ctx
    npair = S5_BLOCK_PAIRS
    half = SUBLANES * LANES
    for tl in range(L):
        u = jnp.concatenate([uc_ref[pl.ds(tl, n_ctx, stride=L), :], ul_ref[pl.ds(tl, n_lat, stride=L), :]], axis=0)
        ub_ref[:, tl * LANES:(tl + 1) * LANES] = u.astype(BF16)
    for pp in range(npair):
        for hh in range(2):
            xs = _dot(ub_ref[:, hh * half:(hh + 1) * half], sel_in_ref[pp]).astype(BF16)
            x_ref[pp, :, hh * LANES:(hh + 1) * LANES] = xs[:, :LANES]
            x_ref[pp, :, MXU_TILE + hh * LANES:MXU_TILE + (hh + 1) * LANES] = xs[:, LANES:]
    for pp in range(npair):
        z_ref[pp] = _dot(x_ref[pp], bst_ref[pp])
    t_ctx, t_all = n_ctx // SUBLANES, n_all // SUBLANES
    row = lax.broadcasted_iota(jnp.int32, (SUBLANES, LANES), 0)

    def cmul(a_re, a_im, b_re, b_im):
        return a_re * b_re - a_im * b_im, a_re * b_im + a_im * b_re

    def tile_scan(z_re, z_im, c_re, c_im, a_re, a_im, fwd):
        for sft in (1, 2, 4):
            k = sft - 1 if fwd else SUBLANES - sft
            p_re, p_im = a_re[k:k + 1, :], a_im[k:k + 1, :]
            amt = sft if fwd else SUBLANES - sft
            keep = (row >= sft) if fwd else (row < SUBLANES - sft)
            s_re = jnp.where(keep, pltpu.roll(z_re, amt, axis=0), 0.0)
            s_im = jnp.where(keep, pltpu.roll(z_im, amt, axis=0), 0.0)
            m_re, m_im = cmul(p_re, p_im, s_re, s_im)
            z_re, z_im = z_re + m_re, z_im + m_im
        m_re, m_im = cmul(a_re, a_im, c_re, c_im)
        s_re, s_im = z_re + m_re, z_im + m_im
        edge = (row == 0) if fwd else (row == SUBLANES - 1)
        amt = 1 if fwd else SUBLANES - 1
        in_re = jnp.where(edge, c_re, pltpu.roll(s_re, amt, axis=0))
        in_im = jnp.where(edge, c_im, pltpu.roll(s_im, amt, axis=0))
        last = SUBLANES - 1 if fwd else 0
        return in_re, in_im, s_re[last:last + 1, :], s_im[last:last + 1, :]

    def body(it, carry):
        jt = jnp.where(it < t_ctx, t_ctx - 1 - it, t_all + t_ctx - 1 - it)
        rf = pl.multiple_of(it * SUBLANES, SUBLANES)
        rb = pl.multiple_of(jt * SUBLANES, SUBLANES)
        new = []
        for pp in range(npair):
            cf_re, cf_im, cb_re, cb_im = carry[4 * pp:4 * pp + 4]
            zf = z_ref[pp, pl.ds(rf, SUBLANES), 0:2 * LANES]
            zb = z_ref[pp, pl.ds(rb, SUBLANES), 2 * LANES:4 * LANES]
            f_re, f_im, cf_re, cf_im = tile_scan(zf[:, :LANES], zf[:, LANES:], cf_re, cf_im,
                                                 a_ref[pp, 0], a_ref[pp, 1], True)
            b_re, b_im, cb_re, cb_im = tile_scan(zb[:, :LANES], zb[:, LANES:], cb_re, cb_im,
                                                 a_ref[pp, 2], a_ref[pp, 3], False)
            sin_ref[pp, pl.ds(rf, SUBLANES), 0:2 * LANES] = jnp.concatenate([f_re, f_im], axis=1)
            sin_ref[pp, pl.ds(rb, SUBLANES), 2 * LANES:4 * LANES] = jnp.concatenate([b_re, b_im], axis=1)
            new += [cf_re, cf_im, cb_re, cb_im]
        return tuple(new)

    zero = jnp.zeros((1, LANES), F32)
    lax.fori_loop(0, t_all, body, (zero,) * (4 * npair))
    for pp in range(npair):
        x = x_ref[pp]
        y = jnp.concatenate([_dot(x[:, :MXU_TILE], toep_ref[pp, 0]), _dot(x[:, MXU_TILE:], toep_ref[pp, 1])],
                            axis=1)
        y = y + _dot(sin_ref[pp].astype(BF16), cst_ref[pp]) + x.astype(F32) * d_ref[pp]
        yg = jax.nn.gelu(y).astype(BF16)
        for gl in range(2):
            for hh in range(2):
                g8 = 2 * pp + gl
                col = gl * MXU_TILE + hh * LANES
                yb_ref[hh, :, g8 * LANES:(g8 + 1) * LANES] = yg[:, col:col + LANES]
    for hh in range(2):
        for kk in range(SUBLANES // 2):
            two = _dot(yb_ref[hh], sel_out_ref[kk])
            for e in range(2):
                tl = hh * SUBLANES + 2 * kk + e
                yc_ref[pl.ds(tl, n_ctx, stride=L), :] = two[:n_ctx, e * LANES:(e + 1) * LANES]
                yl_ref[pl.ds(tl, n_lat, stride=L), :] = two[n_ctx:, e * LANES:(e + 1) * LANES]


def _s5_selectors():
    S = S5_GROUP_DIM
    r = np.arange(SUBLANES * LANES)[:, None]
    c = np.arange(2 * LANES)[None, :]
    k, l = r // LANES, r % LANES
    pp = np.arange(S5_BLOCK_PAIRS)[:, None, None]
    sel_in = (k == (c % LANES) // S) & (l == 2 * S * pp + S * (c // LANES) + c % S)
    kk = np.arange(SUBLANES // 2)[:, None, None]
    sel_out = (k == (c % LANES) // S) & (l == S * (2 * kk + c // LANES) + c % S)
    return jnp.asarray(sel_in, BF16), jnp.asarray(sel_out, BF16)


def _dot_nt_f32(a, b):
    a_hi, a_lo = _split_bf16(a)
    b_hi, b_lo = _split_bf16(b)
    return _dot_nt(a_hi, b_hi) + _dot_nt(a_lo, b_hi) + _dot_nt(a_hi, b_lo)


def _s5_table_kernel(par_ref, bre_ref, bim_ref, cre_ref, cim_ref, toep_ref, bst_ref, cst_ref, a_ref):
    L, S, P = S5_CHUNK, S5_GROUP_DIM, S5_STATE
    kk = lax.broadcasted_iota(jnp.int32, (3 * SUBLANES, LANES), 0).astype(F32)
    lane = lax.broadcasted_iota(jnp.int32, (L, LANES), 1)
    pair_rows = lax.broadcasted_iota(jnp.int32, (2 * L * S, LANES), 0)
    pair_lanes = lax.broadcasted_iota(jnp.int32, (2 * L * S, LANES), 1)
    own_group = (pair_rows // (L * S)) == (pair_lanes // P)
    lane_pad = jnp.zeros((2 * P, LANES - S), F32)

    def rows_of_powers(pw, ks, groups):
        one = jnp.concatenate([jnp.broadcast_to(pw[k:k + 1, :], (S, LANES)) for k in ks], axis=0)
        return jnp.concatenate([one] * groups, axis=0) if groups > 1 else one

    def cmul(a_re, a_im, b_re, b_im):
        return a_re * b_re - a_im * b_im, a_re * b_im + a_im * b_re

    lag_tables = []
    for d in range(2):
        lam_re, lam_im = par_ref[d, 0:1, :], par_ref[d, 1:2, :]
        step = jnp.exp(par_ref[d, 2:3, :])
        ar, ai = lam_re * step, lam_im * step
        mag = jnp.exp(kk * ar)
        pw_re, pw_im = mag * jnp.cos(kk * ai), mag * jnp.sin(kk * ai)
        th = jnp.tanh(0.5 * ar)
        em1 = 2.0 * th / (1.0 - th)
        sh = jnp.sin(0.5 * ai)
        n_re = em1 * jnp.cos(ai) - 2.0 * sh * sh
        n_im = (em1 + 1.0) * jnp.sin(ai)
        den = lam_re * lam_re + lam_im * lam_im
        co_re = (n_re * lam_re + n_im * lam_im) / den
        co_im = (n_im * lam_re - n_re * lam_im) / den
        bt_re = jnp.concatenate([bre_ref[d], lane_pad], axis=1).T[:S]
        bt_im = jnp.concatenate([bim_ref[d], lane_pad], axis=1).T[:S]
        bb_re, bb_im = cmul(co_re, co_im, bt_re, bt_im)
        cc_re = jnp.concatenate([cre_ref[d, 0], cre_ref[d, 1]], axis=1)
        cc_im = jnp.concatenate([cim_ref[d, 0], cim_ref[d, 1]], axis=1)
        ks = [L - 1 - t for t in range(L)] if d == 0 else list(range(L))
        r_re, r_im = rows_of_powers(pw_re, ks, 2), rows_of_powers(pw_im, ks, 2)
        bbt_re, bbt_im = jnp.concatenate([bb_re] * (2 * L), axis=0), jnp.concatenate([bb_im] * (2 * L), axis=0)
        v_re, v_im = cmul(r_re, r_im, bbt_re, bbt_im)
        bst_ref[:, (2 * d) * LANES:(2 * d + 1) * LANES] = jnp.where(own_group, v_re, 0.0).astype(BF16)
        bst_ref[:, (2 * d + 1) * LANES:(2 * d + 2) * LANES] = jnp.where(own_group, v_im, 0.0).astype(BF16)
        ks = [t + 1 for t in range(L)] if d == 0 else [L - t for t in range(L)]
        r_re, r_im = rows_of_powers(pw_re, ks, 2), rows_of_powers(pw_im, ks, 2)
        cct_re, cct_im = jnp.concatenate([cc_re] * (2 * L), axis=0), jnp.concatenate([cc_im] * (2 * L), axis=0)
        v_re, v_im = cmul(cct_re, cct_im, r_re, r_im)
        cst_ref[(2 * d) * LANES:(2 * d + 1) * LANES, :] = jnp.where(own_group, v_re, 0.0).T.astype(BF16)
        cst_ref[(2 * d + 1) * LANES:(2 * d + 2) * LANES, :] = jnp.where(own_group, -v_im, 0.0).T.astype(BF16)
        ks = list(range(L)) if d == 0 else [L - 1 - j for j in range(L)]
        r_re, r_im = rows_of_powers(pw_re, ks, 1), rows_of_powers(pw_im, ks, 1)
        cl_re, cl_im = cmul(jnp.concatenate([cc_re] * L, axis=0), jnp.concatenate([cc_im] * L, axis=0), r_re, r_im)
        per_group = []
        for g in range(2):
            mine = (lane // P) == g
            per_group.append(_dot_nt_f32(jnp.where(mine, bb_re, 0.0), cl_re)
                             - _dot_nt_f32(jnp.where(mine, bb_im, 0.0), cl_im))
        lag_tables.append(per_group)
        row8 = lax.broadcasted_iota(jnp.int32, (SUBLANES, LANES), 0)
        n_chunks = ((row8 + 1) if d == 0 else (SUBLANES - row8)).astype(F32) * float(L)
        mag8 = jnp.exp(n_chunks * ar)
        a_ref[2 * d] = mag8 * jnp.cos(n_chunks * ai)
        a_ref[2 * d + 1] = mag8 * jnp.sin(n_chunks * ai)

    def shift_right(x, s):
        x0, x1 = x[:, :LANES], x[:, LANES:]
        a, r = divmod(s, LANES)
        r0 = pltpu.roll(x0, r, axis=1) if r else x0
        r1 = pltpu.roll(x1, r, axis=1) if r else x1
        if a == 0:
            return jnp.concatenate([jnp.where(lane >= r, r0, 0.0), jnp.where(lane >= r, r1, r0)], axis=1)
        return jnp.concatenate([jnp.zeros_like(x0), jnp.where(lane >= r, r0, 0.0)], axis=1)

    def shift_left(x, s):
        x0, x1 = x[:, :LANES], x[:, LANES:]
        a, r = divmod(s, LANES)
        r0 = pltpu.roll(x0, LANES - r, axis=1) if r else x0
        r1 = pltpu.roll(x1, LANES - r, axis=1) if r else x1
        if a == 0:
            return jnp.concatenate([jnp.where(lane < LANES - r, r0, r1), jnp.where(lane < LANES - r, r1, 0.0)], axis=1)
        return jnp.concatenate([jnp.where(lane < LANES - r, r1, 0.0), jnp.zeros_like(x0)], axis=1)

    for g in range(2):
        kf, kb = lag_tables[0][g], lag_tables[1][g]
        for tau in range(L):
            blk = shift_right(kf, S * tau) + shift_left(kb, S * (L - 1 - tau))
            toep_ref[g, tau * S:(tau + 1) * S, :] = blk.astype(BF16)


def _s5_tables(lam_re, lam_im, log_step, b_re, b_im, c_re, c_im, d_skip):
    _, G, P = lam_re.shape
    S, L = S5_GROUP_DIM, S5_CHUNK
    assert P == S5_STATE and 2 * P == LANES and 2 * L * S == 2 * MXU_TILE
    pairs = G // 2
    par = jnp.stack([lam_re.reshape(2, pairs, 2 * P), lam_im.reshape(2, pairs, 2 * P),
                     jnp.repeat(log_step, P, axis=-1).reshape(2, pairs, 2 * P)], axis=2)
    par = jnp.transpose(par, (1, 0, 2, 3)).astype(F32)
    bshape = (2, pairs, 2 * P, S)
    cshape = (2, pairs, 2, S, P)
    pw = 2 * L * S
    bspec = pl.BlockSpec((2, None, 2 * P, S), lambda g: (0, g, 0, 0))
    cspec = pl.BlockSpec((2, None, 2, S, P), lambda g: (0, g, 0, 0, 0))
    toep, bst, cst, a_chunk = pl.pallas_call(
        _s5_table_kernel,
        grid=(pairs,),
        in_specs=[pl.BlockSpec((None, 2, 3, 2 * P), lambda g: (g, 0, 0, 0)), bspec, bspec, cspec, cspec],
        out_specs=[pl.BlockSpec((None, 2, MXU_TILE, MXU_TILE), lambda g: (g, 0, 0, 0)),
                   pl.BlockSpec((None, pw, pw), lambda g: (g, 0, 0)),
                   pl.BlockSpec((None, pw, pw), lambda g: (g, 0, 0)),
                   pl.BlockSpec((None, 4, SUBLANES, LANES), lambda g: (g, 0, 0, 0))],
        out_shape=[jax.ShapeDtypeStruct((pairs, 2, MXU_TILE, MXU_TILE), BF16),
                   jax.ShapeDtypeStruct((pairs, pw, pw), BF16),
                   jax.ShapeDtypeStruct((pairs, pw, pw), BF16),
                   jax.ShapeDtypeStruct((pairs, 4, SUBLANES, LANES), F32)],
        compiler_params=_params("parallel"),
        name="s5_tables",
    )(par, b_re.reshape(bshape), b_im.reshape(bshape), c_re.reshape(cshape), c_im.reshape(cshape))
    d_pair = jnp.broadcast_to(d_skip.astype(F32).reshape(pairs, 2, 1, S), (pairs, 2, L, S))
    return toep, bst, cst, a_chunk, d_pair.reshape(pairs, 1, pw)


def _s5(u_ctx, u_lat, tables):
    toep, bst, cst, a_pow, d_pair = tables
    b, n_c, w = u_ctx.shape
    n_l = u_lat.shape[1]
    L = S5_CHUNK
    n_ctx, n_all = n_c // L, (n_c + n_l) // L
    nblk = w // LANES
    pw = 2 * L * S5_GROUP_DIM
    npair = S5_BLOCK_PAIRS
    sel_in, sel_out = _s5_selectors()
    wspec = lambda shape: pl.BlockSpec((npair,) + shape, lambda g, bi: (g,) + (0,) * len(shape),
                                       pipeline_mode=pl.Buffered(1))
    return pl.pallas_call(
        functools.partial(_s5_kernel, n_ctx=n_ctx, n_all=n_all),
        grid=(nblk, b),
        in_specs=[pl.BlockSpec((None, n_c, LANES), lambda g, bi: (bi, 0, g)),
                  pl.BlockSpec((None, n_l, LANES), lambda g, bi: (bi, 0, g)),
                  _const_spec(sel_in.shape), _const_spec(sel_out.shape),
                  wspec((2, MXU_TILE, MXU_TILE)), wspec((pw, pw)), wspec((pw, pw)),
                  wspec((4, SUBLANES, LANES)), wspec((1, pw))],
        out_specs=[pl.BlockSpec((None, n_c, LANES), lambda g, bi: (bi, 0, g)),
                   pl.BlockSpec((None, n_l, LANES), lambda g, bi: (bi, 0, g))],
        out_shape=[jax.ShapeDtypeStruct((b, n_c, w), F32), jax.ShapeDtypeStruct((b, n_l, w), F32)],
        scratch_shapes=[pltpu.VMEM((n_all, L * LANES), BF16), pltpu.VMEM((npair, n_all, pw), BF16),
                        pltpu.VMEM((npair, n_all, pw), F32), pltpu.VMEM((npair, n_all, pw), F32),
                        pltpu.VMEM((2, n_all, SUBLANES * LANES), BF16)],
        compiler_params=_params("parallel", "arbitrary"),
        name="s5_scan",
    )(u_ctx, u_lat, sel_in, sel_out, toep, bst, cst, a_pow, d_pair)


def _mla_kernel(*refs, tk, n_steps):
    if n_steps:
        q_ref, kc_ref, vc_ref, k_ref, v_ref, o_ref, qt_ref, s_ref, m_ref, l_ref, acc_ref = refs
    else:
        q_ref, kc_ref, vc_ref, o_ref, qt_ref, s_ref, m_ref, l_ref, acc_ref = refs
    tq = q_ref.shape[0]
    ncb = tq // MLA_Q_BLOCK
    kp = MLA_K_PIECE
    qt_ref[...] = q_ref[...].astype(F32).T.astype(BF16)

    def chunk(load_k, load_vt, nkeys, first):
        nr = nkeys // kp

        def score_piece(c, r):
            st = _dot(load_k(r), qt_ref[:, c * MLA_Q_BLOCK:(c + 1) * MLA_Q_BLOCK])
            s_ref[c % 2, r * kp:(r + 1) * kp, :] = st
            return jnp.max(st, axis=0, keepdims=True)

        def block_stats(c, mx):
            if first:
                return mx, None
            m_old = m_ref[:, c * MLA_Q_BLOCK:(c + 1) * MLA_Q_BLOCK]
            m_new = jnp.maximum(m_old, mx)
            return m_new, jnp.exp2(m_old - m_new)

        def prob_piece(c, r, m_new):
            p = jnp.exp2(s_ref[c % 2, r * kp:(r + 1) * kp, :] - m_new)
            return jnp.sum(p, axis=0, keepdims=True), _dot(load_vt(r), p.astype(BF16))

        def finish(c, m_new, alpha, lsum, pv):
            cols = slice(c * MLA_Q_BLOCK, (c + 1) * MLA_Q_BLOCK)
            if first:
                l_ref[:, cols] = lsum
                acc_ref[:, cols] = pv
            else:
                l_ref[:, cols] = alpha * l_ref[:, cols] + lsum
                acc_ref[:, cols] = alpha * acc_ref[:, cols] + pv
            m_ref[:, cols] = m_new

        mx = None
        for r in range(nr):
            pm = score_piece(0, r)
            mx = pm if mx is None else jnp.maximum(mx, pm)
        for c in range(ncb):
            m_new, alpha = block_stats(c, mx)
            mx = lsum = pv = None
            for r in range(nr):
                if c + 1 < ncb:
                    pm = score_piece(c + 1, r)
                    mx = pm if mx is None else jnp.maximum(mx, pm)
                ls, pvr = prob_piece(c, r, m_new)
                lsum = ls if lsum is None else lsum + ls
                pv = pvr if pv is None else pv + pvr
            finish(c, m_new, alpha, lsum, pv)

    chunk(lambda r: kc_ref[r * kp:(r + 1) * kp, :], lambda r: vc_ref[:, r * kp:(r + 1) * kp],
          kc_ref.shape[0], True)
    if n_steps:
        def body(j, _):
            off = pl.multiple_of(j * tk, tk)
            chunk(lambda r: k_ref[pl.ds(off + r * kp, kp), :], lambda r: v_ref[:, pl.ds(off + r * kp, kp)],
                  tk, False)
            return 0
        lax.fori_loop(0, n_steps, body, 0)
    o_ref[...] = (acc_ref[...] / l_ref[...]).T.astype(BF16)


def _mla_attention(q, k_ctx, vt_ctx, k_lat=None, vt_lat=None, *, tq, tk=TK_MLA):
    b, t, _ = q.shape
    n_c = k_ctx.shape[1]
    in_specs = [pl.BlockSpec((None, tq, MLA_HEAD_PAD), lambda bi, h, i: (bi, i, h)),
                pl.BlockSpec((None, n_c, MLA_HEAD_PAD), lambda bi, h, i: (bi, 0, h)),
                pl.BlockSpec((None, MLA_V, n_c), lambda bi, h, i: (bi, h, 0))]
    args = [q, k_ctx, vt_ctx]
    n_steps = 0
    if k_lat is not None:
        n_l = k_lat.shape[1]
        n_steps = n_l // tk
        in_specs += [pl.BlockSpec((None, n_l, MLA_HEAD_PAD), lambda bi, h, i: (bi, 0, h)),
                     pl.BlockSpec((None, MLA_V, n_l), lambda bi, h, i: (bi, h, 0))]
        args += [k_lat, vt_lat]
    return pl.pallas_call(
        functools.partial(_mla_kernel, tk=tk, n_steps=n_steps),
        grid=(b, MLA_HEADS, t // tq),
        in_specs=in_specs,
        out_specs=pl.BlockSpec((None, tq, MLA_V), lambda bi, h, i: (bi, i, h)),
        out_shape=jax.ShapeDtypeStruct((b, t, MLA_HEADS * MLA_V), BF16),
        scratch_shapes=[pltpu.VMEM((MLA_HEAD_PAD, tq), BF16), pltpu.VMEM((2, max(tk, n_c), MLA_Q_BLOCK), F32),
                        pltpu.VMEM((1, tq), F32), pltpu.VMEM((1, tq), F32), pltpu.VMEM((MLA_V, tq), F32)],
        compiler_params=_params("parallel", "parallel", "arbitrary"),
        name="mla_attention",
    )(*args)


def _proj_c_kernel(h_ref, sh_ref, sc_ref, gn_ref, wc_ref, wvt_ref, gqk_ref, cos_ref, sin_ref, bd_ref,
                   q_ref, k_ref, vt_ref):
    qw = WIN_HEADS * WIN_HEAD_DIM
    kw = WIN_KV_HEADS * WIN_HEAD_DIM
    bd = bd_ref[...]
    tm = h_ref.shape[0]
    rb = min(PROJ_ROWS, tm)
    lane = lax.broadcasted_iota(jnp.int32, (rb, LANES), 1)
    first_quarter = (lane % (WIN_HEAD_DIM // 2)) < (WIN_HEAD_DIM // 4)
    for r in range(tm // rb):
        rows = slice(r * rb, (r + 1) * rb)
        a = (_rms(h_ref[rows, :], gn_ref[...]) * (1.0 + sc_ref[...]) + sh_ref[...]).astype(BF16)
        p = _dot(a, wc_ref[...])
        vt_ref[:, rows] = _dot_nt(wvt_ref[...], a).astype(BF16)
        cos, sin = cos_ref[rows, :], sin_ref[rows, :]
        for j in range((qw + kw) // MXU_TILE):
            xh = p[:, j * MXU_TILE:(j + 1) * MXU_TILE]
            ssq = _dot((xh * xh).astype(BF16), bd)
            xn = xh * lax.rsqrt(ssq * (1.0 / WIN_HEAD_DIM) + NORM_EPS) * gqk_ref[:, j * MXU_TILE:(j + 1) * MXU_TILE]
            for c in range(MXU_TILE // LANES):
                xc = xn[:, c * LANES:(c + 1) * LANES]
                fwd = pltpu.roll(xc, WIN_HEAD_DIM // 4, axis=1)
                bwd = pltpu.roll(xc, LANES - WIN_HEAD_DIM // 4, axis=1)
                y = (xc * cos + jnp.where(first_quarter, -bwd, fwd) * sin).astype(BF16)
                col = j * MXU_TILE + c * LANES
                if col < qw:
                    q_ref[rows, col:col + LANES] = y
                else:
                    k_ref[rows, col - qw:col - qw + LANES] = y


def _proj_c(h, mods, ctx_row, tm, wts, cos2, sin2):
    b, t, d = h.shape
    qw = WIN_HEADS * WIN_HEAD_DIM
    kw = WIN_KV_HEADS * WIN_HEAD_DIM
    tile = lambda w: pl.BlockSpec((None, tm, w), lambda bi, i: (bi, i, 0))
    tab = pl.BlockSpec((tm, LANES), lambda bi, i: (i, 0))
    in_specs = ([tile(d)] + _mod_specs(d, (0, 1), ctx_row)
                + [_const_spec(wts[k].shape) for k in ("gn", "wc", "wvt", "gqk")]
                + [tab, tab, _const_spec(wts["bd"].shape)])
    return pl.pallas_call(
        _proj_c_kernel,
        grid=(b, t // tm),
        in_specs=in_specs,
        out_specs=[tile(qw), tile(kw), pl.BlockSpec((None, kw, tm), lambda bi, i: (bi, 0, i))],
        out_shape=[jax.ShapeDtypeStruct((b, t, qw), BF16),
                   jax.ShapeDtypeStruct((b, t, kw), BF16),
                   jax.ShapeDtypeStruct((b, kw, t), BF16)],
        compiler_params=_params("parallel", "parallel"),
        name="proj_c",
    )(h, mods, mods, wts["gn"], wts["wc"], wts["wvt"], wts["gqk"], cos2, sin2, wts["bd"])


def _prep_proj_c(norm_mix, c_w_in, q_norm, k_norm):
    gqk = jnp.concatenate([jnp.tile(q_norm * (WIN_SCALE * LOG2_E), WIN_HEADS), jnp.tile(k_norm, WIN_KV_HEADS)])
    idx = np.arange(MXU_TILE) // WIN_HEAD_DIM
    bd = jnp.asarray(idx[:, None] == idx[None, :], BF16)
    qk = (WIN_HEADS + WIN_KV_HEADS) * WIN_HEAD_DIM
    return dict(gn=norm_mix[None, :], wc=c_w_in[:, :qk].astype(BF16), wvt=c_w_in[:, qk:].T.astype(BF16),
                gqk=gqk[None, :], bd=bd)


def _win_kernel(sc_ref, q_ref, k_ref, vt_ref, kc_ref, vct_ref, o_ref, ot_ref, *, tq, band, n_lat):
    i = pl.program_id(1)
    start = pl.multiple_of(jnp.clip(i * tq - WINDOW, 0, n_lat - band), WINDOW)
    hd, grp = WIN_HEAD_DIM, WIN_GROUP
    qt = q_ref[...].astype(F32).T.astype(BF16)
    k_pos = start + lax.broadcasted_iota(jnp.int32, (band, tq), 0)
    q_pos = i * tq + lax.broadcasted_iota(jnp.int32, (band, tq), 1)
    bias1 = jnp.where(jnp.abs(k_pos - q_pos) <= WINDOW, 0.0, NEG_BIG)
    bias = jnp.concatenate([bias1] * grp, axis=1)
    zeros = jnp.zeros((hd, grp * tq), BF16)

    def group_inputs(kv):
        qg = jnp.concatenate([qt[(kv * grp + g) * hd:(kv * grp + g + 1) * hd, :] for g in range(grp)], axis=1)
        qg = jnp.concatenate([qg, zeros] if kv % 2 == 0 else [zeros, qg], axis=0)
        col = (kv // 2) * LANES
        sink = jnp.concatenate([jnp.full((1, tq), sc_ref[kv * grp + g], F32) for g in range(grp)], axis=1)
        return qg, col, sink

    def weighted_values(kv, p_ctx, p_loc):
        return (_dot(vct_ref[kv * hd:(kv + 1) * hd, :], p_ctx.astype(BF16))
                + _dot(vt_ref[kv * hd:(kv + 1) * hd, pl.ds(start, band)], p_loc.astype(BF16)))

    def regroup(ot):
        return jnp.concatenate([ot[:, g * tq:(g + 1) * tq] for g in range(grp)], axis=0)

    def one_pass(kv):
        qg, col, sink = group_inputs(kv)
        p_ctx = jnp.exp2(_dot(kc_ref[:, col:col + LANES], qg))
        p_loc = jnp.exp2(_dot(k_ref[pl.ds(start, band), col:col + LANES], qg) + bias)
        den = (jnp.sum(p_loc, axis=0, keepdims=True) + jnp.sum(p_ctx, axis=0, keepdims=True)
               + jnp.exp2(sink))
        ot_ref[kv * grp * hd:(kv + 1) * grp * hd, :] = regroup(weighted_values(kv, p_ctx, p_loc) / den)
        return den

    def two_pass(kv):
        qg, col, sink = group_inputs(kv)
        s_ctx = _dot(kc_ref[:, col:col + LANES], qg)
        s_loc = _dot(k_ref[pl.ds(start, band), col:col + LANES], qg) + bias
        m = jnp.maximum(jnp.maximum(jnp.max(s_loc, axis=0, keepdims=True),
                                    jnp.max(s_ctx, axis=0, keepdims=True)), sink)
        p_loc = jnp.exp2(s_loc - m)
        p_ctx = jnp.exp2(s_ctx - m)
        den = (jnp.sum(p_loc, axis=0, keepdims=True) + jnp.sum(p_ctx, axis=0, keepdims=True)
               + jnp.exp2(sink - m))
        ot_ref[kv * grp * hd:(kv + 1) * grp * hd, :] = regroup(weighted_values(kv, p_ctx, p_loc) / den)

    lo = hi = None
    for kv in range(WIN_KV_HEADS):
        den = one_pass(kv)
        lo = den if lo is None else jnp.minimum(lo, den)
        hi = den if hi is None else jnp.maximum(hi, den)
    unsafe = jnp.logical_or(jnp.min(lo) < SOFTMAX_DEN_MIN, jnp.max(hi) > SOFTMAX_DEN_MAX)

    @pl.when(unsafe)
    def _():
        for kv in range(WIN_KV_HEADS):
            two_pass(kv)

    o_ref[...] = ot_ref[...].T.astype(BF16)


def _win_attention(q, k, vt, k_ctx, vt_ctx, sink, *, tq):
    b, n, qw = q.shape
    n_c = k_ctx.shape[1]
    kw = k.shape[2]
    band = tq + 2 * WINDOW
    full = lambda r, w: pl.BlockSpec((None, r, w), lambda bi, i: (bi, 0, 0))
    return pl.pallas_call(
        functools.partial(_win_kernel, tq=tq, band=band, n_lat=n),
        grid=(b, n // tq),
        in_specs=[pl.BlockSpec(memory_space=pltpu.SMEM),
                  pl.BlockSpec((None, tq, qw), lambda bi, i: (bi, i, 0)),
                  full(n, kw), full(kw, n), full(n_c, kw), full(kw, n_c)],
        out_specs=pl.BlockSpec((None, tq, qw), lambda bi, i: (bi, i, 0)),
        out_shape=jax.ShapeDtypeStruct((b, n, qw), BF16),
        scratch_shapes=[pltpu.VMEM((qw, tq), F32)],
        compiler_params=_params("parallel", "arbitrary"),
        name="win_attention",
    )(sink.astype(F32) * LOG2_E, q, k, vt, k_ctx, vt_ctx)


def _post_kernel(*refs, s5_width, n_chunks):
    if s5_width:
        (h_ref, g_ref, sh_ref, sc_ref, g2_ref, gn_ref, yg_ref, o_ref, wglu_ref, bglu_ref, wo_ref,
         wg_ref, wu_ref, wd_ref, out_ref, a_ref, acc_ref) = refs
        yg = yg_ref[...]
        s5 = yg * jax.nn.sigmoid(_dot(yg.astype(BF16), wglu_ref[...]) + bglu_ref[...])
        mix = _dot(s5.astype(BF16), wo_ref[:s5_width, :]) + _dot(o_ref[...], wo_ref[s5_width:, :])
    else:
        (h_ref, g_ref, sh_ref, sc_ref, g2_ref, gn_ref, o_ref, wo_ref,
         wg_ref, wu_ref, wd_ref, out_ref, a_ref, acc_ref) = refs
        mix = _dot(o_ref[...], wo_ref[...])
    h1 = h_ref[...] + g_ref[...] * mix
    a_ref[...] = (_rms(h1, gn_ref[...]) * (1.0 + sc_ref[...]) + sh_ref[...]).astype(BF16)
    acc_ref[...] = jnp.zeros_like(acc_ref)

    def body(c, _):
        a = a_ref[...]
        cols = pl.ds(pl.multiple_of(c * MXU_TILE, MXU_TILE), MXU_TILE)
        act = _silu(_dot(a, wg_ref[:, cols])) * _dot(a, wu_ref[:, cols])
        acc_ref[...] += _dot(act.astype(BF16), wd_ref[cols, :])
        return 0

    lax.fori_loop(0, n_chunks, body, 0, unroll=True)
    out_ref[...] = h1 + g2_ref[...] * acc_ref[...]


def _post(h, mods, ctx_row, tm, wts, ffn, layer, o, yg=None):
    b, t, d = h.shape
    tile = lambda w: pl.BlockSpec((None, tm, w), lambda bi, i: (bi, i, 0))
    s5_width = 0 if yg is None else yg.shape[2]
    in_specs = [tile(d)] + _mod_specs(d, (2, 3, 4, 5), ctx_row) + [_const_spec(wts["gn"].shape)]
    args = [h, mods, mods, mods, mods, wts["gn"]]
    if yg is not None:
        in_specs += [tile(s5_width), tile(o.shape[2]), _const_spec(wts["wglu"].shape),
                     _const_spec(wts["bglu"].shape)]
        args += [yg, o, wts["wglu"], wts["bglu"]]
    else:
        in_specs += [tile(o.shape[2])]
        args += [o]
    in_specs.append(_const_spec(wts["wo"].shape))
    args.append(wts["wo"])
    for k in ("wg", "wu", "wd"):
        in_specs.append(_const_spec(ffn[k].shape, layer))
        args.append(ffn[k])
    return pl.pallas_call(
        functools.partial(_post_kernel, s5_width=s5_width, n_chunks=ffn["wg"].shape[2] // MXU_TILE),
        grid=(b, t // tm),
        in_specs=in_specs,
        out_specs=tile(d),
        out_shape=jax.ShapeDtypeStruct((b, t, d), F32),
        scratch_shapes=[pltpu.VMEM((tm, d), BF16), pltpu.VMEM((tm, d), F32)],
        compiler_params=_params("parallel", "parallel"),
        name="post_ffn",
    )(*args)


def _prep_post(norm_ffn, w_out, w_glu=None, b_glu=None):
    wts = dict(gn=norm_ffn[None, :], wo=w_out.astype(BF16))
    if w_glu is not None:
        wts.update(wglu=w_glu.astype(BF16), bglu=b_glu[None, :])
    return wts


def kernel(x, c, ctx, c_ctx, ada_w, ada_b, norm_mix, norm_ffn, ffn_w_gate, ffn_w_up, ffn_w_down,
           a_w_in, a_w_out, s5_lam_re, s5_lam_im, s5_log_step, s5_b_re, s5_b_im, s5_c_re, s5_c_im,
           s5_d, s5_w_glu, s5_b_glu, mla_qa_norm, mla_w_q_b, mla_kva_norm, mla_w_kv_b,
           mla_q_norm, mla_k_norm, c_w_in, c_w_out, c_q_norm, c_k_norm, c_sink):
    b, n, d = x.shape
    n_c = ctx.shape[1]
    depth = ada_w.shape[0]
    assert b + 1 <= MOD_ROWS and n % max(TM_PROJ_C, TK_MLA) == 0 and n_c % (S5_CHUNK * SUBLANES) == 0
    rows = n // GRID_W
    tm_ctx = n_c

    cond = jnp.zeros((MOD_ROWS, d), F32).at[:b].set(c).at[b].set(c_ctx)
    mods = _ada_modulation(cond, ada_w, ada_b)
    mods = mods.reshape(depth, MOD_ROWS, N_MOD, 1, d)

    cos_a, sin_a = _grid_rope_tables(rows, MLA_ROPE)
    cs_a_lat = np.concatenate([cos_a, sin_a], axis=1)
    cs_a_ctx = np.concatenate([np.ones((n_c, MLA_ROPE), np.float32), np.zeros((n_c, MLA_ROPE), np.float32)], axis=1)
    cos_c, sin_c = _grid_rope_tables(rows, WIN_HEAD_DIM)
    cos_c2, sin_c2 = np.tile(cos_c, (1, 2)), np.tile(sin_c, (1, 2))
    one_c, zero_c = np.ones((n_c, LANES), np.float32), np.zeros((n_c, LANES), np.float32)
    assert ffn_w_gate.shape[2] % MXU_TILE == 0
    ffn = dict(wg=ffn_w_gate.astype(BF16), wu=ffn_w_up.astype(BF16), wd=ffn_w_down.astype(BF16))

    h_ctx, h_lat = ctx, x
    for i in range(depth):
        need_ctx = i < depth - 1
        j = i // 2
        m_i = mods[i]
        if i % 2 == 0:
            pw = _prep_proj_a(norm_mix[i], a_w_in[j], mla_qa_norm[j], mla_w_q_b[j], mla_kva_norm[j],
                              mla_w_kv_b[j], mla_q_norm[j], mla_k_norm[j])
            u_l, q_l, k_l, vt_l = _proj_a(h_lat, m_i, None, TM_PROJ_A, pw, cs_a_lat)
            u_c, q_c, k_c, vt_c = _proj_a(h_ctx, m_i, b, tm_ctx, pw, cs_a_ctx)
            tables = _s5_tables(s5_lam_re[j], s5_lam_im[j], s5_log_step[j], s5_b_re[j], s5_b_im[j],
                                s5_c_re[j], s5_c_im[j], s5_d[j])
            yg_c, yg_l = _s5(u_c, u_l, tables)
            o_l = _mla_attention(q_l, k_c, vt_c, k_l, vt_l, tq=n)
            post_w = _prep_post(norm_ffn[i], a_w_out[j], s5_w_glu[j], s5_b_glu[j])
            h_lat_new = _post(h_lat, m_i, None, TM_FFN, post_w, ffn, i, o_l, yg_l)
            if need_ctx:
                o_c = _mla_attention(q_c, k_c, vt_c, tq=n_c)
                h_ctx = _post(h_ctx, m_i, b, tm_ctx, post_w, ffn, i, o_c, yg_c)
            h_lat = h_lat_new
        else:
            pw = _prep_proj_c(norm_mix[i], c_w_in[j], c_q_norm[j], c_k_norm[j])
            q_l, k_l, vt_l = _proj_c(h_lat, m_i, None, TM_PROJ_C, pw, cos_c2, sin_c2)
            q_c, k_c, vt_c = _proj_c(h_ctx, m_i, b, tm_ctx, pw, one_c, zero_c)
            o_l = _win_attention(q_l, k_l, vt_l, k_c, vt_c, c_sink[j], tq=TQ_WIN)
            post_w = _prep_post(norm_ffn[i], c_w_out[j])
            h_lat_new = _post(h_lat, m_i, None, TM_FFN, post_w, ffn, i, o_l)
            if need_ctx:
                raise NotImplementedError("context queries of a windowed layer")
            h_lat = h_lat_new
    return h_lat
```

```python
import functools
import math

import jax
import jax.numpy as jnp
import numpy as np
from jax import lax
from jax.experimental import pallas as pl
from jax.experimental.pallas import tpu as pltpu

F32 = jnp.float32
BF16 = jnp.bfloat16

GRID_W = 64
NORM_EPS = 1e-6
ROPE_THETA = 10000.0
N_MOD = 6
S5_GROUP_DIM = 16
S5_STATE = 64
S5_CHUNK = 16
MLA_HEADS = 4
MLA_NOPE = 128
MLA_ROPE = 64
MLA_QK_DIM = MLA_NOPE + MLA_ROPE
MLA_V = 128
MLA_Q_RANK = 384
MLA_KV_RANK = 256
MLA_SCALE = MLA_QK_DIM ** -0.5
MLA_HEAD_PAD = 256
WIN_HEADS = 16
WIN_KV_HEADS = 4
WIN_GROUP = WIN_HEADS // WIN_KV_HEADS
WIN_HEAD_DIM = 64
WINDOW = 128
WIN_SCALE = WIN_HEAD_DIM ** -0.5
LANES = 128
SUBLANES = 8
MXU_TILE = 256
VMEM_BYTES_V7X = 64 * 1024 * 1024
VMEM_LIMIT_BYTES = (VMEM_BYTES_V7X * 3) // 4
NEG_BIG = -1e30
LOG2_E = math.log2(math.e)
TM_FFN = 512
TM_PROJ_A = 512
TM_PROJ_C = 1024
TQ_WIN = 256
TK_MLA = 1024
TN_ADA = 1024
PROJ_ROWS = 256
SOFTMAX_DEN_MIN = 2.0 ** -60
SOFTMAX_DEN_MAX = 2.0 ** 60
S5_BLOCK_PAIRS = 4
MLA_Q_BLOCK = 256
MLA_K_PIECE = 256
MOD_ROWS = 8


def _dot(a, b):
    return jnp.dot(a, b, preferred_element_type=F32)


def _dot_nt(a, b):
    return lax.dot_general(a, b, (((1,), (1,)), ((), ())), preferred_element_type=F32)


def _split_bf16(x):
    hi = x.astype(BF16)
    lo = (x - hi.astype(F32)).astype(BF16)
    return hi, lo


def _rms(x, gain):
    return x * lax.rsqrt(jnp.mean(x * x, axis=-1, keepdims=True) + NORM_EPS) * gain


def _silu(x):
    return x * jax.nn.sigmoid(x)


def _params(*sem):
    return pltpu.CompilerParams(dimension_semantics=sem, vmem_limit_bytes=VMEM_LIMIT_BYTES)


def _const_spec(shape, layer=None):
    if layer is None:
        nd = len(shape)
        return pl.BlockSpec(shape, lambda *_: (0,) * nd, pipeline_mode=pl.Buffered(1))
    nd = len(shape) - 1
    return pl.BlockSpec((None,) + tuple(shape[1:]), lambda *_: (layer,) + (0,) * nd,
                        pipeline_mode=pl.Buffered(1))


def _ada_kernel(cond_ref, w_ref, b_ref, o_ref):
    s = _silu(cond_ref[...])
    s_hi, s_lo = _split_bf16(s)
    w_hi, w_lo = _split_bf16(w_ref[...])
    o_ref[...] = _dot(s_hi, w_hi) + _dot(s_lo, w_hi) + _dot(s_hi, w_lo) + b_ref[...]


def _ada_modulation(cond, ada_w, ada_b):
    depth, d, n = ada_w.shape
    tn = TN_ADA
    return pl.pallas_call(
        _ada_kernel,
        grid=(depth, n // tn),
        in_specs=[pl.BlockSpec((MOD_ROWS, d), lambda i, j: (0, 0)),
                  pl.BlockSpec((None, d, tn), lambda i, j: (i, 0, j)),
                  pl.BlockSpec((None, 1, tn), lambda i, j: (i, 0, j))],
        out_specs=pl.BlockSpec((None, MOD_ROWS, tn), lambda i, j: (i, 0, j)),
        out_shape=jax.ShapeDtypeStruct((depth, MOD_ROWS, n), F32),
        compiler_params=_params("arbitrary", "arbitrary"),
        name="ada_modulation",
    )(cond, ada_w, ada_b.reshape(depth, 1, n))


def _mod_specs(d, slots, ctx_row):
    def make(slot):
        if ctx_row is None:
            return pl.BlockSpec((None, None, 1, d), lambda b, i: (b, slot, 0, 0))
        return pl.BlockSpec((None, None, 1, d), lambda b, i: (ctx_row, slot, 0, 0))
    return [make(s) for s in slots]


def _grid_rope_tables(rows, rot_dim):
    n_freq = rot_dim // 4
    inv_freq = np.power(np.float32(ROPE_THETA), -np.arange(n_freq, dtype=np.float32) / np.float32(n_freq))
    ang_r = np.arange(rows, dtype=np.float32)[:, None] * inv_freq.astype(np.float32)
    ang_c = np.arange(GRID_W, dtype=np.float32)[:, None] * inv_freq.astype(np.float32)

    def expand(r, c):
        r = np.broadcast_to(r[:, None, :], (rows, GRID_W, n_freq))
        c = np.broadcast_to(c[None, :, :], (rows, GRID_W, n_freq))
        return np.concatenate([r, r, c, c], axis=-1).reshape(rows * GRID_W, rot_dim)

    return expand(np.cos(ang_r), np.cos(ang_c)), expand(np.sin(ang_r), np.sin(ang_c))


def _rot_perm_sign(rot_dim):
    q = rot_dim // 4
    idx = np.arange(rot_dim)
    perm = np.where((idx // q) % 2 == 0, idx + q, idx - q)
    sign = np.where((idx // q) % 2 == 0, -1.0, 1.0).astype(np.float32)
    return perm, sign


def _proj_a_kernel(h_ref, sh_ref, sc_ref, gn_ref, w1_ref, gqa_ref, wq_ref, gkv_ref, wkv_ref, wvt_ref,
                   gq_ref, gk_ref, cs_ref, msk_ref, u_ref, q_ref, k_ref, vt_ref):
    tm = h_ref.shape[0]
    rb = min(PROJ_ROWS, tm)
    msk = msk_ref[...]
    low_half = lax.broadcasted_iota(jnp.int32, (rb, LANES), 1) < MLA_ROPE

    def finish(xh, gain, cs, out_ref, rows, h):
        ssq = _dot((xh * xh).astype(BF16), msk)
        xn = xh * lax.rsqrt(ssq * (1.0 / MLA_QK_DIM) + NORM_EPS) * gain
        rr = xn[:, LANES:] * cs
        rot = rr + pltpu.roll(rr, MLA_ROPE, axis=1)
        base = h * MLA_HEAD_PAD
        out_ref[rows, base:base + LANES] = xn[:, :LANES].astype(BF16)
        out_ref[rows, base + LANES:base + 2 * LANES] = jnp.where(low_half, rot, 0.0).astype(BF16)

    for r in range(tm // rb):
        rows = slice(r * rb, (r + 1) * rb)
        a = _rms(h_ref[rows, :], gn_ref[...]) * (1.0 + sc_ref[...]) + sh_ref[...]
        p1 = _dot(a.astype(BF16), w1_ref[...])
        s5w = u_ref.shape[1]
        u_ref[rows, :] = p1[:, :s5w]
        cq = p1[:, s5w:s5w + MLA_Q_RANK]
        ckv = p1[:, s5w + MLA_Q_RANK:s5w + MLA_Q_RANK + MLA_KV_RANK]
        krr = p1[:, s5w + MLA_Q_RANK + MLA_KV_RANK:]
        qb = _dot(_rms(cq, gqa_ref[...]).astype(BF16), wq_ref[...])
        ckv_n = _rms(ckv, gkv_ref[...]).astype(BF16)
        kv = _dot(ckv_n, wkv_ref[...])
        vt_ref[:, rows] = _dot_nt(wvt_ref[...], ckv_n).astype(BF16)
        cs = cs_ref[rows, :]
        for h in range(MLA_HEADS):
            base = h * MLA_HEAD_PAD
            finish(qb[:, base:base + MLA_HEAD_PAD], gq_ref[:, base:base + MLA_HEAD_PAD], cs, q_ref, rows, h)
            kh = jnp.concatenate([kv[:, h * MLA_NOPE:(h + 1) * MLA_NOPE], krr], axis=1)
            finish(kh, gk_ref[:, base:base + MLA_HEAD_PAD], cs, k_ref, rows, h)


def _proj_a(h, mods, ctx_row, tm, wts, cs):
    b, t, d = h.shape
    qw = MLA_HEADS * MLA_HEAD_PAD
    grid = (b, t // tm)
    tile = lambda w: pl.BlockSpec((None, tm, w), lambda bi, i: (bi, i, 0))
    in_specs = ([tile(d)] + _mod_specs(d, (0, 1), ctx_row)
                + [_const_spec(wts[k].shape) for k in
                   ("gn", "w1", "gqa", "wq", "gkv", "wkv", "wvt", "gq", "gk")]
                + [pl.BlockSpec((tm, LANES), lambda bi, i: (i, 0)), _const_spec(wts["msk"].shape)])
    vw = MLA_HEADS * MLA_V
    s5w = wts["w1"].shape[1] - (MLA_Q_RANK + MLA_KV_RANK + 2 * MLA_ROPE)
    return pl.pallas_call(
        _proj_a_kernel,
        grid=grid,
        in_specs=in_specs,
        out_specs=[tile(s5w), tile(qw), tile(qw), pl.BlockSpec((None, vw, tm), lambda bi, i: (bi, 0, i))],
        out_shape=[jax.ShapeDtypeStruct((b, t, s5w), F32),
                   jax.ShapeDtypeStruct((b, t, qw), BF16),
                   jax.ShapeDtypeStruct((b, t, qw), BF16),
                   jax.ShapeDtypeStruct((b, vw, t), BF16)],
        compiler_params=_params("parallel", "parallel"),
        name="proj_a",
    )(h, mods, mods, wts["gn"], wts["w1"], wts["gqa"], wts["wq"], wts["gkv"], wts["wkv"], wts["wvt"],
      wts["gq"], wts["gk"], cs, wts["msk"])


def _prep_proj_a(norm_mix, a_w_in, qa_norm, w_q_b, kva_norm, w_kv_b, q_norm, k_norm):
    perm, sign = _rot_perm_sign(MLA_ROPE)
    s5w = a_w_in.shape[1] - (MLA_Q_RANK + MLA_KV_RANK + MLA_ROPE)
    assert s5w % LANES == 0 and MLA_Q_RANK % LANES == 0 and MLA_KV_RANK % LANES == 0
    kr = a_w_in[:, -MLA_ROPE:]
    w1 = jnp.concatenate([a_w_in, kr[:, perm] * sign], axis=1).astype(BF16)
    wq = w_q_b.reshape(MLA_Q_RANK, MLA_HEADS, MLA_QK_DIM)
    rope = wq[:, :, MLA_NOPE:]
    wq = jnp.concatenate([wq, rope[:, :, perm] * sign], axis=2)
    wq = wq.reshape(MLA_Q_RANK, MLA_HEADS * MLA_HEAD_PAD).astype(BF16)
    wkv3 = w_kv_b.reshape(MLA_KV_RANK, MLA_HEADS, MLA_NOPE + MLA_V)
    wkv = wkv3[:, :, :MLA_NOPE].reshape(MLA_KV_RANK, -1).astype(BF16)
    wvt = wkv3[:, :, MLA_NOPE:].reshape(MLA_KV_RANK, -1).T.astype(BF16)

    def head_gain(g, scale):
        gb = jnp.concatenate([g, g[MLA_NOPE:][perm]]) * scale
        return jnp.tile(gb, MLA_HEADS)[None, :]

    rows = np.arange(MLA_HEAD_PAD)[:, None] < MLA_QK_DIM
    msk = jnp.asarray(np.broadcast_to(rows, (MLA_HEAD_PAD, MLA_HEAD_PAD)), BF16)
    return dict(gn=norm_mix[None, :], w1=w1, gqa=qa_norm[None, :], wq=wq, gkv=kva_norm[None, :],
                wkv=wkv, wvt=wvt, gq=head_gain(q_norm, MLA_SCALE * LOG2_E), gk=head_gain(k_norm, 1.0),
                msk=msk)


def _s5_kernel(uc_ref, ul_ref, sel_in_ref, sel_out_ref, toep_ref, bst_ref, cst_ref, a_ref, d_ref,
               yc_ref, yl_ref, ub_ref, x_ref, z_ref, sin_ref, yb_ref, *, n_ctx, n_all):
    L, S = S5_CHUNK, S5_GROUP_DIM
    n_lat = n_all - n_ctx
    npair = S5_BLOCK_PAIRS
    half = SUBLANES * LANES
    for tl in range(L):
        u = jnp.concatenate([uc_ref[pl.ds(tl, n_ctx, stride=L), :], ul_ref[pl.ds(tl, n_lat, stride=L), :]], axis=0)
        ub_ref[:, tl * LANES:(tl + 1) * LANES] = u.astype(BF16)
    t_ctx, t_all = n_ctx // SUBLANES, n_all // SUBLANES
    row = lax.broadcasted_iota(jnp.int32, (SUBLANES, LANES), 0)

    def cmul(a_re, a_im, b_re, b_im):
        return a_re * b_re - a_im * b_im, a_re * b_im + a_im * b_re

    def tile_scan(z_re, z_im, c_re, c_im, a_re, a_im, fwd):
        for sft in (1, 2, 4):
            k = sft - 1 if fwd else SUBLANES - sft
            p_re, p_im = a_re[k:k + 1, :], a_im[k:k + 1, :]
            amt = sft if fwd else SUBLANES - sft
            keep = (row >= sft) if fwd else (row < SUBLANES - sft)
            s_re = jnp.where(keep, pltpu.roll(z_re, amt, axis=0), 0.0)
            s_im = jnp.where(keep, pltpu.roll(z_im, amt, axis=0), 0.0)
            m_re, m_im = cmul(p_re, p_im, s_re, s_im)
            z_re, z_im = z_re + m_re, z_im + m_im
        m_re, m_im = cmul(a_re, a_im, c_re, c_im)
        s_re, s_im = z_re + m_re, z_im + m_im
        edge = (row == 0) if fwd else (row == SUBLANES - 1)
        amt = 1 if fwd else SUBLANES - 1
        in_re = jnp.where(edge, c_re, pltpu.roll(s_re, amt, axis=0))
        in_im = jnp.where(edge, c_im, pltpu.roll(s_im, amt, axis=0))
        last = SUBLANES - 1 if fwd else 0
        return in_re, in_im, s_re[last:last + 1, :], s_im[last:last + 1, :]

    def relayout_items(pairs):
        items = []
        for pp in pairs:
            for hh in range(2):
                def sel(pp=pp, hh=hh):
                    xs = _dot(ub_ref[:, hh * half:(hh + 1) * half], sel_in_ref[pp]).astype(BF16)
                    x_ref[pp, :, hh * LANES:(hh + 1) * LANES] = xs[:, :LANES]
                    x_ref[pp, :, MXU_TILE + hh * LANES:MXU_TILE + (hh + 1) * LANES] = xs[:, LANES:]
                items.append(sel)

            def drive(pp=pp):
                z_ref[pp] = _dot(x_ref[pp], bst_ref[pp])
            items.append(drive)
        return items

    def readout_items(pairs):
        items = []
        for pp in pairs:
            def readout(pp=pp):
                x = x_ref[pp]
                y = jnp.concatenate([_dot(x[:, :MXU_TILE], toep_ref[pp, 0]), _dot(x[:, MXU_TILE:], toep_ref[pp, 1])],
                                    axis=1)
                y = y + _dot(sin_ref[pp].astype(BF16), cst_ref[pp]) + x.astype(F32) * d_ref[pp]
                yg = jax.nn.gelu(y).astype(BF16)
                for gl in range(2):
                    for hh in range(2):
                        g8 = 2 * pp + gl
                        col = gl * MXU_TILE + hh * LANES
                        yb_ref[hh, :, g8 * LANES:(g8 + 1) * LANES] = yg[:, col:col + LANES]
            items.append(readout)
        return items

    def scan_with(pairs, items):
        zero = jnp.zeros((1, LANES), F32)
        carry = {pp: (zero, zero, zero, zero) for pp in pairs}
        every = max(1, t_all // max(1, len(items)))
        pending = list(items)
        for it in range(t_all):
            jt = t_ctx - 1 - it if it < t_ctx else t_all + t_ctx - 1 - it
            rf, rb = it * SUBLANES, jt * SUBLANES
            for pp in pairs:
                cf_re, cf_im, cb_re, cb_im = carry[pp]
                zf = z_ref[pp, rf:rf + SUBLANES, 0:2 * LANES]
                zb = z_ref[pp, rb:rb + SUBLANES, 2 * LANES:4 * LANES]
                f_re, f_im, cf_re, cf_im = tile_scan(zf[:, :LANES], zf[:, LANES:], cf_re, cf_im,
                                                     a_ref[pp, 0], a_ref[pp, 1], True)
                b_re, b_im, cb_re, cb_im = tile_scan(zb[:, :LANES], zb[:, LANES:], cb_re, cb_im,
                                                     a_ref[pp, 2], a_ref[pp, 3], False)
                sin_ref[pp, rf:rf + SUBLANES, 0:2 * LANES] = jnp.concatenate([f_re, f_im], axis=1)
                sin_ref[pp, rb:rb + SUBLANES, 2 * LANES:4 * LANES] = jnp.concatenate([b_re, b_im], axis=1)
                carry[pp] = (cf_re, cf_im, cb_re, cb_im)
            if pending and (it + 1) % every == 0:
                pending.pop(0)()
        for item in pending:
            item()

    first, second = tuple(range(npair // 2)), tuple(range(npair // 2, npair))
    for item in relayout_items(first):
        item()
    scan_with(first, relayout_items(second))
    scan_with(second, readout_items(first))
    for item in readout_items(second):
        item()
    for hh in range(2):
        for kk in range(SUBLANES // 2):
            two = _dot(yb_ref[hh], sel_out_ref[kk])
            for e in range(2):
                tl = hh * SUBLANES + 2 * kk + e
                yc_ref[pl.ds(tl, n_ctx, stride=L), :] = two[:n_ctx, e * LANES:(e + 1) * LANES]
                yl_ref[pl.ds(tl, n_lat, stride=L), :] = two[n_ctx:, e * LANES:(e + 1) * LANES]


def _s5_selectors():
    S = S5_GROUP_DIM
    r = np.arange(SUBLANES * LANES)[:, None]
    c = np.arange(2 * LANES)[None, :]
    k, l = r // LANES, r % LANES
    pp = np.arange(S5_BLOCK_PAIRS)[:, None, None]
    sel_in = (k == (c % LANES) // S) & (l == 2 * S * pp + S * (c // LANES) + c % S)
    kk = np.arange(SUBLANES // 2)[:, None, None]
    sel_out = (k == (c % LANES) // S) & (l == S * (2 * kk + c // LANES) + c % S)
    return jnp.asarray(sel_in, BF16), jnp.asarray(sel_out, BF16)


def _dot_nt_f32(a, b):
    a_hi, a_lo = _split_bf16(a)
    b_hi, b_lo = _split_bf16(b)
    return _dot_nt(a_hi, b_hi) + _dot_nt(a_lo, b_hi) + _dot_nt(a_hi, b_lo)


def _s5_table_kernel(par_ref, bre_ref, bim_ref, cre_ref, cim_ref, toep_ref, bst_ref, cst_ref, a_ref):
    L, S, P = S5_CHUNK, S5_GROUP_DIM, S5_STATE
    kk = lax.broadcasted_iota(jnp.int32, (3 * SUBLANES, LANES), 0).astype(F32)
    lane = lax.broadcasted_iota(jnp.int32, (L, LANES), 1)
    pair_rows = lax.broadcasted_iota(jnp.int32, (2 * L * S, LANES), 0)
    pair_lanes = lax.broadcasted_iota(jnp.int32, (2 * L * S, LANES), 1)
    own_group = (pair_rows // (L * S)) == (pair_lanes // P)
    lane_pad = jnp.zeros((2 * P, LANES - S), F32)

    def rows_of_powers(pw, ks, groups):
        one = jnp.concatenate([jnp.broadcast_to(pw[k:k + 1, :], (S, LANES)) for k in ks], axis=0)
        return jnp.concatenate([one] * groups, axis=0) if groups > 1 else one

    def cmul(a_re, a_im, b_re, b_im):
        return a_re * b_re - a_im * b_im, a_re * b_im + a_im * b_re

    lag_tables = []
    for d in range(2):
        lam_re, lam_im = par_ref[d, 0:1, :], par_ref[d, 1:2, :]
        step = jnp.exp(par_ref[d, 2:3, :])
        ar, ai = lam_re * step, lam_im * step
        mag = jnp.exp(kk * ar)
        pw_re, pw_im = mag * jnp.cos(kk * ai), mag * jnp.sin(kk * ai)
        th = jnp.tanh(0.5 * ar)
        em1 = 2.0 * th / (1.0 - th)
        sh = jnp.sin(0.5 * ai)
        n_re = em1 * jnp.cos(ai) - 2.0 * sh * sh
        n_im = (em1 + 1.0) * jnp.sin(ai)
        den = lam_re * lam_re + lam_im * lam_im
        co_re = (n_re * lam_re + n_im * lam_im) / den
        co_im = (n_im * lam_re - n_re * lam_im) / den
        bt_re = jnp.concatenate([bre_ref[d], lane_pad], axis=1).T[:S]
        bt_im = jnp.concatenate([bim_ref[d], lane_pad], axis=1).T[:S]
        bb_re, bb_im = cmul(co_re, co_im, bt_re, bt_im)
        cc_re = jnp.concatenate([cre_ref[d, 0], cre_ref[d, 1]], axis=1)
        cc_im = jnp.concatenate([cim_ref[d, 0], cim_ref[d, 1]], axis=1)
        ks = [L - 1 - t for t in range(L)] if d == 0 else list(range(L))
        r_re, r_im = rows_of_powers(pw_re, ks, 2), rows_of_powers(pw_im, ks, 2)
        bbt_re, bbt_im = jnp.concatenate([bb_re] * (2 * L), axis=0), jnp.concatenate([bb_im] * (2 * L), axis=0)
        v_re, v_im = cmul(r_re, r_im, bbt_re, bbt_im)
        bst_ref[:, (2 * d) * LANES:(2 * d + 1) * LANES] = jnp.where(own_group, v_re, 0.0).astype(BF16)
        bst_ref[:, (2 * d + 1) * LANES:(2 * d + 2) * LANES] = jnp.where(own_group, v_im, 0.0).astype(BF16)
        ks = [t + 1 for t in range(L)] if d == 0 else [L - t for t in range(L)]
        r_re, r_im = rows_of_powers(pw_re, ks, 2), rows_of_powers(pw_im, ks, 2)
        cct_re, cct_im = jnp.concatenate([cc_re] * (2 * L), axis=0), jnp.concatenate([cc_im] * (2 * L), axis=0)
        v_re, v_im = cmul(cct_re, cct_im, r_re, r_im)
        cst_ref[(2 * d) * LANES:(2 * d + 1) * LANES, :] = jnp.where(own_group, v_re, 0.0).T.astype(BF16)
        cst_ref[(2 * d + 1) * LANES:(2 * d + 2) * LANES, :] = jnp.where(own_group, -v_im, 0.0).T.astype(BF16)
        ks = list(range(L)) if d == 0 else [L - 1 - j for j in range(L)]
        r_re, r_im = rows_of_powers(pw_re, ks, 1), rows_of_powers(pw_im, ks, 1)
        cl_re, cl_im = cmul(jnp.concatenate([cc_re] * L, axis=0), jnp.concatenate([cc_im] * L, axis=0), r_re, r_im)
        per_group = []
        for g in range(2):
            mine = (lane // P) == g
            per_group.append(_dot_nt_f32(jnp.where(mine, bb_re, 0.0), cl_re)
                             - _dot_nt_f32(jnp.where(mine, bb_im, 0.0), cl_im))
        lag_tables.append(per_group)
        row8 = lax.broadcasted_iota(jnp.int32, (SUBLANES, LANES), 0)
        n_chunks = ((row8 + 1) if d == 0 else (SUBLANES - row8)).astype(F32) * float(L)
        mag8 = jnp.exp(n_chunks * ar)
        a_ref[2 * d] = mag8 * jnp.cos(n_chunks * ai)
        a_ref[2 * d + 1] = mag8 * jnp.sin(n_chunks * ai)

    def shift_right(x, s):
        x0, x1 = x[:, :LANES], x[:, LANES:]
        a, r = divmod(s, LANES)
        r0 = pltpu.roll(x0, r, axis=1) if r else x0
        r1 = pltpu.roll(x1, r, axis=1) if r else x1
        if a == 0:
            return jnp.concatenate([jnp.where(lane >= r, r0, 0.0), jnp.where(lane >= r, r1, r0)], axis=1)
        return jnp.concatenate([jnp.zeros_like(x0), jnp.where(lane >= r, r0, 0.0)], axis=1)

    def shift_left(x, s):
        x0, x1 = x[:, :LANES], x[:, LANES:]
        a, r = divmod(s, LANES)
        r0 = pltpu.roll(x0, LANES - r, axis=1) if r else x0
        r1 = pltpu.roll(x1, LANES - r, axis=1) if r else x1
        if a == 0:
            return jnp.concatenate([jnp.where(lane < LANES - r, r0, r1), jnp.where(lane < LANES - r, r1, 0.0)], axis=1)
        return jnp.concatenate([jnp.where(lane < LANES - r, r1, 0.0), jnp.zeros_like(x0)], axis=1)

    for g in range(2):
        kf, kb = lag_tables[0][g], lag_tables[1][g]
        for tau in range(L):
            blk = shift_right(kf, S * tau) + shift_left(kb, S * (L - 1 - tau))
            toep_ref[g, tau * S:(tau + 1) * S, :] = blk.astype(BF16)


def _s5_tables(lam_re, lam_im, log_step, b_re, b_im, c_re, c_im, d_skip):
    _, G, P = lam_re.shape
    S, L = S5_GROUP_DIM, S5_CHUNK
    assert P == S5_STATE and 2 * P == LANES and 2 * L * S == 2 * MXU_TILE
    pairs = G // 2
    par = jnp.stack([lam_re.reshape(2, pairs, 2 * P), lam_im.reshape(2, pairs, 2 * P),
                     jnp.repeat(log_step, P, axis=-1).reshape(2, pairs, 2 * P)], axis=2)
    par = jnp.transpose(par, (1, 0, 2, 3)).astype(F32)
    bshape = (2, pairs, 2 * P, S)
    cshape = (2, pairs, 2, S, P)
    pw = 2 * L * S
    bspec = pl.BlockSpec((2, None, 2 * P, S), lambda g: (0, g, 0, 0))
    cspec = pl.BlockSpec((2, None, 2, S, P), lambda g: (0, g, 0, 0, 0))
    toep, bst, cst, a_chunk = pl.pallas_call(
        _s5_table_kernel,
        grid=(pairs,),
        in_specs=[pl.BlockSpec((None, 2, 3, 2 * P), lambda g: (g, 0, 0, 0)), bspec, bspec, cspec, cspec],
        out_specs=[pl.BlockSpec((None, 2, MXU_TILE, MXU_TILE), lambda g: (g, 0, 0, 0)),
                   pl.BlockSpec((None, pw, pw), lambda g: (g, 0, 0)),
                   pl.BlockSpec((None, pw, pw), lambda g: (g, 0, 0)),
                   pl.BlockSpec((None, 4, SUBLANES, LANES), lambda g: (g, 0, 0, 0))],
        out_shape=[jax.ShapeDtypeStruct((pairs, 2, MXU_TILE, MXU_TILE), BF16),
                   jax.ShapeDtypeStruct((pairs, pw, pw), BF16),
                   jax.ShapeDtypeStruct((pairs, pw, pw), BF16),
                   jax.ShapeDtypeStruct((pairs, 4, SUBLANES, LANES), F32)],
        compiler_params=_params("parallel"),
        name="s5_tables",
    )(par, b_re.reshape(bshape), b_im.reshape(bshape), c_re.reshape(cshape), c_im.reshape(cshape))
    d_pair = jnp.broadcast_to(d_skip.astype(F32).reshape(pairs, 2, 1, S), (pairs, 2, L, S))
    return toep, bst, cst, a_chunk, d_pair.reshape(pairs, 1, pw)


def _s5(u_ctx, u_lat, tables):
    toep, bst, cst, a_pow, d_pair = tables
    b, n_c, w = u_ctx.shape
    n_l = u_lat.shape[1]
    L = S5_CHUNK
    n_ctx, n_all = n_c // L, (n_c + n_l) // L
    nblk = w // LANES
    pw = 2 * L * S5_GROUP_DIM
    npair = S5_BLOCK_PAIRS
    sel_in, sel_out = _s5_selectors()
    wspec = lambda shape: pl.BlockSpec((npair,) + shape, lambda g, bi: (g,) + (0,) * len(shape),
                                       pipeline_mode=pl.Buffered(1))
    return pl.pallas_call(
        functools.partial(_s5_kernel, n_ctx=n_ctx, n_all=n_all),
        grid=(nblk, b),
        in_specs=[pl.BlockSpec((None, n_c, LANES), lambda g, bi: (bi, 0, g)),
                  pl.BlockSpec((None, n_l, LANES), lambda g, bi: (bi, 0, g)),
                  _const_spec(sel_in.shape), _const_spec(sel_out.shape),
                  wspec((2, MXU_TILE, MXU_TILE)), wspec((pw, pw)), wspec((pw, pw)),
                  wspec((4, SUBLANES, LANES)), wspec((1, pw))],
        out_specs=[pl.BlockSpec((None, n_c, LANES), lambda g, bi: (bi, 0, g)),
                   pl.BlockSpec((None, n_l, LANES), lambda g, bi: (bi, 0, g))],
        out_shape=[jax.ShapeDtypeStruct((b, n_c, w), F32), jax.ShapeDtypeStruct((b, n_l, w), F32)],
        scratch_shapes=[pltpu.VMEM((n_all, L * LANES), BF16), pltpu.VMEM((npair, n_all, pw), BF16),
                        pltpu.VMEM((npair, n_all, pw), F32), pltpu.VMEM((npair, n_all, pw), F32),
                        pltpu.VMEM((2, n_all, SUBLANES * LANES), BF16)],
        compiler_params=_params("parallel", "arbitrary"),
        name="s5_scan",
    )(u_ctx, u_lat, sel_in, sel_out, toep, bst, cst, a_pow, d_pair)


def _mla_kernel(*refs, tk, n_steps):
    if n_steps:
        q_ref, kc_ref, vc_ref, k_ref, v_ref, o_ref, qt_ref, s_ref, m_ref, l_ref, acc_ref = refs
    else:
        q_ref, kc_ref, vc_ref, o_ref, qt_ref, s_ref, m_ref, l_ref, acc_ref = refs
    tq = q_ref.shape[0]
    ncb = tq // MLA_Q_BLOCK
    kp = MLA_K_PIECE
    qt_ref[...] = q_ref[...].astype(F32).T.astype(BF16)

    def chunk(load_k, load_vt, nkeys, first):
        nr = nkeys // kp

        def score_piece(c, r):
            st = _dot(load_k(r), qt_ref[:, c * MLA_Q_BLOCK:(c + 1) * MLA_Q_BLOCK])
            s_ref[c % 2, r * kp:(r + 1) * kp, :] = st
            return jnp.max(st, axis=0, keepdims=True)

        def block_stats(c, mx):
            if first:
                return mx, None
            m_old = m_ref[:, c * MLA_Q_BLOCK:(c + 1) * MLA_Q_BLOCK]
            m_new = jnp.maximum(m_old, mx)
            return m_new, jnp.exp2(m_old - m_new)

        def prob_piece(c, r, m_new):
            p = jnp.exp2(s_ref[c % 2, r * kp:(r + 1) * kp, :] - m_new)
            return jnp.sum(p, axis=0, keepdims=True), _dot(load_vt(r), p.astype(BF16))

        def finish(c, m_new, alpha, lsum, pv):
            cols = slice(c * MLA_Q_BLOCK, (c + 1) * MLA_Q_BLOCK)
            if first:
                l_ref[:, cols] = lsum
                acc_ref[:, cols] = pv
            else:
                l_ref[:, cols] = alpha * l_ref[:, cols] + lsum
                acc_ref[:, cols] = alpha * acc_ref[:, cols] + pv
            m_ref[:, cols] = m_new

        mx = None
        for r in range(nr):
            pm = score_piece(0, r)
            mx = pm if mx is None else jnp.maximum(mx, pm)
        for c in range(ncb):
            m_new, alpha = block_stats(c, mx)
            mx = lsum = pv = None
            for r in range(nr):
                if c + 1 < ncb:
                    pm = score_piece(c + 1, r)
                    mx = pm if mx is None else jnp.maximum(mx, pm)
                ls, pvr = prob_piece(c, r, m_new)
                lsum = ls if lsum is None else lsum + ls
                pv = pvr if pv is None else pv + pvr
            finish(c, m_new, alpha, lsum, pv)

    chunk(lambda r: kc_ref[r * kp:(r + 1) * kp, :], lambda r: vc_ref[:, r * kp:(r + 1) * kp],
          kc_ref.shape[0], True)
    if n_steps:
        def body(j, _):
            off = pl.multiple_of(j * tk, tk)
            chunk(lambda r: k_ref[pl.ds(off + r * kp, kp), :], lambda r: v_ref[:, pl.ds(off + r * kp, kp)],
                  tk, False)
            return 0
        lax.fori_loop(0, n_steps, body, 0)
    o_ref[...] = (acc_ref[...] / l_ref[...]).T.astype(BF16)


def _mla_attention(q, k_ctx, vt_ctx, k_lat=None, vt_lat=None, *, tq, tk=TK_MLA):
    b, t, _ = q.shape
    n_c = k_ctx.shape[1]
    in_specs = [pl.BlockSpec((None, tq, MLA_HEAD_PAD), lambda bi, h, i: (bi, i, h)),
                pl.BlockSpec((None, n_c, MLA_HEAD_PAD), lambda bi, h, i: (bi, 0, h)),
                pl.BlockSpec((None, MLA_V, n_c), lambda bi, h, i: (bi, h, 0))]
    args = [q, k_ctx, vt_ctx]
    n_steps = 0
    if k_lat is not None:
        n_l = k_lat.shape[1]
        n_steps = n_l // tk
        in_specs += [pl.BlockSpec((None, n_l, MLA_HEAD_PAD), lambda bi, h, i: (bi, 0, h)),
                     pl.BlockSpec((None, MLA_V, n_l), lambda bi, h, i: (bi, h, 0))]
        args += [k_lat, vt_lat]
    return pl.pallas_call(
        functools.partial(_mla_kernel, tk=tk, n_steps=n_steps),
        grid=(b, MLA_HEADS, t // tq),
        in_specs=in_specs,
        out_specs=pl.BlockSpec((None, tq, MLA_V), lambda bi, h, i: (bi, i, h)),
        out_shape=jax.ShapeDtypeStruct((b, t, MLA_HEADS * MLA_V), BF16),
        scratch_shapes=[pltpu.VMEM((MLA_HEAD_PAD, tq), BF16), pltpu.VMEM((2, max(tk, n_c), MLA_Q_BLOCK), F32),
                        pltpu.VMEM((1, tq), F32), pltpu.VMEM((1, tq), F32), pltpu.VMEM((MLA_V, tq), F32)],
        compiler_params=_params("parallel", "parallel", "arbitrary"),
        name="mla_attention",
    )(*args)


def _proj_c_kernel(h_ref, sh_ref, sc_ref, gn_ref, wc_ref, wvt_ref, gqk_ref, cos_ref, sin_ref, bd_ref,
                   q_ref, k_ref, vt_ref):
    qw = WIN_HEADS * WIN_HEAD_DIM
    kw = WIN_KV_HEADS * WIN_HEAD_DIM
    bd = bd_ref[...]
    tm = h_ref.shape[0]
    rb = min(PROJ_ROWS, tm)
    lane = lax.broadcasted_iota(jnp.int32, (rb, LANES), 1)
    first_quarter = (lane % (WIN_HEAD_DIM // 2)) < (WIN_HEAD_DIM // 4)
    for r in range(tm // rb):
        rows = slice(r * rb, (r + 1) * rb)
        a = (_rms(h_ref[rows, :], gn_ref[...]) * (1.0 + sc_ref[...]) + sh_ref[...]).astype(BF16)
        p = _dot(a, wc_ref[...])
        vt_ref[:, rows] = _dot_nt(wvt_ref[...], a).astype(BF16)
        cos, sin = cos_ref[rows, :], sin_ref[rows, :]
        for j in range((qw + kw) // MXU_TILE):
            xh = p[:, j * MXU_TILE:(j + 1) * MXU_TILE]
            ssq = _dot((xh * xh).astype(BF16), bd)
            xn = xh * lax.rsqrt(ssq * (1.0 / WIN_HEAD_DIM) + NORM_EPS) * gqk_ref[:, j * MXU_TILE:(j + 1) * MXU_TILE]
            for c in range(MXU_TILE // LANES):
                xc = xn[:, c * LANES:(c + 1) * LANES]
                fwd = pltpu.roll(xc, WIN_HEAD_DIM // 4, axis=1)
                bwd = pltpu.roll(xc, LANES - WIN_HEAD_DIM // 4, axis=1)
                y = (xc * cos + jnp.where(first_quarter, -bwd, fwd) * sin).astype(BF16)
                col = j * MXU_TILE + c * LANES
                if col < qw:
                    q_ref[rows, col:col + LANES] = y
                else:
                    k_ref[rows, col - qw:col - qw + LANES] = y


def _proj_c(h, mods, ctx_row, tm, wts, cos2, sin2):
    b, t, d = h.shape
    qw = WIN_HEADS * WIN_HEAD_DIM
    kw = WIN_KV_HEADS * WIN_HEAD_DIM
    tile = lambda w: pl.BlockSpec((None, tm, w), lambda bi, i: (bi, i, 0))
    tab = pl.BlockSpec((tm, LANES), lambda bi, i: (i, 0))
    in_specs = ([tile(d)] + _mod_specs(d, (0, 1), ctx_row)
                + [_const_spec(wts[k].shape) for k in ("gn", "wc", "wvt", "gqk")]
                + [tab, tab, _const_spec(wts["bd"].shape)])
    return pl.pallas_call(
        _proj_c_kernel,
        grid=(b, t // tm),
        in_specs=in_specs,
        out_specs=[tile(qw), tile(kw), pl.BlockSpec((None, kw, tm), lambda bi, i: (bi, 0, i))],
        out_shape=[jax.ShapeDtypeStruct((b, t, qw), BF16),
                   jax.ShapeDtypeStruct((b, t, kw), BF16),
                   jax.ShapeDtypeStruct((b, kw, t), BF16)],
        compiler_params=_params("parallel", "parallel"),
        name="proj_c",
    )(h, mods, mods, wts["gn"], wts["wc"], wts["wvt"], wts["gqk"], cos2, sin2, wts["bd"])


def _prep_proj_c(norm_mix, c_w_in, q_norm, k_norm):
    gqk = jnp.concatenate([jnp.tile(q_norm * (WIN_SCALE * LOG2_E), WIN_HEADS), jnp.tile(k_norm, WIN_KV_HEADS)])
    idx = np.arange(MXU_TILE) // WIN_HEAD_DIM
    bd = jnp.asarray(idx[:, None] == idx[None, :], BF16)
    qk = (WIN_HEADS + WIN_KV_HEADS) * WIN_HEAD_DIM
    return dict(gn=norm_mix[None, :], wc=c_w_in[:, :qk].astype(BF16), wvt=c_w_in[:, qk:].T.astype(BF16),
                gqk=gqk[None, :], bd=bd)


def _win_kernel(sc_ref, q_ref, k_ref, vt_ref, kc_ref, vct_ref, o_ref, ot_ref, *, tq, band, n_lat):
    i = pl.program_id(1)
    start = pl.multiple_of(jnp.clip(i * tq - WINDOW, 0, n_lat - band), WINDOW)
    hd, grp = WIN_HEAD_DIM, WIN_GROUP
    qt = q_ref[...].astype(F32).T.astype(BF16)
    k_pos = start + lax.broadcasted_iota(jnp.int32, (band, tq), 0)
    q_pos = i * tq + lax.broadcasted_iota(jnp.int32, (band, tq), 1)
    bias1 = jnp.where(jnp.abs(k_pos - q_pos) <= WINDOW, 0.0, NEG_BIG)
    bias = jnp.concatenate([bias1] * grp, axis=1)
    zeros = jnp.zeros((hd, grp * tq), BF16)

    def group_inputs(kv):
        qg = jnp.concatenate([qt[(kv * grp + g) * hd:(kv * grp + g + 1) * hd, :] for g in range(grp)], axis=1)
        qg = jnp.concatenate([qg, zeros] if kv % 2 == 0 else [zeros, qg], axis=0)
        col = (kv // 2) * LANES
        sink = jnp.concatenate([jnp.full((1, tq), sc_ref[kv * grp + g], F32) for g in range(grp)], axis=1)
        return qg, col, sink

    def weighted_values(kv, p_ctx, p_loc):
        return (_dot(vct_ref[kv * hd:(kv + 1) * hd, :], p_ctx.astype(BF16))
                + _dot(vt_ref[kv * hd:(kv + 1) * hd, pl.ds(start, band)], p_loc.astype(BF16)))

    def regroup(ot):
        return jnp.concatenate([ot[:, g * tq:(g + 1) * tq] for g in range(grp)], axis=0)

    def one_pass(kv):
        qg, col, sink = group_inputs(kv)
        p_ctx = jnp.exp2(_dot(kc_ref[:, col:col + LANES], qg))
        p_loc = jnp.exp2(_dot(k_ref[pl.ds(start, band), col:col + LANES], qg) + bias)
        den = (jnp.sum(p_loc, axis=0, keepdims=True) + jnp.sum(p_ctx, axis=0, keepdims=True)
               + jnp.exp2(sink))
        ot_ref[kv * grp * hd:(kv + 1) * grp * hd, :] = regroup(weighted_values(kv, p_ctx, p_loc) / den)
        return den

    def two_pass(kv):
        qg, col, sink = group_inputs(kv)
        s_ctx = _dot(kc_ref[:, col:col + LANES], qg)
        s_loc = _dot(k_ref[pl.ds(start, band), col:col + LANES], qg) + bias
        m = jnp.maximum(jnp.maximum(jnp.max(s_loc, axis=0, keepdims=True),
                                    jnp.max(s_ctx, axis=0, keepdims=True)), sink)
        p_loc = jnp.exp2(s_loc - m)
        p_ctx = jnp.exp2(s_ctx - m)
        den = (jnp.sum(p_loc, axis=0, keepdims=True) + jnp.sum(p_ctx, axis=0, keepdims=True)
               + jnp.exp2(sink - m))
        ot_ref[kv * grp * hd:(kv + 1) * grp * hd, :] = regroup(weighted_values(kv, p_ctx, p_loc) / den)

    lo = hi = None
    for kv in range(WIN_KV_HEADS):
        den = one_pass(kv)
        lo = den if lo is None else jnp.minimum(lo, den)
        hi = den if hi is None else jnp.maximum(hi, den)
    unsafe = jnp.logical_or(jnp.min(lo) < SOFTMAX_DEN_MIN, jnp.max(hi) > SOFTMAX_DEN_MAX)

    @pl.when(unsafe)
    def _():
        for kv in range(WIN_KV_HEADS):
            two_pass(kv)

    o_ref[...] = ot_ref[...].T.astype(BF16)


def _win_attention(q, k, vt, k_ctx, vt_ctx, sink, *, tq):
    b, n, qw = q.shape
    n_c = k_ctx.shape[1]
    kw = k.shape[2]
    band = tq + 2 * WINDOW
    full = lambda r, w: pl.BlockSpec((None, r, w), lambda bi, i: (bi, 0, 0))
    return pl.pallas_call(
        functools.partial(_win_kernel, tq=tq, band=band, n_lat=n),
        grid=(b, n // tq),
        in_specs=[pl.BlockSpec(memory_space=pltpu.SMEM),
                  pl.BlockSpec((None, tq, qw), lambda bi, i: (bi, i, 0)),
                  full(n, kw), full(kw, n), full(n_c, kw), full(kw, n_c)],
        out_specs=pl.BlockSpec((None, tq, qw), lambda bi, i: (bi, i, 0)),
        out_shape=jax.ShapeDtypeStruct((b, n, qw), BF16),
        scratch_shapes=[pltpu.VMEM((qw, tq), F32)],
        compiler_params=_params("parallel", "arbitrary"),
        name="win_attention",
    )(sink.astype(F32) * LOG2_E, q, k, vt, k_ctx, vt_ctx)


def _post_kernel(*refs, s5_width, n_chunks):
    if s5_width:
        (h_ref, g_ref, sh_ref, sc_ref, g2_ref, gn_ref, yg_ref, o_ref, wglu_ref, bglu_ref, wo_ref,
         wg_ref, wu_ref, wd_ref, out_ref, a_ref, acc_ref) = refs
        yg = yg_ref[...]
        s5 = yg * jax.nn.sigmoid(_dot(yg.astype(BF16), wglu_ref[...]) + bglu_ref[...])
        mix = _dot(s5.astype(BF16), wo_ref[:s5_width, :]) + _dot(o_ref[...], wo_ref[s5_width:, :])
    else:
        (h_ref, g_ref, sh_ref, sc_ref, g2_ref, gn_ref, o_ref, wo_ref,
         wg_ref, wu_ref, wd_ref, out_ref, a_ref, acc_ref) = refs
        mix = _dot(o_ref[...], wo_ref[...])
    h1 = h_ref[...] + g_ref[...] * mix
    a_ref[...] = (_rms(h1, gn_ref[...]) * (1.0 + sc_ref[...]) + sh_ref[...]).astype(BF16)
    acc_ref[...] = jnp.zeros_like(acc_ref)

    def body(c, _):
        a = a_ref[...]
        cols = pl.ds(pl.multiple_of(c * MXU_TILE, MXU_TILE), MXU_TILE)
        act = _silu(_dot(a, wg_ref[:, cols])) * _dot(a, wu_ref[:, cols])
        acc_ref[...] += _dot(act.astype(BF16), wd_ref[cols, :])
        return 0

    lax.fori_loop(0, n_chunks, body, 0, unroll=True)
    out_ref[...] = h1 + g2_ref[...] * acc_ref[...]


def _post(h, mods, ctx_row, tm, wts, ffn, layer, o, yg=None):
    b, t, d = h.shape
    tile = lambda w: pl.BlockSpec((None, tm, w), lambda bi, i: (bi, i, 0))
    s5_width = 0 if yg is None else yg.shape[2]
    in_specs = [tile(d)] + _mod_specs(d, (2, 3, 4, 5), ctx_row) + [_const_spec(wts["gn"].shape)]
    args = [h, mods, mods, mods, mods, wts["gn"]]
    if yg is not None:
        in_specs += [tile(s5_width), tile(o.shape[2]), _const_spec(wts["wglu"].shape),
                     _const_spec(wts["bglu"].shape)]
        args += [yg, o, wts["wglu"], wts["bglu"]]
    else:
        in_specs += [tile(o.shape[2])]
        args += [o]
    in_specs.append(_const_spec(wts["wo"].shape))
    args.append(wts["wo"])
    for k in ("wg", "wu", "wd"):
        in_specs.append(_const_spec(ffn[k].shape, layer))
        args.append(ffn[k])
    return pl.pallas_call(
        functools.partial(_post_kernel, s5_width=s5_width, n_chunks=ffn["wg"].shape[2] // MXU_TILE),
        grid=(b, t // tm),
        in_specs=in_specs,
        out_specs=tile(d),
        out_shape=jax.ShapeDtypeStruct((b, t, d), F32),
        scratch_shapes=[pltpu.VMEM((tm, d), BF16), pltpu.VMEM((tm, d), F32)],
        compiler_params=_params("parallel", "parallel"),
        name="post_ffn",
    )(*args)


def _prep_post(norm_ffn, w_out, w_glu=None, b_glu=None):
    wts = dict(gn=norm_ffn[None, :], wo=w_out.astype(BF16))
    if w_glu is not None:
        wts.update(wglu=w_glu.astype(BF16), bglu=b_glu[None, :])
    return wts


def kernel(x, c, ctx, c_ctx, ada_w, ada_b, norm_mix, norm_ffn, ffn_w_gate, ffn_w_up, ffn_w_down,
           a_w_in, a_w_out, s5_lam_re, s5_lam_im, s5_log_step, s5_b_re, s5_b_im, s5_c_re, s5_c_im,
           s5_d, s5_w_glu, s5_b_glu, mla_qa_norm, mla_w_q_b, mla_kva_norm, mla_w_kv_b,
           mla_q_norm, mla_k_norm, c_w_in, c_w_out, c_q_norm, c_k_norm, c_sink):
    b, n, d = x.shape
    n_c = ctx.shape[1]
    depth = ada_w.shape[0]
    assert b + 1 <= MOD_ROWS and n % max(TM_PROJ_C, TK_MLA) == 0 and n_c % (S5_CHUNK * SUBLANES) == 0
    rows = n // GRID_W
    tm_ctx = n_c

    cond = jnp.zeros((MOD_ROWS, d), F32).at[:b].set(c).at[b].set(c_ctx)
    mods = _ada_modulation(cond, ada_w, ada_b)
    mods = mods.reshape(depth, MOD_ROWS, N_MOD, 1, d)

    cos_a, sin_a = _grid_rope_tables(rows, MLA_ROPE)
    cs_a_lat = np.concatenate([cos_a, sin_a], axis=1)
    cs_a_ctx = np.concatenate([np.ones((n_c, MLA_ROPE), np.float32), np.zeros((n_c, MLA_ROPE), np.float32)], axis=1)
    cos_c, sin_c = _grid_rope_tables(rows, WIN_HEAD_DIM)
    cos_c2, sin_c2 = np.tile(cos_c, (1, 2)), np.tile(sin_c, (1, 2))
    one_c, zero_c = np.ones((n_c, LANES), np.float32), np.zeros((n_c, LANES), np.float32)
    assert ffn_w_gate.shape[2] % MXU_TILE == 0
    ffn = dict(wg=ffn_w_gate.astype(BF16), wu=ffn_w_up.astype(BF16), wd=ffn_w_down.astype(BF16))

    h_ctx, h_lat = ctx, x
    for i in range(depth):
        need_ctx = i < depth - 1
        j = i // 2
        m_i = mods[i]
        if i % 2 == 0:
            pw = _prep_proj_a(norm_mix[i], a_w_in[j], mla_qa_norm[j], mla_w_q_b[j], mla_kva_norm[j],
                              mla_w_kv_b[j], mla_q_norm[j], mla_k_norm[j])
            u_l, q_l, k_l, vt_l = _proj_a(h_lat, m_i, None, TM_PROJ_A, pw, cs_a_lat)
            u_c, q_c, k_c, vt_c = _proj_a(h_ctx, m_i, b, tm_ctx, pw, cs_a_ctx)
            tables = _s5_tables(s5_lam_re[j], s5_lam_im[j], s5_log_step[j], s5_b_re[j], s5_b_im[j],
                                s5_c_re[j], s5_c_im[j], s5_d[j])
            yg_c, yg_l = _s5(u_c, u_l, tables)
            o_l = _mla_attention(q_l, k_c, vt_c, k_l, vt_l, tq=n)
            post_w = _prep_post(norm_ffn[i], a_w_out[j], s5_w_glu[j], s5_b_glu[j])
            h_lat_new = _post(h_lat, m_i, None, TM_FFN, post_w, ffn, i, o_l, yg_l)
            if need_ctx:
                o_c = _mla_attention(q_c, k_c, vt_c, tq=n_c)
                h_ctx = _post(h_ctx, m_i, b, tm_ctx, post_w, ffn, i, o_c, yg_c)
            h_lat = h_lat_new
        else:
            pw = _prep_proj_c(norm_mix[i], c_w_in[j], c_q_norm[j], c_k_norm[j])
            q_l, k_l, vt_l = _proj_c(h_lat, m_i, None, TM_PROJ_C, pw, cos_c2, sin_c2)
            q_c, k_c, vt_c = _proj_c(h_ctx, m_i, b, tm_ctx, pw, one_c, zero_c)
            o_l = _win_attention(q_l, k_l, vt_l, k_c, vt_c, c_sink[j], tq=TQ_WIN)
            post_w = _prep_post(norm_ffn[i], c_w_out[j])
            h_lat_new = _post(h_lat, m_i, None, TM_FFN, post_w, ffn, i, o_l)
            if need_ctx:
                raise NotImplementedError("context queries of a windowed layer")
            h_lat = h_lat_new
    return h_lat
```

```python
import functools
import math

import jax
import jax.numpy as jnp
import numpy as np
from jax import lax
from jax.experimental import pallas as pl
from jax.experimental.pallas import tpu as pltpu

F32 = jnp.float32
BF16 = jnp.bfloat16

GRID_W = 64
NORM_EPS = 1e-6
ROPE_THETA = 10000.0
N_MOD = 6
S5_GROUP_DIM = 16
S5_STATE = 64
S5_CHUNK = 16
MLA_HEADS = 4
MLA_NOPE = 128
MLA_ROPE = 64
MLA_QK_DIM = MLA_NOPE + MLA_ROPE
MLA_V = 128
MLA_Q_RANK = 384
MLA_KV_RANK = 256
MLA_SCALE = MLA_QK_DIM ** -0.5
MLA_HEAD_PAD = 256
WIN_HEADS = 16
WIN_KV_HEADS = 4
WIN_GROUP = WIN_HEADS // WIN_KV_HEADS
WIN_HEAD_DIM = 64
WINDOW = 128
WIN_SCALE = WIN_HEAD_DIM ** -0.5
LANES = 128
SUBLANES = 8
MXU_TILE = 256
VMEM_BYTES_V7X = 64 * 1024 * 1024
VMEM_LIMIT_BYTES = (VMEM_BYTES_V7X * 3) // 4
NEG_BIG = -1e30
LOG2_E = math.log2(math.e)
TM_FFN = 512
TM_PROJ_A = 512
TM_PROJ_C = 1024
TQ_WIN = 256
TK_MLA = 1024
TN_ADA = 1024
PROJ_ROWS = 256
SOFTMAX_DEN_MIN = 2.0 ** -60
SOFTMAX_DEN_MAX = 2.0 ** 60
S5_BLOCK_PAIRS = 4
MLA_Q_BLOCK = 256
MLA_K_PIECE = 256
MOD_ROWS = 8


def _dot(a, b):
    return jnp.dot(a, b, preferred_element_type=F32)


def _dot_nt(a, b):
    return lax.dot_general(a, b, (((1,), (1,)), ((), ())), preferred_element_type=F32)


def _split_bf16(x):
    hi = x.astype(BF16)
    lo = (x - hi.astype(F32)).astype(BF16)
    return hi, lo


def _rms(x, gain):
    return x * lax.rsqrt(jnp.mean(x * x, axis=-1, keepdims=True) + NORM_EPS) * gain


def _silu(x):
    return x * jax.nn.sigmoid(x)


def _params(*sem):
    return pltpu.CompilerParams(dimension_semantics=sem, vmem_limit_bytes=VMEM_LIMIT_BYTES)


def _const_spec(shape, layer=None):
    if layer is None:
        nd = len(shape)
        return pl.BlockSpec(shape, lambda *_: (0,) * nd, pipeline_mode=pl.Buffered(1))
    nd = len(shape) - 1
    return pl.BlockSpec((None,) + tuple(shape[1:]), lambda *_: (layer,) + (0,) * nd,
                        pipeline_mode=pl.Buffered(1))


def _ada_kernel(cond_ref, w_ref, b_ref, o_ref):
    s = _silu(cond_ref[...])
    s_hi, s_lo = _split_bf16(s)
    w_hi, w_lo = _split_bf16(w_ref[...])
    o_ref[...] = _dot(s_hi, w_hi) + _dot(s_lo, w_hi) + _dot(s_hi, w_lo) + b_ref[...]


def _ada_modulation(cond, ada_w, ada_b):
    depth, d, n = ada_w.shape
    tn = TN_ADA
    return pl.pallas_call(
        _ada_kernel,
        grid=(depth, n // tn),
        in_specs=[pl.BlockSpec((MOD_ROWS, d), lambda i, j: (0, 0)),
                  pl.BlockSpec((None, d, tn), lambda i, j: (i, 0, j)),
                  pl.BlockSpec((None, 1, tn), lambda i, j: (i, 0, j))],
        out_specs=pl.BlockSpec((None, MOD_ROWS, tn), lambda i, j: (i, 0, j)),
        out_shape=jax.ShapeDtypeStruct((depth, MOD_ROWS, n), F32),
        compiler_params=_params("arbitrary", "arbitrary"),
        name="ada_modulation",
    )(cond, ada_w, ada_b.reshape(depth, 1, n))


def _mod_specs(d, slots, ctx_row):
    def make(slot):
        if ctx_row is None:
            return pl.BlockSpec((None, None, 1, d), lambda b, i: (b, slot, 0, 0))
        return pl.BlockSpec((None, None, 1, d), lambda b, i: (ctx_row, slot, 0, 0))
    return [make(s) for s in slots]


def _grid_rope_tables(rows, rot_dim):
    n_freq = rot_dim // 4
    inv_freq = np.power(np.float32(ROPE_THETA), -np.arange(n_freq, dtype=np.float32) / np.float32(n_freq))
    ang_r = np.arange(rows, dtype=np.float32)[:, None] * inv_freq.astype(np.float32)
    ang_c = np.arange(GRID_W, dtype=np.float32)[:, None] * inv_freq.astype(np.float32)

    def expand(r, c):
        r = np.broadcast_to(r[:, None, :], (rows, GRID_W, n_freq))
        c = np.broadcast_to(c[None, :, :], (rows, GRID_W, n_freq))
        return np.concatenate([r, r, c, c], axis=-1).reshape(rows * GRID_W, rot_dim)

    return expand(np.cos(ang_r), np.cos(ang_c)), expand(np.sin(ang_r), np.sin(ang_c))


def _rot_perm_sign(rot_dim):
    q = rot_dim // 4
    idx = np.arange(rot_dim)
    perm = np.where((idx // q) % 2 == 0, idx + q, idx - q)
    sign = np.where((idx // q) % 2 == 0, -1.0, 1.0).astype(np.float32)
    return perm, sign


def _proj_a_kernel(h_ref, sh_ref, sc_ref, gn_ref, w1_ref, gqa_ref, wq_ref, gkv_ref, wkv_ref, wvt_ref,
                   gq_ref, gk_ref, cs_ref, msk_ref, u_ref, q_ref, k_ref, vt_ref):
    tm = h_ref.shape[0]
    rb = min(PROJ_ROWS, tm)
    msk = msk_ref[...]
    low_half = lax.broadcasted_iota(jnp.int32, (rb, LANES), 1) < MLA_ROPE

    def finish(xh, gain, cs, out_ref, rows, h):
        ssq = _dot((xh * xh).astype(BF16), msk)
        xn = xh * lax.rsqrt(ssq * (1.0 / MLA_QK_DIM) + NORM_EPS) * gain
        rr = xn[:, LANES:] * cs
        rot = rr + pltpu.roll(rr, MLA_ROPE, axis=1)
        base = h * MLA_HEAD_PAD
        out_ref[rows, base:base + LANES] = xn[:, :LANES].astype(BF16)
        out_ref[rows, base + LANES:base + 2 * LANES] = jnp.where(low_half, rot, 0.0).astype(BF16)

    for r in range(tm // rb):
        rows = slice(r * rb, (r + 1) * rb)
        a = _rms(h_ref[rows, :], gn_ref[...]) * (1.0 + sc_ref[...]) + sh_ref[...]
        p1 = _dot(a.astype(BF16), w1_ref[...])
        s5w = u_ref.shape[1]
        u_ref[rows, :] = p1[:, :s5w]
        cq = p1[:, s5w:s5w + MLA_Q_RANK]
        ckv = p1[:, s5w + MLA_Q_RANK:s5w + MLA_Q_RANK + MLA_KV_RANK]
        krr = p1[:, s5w + MLA_Q_RANK + MLA_KV_RANK:]
        qb = _dot(_rms(cq, gqa_ref[...]).astype(BF16), wq_ref[...])
        ckv_n = _rms(ckv, gkv_ref[...]).astype(BF16)
        kv = _dot(ckv_n, wkv_ref[...])
        vt_ref[:, rows] = _dot_nt(wvt_ref[...], ckv_n).astype(BF16)
        cs = cs_ref[rows, :]
        for h in range(MLA_HEADS):
            base = h * MLA_HEAD_PAD
            finish(qb[:, base:base + MLA_HEAD_PAD], gq_ref[:, base:base + MLA_HEAD_PAD], cs, q_ref, rows, h)
            kh = jnp.concatenate([kv[:, h * MLA_NOPE:(h + 1) * MLA_NOPE], krr], axis=1)
            finish(kh, gk_ref[:, base:base + MLA_HEAD_PAD], cs, k_ref, rows, h)


def _proj_a(h, mods, ctx_row, tm, wts, cs):
    b, t, d = h.shape
    qw = MLA_HEADS * MLA_HEAD_PAD
    grid = (b, t // tm)
    tile = lambda w: pl.BlockSpec((None, tm, w), lambda bi, i: (bi, i, 0))
    in_specs = ([tile(d)] + _mod_specs(d, (0, 1), ctx_row)
                + [_const_spec(wts[k].shape) for k in
                   ("gn", "w1", "gqa", "wq", "gkv", "wkv", "wvt", "gq", "gk")]
                + [pl.BlockSpec((tm, LANES), lambda bi, i: (i, 0)), _const_spec(wts["msk"].shape)])
    vw = MLA_HEADS * MLA_V
    s5w = wts["w1"].shape[1] - (MLA_Q_RANK + MLA_KV_RANK + 2 * MLA_ROPE)
    return pl.pallas_call(
        _proj_a_kernel,
        grid=grid,
        in_specs=in_specs,
        out_specs=[tile(s5w), tile(qw), tile(qw), pl.BlockSpec((None, vw, tm), lambda bi, i: (bi, 0, i))],
        out_shape=[jax.ShapeDtypeStruct((b, t, s5w), F32),
                   jax.ShapeDtypeStruct((b, t, qw), BF16),
                   jax.ShapeDtypeStruct((b, t, qw), BF16),
                   jax.ShapeDtypeStruct((b, vw, t), BF16)],
        compiler_params=_params("parallel", "parallel"),
        name="proj_a",
    )(h, mods, mods, wts["gn"], wts["w1"], wts["gqa"], wts["wq"], wts["gkv"], wts["wkv"], wts["wvt"],
      wts["gq"], wts["gk"], cs, wts["msk"])


def _prep_proj_a(norm_mix, a_w_in, qa_norm, w_q_b, kva_norm, w_kv_b, q_norm, k_norm):
    perm, sign = _rot_perm_sign(MLA_ROPE)
    s5w = a_w_in.shape[1] - (MLA_Q_RANK + MLA_KV_RANK + MLA_ROPE)
    assert s5w % LANES == 0 and MLA_Q_RANK % LANES == 0 and MLA_KV_RANK % LANES == 0
    kr = a_w_in[:, -MLA_ROPE:]
    w1 = jnp.concatenate([a_w_in, kr[:, perm] * sign], axis=1).astype(BF16)
    wq = w_q_b.reshape(MLA_Q_RANK, MLA_HEADS, MLA_QK_DIM)
    rope = wq[:, :, MLA_NOPE:]
    wq = jnp.concatenate([wq, rope[:, :, perm] * sign], axis=2)
    wq = wq.reshape(MLA_Q_RANK, MLA_HEADS * MLA_HEAD_PAD).astype(BF16)
    wkv3 = w_kv_b.reshape(MLA_KV_RANK, MLA_HEADS, MLA_NOPE + MLA_V)
    wkv = wkv3[:, :, :MLA_NOPE].reshape(MLA_KV_RANK, -1).astype(BF16)
    wvt = wkv3[:, :, MLA_NOPE:].reshape(MLA_KV_RANK, -1).T.astype(BF16)

    def head_gain(g, scale):
        gb = jnp.concatenate([g, g[MLA_NOPE:][perm]]) * scale
        return jnp.tile(gb, MLA_HEADS)[None, :]

    rows = np.arange(MLA_HEAD_PAD)[:, None] < MLA_QK_DIM
    msk = jnp.asarray(np.broadcast_to(rows, (MLA_HEAD_PAD, MLA_HEAD_PAD)), BF16)
    return dict(gn=norm_mix[None, :], w1=w1, gqa=qa_norm[None, :], wq=wq, gkv=kva_norm[None, :],
                wkv=wkv, wvt=wvt, gq=head_gain(q_norm, MLA_SCALE * LOG2_E), gk=head_gain(k_norm, 1.0),
                msk=msk)


def _s5_kernel(uc_ref, ul_ref, sel_in_ref, sel_out_ref, toep_ref, bst_ref, cst_ref, a_ref, d_ref,
               yc_ref, yl_ref, ub_ref, x_ref, z_ref, sin_ref, yb_ref, *, n_ctx, n_all):
    L, S = S5_CHUNK, S5_GROUP_DIM
    n_lat = n_all - n_ctx
    npair = S5_BLOCK_PAIRS
    half = SUBLANES * LANES
    for tl in range(L):
        u = jnp.concatenate([uc_ref[pl.ds(tl, n_ctx, stride=L), :], ul_ref[pl.ds(tl, n_lat, stride=L), :]], axis=0)
        ub_ref[:, tl * LANES:(tl + 1) * LANES] = u.astype(BF16)
    t_ctx, t_all = n_ctx // SUBLANES, n_all // SUBLANES
    row = lax.broadcasted_iota(jnp.int32, (SUBLANES, LANES), 0)

    def cmul(a_re, a_im, b_re, b_im):
        return a_re * b_re - a_im * b_im, a_re * b_im + a_im * b_re

    def tile_scan(z_re, z_im, c_re, c_im, a_re, a_im, fwd):
        for sft in (1, 2, 4):
            k = sft - 1 if fwd else SUBLANES - sft
            p_re, p_im = a_re[k:k + 1, :], a_im[k:k + 1, :]
            amt = sft if fwd else SUBLANES - sft
            keep = (row >= sft) if fwd else (row < SUBLANES - sft)
            s_re = jnp.where(keep, pltpu.roll(z_re, amt, axis=0), 0.0)
            s_im = jnp.where(keep, pltpu.roll(z_im, amt, axis=0), 0.0)
            m_re, m_im = cmul(p_re, p_im, s_re, s_im)
            z_re, z_im = z_re + m_re, z_im + m_im
        m_re, m_im = cmul(a_re, a_im, c_re, c_im)
        s_re, s_im = z_re + m_re, z_im + m_im
        edge = (row == 0) if fwd else (row == SUBLANES - 1)
        amt = 1 if fwd else SUBLANES - 1
        in_re = jnp.where(edge, c_re, pltpu.roll(s_re, amt, axis=0))
        in_im = jnp.where(edge, c_im, pltpu.roll(s_im, amt, axis=0))
        last = SUBLANES - 1 if fwd else 0
        return in_re, in_im, s_re[last:last + 1, :], s_im[last:last + 1, :]

    def relayout_items(pairs):
        items = []
        for pp in pairs:
            for hh in range(2):
                def sel(pp=pp, hh=hh):
                    xs = _dot(ub_ref[:, hh * half:(hh + 1) * half], sel_in_ref[pp]).astype(BF16)
                    x_ref[pp, :, hh * LANES:(hh + 1) * LANES] = xs[:, :LANES]
                    x_ref[pp, :, MXU_TILE + hh * LANES:MXU_TILE + (hh + 1) * LANES] = xs[:, LANES:]
                items.append(sel)

            def drive(pp=pp):
                z_ref[pp] = _dot(x_ref[pp], bst_ref[pp])
            items.append(drive)
        return items

    def readout_items(pairs):
        items = []
        for pp in pairs:
            def readout(pp=pp):
                x = x_ref[pp]
                y = jnp.concatenate([_dot(x[:, :MXU_TILE], toep_ref[pp, 0]), _dot(x[:, MXU_TILE:], toep_ref[pp, 1])],
                                    axis=1)
                y = y + _dot(sin_ref[pp].astype(BF16), cst_ref[pp]) + x.astype(F32) * d_ref[pp]
                yg = jax.nn.gelu(y).astype(BF16)
                for gl in range(2):
                    for hh in range(2):
                        g8 = 2 * pp + gl
                        col = gl * MXU_TILE + hh * LANES
                        yb_ref[hh, :, g8 * LANES:(g8 + 1) * LANES] = yg[:, col:col + LANES]
            items.append(readout)
        return items

    def scan_with(pairs, items):
        zero = jnp.zeros((1, LANES), F32)
        carry = {pp: (zero, zero, zero, zero) for pp in pairs}
        every = max(1, t_all // max(1, len(items)))
        pending = list(items)
        for it in range(t_all):
            jt = t_ctx - 1 - it if it < t_ctx else t_all + t_ctx - 1 - it
            rf, rb = it * SUBLANES, jt * SUBLANES
            for pp in pairs:
                cf_re, cf_im, cb_re, cb_im = carry[pp]
                zf = z_ref[pp, rf:rf + SUBLANES, 0:2 * LANES]
                zb = z_ref[pp, rb:rb + SUBLANES, 2 * LANES:4 * LANES]
                f_re, f_im, cf_re, cf_im = tile_scan(zf[:, :LANES], zf[:, LANES:], cf_re, cf_im,
                                                     a_ref[pp, 0], a_ref[pp, 1], True)
                b_re, b_im, cb_re, cb_im = tile_scan(zb[:, :LANES], zb[:, LANES:], cb_re, cb_im,
                                                     a_ref[pp, 2], a_ref[pp, 3], False)
                sin_ref[pp, rf:rf + SUBLANES, 0:2 * LANES] = jnp.concatenate([f_re, f_im], axis=1)
                sin_ref[pp, rb:rb + SUBLANES, 2 * LANES:4 * LANES] = jnp.concatenate([b_re, b_im], axis=1)
                carry[pp] = (cf_re, cf_im, cb_re, cb_im)
            if pending and (it + 1) % every == 0:
                pending.pop(0)()
        for item in pending:
            item()

    first, second = tuple(range(npair // 2)), tuple(range(npair // 2, npair))
    for item in relayout_items(first):
        item()
    scan_with(first, relayout_items(second))
    scan_with(second, readout_items(first))
    for item in readout_items(second):
        item()
    for hh in range(2):
        for kk in range(SUBLANES // 2):
            two = _dot(yb_ref[hh], sel_out_ref[kk])
            for e in range(2):
                tl = hh * SUBLANES + 2 * kk + e
                yc_ref[pl.ds(tl, n_ctx, stride=L), :] = two[:n_ctx, e * LANES:(e + 1) * LANES]
                yl_ref[pl.ds(tl, n_lat, stride=L), :] = two[n_ctx:, e * LANES:(e + 1) * LANES]


def _s5_selectors():
    S = S5_GROUP_DIM
    r = np.arange(SUBLANES * LANES)[:, None]
    c = np.arange(2 * LANES)[None, :]
    k, l = r // LANES, r % LANES
    pp = np.arange(S5_BLOCK_PAIRS)[:, None, None]
    sel_in = (k == (c % LANES) // S) & (l == 2 * S * pp + S * (c // LANES) + c % S)
    kk = np.arange(SUBLANES // 2)[:, None, None]
    sel_out = (k == (c % LANES) // S) & (l == S * (2 * kk + c // LANES) + c % S)
    return jnp.asarray(sel_in, BF16), jnp.asarray(sel_out, BF16)


def _dot_nt_f32(a, b):
    a_hi, a_lo = _split_bf16(a)
    b_hi, b_lo = _split_bf16(b)
    return _dot_nt(a_hi, b_hi) + _dot_nt(a_lo, b_hi) + _dot_nt(a_hi, b_lo)


def _s5_table_kernel(par_ref, bre_ref, bim_ref, cre_ref, cim_ref, toep_ref, bst_ref, cst_ref, a_ref):
    L, S, P = S5_CHUNK, S5_GROUP_DIM, S5_STATE
    kk = lax.broadcasted_iota(jnp.int32, (3 * SUBLANES, LANES), 0).astype(F32)
    lane = lax.broadcasted_iota(jnp.int32, (L, LANES), 1)
    pair_rows = lax.broadcasted_iota(jnp.int32, (2 * L * S, LANES), 0)
    pair_lanes = lax.broadcasted_iota(jnp.int32, (2 * L * S, LANES), 1)
    own_group = (pair_rows // (L * S)) == (pair_lanes // P)
    lane_pad = jnp.zeros((2 * P, LANES - S), F32)

    def rows_of_powers(pw, ks, groups):
        one = jnp.concatenate([jnp.broadcast_to(pw[k:k + 1, :], (S, LANES)) for k in ks], axis=0)
        return jnp.concatenate([one] * groups, axis=0) if groups > 1 else one

    def cmul(a_re, a_im, b_re, b_im):
        return a_re * b_re - a_im * b_im, a_re * b_im + a_im * b_re

    lag_tables = []
    for d in range(2):
        lam_re, lam_im = par_ref[d, 0:1, :], par_ref[d, 1:2, :]
        step = jnp.exp(par_ref[d, 2:3, :])
        ar, ai = lam_re * step, lam_im * step
        mag = jnp.exp(kk * ar)
        pw_re, pw_im = mag * jnp.cos(kk * ai), mag * jnp.sin(kk * ai)
        th = jnp.tanh(0.5 * ar)
        em1 = 2.0 * th / (1.0 - th)
        sh = jnp.sin(0.5 * ai)
        n_re = em1 * jnp.cos(ai) - 2.0 * sh * sh
        n_im = (em1 + 1.0) * jnp.sin(ai)
        den = lam_re * lam_re + lam_im * lam_im
        co_re = (n_re * lam_re + n_im * lam_im) / den
        co_im = (n_im * lam_re - n_re * lam_im) / den
        bt_re = jnp.concatenate([bre_ref[d], lane_pad], axis=1).T[:S]
        bt_im = jnp.concatenate([bim_ref[d], lane_pad], axis=1).T[:S]
        bb_re, bb_im = cmul(co_re, co_im, bt_re, bt_im)
        cc_re = jnp.concatenate([cre_ref[d, 0], cre_ref[d, 1]], axis=1)
        cc_im = jnp.concatenate([cim_ref[d, 0], cim_ref[d, 1]], axis=1)
        ks = [L - 1 - t for t in range(L)] if d == 0 else list(range(L))
        r_re, r_im = rows_of_powers(pw_re, ks, 2), rows_of_powers(pw_im, ks, 2)
        bbt_re, bbt_im = jnp.concatenate([bb_re] * (2 * L), axis=0), jnp.concatenate([bb_im] * (2 * L), axis=0)
        v_re, v_im = cmul(r_re, r_im, bbt_re, bbt_im)
        bst_ref[:, (2 * d) * LANES:(2 * d + 1) * LANES] = jnp.where(own_group, v_re, 0.0).astype(BF16)
        bst_ref[:, (2 * d + 1) * LANES:(2 * d + 2) * LANES] = jnp.where(own_group, v_im, 0.0).astype(BF16)
        ks = [t + 1 for t in range(L)] if d == 0 else [L - t for t in range(L)]
        r_re, r_im = rows_of_powers(pw_re, ks, 2), rows_of_powers(pw_im, ks, 2)
        cct_re, cct_im = jnp.concatenate([cc_re] * (2 * L), axis=0), jnp.concatenate([cc_im] * (2 * L), axis=0)
        v_re, v_im = cmul(cct_re, cct_im, r_re, r_im)
        cst_ref[(2 * d) * LANES:(2 * d + 1) * LANES, :] = jnp.where(own_group, v_re, 0.0).T.astype(BF16)
        cst_ref[(2 * d + 1) * LANES:(2 * d + 2) * LANES, :] = jnp.where(own_group, -v_im, 0.0).T.astype(BF16)
        ks = list(range(L)) if d == 0 else [L - 1 - j for j in range(L)]
        r_re, r_im = rows_of_powers(pw_re, ks, 1), rows_of_powers(pw_im, ks, 1)
        cl_re, cl_im = cmul(jnp.concatenate([cc_re] * L, axis=0), jnp.concatenate([cc_im] * L, axis=0), r_re, r_im)
        per_group = []
        for g in range(2):
            mine = (lane // P) == g
            per_group.append(_dot_nt_f32(jnp.where(mine, bb_re, 0.0), cl_re)
                             - _dot_nt_f32(jnp.where(mine, bb_im, 0.0), cl_im))
        lag_tables.append(per_group)
        row8 = lax.broadcasted_iota(jnp.int32, (SUBLANES, LANES), 0)
        n_chunks = ((row8 + 1) if d == 0 else (SUBLANES - row8)).astype(F32) * float(L)
        mag8 = jnp.exp(n_chunks * ar)
        a_ref[2 * d] = mag8 * jnp.cos(n_chunks * ai)
        a_ref[2 * d + 1] = mag8 * jnp.sin(n_chunks * ai)

    def shift_right(x, s):
        x0, x1 = x[:, :LANES], x[:, LANES:]
        a, r = divmod(s, LANES)
        r0 = pltpu.roll(x0, r, axis=1) if r else x0
        r1 = pltpu.roll(x1, r, axis=1) if r else x1
        if a == 0:
            return jnp.concatenate([jnp.where(lane >= r, r0, 0.0), jnp.where(lane >= r, r1, r0)], axis=1)
        return jnp.concatenate([jnp.zeros_like(x0), jnp.where(lane >= r, r0, 0.0)], axis=1)

    def shift_left(x, s):
        x0, x1 = x[:, :LANES], x[:, LANES:]
        a, r = divmod(s, LANES)
        r0 = pltpu.roll(x0, LANES - r, axis=1) if r else x0
        r1 = pltpu.roll(x1, LANES - r, axis=1) if r else x1
        if a == 0:
            return jnp.concatenate([jnp.where(lane < LANES - r, r0, r1), jnp.where(lane < LANES - r, r1, 0.0)], axis=1)
        return jnp.concatenate([jnp.where(lane < LANES - r, r1, 0.0), jnp.zeros_like(x0)], axis=1)

    for g in range(2):
        kf, kb = lag_tables[0][g], lag_tables[1][g]
        for tau in range(L):
            blk = shift_right(kf, S * tau) + shift_left(kb, S * (L - 1 - tau))
            toep_ref[g, tau * S:(tau + 1) * S, :] = blk.astype(BF16)


def _s5_tables(lam_re, lam_im, log_step, b_re, b_im, c_re, c_im, d_skip):
    _, G, P = lam_re.shape
    S, L = S5_GROUP_DIM, S5_CHUNK
    assert P == S5_STATE and 2 * P == LANES and 2 * L * S == 2 * MXU_TILE
    pairs = G // 2
    par = jnp.stack([lam_re.reshape(2, pairs, 2 * P), lam_im.reshape(2, pairs, 2 * P),
                     jnp.repeat(log_step, P, axis=-1).reshape(2, pairs, 2 * P)], axis=2)
    par = jnp.transpose(par, (1, 0, 2, 3)).astype(F32)
    bshape = (2, pairs, 2 * P, S)
    cshape = (2, pairs, 2, S, P)
    pw = 2 * L * S
    bspec = pl.BlockSpec((2, None, 2 * P, S), lambda g: (0, g, 0, 0))
    cspec = pl.BlockSpec((2, None, 2, S, P), lambda g: (0, g, 0, 0, 0))
    toep, bst, cst, a_chunk = pl.pallas_call(
        _s5_table_kernel,
        grid=(pairs,),
        in_specs=[pl.BlockSpec((None, 2, 3, 2 * P), lambda g: (g, 0, 0, 0)), bspec, bspec, cspec, cspec],
        out_specs=[pl.BlockSpec((None, 2, MXU_TILE, MXU_TILE), lambda g: (g, 0, 0, 0)),
                   pl.BlockSpec((None, pw, pw), lambda g: (g, 0, 0)),
                   pl.BlockSpec((None, pw, pw), lambda g: (g, 0, 0)),
                   pl.BlockSpec((None, 4, SUBLANES, LANES), lambda g: (g, 0, 0, 0))],
        out_shape=[jax.ShapeDtypeStruct((pairs, 2, MXU_TILE, MXU_TILE), BF16),
                   jax.ShapeDtypeStruct((pairs, pw, pw), BF16),
                   jax.ShapeDtypeStruct((pairs, pw, pw), BF16),
                   jax.ShapeDtypeStruct((pairs, 4, SUBLANES, LANES), F32)],
        compiler_params=_params("parallel"),
        name="s5_tables",
    )(par, b_re.reshape(bshape), b_im.reshape(bshape), c_re.reshape(cshape), c_im.reshape(cshape))
    d_pair = jnp.broadcast_to(d_skip.astype(F32).reshape(pairs, 2, 1, S), (pairs, 2, L, S))
    return toep, bst, cst, a_chunk, d_pair.reshape(pairs, 1, pw)


def _s5(u_ctx, u_lat, tables):
    toep, bst, cst, a_pow, d_pair = tables
    b, n_c, w = u_ctx.shape
    n_l = u_lat.shape[1]
    L = S5_CHUNK
    n_ctx, n_all = n_c // L, (n_c + n_l) // L
    nblk = w // LANES
    pw = 2 * L * S5_GROUP_DIM
    npair = S5_BLOCK_PAIRS
    sel_in, sel_out = _s5_selectors()
    wspec = lambda shape: pl.BlockSpec((npair,) + shape, lambda g, bi: (g,) + (0,) * len(shape),
                                       pipeline_mode=pl.Buffered(1))
    return pl.pallas_call(
        functools.partial(_s5_kernel, n_ctx=n_ctx, n_all=n_all),
        grid=(nblk, b),
        in_specs=[pl.BlockSpec((None, n_c, LANES), lambda g, bi: (bi, 0, g)),
                  pl.BlockSpec((None, n_l, LANES), lambda g, bi: (bi, 0, g)),
                  _const_spec(sel_in.shape), _const_spec(sel_out.shape),
                  wspec((2, MXU_TILE, MXU_TILE)), wspec((pw, pw)), wspec((pw, pw)),
                  wspec((4, SUBLANES, LANES)), wspec((1, pw))],
        out_specs=[pl.BlockSpec((None, n_c, LANES), lambda g, bi: (bi, 0, g)),
                   pl.BlockSpec((None, n_l, LANES), lambda g, bi: (bi, 0, g))],
        out_shape=[jax.ShapeDtypeStruct((b, n_c, w), F32), jax.ShapeDtypeStruct((b, n_l, w), F32)],
        scratch_shapes=[pltpu.VMEM((n_all, L * LANES), BF16), pltpu.VMEM((npair, n_all, pw), BF16),
                        pltpu.VMEM((npair, n_all, pw), F32), pltpu.VMEM((npair, n_all, pw), F32),
                        pltpu.VMEM((2, n_all, SUBLANES * LANES), BF16)],
        compiler_params=_params("parallel", "arbitrary"),
        name="s5_scan",
    )(u_ctx, u_lat, sel_in, sel_out, toep, bst, cst, a_pow, d_pair)


def _mla_kernel(*refs, tk, n_steps):
    if n_steps:
        q_ref, kc_ref, vc_ref, k_ref, v_ref, o_ref, qt_ref, s_ref, m_ref, l_ref, acc_ref = refs
    else:
        q_ref, kc_ref, vc_ref, o_ref, qt_ref, s_ref, m_ref, l_ref, acc_ref = refs
    tq = q_ref.shape[0]
    ncb = tq // MLA_Q_BLOCK
    kp = MLA_K_PIECE
    qt_ref[...] = q_ref[...].astype(F32).T.astype(BF16)

    def chunk(load_k, load_vt, nkeys, first):
        nr = nkeys // kp

        def score_piece(c, r):
            st = _dot(load_k(r), qt_ref[:, c * MLA_Q_BLOCK:(c + 1) * MLA_Q_BLOCK])
            s_ref[c % 2, r * kp:(r + 1) * kp, :] = st
            return jnp.max(st, axis=0, keepdims=True)

        def block_stats(c, mx):
            if first:
                return mx, None
            m_old = m_ref[:, c * MLA_Q_BLOCK:(c + 1) * MLA_Q_BLOCK]
            m_new = jnp.maximum(m_old, mx)
            return m_new, jnp.exp2(m_old - m_new)

        def prob_piece(c, r, m_new):
            p = jnp.exp2(s_ref[c % 2, r * kp:(r + 1) * kp, :] - m_new)
            return jnp.sum(p, axis=0, keepdims=True), _dot(load_vt(r), p.astype(BF16))

        def finish(c, m_new, alpha, lsum, pv):
            cols = slice(c * MLA_Q_BLOCK, (c + 1) * MLA_Q_BLOCK)
            if first:
                l_ref[:, cols] = lsum
                acc_ref[:, cols] = pv
            else:
                l_ref[:, cols] = alpha * l_ref[:, cols] + lsum
                acc_ref[:, cols] = alpha * acc_ref[:, cols] + pv
            m_ref[:, cols] = m_new

        mx = None
        for r in range(nr):
            pm = score_piece(0, r)
            mx = pm if mx is None else jnp.maximum(mx, pm)
        for c in range(ncb):
            m_new, alpha = block_stats(c, mx)
            mx = lsum = pv = None
            for r in range(nr):
                if c + 1 < ncb:
                    pm = score_piece(c + 1, r)
                    mx = pm if mx is None else jnp.maximum(mx, pm)
                ls, pvr = prob_piece(c, r, m_new)
                lsum = ls if lsum is None else lsum + ls
                pv = pvr if pv is None else pv + pvr
            finish(c, m_new, alpha, lsum, pv)

    chunk(lambda r: kc_ref[r * kp:(r + 1) * kp, :], lambda r: vc_ref[:, r * kp:(r + 1) * kp],
          kc_ref.shape[0], True)
    if n_steps:
        def body(j, _):
            off = pl.multiple_of(j * tk, tk)
            chunk(lambda r: k_ref[pl.ds(off + r * kp, kp), :], lambda r: v_ref[:, pl.ds(off + r * kp, kp)],
                  tk, False)
            return 0
        lax.fori_loop(0, n_steps, body, 0)
    o_ref[...] = (acc_ref[...] / l_ref[...]).T.astype(BF16)


def _mla_attention(q, k_ctx, vt_ctx, k_lat=None, vt_lat=None, *, tq, tk=TK_MLA):
    b, t, _ = q.shape
    n_c = k_ctx.shape[1]
    in_specs = [pl.BlockSpec((None, tq, MLA_HEAD_PAD), lambda bi, h, i: (bi, i, h)),
                pl.BlockSpec((None, n_c, MLA_HEAD_PAD), lambda bi, h, i: (bi, 0, h)),
                pl.BlockSpec((None, MLA_V, n_c), lambda bi, h, i: (bi, h, 0))]
    args = [q, k_ctx, vt_ctx]
    n_steps = 0
    if k_lat is not None:
        n_l = k_lat.shape[1]
        n_steps = n_l // tk
        in_specs += [pl.BlockSpec((None, n_l, MLA_HEAD_PAD), lambda bi, h, i: (bi, 0, h)),
                     pl.BlockSpec((None, MLA_V, n_l), lambda bi, h, i: (bi, h, 0))]
        args += [k_lat, vt_lat]
    return pl.pallas_call(
        functools.partial(_mla_kernel, tk=tk, n_steps=n_steps),
        grid=(b, MLA_HEADS, t // tq),
        in_specs=in_specs,
        out_specs=pl.BlockSpec((None, tq, MLA_V), lambda bi, h, i: (bi, i, h)),
        out_shape=jax.ShapeDtypeStruct((b, t, MLA_HEADS * MLA_V), BF16),
        scratch_shapes=[pltpu.VMEM((MLA_HEAD_PAD, tq), BF16), pltpu.VMEM((2, max(tk, n_c), MLA_Q_BLOCK), F32),
                        pltpu.VMEM((1, tq), F32), pltpu.VMEM((1, tq), F32), pltpu.VMEM((MLA_V, tq), F32)],
        compiler_params=_params("parallel", "parallel", "arbitrary"),
        name="mla_attention",
    )(*args)


def _proj_c_kernel(h_ref, sh_ref, sc_ref, gn_ref, wc_ref, wvt_ref, gqk_ref, cos_ref, sin_ref, bd_ref,
                   q_ref, k_ref, vt_ref):
    qw = WIN_HEADS * WIN_HEAD_DIM
    kw = WIN_KV_HEADS * WIN_HEAD_DIM
    bd = bd_ref[...]
    tm = h_ref.shape[0]
    rb = min(PROJ_ROWS, tm)
    lane = lax.broadcasted_iota(jnp.int32, (rb, LANES), 1)
    first_quarter = (lane % (WIN_HEAD_DIM // 2)) < (WIN_HEAD_DIM // 4)
    for r in range(tm // rb):
        rows = slice(r * rb, (r + 1) * rb)
        a = (_rms(h_ref[rows, :], gn_ref[...]) * (1.0 + sc_ref[...]) + sh_ref[...]).astype(BF16)
        p = _dot(a, wc_ref[...])
        vt_ref[:, rows] = _dot_nt(wvt_ref[...], a).astype(BF16)
        cos, sin = cos_ref[rows, :], sin_ref[rows, :]
        for j in range((qw + kw) // MXU_TILE):
            xh = p[:, j * MXU_TILE:(j + 1) * MXU_TILE]
            ssq = _dot((xh * xh).astype(BF16), bd)
            xn = xh * lax.rsqrt(ssq * (1.0 / WIN_HEAD_DIM) + NORM_EPS) * gqk_ref[:, j * MXU_TILE:(j + 1) * MXU_TILE]
            for c in range(MXU_TILE // LANES):
                xc = xn[:, c * LANES:(c + 1) * LANES]
                fwd = pltpu.roll(xc, WIN_HEAD_DIM // 4, axis=1)
                bwd = pltpu.roll(xc, LANES - WIN_HEAD_DIM // 4, axis=1)
                y = (xc * cos + jnp.where(first_quarter, -bwd, fwd) * sin).astype(BF16)
                col = j * MXU_TILE + c * LANES
                if col < qw:
                    q_ref[rows, col:col + LANES] = y
                else:
                    k_ref[rows, col - qw:col - qw + LANES] = y


def _proj_c(h, mods, ctx_row, tm, wts, cos2, sin2):
    b, t, d = h.shape
    qw = WIN_HEADS * WIN_HEAD_DIM
    kw = WIN_KV_HEADS * WIN_HEAD_DIM
    tile = lambda w: pl.BlockSpec((None, tm, w), lambda bi, i: (bi, i, 0))
    tab = pl.BlockSpec((tm, LANES), lambda bi, i: (i, 0))
    in_specs = ([tile(d)] + _mod_specs(d, (0, 1), ctx_row)
                + [_const_spec(wts[k].shape) for k in ("gn", "wc", "wvt", "gqk")]
                + [tab, tab, _const_spec(wts["bd"].shape)])
    return pl.pallas_call(
        _proj_c_kernel,
        grid=(b, t // tm),
        in_specs=in_specs,
        out_specs=[tile(qw), tile(kw), pl.BlockSpec((None, kw, tm), lambda bi, i: (bi, 0, i))],
        out_shape=[jax.ShapeDtypeStruct((b, t, qw), BF16),
                   jax.ShapeDtypeStruct((b, t, kw), BF16),
                   jax.ShapeDtypeStruct((b, kw, t), BF16)],
        compiler_params=_params("parallel", "parallel"),
        name="proj_c",
    )(h, mods, mods, wts["gn"], wts["wc"], wts["wvt"], wts["gqk"], cos2, sin2, wts["bd"])


def _prep_proj_c(norm_mix, c_w_in, q_norm, k_norm):
    gqk = jnp.concatenate([jnp.tile(q_norm * (WIN_SCALE * LOG2_E), WIN_HEADS), jnp.tile(k_norm, WIN_KV_HEADS)])
    idx = np.arange(MXU_TILE) // WIN_HEAD_DIM
    bd = jnp.asarray(idx[:, None] == idx[None, :], BF16)
    qk = (WIN_HEADS + WIN_KV_HEADS) * WIN_HEAD_DIM
    return dict(gn=norm_mix[None, :], wc=c_w_in[:, :qk].astype(BF16), wvt=c_w_in[:, qk:].T.astype(BF16),
                gqk=gqk[None, :], bd=bd)


def _win_kernel(sc_ref, q_ref, k_ref, vt_ref, kc_ref, vct_ref, o_ref, ot_ref, *, tq, band, n_lat):
    i = pl.program_id(1)
    start = pl.multiple_of(jnp.clip(i * tq - WINDOW, 0, n_lat - band), WINDOW)
    hd, grp = WIN_HEAD_DIM, WIN_GROUP
    qt = q_ref[...].astype(F32).T.astype(BF16)
    k_pos = start + lax.broadcasted_iota(jnp.int32, (band, tq), 0)
    q_pos = i * tq + lax.broadcasted_iota(jnp.int32, (band, tq), 1)
    bias1 = jnp.where(jnp.abs(k_pos - q_pos) <= WINDOW, 0.0, NEG_BIG)
    bias = jnp.concatenate([bias1] * grp, axis=1)
    zeros = jnp.zeros((hd, grp * tq), BF16)

    def group_inputs(kv):
        qg = jnp.concatenate([qt[(kv * grp + g) * hd:(kv * grp + g + 1) * hd, :] for g in range(grp)], axis=1)
        qg = jnp.concatenate([qg, zeros] if kv % 2 == 0 else [zeros, qg], axis=0)
        col = (kv // 2) * LANES
        sink = jnp.concatenate([jnp.full((1, tq), sc_ref[kv * grp + g], F32) for g in range(grp)], axis=1)
        return qg, col, sink

    def weighted_values(kv, p_ctx, p_loc):
        return (_dot(vct_ref[kv * hd:(kv + 1) * hd, :], p_ctx.astype(BF16))
                + _dot(vt_ref[kv * hd:(kv + 1) * hd, pl.ds(start, band)], p_loc.astype(BF16)))

    def regroup(ot):
        return jnp.concatenate([ot[:, g * tq:(g + 1) * tq] for g in range(grp)], axis=0)

    def one_pass(kv):
        qg, col, sink = group_inputs(kv)
        p_ctx = jnp.exp2(_dot(kc_ref[:, col:col + LANES], qg))
        p_loc = jnp.exp2(_dot(k_ref[pl.ds(start, band), col:col + LANES], qg) + bias)
        den = (jnp.sum(p_loc, axis=0, keepdims=True) + jnp.sum(p_ctx, axis=0, keepdims=True)
               + jnp.exp2(sink))
        ot_ref[kv * grp * hd:(kv + 1) * grp * hd, :] = regroup(weighted_values(kv, p_ctx, p_loc) / den)
        return den

    def two_pass(kv):
        qg, col, sink = group_inputs(kv)
        s_ctx = _dot(kc_ref[:, col:col + LANES], qg)
        s_loc = _dot(k_ref[pl.ds(start, band), col:col + LANES], qg) + bias
        m = jnp.maximum(jnp.maximum(jnp.max(s_loc, axis=0, keepdims=True),
                                    jnp.max(s_ctx, axis=0, keepdims=True)), sink)
        p_loc = jnp.exp2(s_loc - m)
        p_ctx = jnp.exp2(s_ctx - m)
        den = (jnp.sum(p_loc, axis=0, keepdims=True) + jnp.sum(p_ctx, axis=0, keepdims=True)
               + jnp.exp2(sink - m))
        ot_ref[kv * grp * hd:(kv + 1) * grp * hd, :] = regroup(weighted_values(kv, p_ctx, p_loc) / den)

    lo = hi = None
    for kv in range(WIN_KV_HEADS):
        den = one_pass(kv)
        lo = den if lo is None else jnp.minimum(lo, den)
        hi = den if hi is None else jnp.maximum(hi, den)
    unsafe = jnp.logical_or(jnp.min(lo) < SOFTMAX_DEN_MIN, jnp.max(hi) > SOFTMAX_DEN_MAX)

    @pl.when(unsafe)
    def _():
        for kv in range(WIN_KV_HEADS):
            two_pass(kv)

    o_ref[...] = ot_ref[...].T.astype(BF16)


def _win_attention(q, k, vt, k_ctx, vt_ctx, sink, *, tq):
    b, n, qw = q.shape
    n_c = k_ctx.shape[1]
    kw = k.shape[2]
    band = tq + 2 * WINDOW
    full = lambda r, w: pl.BlockSpec((None, r, w), lambda bi, i: (bi, 0, 0))
    return pl.pallas_call(
        functools.partial(_win_kernel, tq=tq, band=band, n_lat=n),
        grid=(b, n // tq),
        in_specs=[pl.BlockSpec(memory_space=pltpu.SMEM),
                  pl.BlockSpec((None, tq, qw), lambda bi, i: (bi, i, 0)),
                  full(n, kw), full(kw, n), full(n_c, kw), full(kw, n_c)],
        out_specs=pl.BlockSpec((None, tq, qw), lambda bi, i: (bi, i, 0)),
        out_shape=jax.ShapeDtypeStruct((b, n, qw), BF16),
        scratch_shapes=[pltpu.VMEM((qw, tq), F32)],
        compiler_params=_params("parallel", "arbitrary"),
        name="win_attention",
    )(sink.astype(F32) * LOG2_E, q, k, vt, k_ctx, vt_ctx)


def _post_kernel(*refs, s5_width, n_chunks):
    if s5_width:
        (h_ref, g_ref, sh_ref, sc_ref, g2_ref, gn_ref, yg_ref, o_ref, wglu_ref, bglu_ref, wo_ref,
         wg_ref, wu_ref, wd_ref, out_ref, a_ref, acc_ref) = refs
        yg = yg_ref[...]
        s5 = yg * jax.nn.sigmoid(_dot(yg.astype(BF16), wglu_ref[...]) + bglu_ref[...])
        mix = _dot(s5.astype(BF16), wo_ref[:s5_width, :]) + _dot(o_ref[...], wo_ref[s5_width:, :])
    else:
        (h_ref, g_ref, sh_ref, sc_ref, g2_ref, gn_ref, o_ref, wo_ref,
         wg_ref, wu_ref, wd_ref, out_ref, a_ref, acc_ref) = refs
        mix = _dot(o_ref[...], wo_ref[...])
    h1 = h_ref[...] + g_ref[...] * mix
    a_ref[...] = (_rms(h1, gn_ref[...]) * (1.0 + sc_ref[...]) + sh_ref[...]).astype(BF16)
    acc_ref[...] = jnp.zeros_like(acc_ref)

    def body(c, _):
        a = a_ref[...]
        cols = pl.ds(pl.multiple_of(c * MXU_TILE, MXU_TILE), MXU_TILE)
        act = _silu(_dot(a, wg_ref[:, cols])) * _dot(a, wu_ref[:, cols])
        acc_ref[...] += _dot(act.astype(BF16), wd_ref[cols, :])
        return 0

    lax.fori_loop(0, n_chunks, body, 0, unroll=True)
    out_ref[...] = h1 + g2_ref[...] * acc_ref[...]


def _post(h, mods, ctx_row, tm, wts, ffn, layer, o, yg=None):
    b, t, d = h.shape
    tile = lambda w: pl.BlockSpec((None, tm, w), lambda bi, i: (bi, i, 0))
    s5_width = 0 if yg is None else yg.shape[2]
    in_specs = [tile(d)] + _mod_specs(d, (2, 3, 4, 5), ctx_row) + [_const_spec(wts["gn"].shape)]
    args = [h, mods, mods, mods, mods, wts["gn"]]
    if yg is not None:
        in_specs += [tile(s5_width), tile(o.shape[2]), _const_spec(wts["wglu"].shape),
                     _const_spec(wts["bglu"].shape)]
        args += [yg, o, wts["wglu"], wts["bglu"]]
    else:
        in_specs += [tile(o.shape[2])]
        args += [o]
    in_specs.append(_const_spec(wts["wo"].shape))
    args.append(wts["wo"])
    for k in ("wg", "wu", "wd"):
        in_specs.append(_const_spec(ffn[k].shape, layer))
        args.append(ffn[k])
    return pl.pallas_call(
        functools.partial(_post_kernel, s5_width=s5_width, n_chunks=ffn["wg"].shape[2] // MXU_TILE),
        grid=(b, t // tm),
        in_specs=in_specs,
        out_specs=tile(d),
        out_shape=jax.ShapeDtypeStruct((b, t, d), F32),
        scratch_shapes=[pltpu.VMEM((tm, d), BF16), pltpu.VMEM((tm, d), F32)],
        compiler_params=_params("parallel", "parallel"),
        name="post_ffn",
    )(*args)


def _prep_post(norm_ffn, w_out, w_glu=None, b_glu=None):
    wts = dict(gn=norm_ffn[None, :], wo=w_out.astype(BF16))
    if w_glu is not None:
        wts.update(wglu=w_glu.astype(BF16), bglu=b_glu[None, :])
    return wts


def kernel(x, c, ctx, c_ctx, ada_w, ada_b, norm_mix, norm_ffn, ffn_w_gate, ffn_w_up, ffn_w_down,
           a_w_in, a_w_out, s5_lam_re, s5_lam_im, s5_log_step, s5_b_re, s5_b_im, s5_c_re, s5_c_im,
           s5_d, s5_w_glu, s5_b_glu, mla_qa_norm, mla_w_q_b, mla_kva_norm, mla_w_kv_b,
           mla_q_norm, mla_k_norm, c_w_in, c_w_out, c_q_norm, c_k_norm, c_sink):
    b, n, d = x.shape
    n_c = ctx.shape[1]
    depth = ada_w.shape[0]
    assert b + 1 <= MOD_ROWS and n % max(TM_PROJ_C, TK_MLA) == 0 and n_c % (S5_CHUNK * SUBLANES) == 0
    rows = n // GRID_W
    tm_ctx = n_c

    cond = jnp.zeros((MOD_ROWS, d), F32).at[:b].set(c).at[b].set(c_ctx)
    mods = _ada_modulation(cond, ada_w, ada_b)
    mods = mods.reshape(depth, MOD_ROWS, N_MOD, 1, d)

    cos_a, sin_a = _grid_rope_tables(rows, MLA_ROPE)
    cs_a_lat = np.concatenate([cos_a, sin_a], axis=1)
    cs_a_ctx = np.concatenate([np.ones((n_c, MLA_ROPE), np.float32), np.zeros((n_c, MLA_ROPE), np.float32)], axis=1)
    cos_c, sin_c = _grid_rope_tables(rows, WIN_HEAD_DIM)
    cos_c2, sin_c2 = np.tile(cos_c, (1, 2)), np.tile(sin_c, (1, 2))
    one_c, zero_c = np.ones((n_c, LANES), np.float32), np.zeros((n_c, LANES), np.float32)
    assert ffn_w_gate.shape[2] % MXU_TILE == 0
    ffn = dict(wg=ffn_w_gate.astype(BF16), wu=ffn_w_up.astype(BF16), wd=ffn_w_down.astype(BF16))

    h_ctx, h_lat = ctx, x
    for i in range(depth):
        need_ctx = i < depth - 1
        j = i // 2
        m_i = mods[i]
        if i % 2 == 0:
            pw = _prep_proj_a(norm_mix[i], a_w_in[j], mla_qa_norm[j], mla_w_q_b[j], mla_kva_norm[j],
                              mla_w_kv_b[j], mla_q_norm[j], mla_k_norm[j])
            u_l, q_l, k_l, vt_l = _proj_a(h_lat, m_i, None, TM_PROJ_A, pw, cs_a_lat)
            u_c, q_c, k_c, vt_c = _proj_a(h_ctx, m_i, b, tm_ctx, pw, cs_a_ctx)
            tables = _s5_tables(s5_lam_re[j], s5_lam_im[j], s5_log_step[j], s5_b_re[j], s5_b_im[j],
                                s5_c_re[j], s5_c_im[j], s5_d[j])
            o_l = _mla_attention(q_l, k_c, vt_c, k_l, vt_l, tq=n)
            yg_c, yg_l = _s5(u_c, u_l, tables)
            post_w = _prep_post(norm_ffn[i], a_w_out[j], s5_w_glu[j], s5_b_glu[j])
            h_lat_new = _post(h_lat, m_i, None, TM_FFN, post_w, ffn, i, o_l, yg_l)
            if need_ctx:
                o_c = _mla_attention(q_c, k_c, vt_c, tq=n_c)
                h_ctx = _post(h_ctx, m_i, b, tm_ctx, post_w, ffn, i, o_c, yg_c)
            h_lat = h_lat_new
        else:
            pw = _prep_proj_c(norm_mix[i], c_w_in[j], c_q_norm[j], c_k_norm[j])
            q_l, k_l, vt_l = _proj_c(h_lat, m_i, None, TM_PROJ_C, pw, cos_c2, sin_c2)
            q_c, k_c, vt_c = _proj_c(h_ctx, m_i, b, tm_ctx, pw, one_c, zero_c)
            o_l = _win_attention(q_l, k_l, vt_l, k_c, vt_c, c_sink[j], tq=TQ_WIN)
            post_w = _prep_post(norm_ffn[i], c_w_out[j])
            h_lat_new = _post(h_lat, m_i, None, TM_FFN, post_w, ffn, i, o_l)
            if need_ctx:
                raise NotImplementedError("context queries of a windowed layer")
            h_lat = h_lat_new
    return h_lat
```

```python
import functools
import math

import jax
import jax.numpy as jnp
import numpy as np
from jax import lax
from jax.experimental import pallas as pl
from jax.experimental.pallas import tpu as pltpu

F32 = jnp.float32
BF16 = jnp.bfloat16

GRID_W = 64
NORM_EPS = 1e-6
ROPE_THETA = 10000.0
N_MOD = 6
S5_GROUP_DIM = 16
S5_STATE = 64
S5_CHUNK = 16
MLA_HEADS = 4
MLA_NOPE = 128
MLA_ROPE = 64
MLA_QK_DIM = MLA_NOPE + MLA_ROPE
MLA_V = 128
MLA_Q_RANK = 384
MLA_KV_RANK = 256
MLA_SCALE = MLA_QK_DIM ** -0.5
MLA_HEAD_PAD = 256
WIN_HEADS = 16
WIN_KV_HEADS = 4
WIN_GROUP = WIN_HEADS // WIN_KV_HEADS
WIN_HEAD_DIM = 64
WINDOW = 128
WIN_SCALE = WIN_HEAD_DIM ** -0.5
LANES = 128
SUBLANES = 8
MXU_TILE = 256
VMEM_BYTES_V7X = 64 * 1024 * 1024
VMEM_LIMIT_BYTES = (VMEM_BYTES_V7X * 3) // 4
NEG_BIG = -1e30
LOG2_E = math.log2(math.e)
TM_FFN = 512
TM_PROJ_A = 512
TM_PROJ_C = 1024
TQ_WIN = 256
TK_MLA = 1024
TN_ADA = 1024
PROJ_ROWS = 256
SOFTMAX_DEN_MIN = 2.0 ** -60
SOFTMAX_DEN_MAX = 2.0 ** 60
S5_BLOCK_PAIRS = 4
MLA_Q_BLOCK = 256
MLA_K_PIECE = 256
MOD_ROWS = 8


def _dot(a, b):
    return jnp.dot(a, b, preferred_element_type=F32)


def _dot_nt(a, b):
    return lax.dot_general(a, b, (((1,), (1,)), ((), ())), preferred_element_type=F32)


def _split_bf16(x):
    hi = x.astype(BF16)
    lo = (x - hi.astype(F32)).astype(BF16)
    return hi, lo


def _rms(x, gain):
    return x * lax.rsqrt(jnp.mean(x * x, axis=-1, keepdims=True) + NORM_EPS) * gain


def _silu(x):
    return x * jax.nn.sigmoid(x)


def _params(*sem):
    return pltpu.CompilerParams(dimension_semantics=sem, vmem_limit_bytes=VMEM_LIMIT_BYTES)


def _const_spec(shape, layer=None):
    if layer is None:
        nd = len(shape)
        return pl.BlockSpec(shape, lambda *_: (0,) * nd, pipeline_mode=pl.Buffered(1))
    nd = len(shape) - 1
    return pl.BlockSpec((None,) + tuple(shape[1:]), lambda *_: (layer,) + (0,) * nd,
                        pipeline_mode=pl.Buffered(1))


def _ada_kernel(cond_ref, w_ref, b_ref, o_ref):
    s = _silu(cond_ref[...])
    s_hi, s_lo = _split_bf16(s)
    w_hi, w_lo = _split_bf16(w_ref[...])
    o_ref[...] = _dot(s_hi, w_hi) + _dot(s_lo, w_hi) + _dot(s_hi, w_lo) + b_ref[...]


def _ada_modulation(cond, ada_w, ada_b):
    depth, d, n = ada_w.shape
    tn = TN_ADA
    return pl.pallas_call(
        _ada_kernel,
        grid=(depth, n // tn),
        in_specs=[pl.BlockSpec((MOD_ROWS, d), lambda i, j: (0, 0)),
                  pl.BlockSpec((None, d, tn), lambda i, j: (i, 0, j)),
                  pl.BlockSpec((None, 1, tn), lambda i, j: (i, 0, j))],
        out_specs=pl.BlockSpec((None, MOD_ROWS, tn), lambda i, j: (i, 0, j)),
        out_shape=jax.ShapeDtypeStruct((depth, MOD_ROWS, n), F32),
        compiler_params=_params("arbitrary", "arbitrary"),
        name="ada_modulation",
    )(cond, ada_w, ada_b.reshape(depth, 1, n))


def _mod_specs(d, slots, ctx_row):
    def make(slot):
        if ctx_row is None:
            return pl.BlockSpec((None, None, 1, d), lambda b, i: (b, slot, 0, 0))
        return pl.BlockSpec((None, None, 1, d), lambda b, i: (ctx_row, slot, 0, 0))
    return [make(s) for s in slots]


def _grid_rope_tables(rows, rot_dim):
    n_freq = rot_dim // 4
    inv_freq = np.power(np.float32(ROPE_THETA), -np.arange(n_freq, dtype=np.float32) / np.float32(n_freq))
    ang_r = np.arange(rows, dtype=np.float32)[:, None] * inv_freq.astype(np.float32)
    ang_c = np.arange(GRID_W, dtype=np.float32)[:, None] * inv_freq.astype(np.float32)

    def expand(r, c):
        r = np.broadcast_to(r[:, None, :], (rows, GRID_W, n_freq))
        c = np.broadcast_to(c[None, :, :], (rows, GRID_W, n_freq))
        return np.concatenate([r, r, c, c], axis=-1).reshape(rows * GRID_W, rot_dim)

    return expand(np.cos(ang_r), np.cos(ang_c)), expand(np.sin(ang_r), np.sin(ang_c))


def _rot_perm_sign(rot_dim):
    q = rot_dim // 4
    idx = np.arange(rot_dim)
    perm = np.where((idx // q) % 2 == 0, idx + q, idx - q)
    sign = np.where((idx // q) % 2 == 0, -1.0, 1.0).astype(np.float32)
    return perm, sign


def _proj_a_kernel(h_ref, sh_ref, sc_ref, gn_ref, w1_ref, gqa_ref, wq_ref, gkv_ref, wkv_ref, wvt_ref,
                   gq_ref, gk_ref, cs_ref, msk_ref, u_ref, q_ref, k_ref, vt_ref):
    tm = h_ref.shape[0]
    rb = min(PROJ_ROWS, tm)
    msk = msk_ref[...]
    low_half = lax.broadcasted_iota(jnp.int32, (rb, LANES), 1) < MLA_ROPE

    def finish(xh, gain, cs, out_ref, rows, h):
        ssq = _dot((xh * xh).astype(BF16), msk)
        xn = xh * lax.rsqrt(ssq * (1.0 / MLA_QK_DIM) + NORM_EPS) * gain
        rr = xn[:, LANES:] * cs
        rot = rr + pltpu.roll(rr, MLA_ROPE, axis=1)
        base = h * MLA_HEAD_PAD
        out_ref[rows, base:base + LANES] = xn[:, :LANES].astype(BF16)
        out_ref[rows, base + LANES:base + 2 * LANES] = jnp.where(low_half, rot, 0.0).astype(BF16)

    for r in range(tm // rb):
        rows = slice(r * rb, (r + 1) * rb)
        a = _rms(h_ref[rows, :], gn_ref[...]) * (1.0 + sc_ref[...]) + sh_ref[...]
        p1 = _dot(a.astype(BF16), w1_ref[...])
        s5w = u_ref.shape[1]
        u_ref[rows, :] = p1[:, :s5w]
        cq = p1[:, s5w:s5w + MLA_Q_RANK]
        ckv = p1[:, s5w + MLA_Q_RANK:s5w + MLA_Q_RANK + MLA_KV_RANK]
        krr = p1[:, s5w + MLA_Q_RANK + MLA_KV_RANK:]
        qb = _dot(_rms(cq, gqa_ref[...]).astype(BF16), wq_ref[...])
        ckv_n = _rms(ckv, gkv_ref[...]).astype(BF16)
        kv = _dot(ckv_n, wkv_ref[...])
        vt_ref[:, rows] = _dot_nt(wvt_ref[...], ckv_n).astype(BF16)
        cs = cs_ref[rows, :]
        for h in range(MLA_HEADS):
            base = h * MLA_HEAD_PAD
            finish(qb[:, base:base + MLA_HEAD_PAD], gq_ref[:, base:base + MLA_HEAD_PAD], cs, q_ref, rows, h)
            kh = jnp.concatenate([kv[:, h * MLA_NOPE:(h + 1) * MLA_NOPE], krr], axis=1)
            finish(kh, gk_ref[:, base:base + MLA_HEAD_PAD], cs, k_ref, rows, h)


def _proj_a(h, mods, ctx_row, tm, wts, cs):
    b, t, d = h.shape
    qw = MLA_HEADS * MLA_HEAD_PAD
    grid = (b, t // tm)
    tile = lambda w: pl.BlockSpec((None, tm, w), lambda bi, i: (bi, i, 0))
    in_specs = ([tile(d)] + _mod_specs(d, (0, 1), ctx_row)
                + [_const_spec(wts[k].shape) for k in
                   ("gn", "w1", "gqa", "wq", "gkv", "wkv", "wvt", "gq", "gk")]
                + [pl.BlockSpec((tm, LANES), lambda bi, i: (i, 0)), _const_spec(wts["msk"].shape)])
    vw = MLA_HEADS * MLA_V
    s5w = wts["w1"].shape[1] - (MLA_Q_RANK + MLA_KV_RANK + 2 * MLA_ROPE)
    return pl.pallas_call(
        _proj_a_kernel,
        grid=grid,
        in_specs=in_specs,
        out_specs=[tile(s5w), tile(qw), tile(qw), pl.BlockSpec((None, vw, tm), lambda bi, i: (bi, 0, i))],
        out_shape=[jax.ShapeDtypeStruct((b, t, s5w), F32),
                   jax.ShapeDtypeStruct((b, t, qw), BF16),
                   jax.ShapeDtypeStruct((b, t, qw), BF16),
                   jax.ShapeDtypeStruct((b, vw, t), BF16)],
        compiler_params=_params("parallel", "parallel"),
        name="proj_a",
    )(h, mods, mods, wts["gn"], wts["w1"], wts["gqa"], wts["wq"], wts["gkv"], wts["wkv"], wts["wvt"],
      wts["gq"], wts["gk"], cs, wts["msk"])


def _prep_proj_a(norm_mix, a_w_in, qa_norm, w_q_b, kva_norm, w_kv_b, q_norm, k_norm):
    perm, sign = _rot_perm_sign(MLA_ROPE)
    s5w = a_w_in.shape[1] - (MLA_Q_RANK + MLA_KV_RANK + MLA_ROPE)
    assert s5w % LANES == 0 and MLA_Q_RANK % LANES == 0 and MLA_KV_RANK % LANES == 0
    kr = a_w_in[:, -MLA_ROPE:]
    w1 = jnp.concatenate([a_w_in, kr[:, perm] * sign], axis=1).astype(BF16)
    wq = w_q_b.reshape(MLA_Q_RANK, MLA_HEADS, MLA_QK_DIM)
    rope = wq[:, :, MLA_NOPE:]
    wq = jnp.concatenate([wq, rope[:, :, perm] * sign], axis=2)
    wq = wq.reshape(MLA_Q_RANK, MLA_HEADS * MLA_HEAD_PAD).astype(BF16)
    wkv3 = w_kv_b.reshape(MLA_KV_RANK, MLA_HEADS, MLA_NOPE + MLA_V)
    wkv = wkv3[:, :, :MLA_NOPE].reshape(MLA_KV_RANK, -1).astype(BF16)
    wvt = wkv3[:, :, MLA_NOPE:].reshape(MLA_KV_RANK, -1).T.astype(BF16)

    def head_gain(g, scale):
        gb = jnp.concatenate([g, g[MLA_NOPE:][perm]]) * scale
        return jnp.tile(gb, MLA_HEADS)[None, :]

    rows = np.arange(MLA_HEAD_PAD)[:, None] < MLA_QK_DIM
    msk = jnp.asarray(np.broadcast_to(rows, (MLA_HEAD_PAD, MLA_HEAD_PAD)), BF16)
    return dict(gn=norm_mix[None, :], w1=w1, gqa=qa_norm[None, :], wq=wq, gkv=kva_norm[None, :],
                wkv=wkv, wvt=wvt, gq=head_gain(q_norm, MLA_SCALE * LOG2_E), gk=head_gain(k_norm, 1.0),
                msk=msk)


def _s5_kernel(uc_ref, ul_ref, sel_in_ref, sel_out_ref, toep_ref, bst_ref, cst_ref, a_ref, d_ref,
               yc_ref, yl_ref, ub_ref, x_ref, z_ref, sin_ref, yb_ref, *, n_ctx, n_all):
    L, S = S5_CHUNK, S5_GROUP_DIM
    n_lat = n_all - n_ctx
    npair = S5_BLOCK_PAIRS
    half = SUBLANES * LANES
    for tl in range(L):
        u = jnp.concatenate([uc_ref[pl.ds(tl, n_ctx, stride=L), :], ul_ref[pl.ds(tl, n_lat, stride=L), :]], axis=0)
        ub_ref[:, tl * LANES:(tl + 1) * LANES] = u.astype(BF16)
    t_ctx, t_all = n_ctx // SUBLANES, n_all // SUBLANES
    row = lax.broadcasted_iota(jnp.int32, (SUBLANES, LANES), 0)

    def cmul(a_re, a_im, b_re, b_im):
        return a_re * b_re - a_im * b_im, a_re * b_im + a_im * b_re

    def tile_scan(z_re, z_im, c_re, c_im, a_re, a_im, fwd):
        for sft in (1, 2, 4):
            k = sft - 1 if fwd else SUBLANES - sft
            p_re, p_im = a_re[k:k + 1, :], a_im[k:k + 1, :]
            amt = sft if fwd else SUBLANES - sft
            keep = (row >= sft) if fwd else (row < SUBLANES - sft)
            s_re = jnp.where(keep, pltpu.roll(z_re, amt, axis=0), 0.0)
            s_im = jnp.where(keep, pltpu.roll(z_im, amt, axis=0), 0.0)
            m_re, m_im = cmul(p_re, p_im, s_re, s_im)
            z_re, z_im = z_re + m_re, z_im + m_im
        m_re, m_im = cmul(a_re, a_im, c_re, c_im)
        s_re, s_im = z_re + m_re, z_im + m_im
        edge = (row == 0) if fwd else (row == SUBLANES - 1)
        amt = 1 if fwd else SUBLANES - 1
        in_re = jnp.where(edge, c_re, pltpu.roll(s_re, amt, axis=0))
        in_im = jnp.where(edge, c_im, pltpu.roll(s_im, amt, axis=0))
        last = SUBLANES - 1 if fwd else 0
        return in_re, in_im, s_re[last:last + 1, :], s_im[last:last + 1, :]

    def relayout_items(pairs):
        items = []
        for pp in pairs:
            for hh in range(2):
                def sel(pp=pp, hh=hh):
                    xs = _dot(ub_ref[:, hh * half:(hh + 1) * half], sel_in_ref[pp]).astype(BF16)
                    x_ref[pp, :, hh * LANES:(hh + 1) * LANES] = xs[:, :LANES]
                    x_ref[pp, :, MXU_TILE + hh * LANES:MXU_TILE + (hh + 1) * LANES] = xs[:, LANES:]
                items.append(sel)

            def drive(pp=pp):
                z_ref[pp] = _dot(x_ref[pp], bst_ref[pp])
            items.append(drive)
        return items

    def readout_items(pairs):
        items = []
        for pp in pairs:
            def readout(pp=pp):
                x = x_ref[pp]
                y = jnp.concatenate([_dot(x[:, :MXU_TILE], toep_ref[pp, 0]), _dot(x[:, MXU_TILE:], toep_ref[pp, 1])],
                                    axis=1)
                y = y + _dot(sin_ref[pp].astype(BF16), cst_ref[pp]) + x.astype(F32) * d_ref[pp]
                yg = jax.nn.gelu(y).astype(BF16)
                for gl in range(2):
                    for hh in range(2):
                        g8 = 2 * pp + gl
                        col = gl * MXU_TILE + hh * LANES
                        yb_ref[hh, :, g8 * LANES:(g8 + 1) * LANES] = yg[:, col:col + LANES]
            items.append(readout)
        return items

    def scan_with(pairs, items):
        zero = jnp.zeros((1, LANES), F32)
        carry = {pp: (zero, zero, zero, zero) for pp in pairs}
        every = max(1, t_all // max(1, len(items)))
        pending = list(items)
        for it in range(t_all):
            jt = t_ctx - 1 - it if it < t_ctx else t_all + t_ctx - 1 - it
            rf, rb = it * SUBLANES, jt * SUBLANES
            for pp in pairs:
                cf_re, cf_im, cb_re, cb_im = carry[pp]
                zf = z_ref[pp, rf:rf + SUBLANES, 0:2 * LANES]
                zb = z_ref[pp, rb:rb + SUBLANES, 2 * LANES:4 * LANES]
                f_re, f_im, cf_re, cf_im = tile_scan(zf[:, :LANES], zf[:, LANES:], cf_re, cf_im,
                                                     a_ref[pp, 0], a_ref[pp, 1], True)
                b_re, b_im, cb_re, cb_im = tile_scan(zb[:, :LANES], zb[:, LANES:], cb_re, cb_im,
                                                     a_ref[pp, 2], a_ref[pp, 3], False)
                sin_ref[pp, rf:rf + SUBLANES, 0:2 * LANES] = jnp.concatenate([f_re, f_im], axis=1)
                sin_ref[pp, rb:rb + SUBLANES, 2 * LANES:4 * LANES] = jnp.concatenate([b_re, b_im], axis=1)
                carry[pp] = (cf_re, cf_im, cb_re, cb_im)
            if pending and (it + 1) % every == 0:
                pending.pop(0)()
        for item in pending:
            item()

    first, second = tuple(range(npair // 2)), tuple(range(npair // 2, npair))
    for item in relayout_items(first):
        item()
    scan_with(first, relayout_items(second))
    scan_with(second, readout_items(first))
    for item in readout_items(second):
        item()
    for hh in range(2):
        for kk in range(SUBLANES // 2):
            two = _dot(yb_ref[hh], sel_out_ref[kk])
            for e in range(2):
                tl = hh * SUBLANES + 2 * kk + e
                yc_ref[pl.ds(tl, n_ctx, stride=L), :] = two[:n_ctx, e * LANES:(e + 1) * LANES]
                yl_ref[pl.ds(tl, n_lat, stride=L), :] = two[n_ctx:, e * LANES:(e + 1) * LANES]


def _s5_selectors():
    S = S5_GROUP_DIM
    r = np.arange(SUBLANES * LANES)[:, None]
    c = np.arange(2 * LANES)[None, :]
    k, l = r // LANES, r % LANES
    pp = np.arange(S5_BLOCK_PAIRS)[:, None, None]
    sel_in = (k == (c % LANES) // S) & (l == 2 * S * pp + S * (c // LANES) + c % S)
    kk = np.arange(SUBLANES // 2)[:, None, None]
    sel_out = (k == (c % LANES) // S) & (l == S * (2 * kk + c // LANES) + c % S)
    return jnp.asarray(sel_in, BF16), jnp.asarray(sel_out, BF16)


def _dot_nt_f32(a, b):
    a_hi, a_lo = _split_bf16(a)
    b_hi, b_lo = _split_bf16(b)
    return _dot_nt(a_hi, b_hi) + _dot_nt(a_lo, b_hi) + _dot_nt(a_hi, b_lo)


def _s5_table_kernel(par_ref, bre_ref, bim_ref, cre_ref, cim_ref, toep_ref, bst_ref, cst_ref, a_ref):
    L, S, P = S5_CHUNK, S5_GROUP_DIM, S5_STATE
    kk = lax.broadcasted_iota(jnp.int32, (3 * SUBLANES, LANES), 0).astype(F32)
    lane = lax.broadcasted_iota(jnp.int32, (L, LANES), 1)
    pair_rows = lax.broadcasted_iota(jnp.int32, (2 * L * S, LANES), 0)
    pair_lanes = lax.broadcasted_iota(jnp.int32, (2 * L * S, LANES), 1)
    own_group = (pair_rows // (L * S)) == (pair_lanes // P)
    lane_pad = jnp.zeros((2 * P, LANES - S), F32)

    def rows_of_powers(pw, ks, groups):
        one = jnp.concatenate([jnp.broadcast_to(pw[k:k + 1, :], (S, LANES)) for k in ks], axis=0)
        return jnp.concatenate([one] * groups, axis=0) if groups > 1 else one

    def cmul(a_re, a_im, b_re, b_im):
        return a_re * b_re - a_im * b_im, a_re * b_im + a_im * b_re

    lag_tables = []
    for d in range(2):
        lam_re, lam_im = par_ref[d, 0:1, :], par_ref[d, 1:2, :]
        step = jnp.exp(par_ref[d, 2:3, :])
        ar, ai = lam_re * step, lam_im * step
        mag = jnp.exp(kk * ar)
        pw_re, pw_im = mag * jnp.cos(kk * ai), mag * jnp.sin(kk * ai)
        th = jnp.tanh(0.5 * ar)
        em1 = 2.0 * th / (1.0 - th)
        sh = jnp.sin(0.5 * ai)
        n_re = em1 * jnp.cos(ai) - 2.0 * sh * sh
        n_im = (em1 + 1.0) * jnp.sin(ai)
        den = lam_re * lam_re + lam_im * lam_im
        co_re = (n_re * lam_re + n_im * lam_im) / den
        co_im = (n_im * lam_re - n_re * lam_im) / den
        bt_re = jnp.concatenate([bre_ref[d], lane_pad], axis=1).T[:S]
        bt_im = jnp.concatenate([bim_ref[d], lane_pad], axis=1).T[:S]
        bb_re, bb_im = cmul(co_re, co_im, bt_re, bt_im)
        cc_re = jnp.concatenate([cre_ref[d, 0], cre_ref[d, 1]], axis=1)
        cc_im = jnp.concatenate([cim_ref[d, 0], cim_ref[d, 1]], axis=1)
        ks = [L - 1 - t for t in range(L)] if d == 0 else list(range(L))
        r_re, r_im = rows_of_powers(pw_re, ks, 2), rows_of_powers(pw_im, ks, 2)
        bbt_re, bbt_im = jnp.concatenate([bb_re] * (2 * L), axis=0), jnp.concatenate([bb_im] * (2 * L), axis=0)
        v_re, v_im = cmul(r_re, r_im, bbt_re, bbt_im)
        bst_ref[:, (2 * d) * LANES:(2 * d + 1) * LANES] = jnp.where(own_group, v_re, 0.0).astype(BF16)
        bst_ref[:, (2 * d + 1) * LANES:(2 * d + 2) * LANES] = jnp.where(own_group, v_im, 0.0).astype(BF16)
        ks = [t + 1 for t in range(L)] if d == 0 else [L - t for t in range(L)]
        r_re, r_im = rows_of_powers(pw_re, ks, 2), rows_of_powers(pw_im, ks, 2)
        cct_re, cct_im = jnp.concatenate([cc_re] * (2 * L), axis=0), jnp.concatenate([cc_im] * (2 * L), axis=0)
        v_re, v_im = cmul(cct_re, cct_im, r_re, r_im)
        cst_ref[(2 * d) * LANES:(2 * d + 1) * LANES, :] = jnp.where(own_group, v_re, 0.0).T.astype(BF16)
        cst_ref[(2 * d + 1) * LANES:(2 * d + 2) * LANES, :] = jnp.where(own_group, -v_im, 0.0).T.astype(BF16)
        ks = list(range(L)) if d == 0 else [L - 1 - j for j in range(L)]
        r_re, r_im = rows_of_powers(pw_re, ks, 1), rows_of_powers(pw_im, ks, 1)
        cl_re, cl_im = cmul(jnp.concatenate([cc_re] * L, axis=0), jnp.concatenate([cc_im] * L, axis=0), r_re, r_im)
        per_group = []
        for g in range(2):
            mine = (lane // P) == g
            per_group.append(_dot_nt_f32(jnp.where(mine, bb_re, 0.0), cl_re)
                             - _dot_nt_f32(jnp.where(mine, bb_im, 0.0), cl_im))
        lag_tables.append(per_group)
        row8 = lax.broadcasted_iota(jnp.int32, (SUBLANES, LANES), 0)
        n_chunks = ((row8 + 1) if d == 0 else (SUBLANES - row8)).astype(F32) * float(L)
        mag8 = jnp.exp(n_chunks * ar)
        a_ref[2 * d] = mag8 * jnp.cos(n_chunks * ai)
        a_ref[2 * d + 1] = mag8 * jnp.sin(n_chunks * ai)

    def shift_right(x, s):
        x0, x1 = x[:, :LANES], x[:, LANES:]
        a, r = divmod(s, LANES)
        r0 = pltpu.roll(x0, r, axis=1) if r else x0
        r1 = pltpu.roll(x1, r, axis=1) if r else x1
        if a == 0:
            return jnp.concatenate([jnp.where(lane >= r, r0, 0.0), jnp.where(lane >= r, r1, r0)], axis=1)
        return jnp.concatenate([jnp.zeros_like(x0), jnp.where(lane >= r, r0, 0.0)], axis=1)

    def shift_left(x, s):
        x0, x1 = x[:, :LANES], x[:, LANES:]
        a, r = divmod(s, LANES)
        r0 = pltpu.roll(x0, LANES - r, axis=1) if r else x0
        r1 = pltpu.roll(x1, LANES - r, axis=1) if r else x1
        if a == 0:
            return jnp.concatenate([jnp.where(lane < LANES - r, r0, r1), jnp.where(lane < LANES - r, r1, 0.0)], axis=1)
        return jnp.concatenate([jnp.where(lane < LANES - r, r1, 0.0), jnp.zeros_like(x0)], axis=1)

    for g in range(2):
        kf, kb = lag_tables[0][g], lag_tables[1][g]
        for tau in range(L):
            blk = shift_right(kf, S * tau) + shift_left(kb, S * (L - 1 - tau))
            toep_ref[g, tau * S:(tau + 1) * S, :] = blk.astype(BF16)


def _s5_tables(lam_re, lam_im, log_step, b_re, b_im, c_re, c_im, d_skip):
    _, G, P = lam_re.shape
    S, L = S5_GROUP_DIM, S5_CHUNK
    assert P == S5_STATE and 2 * P == LANES and 2 * L * S == 2 * MXU_TILE
    pairs = G // 2
    par = jnp.stack([lam_re.reshape(2, pairs, 2 * P), lam_im.reshape(2, pairs, 2 * P),
                     jnp.repeat(log_step, P, axis=-1).reshape(2, pairs, 2 * P)], axis=2)
    par = jnp.transpose(par, (1, 0, 2, 3)).astype(F32)
    bshape = (2, pairs, 2 * P, S)
    cshape = (2, pairs, 2, S, P)
    pw = 2 * L * S
    bspec = pl.BlockSpec((2, None, 2 * P, S), lambda g: (0, g, 0, 0))
    cspec = pl.BlockSpec((2, None, 2, S, P), lambda g: (0, g, 0, 0, 0))
    toep, bst, cst, a_chunk = pl.pallas_call(
        _s5_table_kernel,
        grid=(pairs,),
        in_specs=[pl.BlockSpec((None, 2, 3, 2 * P), lambda g: (g, 0, 0, 0)), bspec, bspec, cspec, cspec],
        out_specs=[pl.BlockSpec((None, 2, MXU_TILE, MXU_TILE), lambda g: (g, 0, 0, 0)),
                   pl.BlockSpec((None, pw, pw), lambda g: (g, 0, 0)),
                   pl.BlockSpec((None, pw, pw), lambda g: (g, 0, 0)),
                   pl.BlockSpec((None, 4, SUBLANES, LANES), lambda g: (g, 0, 0, 0))],
        out_shape=[jax.ShapeDtypeStruct((pairs, 2, MXU_TILE, MXU_TILE), BF16),
                   jax.ShapeDtypeStruct((pairs, pw, pw), BF16),
                   jax.ShapeDtypeStruct((pairs, pw, pw), BF16),
                   jax.ShapeDtypeStruct((pairs, 4, SUBLANES, LANES), F32)],
        compiler_params=_params("parallel"),
        name="s5_tables",
    )(par, b_re.reshape(bshape), b_im.reshape(bshape), c_re.reshape(cshape), c_im.reshape(cshape))
    d_pair = jnp.broadcast_to(d_skip.astype(F32).reshape(pairs, 2, 1, S), (pairs, 2, L, S))
    return toep, bst, cst, a_chunk, d_pair.reshape(pairs, 1, pw)


def _s5(u_ctx, u_lat, tables):
    toep, bst, cst, a_pow, d_pair = tables
    b, n_c, w = u_ctx.shape
    n_l = u_lat.shape[1]
    L = S5_CHUNK
    n_ctx, n_all = n_c // L, (n_c + n_l) // L
    nblk = w // LANES
    pw = 2 * L * S5_GROUP_DIM
    npair = S5_BLOCK_PAIRS
    sel_in, sel_out = _s5_selectors()
    wspec = lambda shape: pl.BlockSpec((npair,) + shape, lambda g, bi: (g,) + (0,) * len(shape),
                                       pipeline_mode=pl.Buffered(1))
    return pl.pallas_call(
        functools.partial(_s5_kernel, n_ctx=n_ctx, n_all=n_all),
        grid=(nblk, b),
        in_specs=[pl.BlockSpec((None, n_c, LANES), lambda g, bi: (bi, 0, g)),
                  pl.BlockSpec((None, n_l, LANES), lambda g, bi: (bi, 0, g)),
                  _const_spec(sel_in.shape), _const_spec(sel_out.shape),
                  wspec((2, MXU_TILE, MXU_TILE)), wspec((pw, pw)), wspec((pw, pw)),
                  wspec((4, SUBLANES, LANES)), wspec((1, pw))],
        out_specs=[pl.BlockSpec((None, n_c, LANES), lambda g, bi: (bi, 0, g)),
                   pl.BlockSpec((None, n_l, LANES), lambda g, bi: (bi, 0, g))],
        out_shape=[jax.ShapeDtypeStruct((b, n_c, w), F32), jax.ShapeDtypeStruct((b, n_l, w), F32)],
        scratch_shapes=[pltpu.VMEM((n_all, L * LANES), BF16), pltpu.VMEM((npair, n_all, pw), BF16),
                        pltpu.VMEM((npair, n_all, pw), F32), pltpu.VMEM((npair, n_all, pw), F32),
                        pltpu.VMEM((2, n_all, SUBLANES * LANES), BF16)],
        compiler_params=_params("parallel", "arbitrary"),
        name="s5_scan",
    )(u_ctx, u_lat, sel_in, sel_out, toep, bst, cst, a_pow, d_pair)


def _mla_kernel(*refs, tk, n_steps):
    if n_steps:
        q_ref, kc_ref, vc_ref, k_ref, v_ref, o_ref, qt_ref, s_ref, m_ref, l_ref, acc_ref = refs
    else:
        q_ref, kc_ref, vc_ref, o_ref, qt_ref, s_ref, m_ref, l_ref, acc_ref = refs
    tq = q_ref.shape[0]
    ncb = tq // MLA_Q_BLOCK
    kp = MLA_K_PIECE
    qt_ref[...] = q_ref[...].astype(F32).T.astype(BF16)

    def chunk(load_k, load_vt, nkeys, first):
        nr = nkeys // kp

        def score_piece(c, r):
            st = _dot(load_k(r), qt_ref[:, c * MLA_Q_BLOCK:(c + 1) * MLA_Q_BLOCK])
            s_ref[c % 2, r * kp:(r + 1) * kp, :] = st
            return jnp.max(st, axis=0, keepdims=True)

        def block_stats(c, mx):
            if first:
                return mx, None
            m_old = m_ref[:, c * MLA_Q_BLOCK:(c + 1) * MLA_Q_BLOCK]
            m_new = jnp.maximum(m_old, mx)
            return m_new, jnp.exp2(m_old - m_new)

        def prob_piece(c, r, m_new):
            p = jnp.exp2(s_ref[c % 2, r * kp:(r + 1) * kp, :] - m_new)
            return jnp.sum(p, axis=0, keepdims=True), _dot(load_vt(r), p.astype(BF16))

        def finish(c, m_new, alpha, lsum, pv):
            cols = slice(c * MLA_Q_BLOCK, (c + 1) * MLA_Q_BLOCK)
            if first:
                l_ref[:, cols] = lsum
                acc_ref[:, cols] = pv
            else:
                l_ref[:, cols] = alpha * l_ref[:, cols] + lsum
                acc_ref[:, cols] = alpha * acc_ref[:, cols] + pv
            m_ref[:, cols] = m_new

        mx = None
        for r in range(nr):
            pm = score_piece(0, r)
            mx = pm if mx is None else jnp.maximum(mx, pm)
        for c in range(ncb):
            m_new, alpha = block_stats(c, mx)
            mx = lsum = pv = None
            for r in range(nr):
                if c + 1 < ncb:
                    pm = score_piece(c + 1, r)
                    mx = pm if mx is None else jnp.maximum(mx, pm)
                ls, pvr = prob_piece(c, r, m_new)
                lsum = ls if lsum is None else lsum + ls
                pv = pvr if pv is None else pv + pvr
            finish(c, m_new, alpha, lsum, pv)

    chunk(lambda r: kc_ref[r * kp:(r + 1) * kp, :], lambda r: vc_ref[:, r * kp:(r + 1) * kp],
          kc_ref.shape[0], True)
    if n_steps:
        def body(j, _):
            off = pl.multiple_of(j * tk, tk)
            chunk(lambda r: k_ref[pl.ds(off + r * kp, kp), :], lambda r: v_ref[:, pl.ds(off + r * kp, kp)],
                  tk, False)
            return 0
        lax.fori_loop(0, n_steps, body, 0)
    o_ref[...] = (acc_ref[...] / l_ref[...]).T.astype(BF16)


def _mla_attention(q, k_ctx, vt_ctx, k_lat=None, vt_lat=None, *, tq, tk=TK_MLA):
    b, t, _ = q.shape
    n_c = k_ctx.shape[1]
    in_specs = [pl.BlockSpec((None, tq, MLA_HEAD_PAD), lambda bi, h, i: (bi, i, h)),
                pl.BlockSpec((None, n_c, MLA_HEAD_PAD), lambda bi, h, i: (bi, 0, h)),
                pl.BlockSpec((None, MLA_V, n_c), lambda bi, h, i: (bi, h, 0))]
    args = [q, k_ctx, vt_ctx]
    n_steps = 0
    if k_lat is not None:
        n_l = k_lat.shape[1]
        n_steps = n_l // tk
        in_specs += [pl.BlockSpec((None, n_l, MLA_HEAD_PAD), lambda bi, h, i: (bi, 0, h)),
                     pl.BlockSpec((None, MLA_V, n_l), lambda bi, h, i: (bi, h, 0))]
        args += [k_lat, vt_lat]
    return pl.pallas_call(
        functools.partial(_mla_kernel, tk=tk, n_steps=n_steps),
        grid=(b, MLA_HEADS, t // tq),
        in_specs=in_specs,
        out_specs=pl.BlockSpec((None, tq, MLA_V), lambda bi, h, i: (bi, i, h)),
        out_shape=jax.ShapeDtypeStruct((b, t, MLA_HEADS * MLA_V), BF16),
        scratch_shapes=[pltpu.VMEM((MLA_HEAD_PAD, tq), BF16), pltpu.VMEM((2, max(tk, n_c), MLA_Q_BLOCK), F32),
                        pltpu.VMEM((1, tq), F32), pltpu.VMEM((1, tq), F32), pltpu.VMEM((MLA_V, tq), F32)],
        compiler_params=_params("parallel", "parallel", "arbitrary"),
        name="mla_attention",
    )(*args)


def _proj_c_kernel(h_ref, sh_ref, sc_ref, gn_ref, wc_ref, wvt_ref, gqk_ref, cos_ref, sin_ref, bd_ref,
                   q_ref, k_ref, vt_ref):
    qw = WIN_HEADS * WIN_HEAD_DIM
    kw = WIN_KV_HEADS * WIN_HEAD_DIM
    bd = bd_ref[...]
    tm = h_ref.shape[0]
    rb = min(PROJ_ROWS, tm)
    lane = lax.broadcasted_iota(jnp.int32, (rb, LANES), 1)
    first_quarter = (lane % (WIN_HEAD_DIM // 2)) < (WIN_HEAD_DIM // 4)
    for r in range(tm // rb):
        rows = slice(r * rb, (r + 1) * rb)
        a = (_rms(h_ref[rows, :], gn_ref[...]) * (1.0 + sc_ref[...]) + sh_ref[...]).astype(BF16)
        p = _dot(a, wc_ref[...])
        vt_ref[:, rows] = _dot_nt(wvt_ref[...], a).astype(BF16)
        cos, sin = cos_ref[rows, :], sin_ref[rows, :]
        for j in range((qw + kw) // MXU_TILE):
            xh = p[:, j * MXU_TILE:(j + 1) * MXU_TILE]
            ssq = _dot((xh * xh).astype(BF16), bd)
            xn = xh * lax.rsqrt(ssq * (1.0 / WIN_HEAD_DIM) + NORM_EPS) * gqk_ref[:, j * MXU_TILE:(j + 1) * MXU_TILE]
            for c in range(MXU_TILE // LANES):
                xc = xn[:, c * LANES:(c + 1) * LANES]
                fwd = pltpu.roll(xc, WIN_HEAD_DIM // 4, axis=1)
                bwd = pltpu.roll(xc, LANES - WIN_HEAD_DIM // 4, axis=1)
                y = (xc * cos + jnp.where(first_quarter, -bwd, fwd) * sin).astype(BF16)
                col = j * MXU_TILE + c * LANES
                if col < qw:
                    q_ref[rows, col:col + LANES] = y
                else:
                    k_ref[rows, col - qw:col - qw + LANES] = y


def _proj_c(h, mods, ctx_row, tm, wts, cos2, sin2):
    b, t, d = h.shape
    qw = WIN_HEADS * WIN_HEAD_DIM
    kw = WIN_KV_HEADS * WIN_HEAD_DIM
    tile = lambda w: pl.BlockSpec((None, tm, w), lambda bi, i: (bi, i, 0))
    tab = pl.BlockSpec((tm, LANES), lambda bi, i: (i, 0))
    in_specs = ([tile(d)] + _mod_specs(d, (0, 1), ctx_row)
                + [_const_spec(wts[k].shape) for k in ("gn", "wc", "wvt", "gqk")]
                + [tab, tab, _const_spec(wts["bd"].shape)])
    return pl.pallas_call(
        _proj_c_kernel,
        grid=(b, t // tm),
        in_specs=in_specs,
        out_specs=[tile(qw), tile(kw), pl.BlockSpec((None, kw, tm), lambda bi, i: (bi, 0, i))],
        out_shape=[jax.ShapeDtypeStruct((b, t, qw), BF16),
                   jax.ShapeDtypeStruct((b, t, kw), BF16),
                   jax.ShapeDtypeStruct((b, kw, t), BF16)],
        compiler_params=_params("parallel", "parallel"),
        name="proj_c",
    )(h, mods, mods, wts["gn"], wts["wc"], wts["wvt"], wts["gqk"], cos2, sin2, wts["bd"])


def _prep_proj_c(norm_mix, c_w_in, q_norm, k_norm):
    gqk = jnp.concatenate([jnp.tile(q_norm * (WIN_SCALE * LOG2_E), WIN_HEADS), jnp.tile(k_norm, WIN_KV_HEADS)])
    idx = np.arange(MXU_TILE) // WIN_HEAD_DIM
    bd = jnp.asarray(idx[:, None] == idx[None, :], BF16)
    qk = (WIN_HEADS + WIN_KV_HEADS) * WIN_HEAD_DIM
    return dict(gn=norm_mix[None, :], wc=c_w_in[:, :qk].astype(BF16), wvt=c_w_in[:, qk:].T.astype(BF16),
                gqk=gqk[None, :], bd=bd)


def _win_kernel(sc_ref, q_ref, k_ref, vt_ref, kc_ref, vct_ref, o_ref, ot_ref, *, tq, band, n_lat):
    i = pl.program_id(1)
    start = pl.multiple_of(jnp.clip(i * tq - WINDOW, 0, n_lat - band), WINDOW)
    hd, grp = WIN_HEAD_DIM, WIN_GROUP
    qt = q_ref[...].astype(F32).T.astype(BF16)
    k_pos = start + lax.broadcasted_iota(jnp.int32, (band, tq), 0)
    q_pos = i * tq + lax.broadcasted_iota(jnp.int32, (band, tq), 1)
    bias1 = jnp.where(jnp.abs(k_pos - q_pos) <= WINDOW, 0.0, NEG_BIG)
    bias = jnp.concatenate([bias1] * grp, axis=1)
    zeros = jnp.zeros((hd, grp * tq), BF16)

    def group_inputs(kv):
        qg = jnp.concatenate([qt[(kv * grp + g) * hd:(kv * grp + g + 1) * hd, :] for g in range(grp)], axis=1)
        qg = jnp.concatenate([qg, zeros] if kv % 2 == 0 else [zeros, qg], axis=0)
        col = (kv // 2) * LANES
        sink = jnp.concatenate([jnp.full((1, tq), sc_ref[kv * grp + g], F32) for g in range(grp)], axis=1)
        return qg, col, sink

    def weighted_values(kv, p_ctx, p_loc):
        return (_dot(vct_ref[kv * hd:(kv + 1) * hd, :], p_ctx.astype(BF16))
                + _dot(vt_ref[kv * hd:(kv + 1) * hd, pl.ds(start, band)], p_loc.astype(BF16)))

    def regroup(ot):
        return jnp.concatenate([ot[:, g * tq:(g + 1) * tq] for g in range(grp)], axis=0)

    def one_pass(kv):
        qg, col, sink = group_inputs(kv)
        p_ctx = jnp.exp2(_dot(kc_ref[:, col:col + LANES], qg))
        p_loc = jnp.exp2(_dot(k_ref[pl.ds(start, band), col:col + LANES], qg) + bias)
        den = (jnp.sum(p_loc, axis=0, keepdims=True) + jnp.sum(p_ctx, axis=0, keepdims=True)
               + jnp.exp2(sink))
        ot_ref[kv * grp * hd:(kv + 1) * grp * hd, :] = regroup(weighted_values(kv, p_ctx, p_loc) / den)
        return den

    def two_pass(kv):
        qg, col, sink = group_inputs(kv)
        s_ctx = _dot(kc_ref[:, col:col + LANES], qg)
        s_loc = _dot(k_ref[pl.ds(start, band), col:col + LANES], qg) + bias
        m = jnp.maximum(jnp.maximum(jnp.max(s_loc, axis=0, keepdims=True),
                                    jnp.max(s_ctx, axis=0, keepdims=True)), sink)
        p_loc = jnp.exp2(s_loc - m)
        p_ctx = jnp.exp2(s_ctx - m)
        den = (jnp.sum(p_loc, axis=0, keepdims=True) + jnp.sum(p_ctx, axis=0, keepdims=True)
               + jnp.exp2(sink - m))
        ot_ref[kv * grp * hd:(kv + 1) * grp * hd, :] = regroup(weighted_values(kv, p_ctx, p_loc) / den)

    lo = hi = None
    for kv in range(WIN_KV_HEADS):
        den = one_pass(kv)
        lo = den if lo is None else jnp.minimum(lo, den)
        hi = den if hi is None else jnp.maximum(hi, den)
    unsafe = jnp.logical_or(jnp.min(lo) < SOFTMAX_DEN_MIN, jnp.max(hi) > SOFTMAX_DEN_MAX)

    @pl.when(unsafe)
    def _():
        for kv in range(WIN_KV_HEADS):
            two_pass(kv)

    o_ref[...] = ot_ref[...].T.astype(BF16)


def _win_attention(q, k, vt, k_ctx, vt_ctx, sink, *, tq):
    b, n, qw = q.shape
    n_c = k_ctx.shape[1]
    kw = k.shape[2]
    band = tq + 2 * WINDOW
    full = lambda r, w: pl.BlockSpec((None, r, w), lambda bi, i: (bi, 0, 0))
    return pl.pallas_call(
        functools.partial(_win_kernel, tq=tq, band=band, n_lat=n),
        grid=(b, n // tq),
        in_specs=[pl.BlockSpec(memory_space=pltpu.SMEM),
                  pl.BlockSpec((None, tq, qw), lambda bi, i: (bi, i, 0)),
                  full(n, kw), full(kw, n), full(n_c, kw), full(kw, n_c)],
        out_specs=pl.BlockSpec((None, tq, qw), lambda bi, i: (bi, i, 0)),
        out_shape=jax.ShapeDtypeStruct((b, n, qw), BF16),
        scratch_shapes=[pltpu.VMEM((qw, tq), F32)],
        compiler_params=_params("parallel", "arbitrary"),
        name="win_attention",
    )(sink.astype(F32) * LOG2_E, q, k, vt, k_ctx, vt_ctx)


def _post_kernel(*refs, s5_width, n_chunks):
    if s5_width:
        (h_ref, g_ref, sh_ref, sc_ref, g2_ref, gn_ref, yg_ref, o_ref, wglu_ref, bglu_ref, wo_ref,
         wg_ref, wu_ref, wd_ref, out_ref, a_ref, acc_ref) = refs
        yg = yg_ref[...]
        s5 = yg * jax.nn.sigmoid(_dot(yg.astype(BF16), wglu_ref[...]) + bglu_ref[...])
        mix = _dot(s5.astype(BF16), wo_ref[:s5_width, :]) + _dot(o_ref[...], wo_ref[s5_width:, :])
    else:
        (h_ref, g_ref, sh_ref, sc_ref, g2_ref, gn_ref, o_ref, wo_ref,
         wg_ref, wu_ref, wd_ref, out_ref, a_ref, acc_ref) = refs
        mix = _dot(o_ref[...], wo_ref[...])
    h1 = h_ref[...] + g_ref[...] * mix
    a_ref[...] = (_rms(h1, gn_ref[...]) * (1.0 + sc_ref[...]) + sh_ref[...]).astype(BF16)
    acc_ref[...] = jnp.zeros_like(acc_ref)

    def body(c, _):
        a = a_ref[...]
        cols = pl.ds(pl.multiple_of(c * MXU_TILE, MXU_TILE), MXU_TILE)
        act = _silu(_dot(a, wg_ref[:, cols])) * _dot(a, wu_ref[:, cols])
        acc_ref[...] += _dot(act.astype(BF16), wd_ref[cols, :])
        return 0

    lax.fori_loop(0, n_chunks, body, 0, unroll=True)
    out_ref[...] = h1 + g2_ref[...] * acc_ref[...]


def _post(h, mods, ctx_row, tm, wts, ffn, layer, o, yg=None):
    b, t, d = h.shape
    tile = lambda w: pl.BlockSpec((None, tm, w), lambda bi, i: (bi, i, 0))
    s5_width = 0 if yg is None else yg.shape[2]
    in_specs = [tile(d)] + _mod_specs(d, (2, 3, 4, 5), ctx_row) + [_const_spec(wts["gn"].shape)]
    args = [h, mods, mods, mods, mods, wts["gn"]]
    if yg is not None:
        in_specs += [tile(s5_width), tile(o.shape[2]), _const_spec(wts["wglu"].shape),
                     _const_spec(wts["bglu"].shape)]
        args += [yg, o, wts["wglu"], wts["bglu"]]
    else:
        in_specs += [tile(o.shape[2])]
        args += [o]
    in_specs.append(_const_spec(wts["wo"].shape))
    args.append(wts["wo"])
    for k in ("wg", "wu", "wd"):
        in_specs.append(_const_spec(ffn[k].shape, layer))
        args.append(ffn[k])
    return pl.pallas_call(
        functools.partial(_post_kernel, s5_width=s5_width, n_chunks=ffn["wg"].shape[2] // MXU_TILE),
        grid=(b, t // tm),
        in_specs=in_specs,
        out_specs=tile(d),
        out_shape=jax.ShapeDtypeStruct((b, t, d), F32),
        scratch_shapes=[pltpu.VMEM((tm, d), BF16), pltpu.VMEM((tm, d), F32)],
        compiler_params=_params("parallel", "parallel"),
        name="post_ffn",
    )(*args)


def _prep_post(norm_ffn, w_out, w_glu=None, b_glu=None):
    wts = dict(gn=norm_ffn[None, :], wo=w_out.astype(BF16))
    if w_glu is not None:
        wts.update(wglu=w_glu.astype(BF16), bglu=b_glu[None, :])
    return wts


def kernel(x, c, ctx, c_ctx, ada_w, ada_b, norm_mix, norm_ffn, ffn_w_gate, ffn_w_up, ffn_w_down,
           a_w_in, a_w_out, s5_lam_re, s5_lam_im, s5_log_step, s5_b_re, s5_b_im, s5_c_re, s5_c_im,
           s5_d, s5_w_glu, s5_b_glu, mla_qa_norm, mla_w_q_b, mla_kva_norm, mla_w_kv_b,
           mla_q_norm, mla_k_norm, c_w_in, c_w_out, c_q_norm, c_k_norm, c_sink):
    b, n, d = x.shape
    n_c = ctx.shape[1]
    depth = ada_w.shape[0]
    assert b + 1 <= MOD_ROWS and n % max(TM_PROJ_C, TK_MLA) == 0 and n_c % (S5_CHUNK * SUBLANES) == 0
    rows = n // GRID_W
    tm_ctx = n_c

    cond = jnp.zeros((MOD_ROWS, d), F32).at[:b].set(c).at[b].set(c_ctx)
    mods = _ada_modulation(cond, ada_w, ada_b)
    mods = mods.reshape(depth, MOD_ROWS, N_MOD, 1, d)

    cos_a, sin_a = _grid_rope_tables(rows, MLA_ROPE)
    cs_a_lat = np.concatenate([cos_a, sin_a], axis=1)
    cs_a_ctx = np.concatenate([np.ones((n_c, MLA_ROPE), np.float32), np.zeros((n_c, MLA_ROPE), np.float32)], axis=1)
    cos_c, sin_c = _grid_rope_tables(rows, WIN_HEAD_DIM)
    cos_c2, sin_c2 = np.tile(cos_c, (1, 2)), np.tile(sin_c, (1, 2))
    one_c, zero_c = np.ones((n_c, LANES), np.float32), np.zeros((n_c, LANES), np.float32)
    assert ffn_w_gate.shape[2] % MXU_TILE == 0
    ffn = dict(wg=ffn_w_gate.astype(BF16), wu=ffn_w_up.astype(BF16), wd=ffn_w_down.astype(BF16))

    h_ctx, h_lat = ctx, x
    for i in range(depth):
        need_ctx = i < depth - 1
        j = i // 2
        m_i = mods[i]
        if i % 2 == 0:
            pw = _prep_proj_a(norm_mix[i], a_w_in[j], mla_qa_norm[j], mla_w_q_b[j], mla_kva_norm[j],
                              mla_w_kv_b[j], mla_q_norm[j], mla_k_norm[j])
            u_l, q_l, k_l, vt_l = _proj_a(h_lat, m_i, None, TM_PROJ_A, pw, cs_a_lat)
            u_c, q_c, k_c, vt_c = _proj_a(h_ctx, m_i, b, tm_ctx, pw, cs_a_ctx)
            tables = _s5_tables(s5_lam_re[j], s5_lam_im[j], s5_log_step[j], s5_b_re[j], s5_b_im[j],
                                s5_c_re[j], s5_c_im[j], s5_d[j])
            yg_c, yg_l = _s5(u_c, u_l, tables)
            o_l = _mla_attention(q_l, k_c, vt_c, k_l, vt_l, tq=n)
            post_w = _prep_post(norm_ffn[i], a_w_out[j], s5_w_glu[j], s5_b_glu[j])
            h_lat_new = _post(h_lat, m_i, None, TM_FFN, post_w, ffn, i, o_l, yg_l)
            if need_ctx:
                o_c = _mla_attention(q_c, k_c, vt_c, tq=n_c)
                flat = lambda t: t.reshape(1, b * n_c, t.shape[2])
                h_ctx = _post(flat(h_ctx), m_i, b, b * n_c, post_w, ffn, i, flat(o_c), flat(yg_c)).reshape(b, n_c, d)
            h_lat = h_lat_new
        else:
            pw = _prep_proj_c(norm_mix[i], c_w_in[j], c_q_norm[j], c_k_norm[j])
            q_l, k_l, vt_l = _proj_c(h_lat, m_i, None, TM_PROJ_C, pw, cos_c2, sin_c2)
            q_c, k_c, vt_c = _proj_c(h_ctx, m_i, b, tm_ctx, pw, one_c, zero_c)
            o_l = _win_attention(q_l, k_l, vt_l, k_c, vt_c, c_sink[j], tq=TQ_WIN)
            post_w = _prep_post(norm_ffn[i], c_w_out[j])
            h_lat_new = _post(h_lat, m_i, None, TM_FFN, post_w, ffn, i, o_l)
            if need_ctx:
                raise NotImplementedError("context queries of a windowed layer")
            h_lat = h_lat_new
    return h_lat
```
